```python
import math
import jax, jax.numpy as jnp
from jax import lax
import numpy as np

D_MODEL = 2048
BATCH = 4
SEQ = 2048
DEPTH = 1
DEC_BATCH = 16
DEC_SEQ = 32
PAST_LEN = 2048

CHUNK = 64
D_SSM = D_MODEL // 2
SSM_GROUP = 16
N_SSM_GROUPS = D_SSM // SSM_GROUP
SSM_STATE = 64
DT_MIN = 0.001
DT_MAX = 0.1
D_CONV = D_MODEL // 2
CONV_WIDTH = 31
CONV_BUF = CONV_WIDTH - 1
N_EXP_GROUPS = 4
EXP_PER_GROUP = 8
N_EXPERTS = N_EXP_GROUPS * EXP_PER_GROUP
TOP_K_INNER = 2
D_EXPERT = D_MODEL // 8
EPS = 1e-6
IN_COLS = D_SSM + 2 * D_CONV + 2 * D_MODEL

kernel_name = "hybrid_s5_conformer_hmoe_stream_step"


def rmsnorm(x, g):
    xf = x.astype(jnp.float32)
    r = lax.rsqrt(jnp.mean(xf * xf, axis=-1, keepdims=True) + EPS)
    return (xf * r * g.astype(jnp.float32)).astype(x.dtype)


def layernorm(x, g, b):
    xf = x.astype(jnp.float32)
    mu = jnp.mean(xf, axis=-1, keepdims=True)
    xc = xf - mu
    r = lax.rsqrt(jnp.mean(xc * xc, axis=-1, keepdims=True) + EPS)
    return (xc * r * g.astype(jnp.float32) + b.astype(jnp.float32)).astype(x.dtype)


def s5_scan(u, s_re, s_im, lam_re, lam_im, log_dt, b_re, b_im, c_re, c_im, d_skip):
    f32 = jnp.float32
    n, l, _ = u.shape
    uf = u.astype(f32).reshape(n, l, N_SSM_GROUPS, SSM_GROUP)
    lr = lam_re.astype(f32)
    li = lam_im.astype(f32)
    dt = jnp.exp(log_dt.astype(f32))[:, None]
    mag = jnp.exp(lr * dt)
    ar = mag * jnp.cos(li * dt)
    ai = mag * jnp.sin(li * dt)
    den = lr * lr + li * li
    nr = ar - 1.0
    cr = (nr * lr + ai * li) / den
    ci = (ai * lr - nr * li) / den
    br = b_re.astype(f32)
    bi = b_im.astype(f32)
    bbr = cr[..., None] * br - ci[..., None] * bi
    bbi = cr[..., None] * bi + ci[..., None] * br
    xr = jnp.einsum('nlgh,gph->nlgp', uf, bbr)
    xi = jnp.einsum('nlgh,gph->nlgp', uf, bbi)
    sr = s_re.astype(f32)
    si = s_im.astype(f32)
    xr = xr.at[:, 0].add(ar * sr - ai * si)
    xi = xi.at[:, 0].add(ar * si + ai * sr)
    a_r = jnp.broadcast_to(ar, xr.shape)
    a_i = jnp.broadcast_to(ai, xi.shape)

    def combine(e1, e2):
        a1r, a1i, b1r, b1i = e1
        a2r, a2i, b2r, b2i = e2
        return (a1r * a2r - a1i * a2i,
                a1r * a2i + a1i * a2r,
                a2r * b1r - a2i * b1i + b2r,
                a2r * b1i + a2i * b1r + b2i)

    _, _, hr, hi = lax.associative_scan(combine, (a_r, a_i, xr, xi), axis=1)
    y = (jnp.einsum('nlgp,ghp->nlgh', hr, c_re.astype(f32))
         - jnp.einsum('nlgp,ghp->nlgh', hi, c_im.astype(f32))
         + d_skip.astype(f32) * uf)
    return (y.reshape(n, l, D_SSM).astype(u.dtype),
            hr[:, -1].astype(u.dtype), hi[:, -1].astype(u.dtype))


def conv_branch(v, buf, conv_w, conv_b, ln_g, ln_b, w_conv_out):
    full = jnp.concatenate([buf.astype(v.dtype), v], axis=1)
    y = lax.conv_general_dilated(
        full, conv_w[:, None, :].astype(v.dtype), window_strides=(1,), padding='VALID',
        dimension_numbers=('NWC', 'WIO', 'NWC'), feature_group_count=D_CONV)
    y = y + conv_b.astype(v.dtype)
    y = jax.nn.silu(layernorm(y, ln_g, ln_b))
    return y @ w_conv_out.astype(v.dtype), full[:, -CONV_BUF:]


def mixer_block(x, s_re, s_im, buf, norm_mix_g, w_in, lam_re, lam_im, log_dt, b_re, b_im,
                c_re, c_im, d_skip, w_ssm_glu, conv_w, conv_b, conv_ln_g, conv_ln_b,
                w_conv_out, w_out):
    xn = rmsnorm(x, norm_mix_g)
    proj = xn @ w_in.astype(x.dtype)
    i1 = D_SSM
    i2 = i1 + D_CONV
    i3 = i2 + D_CONV
    i4 = i3 + D_MODEL
    u_ssm = proj[..., :i1]
    v_a = proj[..., i1:i2]
    v_b = proj[..., i2:i3]
    g_a = proj[..., i3:i4]
    g_b = proj[..., i4:]
    y_s, new_re, new_im = s5_scan(u_ssm, s_re, s_im, lam_re, lam_im, log_dt, b_re, b_im,
                                  c_re, c_im, d_skip)
    gl = jax.nn.gelu(y_s) @ w_ssm_glu.astype(x.dtype)
    y_a = gl[..., :D_MODEL] * jax.nn.sigmoid(gl[..., D_MODEL:])
    v = v_a * jax.nn.sigmoid(v_b)
    y_b, new_buf = conv_branch(v, buf, conv_w, conv_b, conv_ln_g, conv_ln_b, w_conv_out)
    m = jax.nn.sigmoid(g_a) * y_a + jax.nn.sigmoid(g_b) * y_b
    return x + m @ w_out.astype(x.dtype), new_re, new_im, new_buf


def hier_moe(x, norm_ffn_g, w_router_group, b_router_group, w_router_expert, b_router_expert,
             w_exp_gate, w_exp_up, w_exp_down):
    f32 = jnp.float32
    xn = rmsnorm(x, norm_ffn_g)
    lg = (xn @ w_router_group.astype(x.dtype)).astype(f32) + b_router_group.astype(f32)
    pg = jax.nn.softmax(lg, axis=-1)
    _, gsel = lax.top_k(lg, 1)
    pg_sel = jnp.take_along_axis(pg, gsel, axis=-1)
    le = (xn @ w_router_expert.astype(x.dtype)).astype(f32) + b_router_expert.astype(f32)
    le = le.reshape(le.shape[:-1] + (N_EXP_GROUPS, EXP_PER_GROUP))
    le_sel = jnp.take_along_axis(le, gsel[..., None], axis=-2)[..., 0, :]
    v2, i2 = lax.top_k(le_sel, TOP_K_INNER)
    w2 = jax.nn.softmax(v2, axis=-1) * pg_sel
    eid = gsel * EXP_PER_GROUP + i2
    gates = jnp.sum(jax.nn.one_hot(eid, N_EXPERTS, dtype=f32) * w2[..., None], axis=-2)
    hg = jnp.einsum('nld,edf->nlef', xn, w_exp_gate.astype(x.dtype))
    hu = jnp.einsum('nld,edf->nlef', xn, w_exp_up.astype(x.dtype))
    h = jax.nn.silu(hg) * hu * gates[..., None].astype(x.dtype)
    return x + jnp.einsum('nlef,efd->nld', h, w_exp_down.astype(x.dtype))


def setup_inputs(seed: int = 0) -> dict:
    key = jax.random.key(seed)
    ks = jax.random.split(key, 32)
    f32 = jnp.float32
    G, P, H = N_SSM_GROUPS, SSM_STATE, SSM_GROUP

    def nrm(k, shape, scale):
        return jax.random.normal(k, shape, f32) * scale

    lam_re = -0.5 + 0.01 * jax.random.uniform(ks[5], (DEPTH, G, P), f32, -1.0, 1.0)
    lam_im = (jnp.pi * jnp.arange(P, dtype=f32))[None, None, :] + nrm(ks[6], (DEPTH, G, P), 0.01)
    log_dt = jax.random.uniform(ks[7], (DEPTH, G), f32, math.log(DT_MIN), math.log(DT_MAX))
    return {
        "x_prompt": nrm(ks[0], (BATCH, SEQ, D_MODEL), 1.0),
        "x_sample": nrm(ks[1], (DEC_BATCH, DEC_SEQ, D_MODEL), 1.0),
        "state_ssm_re": nrm(ks[2], (DEPTH, DEC_BATCH, G, P), 0.1),
        "state_ssm_im": nrm(ks[3], (DEPTH, DEC_BATCH, G, P), 0.1),
        "cache_conv": nrm(ks[4], (DEPTH, DEC_BATCH, CONV_BUF, D_CONV), 0.5),
        "norm_mix_g": 1.0 + nrm(ks[8], (DEPTH, D_MODEL), 0.01),
        "w_in": nrm(ks[9], (DEPTH, D_MODEL, IN_COLS), D_MODEL ** -0.5),
        "lam_re": lam_re,
        "lam_im": lam_im,
        "log_dt": log_dt,
        "b_re": nrm(ks[10], (DEPTH, G, P, H), (2 * H) ** -0.5),
        "b_im": nrm(ks[11], (DEPTH, G, P, H), (2 * H) ** -0.5),
        "c_re": nrm(ks[12], (DEPTH, G, H, P), (2 * P) ** -0.5),
        "c_im": nrm(ks[13], (DEPTH, G, H, P), (2 * P) ** -0.5),
        "d_skip": nrm(ks[14], (DEPTH, G, H), 0.5),
        "w_ssm_glu": nrm(ks[15], (DEPTH, D_SSM, 2 * D_MODEL), D_SSM ** -0.5),
        "conv_w": nrm(ks[16], (DEPTH, CONV_WIDTH, D_CONV), CONV_WIDTH ** -0.5),
        "conv_b": nrm(ks[17], (DEPTH, D_CONV), 0.01),
        "conv_ln_g": 1.0 + nrm(ks[18], (DEPTH, D_CONV), 0.01),
        "conv_ln_b": nrm(ks[19], (DEPTH, D_CONV), 0.01),
        "w_conv_out": nrm(ks[20], (DEPTH, D_CONV, D_MODEL), D_CONV ** -0.5),
        "w_out": nrm(ks[21], (DEPTH, D_MODEL, D_MODEL), D_MODEL ** -0.5),
        "norm_ffn_g": 1.0 + nrm(ks[22], (DEPTH, D_MODEL), 0.01),
        "w_router_group": nrm(ks[23], (DEPTH, D_MODEL, N_EXP_GROUPS), D_MODEL ** -0.5),
        "b_router_group": nrm(ks[24], (DEPTH, N_EXP_GROUPS), 0.01),
        "w_router_expert": nrm(ks[25], (DEPTH, D_MODEL, N_EXPERTS), D_MODEL ** -0.5),
        "b_router_expert": nrm(ks[26], (DEPTH, N_EXPERTS), 0.01),
        "w_exp_gate": nrm(ks[27], (DEPTH, N_EXPERTS, D_MODEL, D_EXPERT), D_MODEL ** -0.5),
        "w_exp_up": nrm(ks[28], (DEPTH, N_EXPERTS, D_MODEL, D_EXPERT), D_MODEL ** -0.5),
        "w_exp_down": nrm(ks[29], (DEPTH, N_EXPERTS, D_EXPERT, D_MODEL), D_EXPERT ** -0.5),
        "norm_final_g": 1.0 + nrm(ks[30], (D_MODEL,), 0.01),
    }


def reference(x_prompt, x_sample, state_ssm_re, state_ssm_im, cache_conv, norm_mix_g, w_in,
              lam_re, lam_im, log_dt, b_re, b_im, c_re, c_im, d_skip, w_ssm_glu, conv_w,
              conv_b, conv_ln_g, conv_ln_b, w_conv_out, w_out, norm_ffn_g, w_router_group,
              b_router_group, w_router_expert, b_router_expert, w_exp_gate, w_exp_up,
              w_exp_down, norm_final_g):
    hp = x_prompt
    hs = x_sample
    n_p = x_prompt.shape[0]
    p_re, p_im, p_conv = [], [], []
    s_re_out, s_im_out, s_conv_out = [], [], []
    for l in range(DEPTH):
        mix_w = (norm_mix_g[l], w_in[l], lam_re[l], lam_im[l], log_dt[l], b_re[l], b_im[l],
                 c_re[l], c_im[l], d_skip[l], w_ssm_glu[l], conv_w[l], conv_b[l],
                 conv_ln_g[l], conv_ln_b[l], w_conv_out[l], w_out[l])
        moe_w = (norm_ffn_g[l], w_router_group[l], b_router_group[l], w_router_expert[l],
                 b_router_expert[l], w_exp_gate[l], w_exp_up[l], w_exp_down[l])
        z_state = jnp.zeros((n_p, N_SSM_GROUPS, SSM_STATE), hp.dtype)
        z_buf = jnp.zeros((n_p, CONV_BUF, D_CONV), hp.dtype)
        hp, nre, nim, nbuf = mixer_block(hp, z_state, z_state, z_buf, *mix_w)
        hp = hier_moe(hp, *moe_w)
        p_re.append(nre)
        p_im.append(nim)
        p_conv.append(nbuf)
        hs, sre, sim, sbuf = mixer_block(hs, state_ssm_re[l], state_ssm_im[l], cache_conv[l], *mix_w)
        hs = hier_moe(hs, *moe_w)
        s_re_out.append(sre)
        s_im_out.append(sim)
        s_conv_out.append(sbuf)
    y_prompt = rmsnorm(hp, norm_final_g)
    y_sample = rmsnorm(hs, norm_final_g)
    return (y_prompt, y_sample, jnp.stack(p_re), jnp.stack(p_im), jnp.stack(p_conv),
            jnp.stack(s_re_out), jnp.stack(s_im_out), jnp.stack(s_conv_out))
```

```python
import functools

import numpy as np
import jax
import jax.numpy as jnp
from jax import lax
from jax.experimental import pallas as pl
from jax.experimental.pallas import tpu as pltpu

F32 = jnp.float32
BF16 = jnp.bfloat16
I32 = jnp.int32

SUBLANES = 8
LANES = 128

D_MODEL = 2048
D_SSM = 1024
D_CONV = 1024
SSM_GROUP = 16
N_SSM_GROUPS = 64
SSM_STATE = 64
CONV_WIDTH = 31
CONV_BUF = CONV_WIDTH - 1
N_EXP_GROUPS = 4
EXP_PER_GROUP = 8
N_EXPERTS = 32
D_EXPERT = 256
EPS = 1e-6
IN_COLS = D_SSM + 2 * D_CONV + 2 * D_MODEL

N_PROMPT_SEQ = 4
PROMPT_LEN = 2048
N_SAMPLE_SEQ = 16
SAMPLE_LEN = 32

TM = 512
PROMPT_STEPS = TM // N_PROMPT_SEQ
N_PROMPT_TILES = PROMPT_LEN // PROMPT_STEPS
N_TILES = N_PROMPT_TILES + 1
N_TOK = N_TILES * TM
GROUP_BLOCKS = 4
GROUPS_PER_BLOCK = N_SSM_GROUPS // GROUP_BLOCKS
GB_IN = GROUPS_PER_BLOCK * SSM_GROUP
GB_STATE = GROUPS_PER_BLOCK * SSM_STATE
HIST_P = CONV_BUF * N_PROMPT_SEQ
HIST_S = CONV_BUF * N_SAMPLE_SEQ
ROUTE_LANES = 128
N_PAIRS = 2 * N_TOK
TME = 256
N_PAD = N_PAIRS + N_EXPERTS * TME
N_ETILES = N_PAD // TME
TMD = 256


def _cparams(sem, vmem_mb):
    return pltpu.CompilerParams(dimension_semantics=sem, vmem_limit_bytes=vmem_mb * 1024 * 1024)


def _disc_kernel(lr_ref, li_ref, ldt_ref, br_ref, bi_ref, ar_ref, ai_ref, bbr_ref, bbi_ref):
    lr = lr_ref[...]
    li = li_ref[...]
    dt = jnp.exp(ldt_ref[...])
    mag = jnp.exp(lr * dt)
    ar = mag * jnp.cos(li * dt)
    ai = mag * jnp.sin(li * dt)
    den = lr * lr + li * li
    nr = ar - 1.0
    cr = (nr * lr + ai * li) / den
    ci = (ai * lr - nr * li) / den
    br = br_ref[...]
    bi = bi_ref[...]
    ar_ref[...] = ar
    ai_ref[...] = ai
    bbr_ref[...] = cr * br - ci * bi
    bbi_ref[...] = cr * bi + ci * br


def _discretise(lam_re, lam_im, log_dt, b_re, b_im):
    shp = jax.ShapeDtypeStruct((N_SSM_GROUPS, SSM_STATE * SSM_GROUP), F32)
    rep = lambda a: jnp.repeat(a, SSM_GROUP, axis=-1)
    ldt = jnp.broadcast_to(log_dt[:, None], (N_SSM_GROUPS, SSM_STATE * SSM_GROUP))
    return pl.pallas_call(_disc_kernel, out_shape=(shp, shp, shp, shp), name="s5_discretise")(
        rep(lam_re), rep(lam_im), ldt,
        b_re.reshape(N_SSM_GROUPS, -1), b_im.reshape(N_SSM_GROUPS, -1))


def _block_diag_in(bb):
    b4 = bb.reshape(GROUP_BLOCKS, GROUPS_PER_BLOCK, SSM_STATE, SSM_GROUP).transpose(0, 1, 3, 2)
    eye = jnp.eye(GROUPS_PER_BLOCK, dtype=bool)[None, :, None, :, None]
    full = jnp.where(eye, b4[:, :, :, None, :], 0.0)
    return full.reshape(GROUP_BLOCKS, GB_IN, GB_STATE)


def _block_diag_out(c):
    c4 = c.reshape(GROUP_BLOCKS, GROUPS_PER_BLOCK, SSM_GROUP, SSM_STATE).transpose(0, 1, 3, 2)
    eye = jnp.eye(GROUPS_PER_BLOCK, dtype=bool)[None, :, None, :, None]
    full = jnp.where(eye, c4[:, :, :, None, :], 0.0)
    return full.reshape(GROUP_BLOCKS, GB_STATE, GB_IN)


def _rmsnorm_rows(x, g):
    r = lax.rsqrt(jnp.mean(x * x, axis=-1, keepdims=True) + EPS)
    return x * r * g


def _inproj_kernel(x_ref, g_ref, w_ref, u_ref, v_ref, xn_s, va_s):
    j = pl.program_id(1)

    @pl.when(j == 0)
    def _():
        xn_s[...] = _rmsnorm_rows(x_ref[...], g_ref[...]).astype(BF16)

    p = jnp.dot(xn_s[...], w_ref[...], preferred_element_type=F32)

    @pl.when(j == 0)
    def _():
        u_ref[...] = p

    @pl.when(j == 1)
    def _():
        va_s[...] = p

    @pl.when(j == 2)
    def _():
        v_ref[...] = va_s[...] * jax.nn.sigmoid(p)


def _inproj(x, g, w_in_bf):
    out = jax.ShapeDtypeStruct((N_TOK, D_SSM), F32)
    return pl.pallas_call(
        _inproj_kernel,
        grid=(N_TILES, 3),
        in_specs=[
            pl.BlockSpec((TM, D_MODEL), lambda i, j: (i, 0)),
            pl.BlockSpec((1, D_MODEL), lambda i, j: (0, 0)),
            pl.BlockSpec((D_MODEL, D_SSM), lambda i, j: (0, j)),
        ],
        out_specs=[
            pl.BlockSpec((TM, D_SSM), lambda i, j: (i, 0)),
            pl.BlockSpec((TM, D_CONV), lambda i, j: (i, 0)),
        ],
        out_shape=(out, out),
        scratch_shapes=[pltpu.VMEM((TM, D_MODEL), BF16), pltpu.VMEM((TM, D_CONV), F32)],
        compiler_params=_cparams(("arbitrary", "arbitrary"), 48),
        name="in_proj",
    )(x, g, w_in_bf)


def _scan_kernel(u_ref, bm_ref, cm_ref, ar_ref, ai_ref, dsk_ref, s0r_ref, s0i_ref,
                 act_ref, pfr_ref, pfi_ref, sfr_ref, sfi_ref, xs_ref, car_ref):
    t = pl.program_id(1)
    u = u_ref[...]
    xs_ref[...] = jnp.dot(u.astype(BF16), bm_ref[0], preferred_element_type=F32)
    ar = ar_ref[0]
    ai = ai_ref[0]

    @pl.when(t < N_PROMPT_TILES)
    def _prompt():
        @pl.when(t == 0)
        def _():
            car_ref[...] = jnp.zeros_like(car_ref)

        shape = (SUBLANES, GB_STATE)
        top = lax.broadcasted_iota(I32, shape, 0) < N_PROMPT_SEQ
        arb = jnp.broadcast_to(ar, shape)
        aib = jnp.broadcast_to(ai, shape)
        a2r = arb * arb - aib * aib
        a2i = 2.0 * arb * aib
        a1r = jnp.where(top, 0.0, arb)
        a1i = jnp.where(top, 0.0, aib)
        a2r = jnp.where(top, arb, a2r)
        a2i = jnp.where(top, aib, a2i)

        def body(r, carry):
            hpr, hpi = carry
            off = pl.multiple_of(r * SUBLANES, SUBLANES)
            xr = xs_ref[pl.ds(off, SUBLANES), 0:GB_STATE]
            xi = xs_ref[pl.ds(off, SUBLANES), GB_STATE:2 * GB_STATE]
            sxr = pltpu.roll(xr, N_PROMPT_SEQ, axis=0)
            sxi = pltpu.roll(xi, N_PROMPT_SEQ, axis=0)
            tr = xr + a1r * sxr - a1i * sxi
            ti = xi + a1r * sxi + a1i * sxr
            hr = tr + a2r * hpr - a2i * hpi
            hi = ti + a2r * hpi + a2i * hpr
            xs_ref[pl.ds(off, SUBLANES), 0:GB_STATE] = hr
            xs_ref[pl.ds(off, SUBLANES), GB_STATE:2 * GB_STATE] = hi
            nhr = jnp.where(top, pltpu.roll(hr, N_PROMPT_SEQ, axis=0), hr)
            nhi = jnp.where(top, pltpu.roll(hi, N_PROMPT_SEQ, axis=0), hi)
            return nhr, nhi

        h0 = (car_ref[0:SUBLANES, 0:GB_STATE], car_ref[0:SUBLANES, GB_STATE:2 * GB_STATE])
        hr, hi = lax.fori_loop(0, TM // SUBLANES, body, h0)
        car_ref[0:SUBLANES, 0:GB_STATE] = hr
        car_ref[0:SUBLANES, GB_STATE:2 * GB_STATE] = hi

        @pl.when(t == N_PROMPT_TILES - 1)
        def _():
            pfr_ref[...] = hr[N_PROMPT_SEQ:SUBLANES]
            pfi_ref[...] = hi[N_PROMPT_SEQ:SUBLANES]

    @pl.when(t == N_PROMPT_TILES)
    def _sample():
        shape = (N_SAMPLE_SEQ, GB_STATE)
        arb = jnp.broadcast_to(ar, shape)
        aib = jnp.broadcast_to(ai, shape)

        def body(k, carry):
            hr, hi = carry
            off = pl.multiple_of(k * N_SAMPLE_SEQ, N_SAMPLE_SEQ)
            xr = xs_ref[pl.ds(off, N_SAMPLE_SEQ), 0:GB_STATE]
            xi = xs_ref[pl.ds(off, N_SAMPLE_SEQ), GB_STATE:2 * GB_STATE]
            nhr = xr + arb * hr - aib * hi
            nhi = xi + arb * hi + aib * hr
            xs_ref[pl.ds(off, N_SAMPLE_SEQ), 0:GB_STATE] = nhr
            xs_ref[pl.ds(off, N_SAMPLE_SEQ), GB_STATE:2 * GB_STATE] = nhi
            return nhr, nhi

        hr, hi = lax.fori_loop(0, SAMPLE_LEN, body, (s0r_ref[...], s0i_ref[...]))
        sfr_ref[...] = hr
        sfi_ref[...] = hi

    y = jnp.dot(xs_ref[...].astype(BF16), cm_ref[0], preferred_element_type=F32) + dsk_ref[0] * u
    act_ref[...] = jax.nn.gelu(y).astype(BF16)


def _scan(u, bm, cm, ar, ai, dsk, s0r, s0i):
    gb3 = lambda last: pl.BlockSpec((1,) + last, lambda g, t: (g, 0, 0))
    pst = jax.ShapeDtypeStruct((N_PROMPT_SEQ, N_SSM_GROUPS * SSM_STATE), F32)
    sst = jax.ShapeDtypeStruct((N_SAMPLE_SEQ, N_SSM_GROUPS * SSM_STATE), F32)
    return pl.pallas_call(
        _scan_kernel,
        grid=(GROUP_BLOCKS, N_TILES),
        in_specs=[
            pl.BlockSpec((TM, GB_IN), lambda g, t: (t, g)),
            gb3((GB_IN, 2 * GB_STATE)),
            gb3((2 * GB_STATE, GB_IN)),
            gb3((1, GB_STATE)),
            gb3((1, GB_STATE)),
            gb3((1, GB_IN)),
            pl.BlockSpec((N_SAMPLE_SEQ, GB_STATE), lambda g, t: (0, g)),
            pl.BlockSpec((N_SAMPLE_SEQ, GB_STATE), lambda g, t: (0, g)),
        ],
        out_specs=[
            pl.BlockSpec((TM, GB_IN), lambda g, t: (t, g)),
            pl.BlockSpec((N_PROMPT_SEQ, GB_STATE), lambda g, t: (0, g)),
            pl.BlockSpec((N_PROMPT_SEQ, GB_STATE), lambda g, t: (0, g)),
            pl.BlockSpec((N_SAMPLE_SEQ, GB_STATE), lambda g, t: (0, g)),
            pl.BlockSpec((N_SAMPLE_SEQ, GB_STATE), lambda g, t: (0, g)),
        ],
        out_shape=(jax.ShapeDtypeStruct((N_TOK, D_SSM), BF16), pst, pst, sst, sst),
        scratch_shapes=[pltpu.VMEM((TM, 2 * GB_STATE), F32),
                        pltpu.VMEM((SUBLANES, 2 * GB_STATE), F32)],
        compiler_params=_cparams(("arbitrary", "arbitrary"), 40),
        name="s5_scan",
    )(u, bm, cm, ar, ai, dsk, s0r, s0i)


CONV_ROWS = 64
EXT_ROWS = HIST_S + TM


def _conv_kernel(v_ref, hist_ref, w_ref, b_ref, lg_ref, lb_ref,
                 cv_ref, ncp_ref, ncs_ref, ext_ref, sh_ref, acc_ref):
    t = pl.program_id(0)

    @pl.when(t == 0)
    def _():
        ext_ref[0:HIST_S, :] = jnp.zeros((HIST_S, D_CONV), F32)

    @pl.when(t == N_PROMPT_TILES)
    def _():
        ext_ref[0:HIST_S, :] = hist_ref[...]

    ext_ref[HIST_S:EXT_ROWS, :] = v_ref[...]

    def taps(n_seq):
        half = SUBLANES // 2
        if n_seq % SUBLANES:
            lo = HIST_S - HIST_P - SUBLANES
            sh_ref[lo:EXT_ROWS - SUBLANES, :] = ext_ref[lo + half:EXT_ROWS - half, :]

        def body(rb, _):
            r0 = pl.multiple_of(rb * CONV_ROWS, CONV_ROWS)
            for cb in range(D_CONV // LANES):
                cols = slice(cb * LANES, (cb + 1) * LANES)
                acc = None
                for j in range(CONV_WIDTH):
                    start = HIST_S - (CONV_BUF - j) * n_seq
                    src = ext_ref
                    if start % SUBLANES:
                        src, start = sh_ref, start - half
                    rows = pl.ds(pl.multiple_of(r0 + start, SUBLANES), CONV_ROWS)
                    term = src[rows, cols] * w_ref[j:j + 1, cols]
                    acc = term if acc is None else acc + term
                acc_ref[pl.ds(r0, CONV_ROWS), cols] = acc + b_ref[:, cols]
            return 0

        lax.fori_loop(0, TM // CONV_ROWS, body, 0)

    @pl.when(t < N_PROMPT_TILES)
    def _():
        taps(N_PROMPT_SEQ)

    @pl.when(t == N_PROMPT_TILES)
    def _():
        taps(N_SAMPLE_SEQ)

    y = acc_ref[...]
    mu = jnp.mean(y, axis=-1, keepdims=True)
    yc = y - mu
    r = lax.rsqrt(jnp.mean(yc * yc, axis=-1, keepdims=True) + EPS)
    z = yc * r * lg_ref[...] + lb_ref[...]
    cv_ref[...] = jax.nn.silu(z).astype(BF16)

    @pl.when(t == N_PROMPT_TILES - 1)
    def _():
        ncp_ref[...] = ext_ref[EXT_ROWS - HIST_P:EXT_ROWS, :]

    @pl.when(t == N_PROMPT_TILES)
    def _():
        ncs_ref[...] = ext_ref[EXT_ROWS - HIST_S:EXT_ROWS, :]

    @pl.when(t < N_PROMPT_TILES - 1)
    def _():
        ext_ref[HIST_S - HIST_P:HIST_S, :] = ext_ref[EXT_ROWS - HIST_P:EXT_ROWS, :]


def _conv(v, hist_s, conv_w, conv_b, ln_g, ln_b):
    row = lambda n: pl.BlockSpec((n, D_CONV), lambda t: (0, 0))
    return pl.pallas_call(
        _conv_kernel,
        grid=(N_TILES,),
        in_specs=[pl.BlockSpec((TM, D_CONV), lambda t: (t, 0)), row(HIST_S), row(CONV_WIDTH),
                  row(1), row(1), row(1)],
        out_specs=[pl.BlockSpec((TM, D_CONV), lambda t: (t, 0)), row(HIST_P), row(HIST_S)],
        out_shape=(jax.ShapeDtypeStruct((N_TOK, D_CONV), BF16),
                   jax.ShapeDtypeStruct((HIST_P, D_CONV), F32),
                   jax.ShapeDtypeStruct((HIST_S, D_CONV), F32)),
        scratch_shapes=[pltpu.VMEM((EXT_ROWS, D_CONV), F32), pltpu.VMEM((EXT_ROWS, D_CONV), F32),
                        pltpu.VMEM((TM, D_CONV), F32)],
        compiler_params=_cparams(("arbitrary",), 40),
        name="conv_branch",
    )(v, hist_s, conv_w, conv_b, ln_g, ln_b)


MERGE_COLS = 512
N_MERGE = D_MODEL // MERGE_COLS


def _merge_kernel(x_ref, g_ref, act_ref, cv_ref, wga_ref, wgb_ref, wg1_ref, wg2_ref, wco_ref,
                  m_ref, xn_s):
    j = pl.program_id(1)

    @pl.when(j == 0)
    def _():
        xn_s[...] = _rmsnorm_rows(x_ref[...], g_ref[...]).astype(BF16)

    xn = xn_s[...]
    act = act_ref[...]
    dot = functools.partial(jnp.dot, preferred_element_type=F32)
    ya = dot(act, wg1_ref[...]) * jax.nn.sigmoid(dot(act, wg2_ref[...]))
    yb = dot(cv_ref[...], wco_ref[...])
    m = jax.nn.sigmoid(dot(xn, wga_ref[...])) * ya + jax.nn.sigmoid(dot(xn, wgb_ref[...])) * yb
    m_ref[...] = m.astype(BF16)


def _merge(x, g, act, cv, w_in_bf, w_glu_bf, w_co_bf):
    ga0 = (D_SSM + 2 * D_CONV) // MERGE_COLS
    gb0 = ga0 + N_MERGE
    tile = lambda n: pl.BlockSpec((TM, n), lambda i, j: (i, 0))
    return pl.pallas_call(
        _merge_kernel,
        grid=(N_TILES, N_MERGE),
        in_specs=[
            tile(D_MODEL),
            pl.BlockSpec((1, D_MODEL), lambda i, j: (0, 0)),
            tile(D_SSM),
            tile(D_CONV),
            pl.BlockSpec((D_MODEL, MERGE_COLS), lambda i, j: (0, ga0 + j)),
            pl.BlockSpec((D_MODEL, MERGE_COLS), lambda i, j: (0, gb0 + j)),
            pl.BlockSpec((D_SSM, MERGE_COLS), lambda i, j: (0, j)),
            pl.BlockSpec((D_SSM, MERGE_COLS), lambda i, j: (0, N_MERGE + j)),
            pl.BlockSpec((D_CONV, MERGE_COLS), lambda i, j: (0, j)),
        ],
        out_specs=pl.BlockSpec((TM, MERGE_COLS), lambda i, j: (i, j)),
        out_shape=jax.ShapeDtypeStruct((N_TOK, D_MODEL), BF16),
        scratch_shapes=[pltpu.VMEM((TM, D_MODEL), BF16)],
        compiler_params=_cparams(("arbitrary", "arbitrary"), 48),
        name="gated_merge",
    )(x, g, act, cv, w_in_bf, w_in_bf, w_glu_bf, w_glu_bf, w_co_bf)


def _outproj_kernel(m_ref, x_ref, wo_ref, g_ref, wr_ref, br_ref, h_ref, xn_ref, eid_ref, gw_ref):
    h = x_ref[...] + jnp.dot(m_ref[...], wo_ref[...], preferred_element_type=F32)
    h_ref[...] = h
    xn = _rmsnorm_rows(h, g_ref[...])
    xn_ref[...] = xn
    logits = jnp.dot(xn, wr_ref[...], preferred_element_type=F32,
                     precision=lax.Precision.HIGHEST) + br_ref[...]
    lane = lax.broadcasted_iota(I32, logits.shape, 1)
    neg = -jnp.inf
    first = lambda hit: jnp.min(jnp.where(hit, lane, ROUTE_LANES), axis=-1, keepdims=True)
    gmask = lane < N_EXP_GROUPS
    lg = jnp.where(gmask, logits, neg)
    gmax = jnp.max(lg, axis=-1, keepdims=True)
    gsel = first(lg == gmax)
    psum = jnp.sum(jnp.where(gmask, jnp.exp(logits - gmax), 0.0), axis=-1, keepdims=True)
    pg_sel = 1.0 / psum
    e_lane = lane - N_EXP_GROUPS
    emask = (e_lane >= 0) & (e_lane < N_EXPERTS) & ((e_lane // EXP_PER_GROUP) == gsel)
    le = jnp.where(emask, logits, neg)
    v1 = jnp.max(le, axis=-1, keepdims=True)
    i1 = first(le == v1)
    le2 = jnp.where(lane == i1, neg, le)
    v2 = jnp.max(le2, axis=-1, keepdims=True)
    i2 = first(le2 == v2)
    z = jnp.exp(v2 - v1)
    w1 = pg_sel / (1.0 + z)
    w2 = pg_sel * z / (1.0 + z)
    eid_ref[...] = jnp.where(lane == 0, i1 - N_EXP_GROUPS, jnp.where(lane == 1, i2 - N_EXP_GROUPS, 0))
    gw_ref[...] = jnp.where(lane == 0, w1, jnp.where(lane == 1, w2, 0.0))


def _outproj(m, x, w_out_bf, g, w_router, b_router):
    tile = lambda n: pl.BlockSpec((TMD, n), lambda i: (i, 0))
    full = lambda r, c: pl.BlockSpec((r, c), lambda i: (0, 0))
    return pl.pallas_call(
        _outproj_kernel,
        grid=(N_TOK // TMD,),
        in_specs=[tile(D_MODEL), tile(D_MODEL), full(D_MODEL, D_MODEL), full(1, D_MODEL),
                  full(D_MODEL, ROUTE_LANES), full(1, ROUTE_LANES)],
        out_specs=[tile(D_MODEL), tile(D_MODEL), tile(ROUTE_LANES), tile(ROUTE_LANES)],
        out_shape=(jax.ShapeDtypeStruct((N_TOK, D_MODEL), F32),
                   jax.ShapeDtypeStruct((N_TOK, D_MODEL), F32),
                   jax.ShapeDtypeStruct((N_TOK, ROUTE_LANES), I32),
                   jax.ShapeDtypeStruct((N_TOK, ROUTE_LANES), F32)),
        compiler_params=_cparams(("arbitrary",), 48),
        name="out_proj_router",
    )(m, x, w_out_bf, g, w_router, b_router)


def _expert_kernel(te_ref, tok_ref, nused_ref, xn_hbm, rw_ref, wg_ref, wu_ref, wd_ref,
                   o_ref, xbuf, wg_s, wu_s, wd_s, sem):
    i = pl.program_id(0)
    nused = nused_ref[0]
    slot = i % 2

    def row_copy(tile, r, s):
        tok = tok_ref[tile * TME + r]
        return pltpu.make_async_copy(xn_hbm.at[pl.ds(tok, 1)], xbuf.at[s, pl.ds(r, 1)], sem.at[s])

    def issue(tile, s):
        def body(r, _):
            row_copy(tile, r, s).start()
            return 0
        lax.fori_loop(0, TME, body, 0)

    @pl.when(i == 0)
    def _():
        issue(0, 0)

    @pl.when(i + 1 < nused)
    def _():
        issue(i + 1, 1 - slot)

    @pl.when(i < nused)
    def _():
        changed = jnp.logical_or(i == 0, te_ref[i] != te_ref[jnp.maximum(i - 1, 0)])

        @pl.when(changed)
        def _():
            wg_s[...] = wg_ref[0].astype(BF16)
            wu_s[...] = wu_ref[0].astype(BF16)
            wd_s[...] = wd_ref[0].astype(BF16)

        pltpu.make_async_copy(xn_hbm.at[pl.ds(0, TME)], xbuf.at[slot], sem.at[slot]).wait()
        xb = xbuf[slot].astype(BF16)
        hg = jnp.dot(xb, wg_s[...], preferred_element_type=F32)
        hu = jnp.dot(xb, wu_s[...], preferred_element_type=F32)
        hid = jax.nn.silu(hg) * hu * rw_ref[...]
        o_ref[...] = jnp.dot(hid.astype(BF16), wd_s[...], preferred_element_type=F32)

    @pl.when(i >= nused)
    def _():
        o_ref[...] = jnp.zeros_like(o_ref)


def _experts(tile_expert, row_token, n_used, xn, row_w, wg, wu, wd):
    last = lambda i, te, tok, nu: jnp.minimum(i, nu[0] - 1)
    grid_spec = pltpu.PrefetchScalarGridSpec(
        num_scalar_prefetch=3,
        grid=(N_ETILES,),
        in_specs=[
            pl.BlockSpec(memory_space=pl.ANY),
            pl.BlockSpec((TME, 1), lambda i, te, tok, nu: (last(i, te, tok, nu), 0)),
            pl.BlockSpec((1, D_MODEL, D_EXPERT), lambda i, te, tok, nu: (te[i], 0, 0)),
            pl.BlockSpec((1, D_MODEL, D_EXPERT), lambda i, te, tok, nu: (te[i], 0, 0)),
            pl.BlockSpec((1, D_EXPERT, D_MODEL), lambda i, te, tok, nu: (te[i], 0, 0)),
        ],
        out_specs=pl.BlockSpec((TME, D_MODEL), lambda i, te, tok, nu: (i, 0)),
        scratch_shapes=[
            pltpu.VMEM((2, TME, D_MODEL), F32),
            pltpu.VMEM((D_MODEL, D_EXPERT), BF16),
            pltpu.VMEM((D_MODEL, D_EXPERT), BF16),
            pltpu.VMEM((D_EXPERT, D_MODEL), BF16),
            pltpu.SemaphoreType.DMA((2,)),
        ],
    )
    return pl.pallas_call(
        _expert_kernel,
        grid_spec=grid_spec,
        out_shape=jax.ShapeDtypeStruct((N_PAD, D_MODEL), F32),
        compiler_params=_cparams(("arbitrary",), 40),
        name="routed_experts",
    )(tile_expert, row_token, n_used, xn, row_w, wg, wu, wd)


def _combine_kernel(pos_ref, h_ref, o_hbm, g_ref, y_ref, buf, sem):
    i = pl.program_id(0)

    def row_copy(r, s):
        p = pos_ref[(i * TM + r) * 2 + s]
        return pltpu.make_async_copy(o_hbm.at[pl.ds(p, 1)], buf.at[s, pl.ds(r, 1)], sem.at[s])

    def body(r, _):
        row_copy(r, 0).start()
        row_copy(r, 1).start()
        return 0

    lax.fori_loop(0, TM, body, 0)
    for s in range(2):
        pltpu.make_async_copy(o_hbm.at[pl.ds(0, TM)], buf.at[s], sem.at[s]).wait()
    y = h_ref[...] + (buf[0] + buf[1])
    y_ref[...] = _rmsnorm_rows(y, g_ref[...])


def _combine(pos, h, o, g):
    grid_spec = pltpu.PrefetchScalarGridSpec(
        num_scalar_prefetch=1,
        grid=(N_TILES,),
        in_specs=[
            pl.BlockSpec((TM, D_MODEL), lambda i, pos: (i, 0)),
            pl.BlockSpec(memory_space=pl.ANY),
            pl.BlockSpec((1, D_MODEL), lambda i, pos: (0, 0)),
        ],
        out_specs=pl.BlockSpec((TM, D_MODEL), lambda i, pos: (i, 0)),
        scratch_shapes=[pltpu.VMEM((2, TM, D_MODEL), F32), pltpu.SemaphoreType.DMA((2,))],
    )
    return pl.pallas_call(
        _combine_kernel,
        grid_spec=grid_spec,
        out_shape=jax.ShapeDtypeStruct((N_TOK, D_MODEL), F32),
        compiler_params=_cparams(("arbitrary",), 40),
        name="combine_norm",
    )(pos, h, o, g)


def _dispatch_plan(eid, gw):
    e_flat = eid[:, :2].reshape(-1)
    w_flat = gw[:, :2].reshape(-1)
    onehot = (e_flat[:, None] == jnp.arange(N_EXPERTS, dtype=I32)[None, :]).astype(I32)
    csum = jnp.cumsum(onehot, axis=0)
    counts = csum[-1]
    rank = jnp.take_along_axis(csum, e_flat[:, None], axis=1)[:, 0] - 1
    padded = ((counts + TME - 1) // TME) * TME
    ends = jnp.cumsum(padded)
    starts = ends - padded
    pos = starts[e_flat] + rank
    row_token = jnp.zeros((N_PAD,), I32).at[pos].set(jnp.arange(N_PAIRS, dtype=I32) // 2)
    row_w = jnp.zeros((N_PAD,), F32).at[pos].set(w_flat)
    n_used = (ends[-1] // TME).astype(I32)
    tile_start = jnp.arange(N_ETILES, dtype=I32) * TME
    te = jnp.sum((ends[None, :] <= tile_start[:, None]).astype(I32), axis=1)
    te = jnp.minimum(te, N_EXPERTS - 1)
    te_last = te[n_used - 1]
    te = jnp.where(jnp.arange(N_ETILES) < n_used, te, te_last).astype(I32)
    return te, row_token, n_used.reshape(1), row_w.reshape(N_PAD, 1), pos.astype(I32)


def _to_rows(x_prompt, x_sample):
    xp = x_prompt.reshape(N_PROMPT_SEQ, N_PROMPT_TILES, PROMPT_STEPS, D_MODEL)
    xp = xp.transpose(1, 2, 0, 3).reshape(N_PROMPT_TILES * TM, D_MODEL)
    xs = x_sample.transpose(1, 0, 2).reshape(TM, D_MODEL)
    return jnp.concatenate([xp, xs], axis=0)


def _from_rows(y):
    yp = y[:N_PROMPT_TILES * TM].reshape(N_PROMPT_TILES, PROMPT_STEPS, N_PROMPT_SEQ, D_MODEL)
    yp = yp.transpose(2, 0, 1, 3).reshape(N_PROMPT_SEQ, PROMPT_LEN, D_MODEL)
    ys = y[N_PROMPT_TILES * TM:].reshape(SAMPLE_LEN, N_SAMPLE_SEQ, D_MODEL).transpose(1, 0, 2)
    return yp, ys


def kernel(x_prompt, x_sample, state_ssm_re, state_ssm_im, cache_conv, norm_mix_g, w_in, lam_re, lam_im, log_dt, b_re, b_im, c_re, c_im, d_skip, w_ssm_glu, conv_w, conv_b, conv_ln_g, conv_ln_b, w_conv_out, w_out, norm_ffn_g, w_router_group, b_router_group, w_router_expert, b_router_expert, w_exp_gate, w_exp_up, w_exp_down, norm_final_g):
    assert w_in.shape[0] == 1, "single-layer trunk"
    x = _to_rows(x_prompt, x_sample)
    w_in_bf = w_in[0].astype(BF16)
    g_mix = norm_mix_g[0].reshape(1, D_MODEL)

    ar_rep, ai_rep, bbr, bbi = _discretise(lam_re[0], lam_im[0], log_dt[0], b_re[0], b_im[0])
    pick = lambda a: a.reshape(N_SSM_GROUPS, SSM_STATE, SSM_GROUP)[:, :, 0].reshape(GROUP_BLOCKS, 1, GB_STATE)
    gph = lambda a: a.reshape(N_SSM_GROUPS, SSM_STATE, SSM_GROUP)
    bm = jnp.concatenate([_block_diag_in(gph(bbr)), _block_diag_in(gph(bbi))], axis=-1).astype(BF16)
    cm = jnp.concatenate([_block_diag_out(c_re[0]), -_block_diag_out(c_im[0])], axis=1).astype(BF16)
    dsk = d_skip[0].reshape(GROUP_BLOCKS, 1, GB_IN)
    s0r = state_ssm_re[0].reshape(N_SAMPLE_SEQ, -1)
    s0i = state_ssm_im[0].reshape(N_SAMPLE_SEQ, -1)

    u, v = _inproj(x, g_mix, w_in_bf)
    act, pfr, pfi, sfr, sfi = _scan(u, bm, cm, pick(ar_rep), pick(ai_rep), dsk, s0r, s0i)

    hist_s = cache_conv[0].transpose(1, 0, 2).reshape(HIST_S, D_CONV)
    row = lambda a: a.reshape(1, -1)
    cv, ncp, ncs = _conv(v, hist_s, conv_w[0], row(conv_b[0]), row(conv_ln_g[0]), row(conv_ln_b[0]))

    m = _merge(x, g_mix, act, cv, w_in_bf, w_ssm_glu[0].astype(BF16), w_conv_out[0].astype(BF16))

    w_router = jnp.zeros((D_MODEL, ROUTE_LANES), F32)
    w_router = w_router.at[:, :N_EXP_GROUPS].set(w_router_group[0])
    w_router = w_router.at[:, N_EXP_GROUPS:N_EXP_GROUPS + N_EXPERTS].set(w_router_expert[0])
    b_router = jnp.zeros((1, ROUTE_LANES), F32)
    b_router = b_router.at[0, :N_EXP_GROUPS].set(b_router_group[0])
    b_router = b_router.at[0, N_EXP_GROUPS:N_EXP_GROUPS + N_EXPERTS].set(b_router_expert[0])
    h, xn2, eid, gw = _outproj(m, x, w_out[0].astype(BF16), row(norm_ffn_g[0]), w_router, b_router)

    te, row_token, n_used, row_w, pos = _dispatch_plan(eid, gw)
    o = _experts(te, row_token, n_used, xn2, row_w, w_exp_gate[0], w_exp_up[0], w_exp_down[0])
    y = _combine(pos, h, o, row(norm_final_g))

    y_prompt, y_sample = _from_rows(y)
    st = lambda a, n: a.reshape(1, n, N_SSM_GROUPS, SSM_STATE)
    ncp = ncp.reshape(CONV_BUF, N_PROMPT_SEQ, D_CONV).transpose(1, 0, 2)[None]
    ncs = ncs.reshape(CONV_BUF, N_SAMPLE_SEQ, D_CONV).transpose(1, 0, 2)[None]
    return (y_prompt, y_sample, st(pfr, N_PROMPT_SEQ), st(pfi, N_PROMPT_SEQ), ncp,
            st(sfr, N_SAMPLE_SEQ), st(sfi, N_SAMPLE_SEQ), ncs)
```

```python
import functools

import jax
import jax.numpy as jnp
from jax import lax
from jax.experimental import pallas as pl
from jax.experimental.pallas import tpu as pltpu

F32 = jnp.float32
BF16 = jnp.bfloat16
I32 = jnp.int32

SUBLANES = 8
LANES = 128

D_MODEL = 2048
D_SSM = 1024
D_CONV = 1024
SSM_GROUP = 16
N_SSM_GROUPS = 64
SSM_STATE = 64
CONV_WIDTH = 31
CONV_BUF = CONV_WIDTH - 1
N_EXP_GROUPS = 4
EXP_PER_GROUP = 8
N_EXPERTS = 32
D_EXPERT = 256
EPS = 1e-6

N_PROMPT_SEQ = 4
PROMPT_LEN = 2048
N_SAMPLE_SEQ = 16
SAMPLE_LEN = 32
N_PROMPT_TOK = N_PROMPT_SEQ * PROMPT_LEN
N_SAMPLE_TOK = N_SAMPLE_SEQ * SAMPLE_LEN
N_TOK = N_PROMPT_TOK + N_SAMPLE_TOK

TM = 512
N_PROMPT_TILES = N_PROMPT_TOK // TM
N_TILES = N_TOK // TM
PROMPT_STEPS = TM // N_PROMPT_SEQ
GROUP_BLOCKS = 4
GROUPS_PER_BLOCK = N_SSM_GROUPS // GROUP_BLOCKS
GB_IN = GROUPS_PER_BLOCK * SSM_GROUP
GB_STATE = GROUPS_PER_BLOCK * SSM_STATE
GB_LB = GB_STATE // LANES
HIST_P = CONV_BUF * N_PROMPT_SEQ
HIST_S = CONV_BUF * N_SAMPLE_SEQ
ROUTE_LANES = 128
N_PAIRS = 2 * N_TOK
TME = 256
N_PAD = N_PAIRS + N_EXPERTS * TME
N_ETILES = N_PAD // TME
TMD = 256
N_DTILES = N_TOK // TMD


def _cparams(sem, vmem_mb):
    return pltpu.CompilerParams(dimension_semantics=sem, vmem_limit_bytes=vmem_mb * 1024 * 1024)


def _pair_specs(rows, cols, n_prompt_tiles):
    p = pl.BlockSpec((rows, cols), lambda i, *_: (jnp.minimum(i, n_prompt_tiles - 1), 0))
    s = pl.BlockSpec((rows, cols), lambda i, *_: (jnp.maximum(i - n_prompt_tiles, 0), 0))
    return p, s


def _on_pair(i, n_prompt_tiles, p_ref, s_ref, fn):
    @pl.when(i < n_prompt_tiles)
    def _():
        fn(p_ref)

    @pl.when(i >= n_prompt_tiles)
    def _():
        fn(s_ref)


def _disc_kernel(lr_ref, li_ref, ldt_ref, br_ref, bi_ref, ar_ref, ai_ref, bbr_ref, bbi_ref):
    lr = lr_ref[...]
    li = li_ref[...]
    dt = jnp.exp(ldt_ref[...])
    mag = jnp.exp(lr * dt)
    ar = mag * jnp.cos(li * dt)
    ai = mag * jnp.sin(li * dt)
    den = lr * lr + li * li
    nr = ar - 1.0
    cr = (nr * lr + ai * li) / den
    ci = (ai * lr - nr * li) / den
    br = br_ref[...]
    bi = bi_ref[...]
    ar_ref[...] = ar
    ai_ref[...] = ai
    bbr_ref[...] = cr * br - ci * bi
    bbi_ref[...] = cr * bi + ci * br


def _discretise(lam_re, lam_im, log_dt, b_re, b_im):
    shp = jax.ShapeDtypeStruct((N_SSM_GROUPS, SSM_STATE * SSM_GROUP), F32)
    rep = lambda a: jnp.repeat(a, SSM_GROUP, axis=-1)
    ldt = jnp.broadcast_to(log_dt[:, None], (N_SSM_GROUPS, SSM_STATE * SSM_GROUP))
    return pl.pallas_call(_disc_kernel, out_shape=(shp, shp, shp, shp), name="s5_discretise")(
        rep(lam_re), rep(lam_im), ldt,
        b_re.reshape(N_SSM_GROUPS, -1), b_im.reshape(N_SSM_GROUPS, -1))


def _block_diag_in(bb):
    b4 = bb.reshape(GROUP_BLOCKS, GROUPS_PER_BLOCK, SSM_STATE, SSM_GROUP).transpose(0, 1, 3, 2)
    eye = jnp.eye(GROUPS_PER_BLOCK, dtype=bool)[None, :, None, :, None]
    full = jnp.where(eye, b4[:, :, :, None, :], 0.0)
    return full.reshape(GROUP_BLOCKS, GB_IN, GB_STATE)


def _block_diag_out(c):
    c4 = c.reshape(GROUP_BLOCKS, GROUPS_PER_BLOCK, SSM_GROUP, SSM_STATE).transpose(0, 1, 3, 2)
    eye = jnp.eye(GROUPS_PER_BLOCK, dtype=bool)[None, :, None, :, None]
    full = jnp.where(eye, c4[:, :, :, None, :], 0.0)
    return full.reshape(GROUP_BLOCKS, GB_STATE, GB_IN)


def _rmsnorm_rows(x, g):
    r = lax.rsqrt(jnp.mean(x * x, axis=-1, keepdims=True) + EPS)
    return x * r * g


def _inproj_kernel(xp_ref, xs_ref, g_ref, w_ref, u_ref, v_ref, xn_s, va_s):
    i = pl.program_id(0)
    j = pl.program_id(1)

    @pl.when(j == 0)
    def _():
        def norm(x_ref):
            xn_s[...] = _rmsnorm_rows(x_ref[...], g_ref[...]).astype(BF16)
        _on_pair(i, N_PROMPT_TILES, xp_ref, xs_ref, norm)

    p = jnp.dot(xn_s[...], w_ref[...], preferred_element_type=F32)

    @pl.when(j == 0)
    def _():
        u_ref[...] = p

    @pl.when(j == 1)
    def _():
        va_s[...] = p

    @pl.when(j == 2)
    def _():
        v_ref[...] = va_s[...] * jax.nn.sigmoid(p)


def _inproj(xp, xs, g, w_in_bf):
    out = jax.ShapeDtypeStruct((N_TOK, D_SSM), F32)
    xps, xss = _pair_specs(TM, D_MODEL, N_PROMPT_TILES)
    return pl.pallas_call(
        _inproj_kernel,
        grid=(N_TILES, 3),
        in_specs=[
            xps, xss,
            pl.BlockSpec((1, D_MODEL), lambda i, j: (0, 0)),
            pl.BlockSpec((D_MODEL, D_SSM), lambda i, j: (0, j)),
        ],
        out_specs=[
            pl.BlockSpec((TM, D_SSM), lambda i, j: (i, 0)),
            pl.BlockSpec((TM, D_CONV), lambda i, j: (i, 0)),
        ],
        out_shape=(out, out),
        scratch_shapes=[pltpu.VMEM((TM, D_MODEL), BF16), pltpu.VMEM((TM, D_CONV), F32)],
        compiler_params=_cparams(("arbitrary", "arbitrary"), 48),
        name="in_proj",
    )(xp, xs, g, w_in_bf)


def _seq_tile_specs(cols, col_index):
    blocks_per_seq = PROMPT_LEN // PROMPT_STEPS

    def prompt(b):
        return pl.BlockSpec(
            (PROMPT_STEPS, cols),
            lambda *ids: (b * blocks_per_seq + jnp.minimum(ids[-1], N_PROMPT_TILES - 1), col_index(*ids)))

    sample = pl.BlockSpec((TM, cols), lambda *ids: (N_PROMPT_TOK // TM, col_index(*ids)))
    return [prompt(b) for b in range(N_PROMPT_SEQ)] + [sample]


def _scan_kernel(u0_ref, u1_ref, u2_ref, u3_ref, us_ref, bm_ref, cm_ref, ar_ref, ai_ref, dsk_ref,
                 s0r_ref, s0i_ref, actp_ref, acts_ref, pfr_ref, pfi_ref, sfr_ref, sfi_ref,
                 xs_ref, car_ref):
    t = pl.program_id(1)
    half_cols = GB_LB // 2 * LANES

    def project_in(u, n_seq, steps):
        x = jnp.dot(u.astype(BF16), bm_ref[0], preferred_element_type=F32)
        for c in range(2 * GB_LB):
            for b in range(n_seq):
                xs_ref[c, pl.ds(b, steps, stride=n_seq), :] = (
                    x[b * steps:(b + 1) * steps, c * LANES:(c + 1) * LANES])

    def project_out(u, n_seq, steps):
        rows = []
        for b in range(n_seq):
            rows.append(jnp.concatenate(
                [xs_ref[c, pl.ds(b, steps, stride=n_seq), :] for c in range(2 * GB_LB)], axis=1))
        h = jnp.concatenate(rows, axis=0).astype(BF16)
        y = jnp.dot(h, cm_ref[0], preferred_element_type=F32) + dsk_ref[0] * u
        return jax.nn.gelu(y).astype(BF16)

    def load_half(rows, half, imag):
        c0 = imag * GB_LB + half * (GB_LB // 2)
        return jnp.concatenate([xs_ref[c0 + c, rows, :] for c in range(GB_LB // 2)], axis=1)

    def store_half(rows, half, imag, val):
        c0 = imag * GB_LB + half * (GB_LB // 2)
        for c in range(GB_LB // 2):
            xs_ref[c0 + c, rows, :] = val[:, c * LANES:(c + 1) * LANES]

    @pl.when(t < N_PROMPT_TILES)
    def _prompt():
        u = jnp.concatenate([u0_ref[...], u1_ref[...], u2_ref[...], u3_ref[...]], axis=0)
        project_in(u, N_PROMPT_SEQ, PROMPT_STEPS)

        @pl.when(t == 0)
        def _():
            car_ref[...] = jnp.zeros_like(car_ref)

        shape = (SUBLANES, half_cols)
        top = lax.broadcasted_iota(I32, shape, 0) < N_PROMPT_SEQ
        for half in range(2):
            cols = slice(half * half_cols, (half + 1) * half_cols)
            icols = slice(GB_STATE + half * half_cols, GB_STATE + (half + 1) * half_cols)
            arb = jnp.broadcast_to(ar_ref[0][:, cols], shape)
            aib = jnp.broadcast_to(ai_ref[0][:, cols], shape)
            a1r = jnp.where(top, 0.0, arb)
            a1i = jnp.where(top, 0.0, aib)
            a2r = jnp.where(top, arb, arb * arb - aib * aib)
            a2i = jnp.where(top, aib, 2.0 * arb * aib)

            def body(r, carry):
                hpr, hpi = carry
                rows = pl.ds(pl.multiple_of(r * SUBLANES, SUBLANES), SUBLANES)
                xr = load_half(rows, half, 0)
                xi = load_half(rows, half, 1)
                sxr = pltpu.roll(xr, N_PROMPT_SEQ, axis=0)
                sxi = pltpu.roll(xi, N_PROMPT_SEQ, axis=0)
                tr = xr + a1r * sxr - a1i * sxi
                ti = xi + a1r * sxi + a1i * sxr
                hr = tr + a2r * hpr - a2i * hpi
                hi = ti + a2r * hpi + a2i * hpr
                store_half(rows, half, 0, hr)
                store_half(rows, half, 1, hi)
                nhr = jnp.where(top, pltpu.roll(hr, N_PROMPT_SEQ, axis=0), hr)
                nhi = jnp.where(top, pltpu.roll(hi, N_PROMPT_SEQ, axis=0), hi)
                return nhr, nhi

            hr, hi = lax.fori_loop(0, TM // SUBLANES, body, (car_ref[:, cols], car_ref[:, icols]))
            car_ref[:, cols] = hr
            car_ref[:, icols] = hi

            @pl.when(t == N_PROMPT_TILES - 1)
            def _():
                pfr_ref[:, cols] = hr[N_PROMPT_SEQ:SUBLANES]
                pfi_ref[:, cols] = hi[N_PROMPT_SEQ:SUBLANES]

        y = project_out(u, N_PROMPT_SEQ, PROMPT_STEPS)
        for b in range(N_PROMPT_SEQ):
            actp_ref[b] = y[b * PROMPT_STEPS:(b + 1) * PROMPT_STEPS]

    @pl.when(t == N_PROMPT_TILES)
    def _sample():
        u = us_ref[...]
        project_in(u, N_SAMPLE_SEQ, SAMPLE_LEN)
        shape = (N_SAMPLE_SEQ, half_cols)
        for half in range(2):
            cols = slice(half * half_cols, (half + 1) * half_cols)
            arb = jnp.broadcast_to(ar_ref[0][:, cols], shape)
            aib = jnp.broadcast_to(ai_ref[0][:, cols], shape)

            def body(k, carry):
                hr, hi = carry
                rows = pl.ds(pl.multiple_of(k * N_SAMPLE_SEQ, N_SAMPLE_SEQ), N_SAMPLE_SEQ)
                xr = load_half(rows, half, 0)
                xi = load_half(rows, half, 1)
                nhr = xr + arb * hr - aib * hi
                nhi = xi + arb * hi + aib * hr
                store_half(rows, half, 0, nhr)
                store_half(rows, half, 1, nhi)
                return nhr, nhi

            hr, hi = lax.fori_loop(0, SAMPLE_LEN, body, (s0r_ref[:, cols], s0i_ref[:, cols]))
            sfr_ref[:, cols] = hr
            sfi_ref[:, cols] = hi

        acts_ref[...] = project_out(u, N_SAMPLE_SEQ, SAMPLE_LEN)


def _scan(u, bm, cm, ar, ai, dsk, s0r, s0i):
    gb3 = lambda last: pl.BlockSpec((1,) + last, lambda g, t: (g, 0, 0))
    state = lambda n: pl.BlockSpec((n, GB_STATE), lambda g, t: (0, g))
    pst = jax.ShapeDtypeStruct((N_PROMPT_SEQ, N_SSM_GROUPS * SSM_STATE), F32)
    sst = jax.ShapeDtypeStruct((N_SAMPLE_SEQ, N_SSM_GROUPS * SSM_STATE), F32)
    return pl.pallas_call(
        _scan_kernel,
        grid=(GROUP_BLOCKS, N_TILES),
        in_specs=_seq_tile_specs(GB_IN, lambda g, t: g) + [
            gb3((GB_IN, 2 * GB_STATE)),
            gb3((2 * GB_STATE, GB_IN)),
            gb3((1, GB_STATE)),
            gb3((1, GB_STATE)),
            gb3((1, GB_IN)),
            state(N_SAMPLE_SEQ), state(N_SAMPLE_SEQ),
        ],
        out_specs=[
            pl.BlockSpec((N_PROMPT_SEQ, PROMPT_STEPS, GB_IN),
                         lambda g, t: (0, jnp.minimum(t, N_PROMPT_TILES - 1), g)),
            pl.BlockSpec((TM, GB_IN), lambda g, t: (0, g)),
            state(N_PROMPT_SEQ), state(N_PROMPT_SEQ), state(N_SAMPLE_SEQ), state(N_SAMPLE_SEQ),
        ],
        out_shape=(jax.ShapeDtypeStruct((N_PROMPT_SEQ, PROMPT_LEN, D_SSM), BF16),
                   jax.ShapeDtypeStruct((N_SAMPLE_TOK, D_SSM), BF16), pst, pst, sst, sst),
        scratch_shapes=[pltpu.VMEM((2 * GB_LB, TM, LANES), F32),
                        pltpu.VMEM((SUBLANES, 2 * GB_STATE), F32)],
        compiler_params=_cparams(("arbitrary", "arbitrary"), 40),
        name="s5_scan",
    )(u, u, u, u, u, bm, cm, ar, ai, dsk, s0r, s0i)


CONV_ROWS = 64
EXT_ROWS = HIST_S + TM
CONV_LB = D_CONV // LANES


def _conv_kernel(v0_ref, v1_ref, v2_ref, v3_ref, vs_ref, hist_ref, w_ref, b_ref, lg_ref, lb_ref,
                 cvp_ref, cvs_ref, ncp_ref, ncs_ref, ext_ref, sh_ref, acc_ref):
    t = pl.program_id(0)

    def load_tile(v, n_seq, steps):
        for c in range(CONV_LB):
            for b in range(n_seq):
                ext_ref[c, pl.ds(HIST_S + b, steps, stride=n_seq), :] = (
                    v[b * steps:(b + 1) * steps, c * LANES:(c + 1) * LANES])

    def taps(n_seq):
        half = SUBLANES // 2
        if n_seq % SUBLANES:
            lo = HIST_S - HIST_P - SUBLANES
            sh_ref[:, lo:EXT_ROWS - SUBLANES, :] = ext_ref[:, lo + half:EXT_ROWS - half, :]

        def body(rb, _):
            r0 = pl.multiple_of(rb * CONV_ROWS, CONV_ROWS)
            for cb in range(CONV_LB):
                cols = slice(cb * LANES, (cb + 1) * LANES)
                acc = None
                for j in range(CONV_WIDTH):
                    start = HIST_S - (CONV_BUF - j) * n_seq
                    src = ext_ref
                    if start % SUBLANES:
                        src, start = sh_ref, start - half
                    rows = pl.ds(pl.multiple_of(r0 + start, SUBLANES), CONV_ROWS)
                    term = src[cb, rows, :] * w_ref[j:j + 1, cols]
                    acc = term if acc is None else acc + term
                acc_ref[cb, pl.ds(r0, CONV_ROWS), :] = acc + b_ref[:, cols]
            return 0

        lax.fori_loop(0, TM // CONV_ROWS, body, 0)

    def norm_act():
        y = acc_ref[...]
        mu = jnp.sum(jnp.sum(y, axis=0), axis=-1, keepdims=True) / D_CONV
        yc = y - mu[None]
        var = jnp.sum(jnp.sum(yc * yc, axis=0), axis=-1, keepdims=True) / D_CONV
        z = yc * lax.rsqrt(var + EPS)[None] * lg_ref[...] + lb_ref[...]
        acc_ref[...] = jax.nn.silu(z)

    def rows_of_seq(b, n_seq, steps):
        return jnp.concatenate(
            [acc_ref[c, pl.ds(b, steps, stride=n_seq), :] for c in range(CONV_LB)], axis=1).astype(BF16)

    @pl.when(t < N_PROMPT_TILES)
    def _prompt():
        @pl.when(t == 0)
        def _():
            ext_ref[:, 0:HIST_S, :] = jnp.zeros((CONV_LB, HIST_S, LANES), F32)

        v = jnp.concatenate([v0_ref[...], v1_ref[...], v2_ref[...], v3_ref[...]], axis=0)
        load_tile(v, N_PROMPT_SEQ, PROMPT_STEPS)
        taps(N_PROMPT_SEQ)
        norm_act()
        for b in range(N_PROMPT_SEQ):
            cvp_ref[b] = rows_of_seq(b, N_PROMPT_SEQ, PROMPT_STEPS)

        @pl.when(t == N_PROMPT_TILES - 1)
        def _():
            for c in range(CONV_LB):
                ncp_ref[:, c * LANES:(c + 1) * LANES] = ext_ref[c, EXT_ROWS - HIST_P:EXT_ROWS, :]

        @pl.when(t < N_PROMPT_TILES - 1)
        def _():
            ext_ref[:, HIST_S - HIST_P:HIST_S, :] = ext_ref[:, EXT_ROWS - HIST_P:EXT_ROWS, :]

    @pl.when(t == N_PROMPT_TILES)
    def _sample():
        for c in range(CONV_LB):
            ext_ref[c, 0:HIST_S, :] = hist_ref[:, c * LANES:(c + 1) * LANES]
        load_tile(vs_ref[...], N_SAMPLE_SEQ, SAMPLE_LEN)
        taps(N_SAMPLE_SEQ)
        norm_act()
        cvs_ref[...] = jnp.concatenate(
            [rows_of_seq(b, N_SAMPLE_SEQ, SAMPLE_LEN) for b in range(N_SAMPLE_SEQ)], axis=0)
        for c in range(CONV_LB):
            ncs_ref[:, c * LANES:(c + 1) * LANES] = ext_ref[c, EXT_ROWS - HIST_S:EXT_ROWS, :]


def _conv(v, hist_s, conv_w, conv_b, ln_g, ln_b):
    row = lambda n: pl.BlockSpec((n, D_CONV), lambda t: (0, 0))
    lane3 = pl.BlockSpec((CONV_LB, 1, LANES), lambda t: (0, 0, 0))
    return pl.pallas_call(
        _conv_kernel,
        grid=(N_TILES,),
        in_specs=_seq_tile_specs(D_CONV, lambda t: 0) + [row(HIST_S), row(CONV_WIDTH), row(1), lane3, lane3],
        out_specs=[
            pl.BlockSpec((N_PROMPT_SEQ, PROMPT_STEPS, D_CONV),
                         lambda t: (0, jnp.minimum(t, N_PROMPT_TILES - 1), 0)),
            row(TM), row(HIST_P), row(HIST_S)],
        out_shape=(jax.ShapeDtypeStruct((N_PROMPT_SEQ, PROMPT_LEN, D_CONV), BF16),
                   jax.ShapeDtypeStruct((N_SAMPLE_TOK, D_CONV), BF16),
                   jax.ShapeDtypeStruct((HIST_P, D_CONV), F32),
                   jax.ShapeDtypeStruct((HIST_S, D_CONV), F32)),
        scratch_shapes=[pltpu.VMEM((CONV_LB, EXT_ROWS, LANES), F32),
                        pltpu.VMEM((CONV_LB, EXT_ROWS, LANES), F32),
                        pltpu.VMEM((CONV_LB, TM, LANES), F32)],
        compiler_params=_cparams(("arbitrary",), 48),
        name="conv_branch",
    )(v, v, v, v, v, hist_s, conv_w, conv_b,
      ln_g.reshape(CONV_LB, 1, LANES), ln_b.reshape(CONV_LB, 1, LANES))


MERGE_COLS = 512
N_MERGE = D_MODEL // MERGE_COLS


def _merge_kernel(xp_ref, xs_ref, g_ref, actp_ref, acts_ref, cvp_ref, cvs_ref,
                  wga_ref, wgb_ref, wg1_ref, wg2_ref, wco_ref, m_ref, xn_s, act_s, cv_s):
    i = pl.program_id(0)
    j = pl.program_id(1)

    @pl.when(j == 0)
    def _():
        def norm(x_ref):
            xn_s[...] = _rmsnorm_rows(x_ref[...], g_ref[...]).astype(BF16)

        def copy_act(ref):
            act_s[...] = ref[...]

        def copy_cv(ref):
            cv_s[...] = ref[...]

        _on_pair(i, N_PROMPT_TILES, xp_ref, xs_ref, norm)
        _on_pair(i, N_PROMPT_TILES, actp_ref, acts_ref, copy_act)
        _on_pair(i, N_PROMPT_TILES, cvp_ref, cvs_ref, copy_cv)

    xn = xn_s[...]
    act = act_s[...]
    dot = functools.partial(jnp.dot, preferred_element_type=F32)
    ya = dot(act, wg1_ref[...]) * jax.nn.sigmoid(dot(act, wg2_ref[...]))
    yb = dot(cv_s[...], wco_ref[...])
    m = jax.nn.sigmoid(dot(xn, wga_ref[...])) * ya + jax.nn.sigmoid(dot(xn, wgb_ref[...])) * yb
    m_ref[...] = m.astype(BF16)


def _merge(xp, xs, g, actp, acts, cvp, cvs, w_in_bf, w_glu_bf, w_co_bf):
    ga0 = (D_SSM + 2 * D_CONV) // MERGE_COLS
    gb0 = ga0 + N_MERGE
    pair = lambda n: list(_pair_specs(TM, n, N_PROMPT_TILES))
    return pl.pallas_call(
        _merge_kernel,
        grid=(N_TILES, N_MERGE),
        in_specs=pair(D_MODEL) + [pl.BlockSpec((1, D_MODEL), lambda i, j: (0, 0))]
        + pair(D_SSM) + pair(D_CONV) + [
            pl.BlockSpec((D_MODEL, MERGE_COLS), lambda i, j: (0, ga0 + j)),
            pl.BlockSpec((D_MODEL, MERGE_COLS), lambda i, j: (0, gb0 + j)),
            pl.BlockSpec((D_SSM, MERGE_COLS), lambda i, j: (0, j)),
            pl.BlockSpec((D_SSM, MERGE_COLS), lambda i, j: (0, N_MERGE + j)),
            pl.BlockSpec((D_CONV, MERGE_COLS), lambda i, j: (0, j)),
        ],
        out_specs=pl.BlockSpec((TM, MERGE_COLS), lambda i, j: (i, j)),
        out_shape=jax.ShapeDtypeStruct((N_TOK, D_MODEL), BF16),
        scratch_shapes=[pltpu.VMEM((TM, D_MODEL), BF16), pltpu.VMEM((TM, D_SSM), BF16),
                        pltpu.VMEM((TM, D_CONV), BF16)],
        compiler_params=_cparams(("arbitrary", "arbitrary"), 56),
        name="gated_merge",
    )(xp, xs, g, actp, acts, cvp, cvs, w_in_bf, w_in_bf, w_glu_bf, w_glu_bf, w_co_bf)


def _outproj_kernel(m_ref, xp_ref, xs_ref, wo_ref, g_ref, wr_ref, br_ref,
                    h_ref, xn_ref, route_ref, gw_ref, cnt_ref):
    i = pl.program_id(0)
    mo = jnp.dot(m_ref[...], wo_ref[...], preferred_element_type=F32)

    def residual(x_ref):
        h_ref[...] = x_ref[...] + mo

    _on_pair(i, N_PROMPT_TOK // TMD, xp_ref, xs_ref, residual)
    xn = _rmsnorm_rows(h_ref[...], g_ref[...])
    xn_ref[...] = xn
    x_hi = xn.astype(BF16)
    x_lo = (xn - x_hi.astype(F32)).astype(BF16)
    w = wr_ref[...]
    w_hi = w.astype(BF16)
    w_lo = (w - w_hi.astype(F32)).astype(BF16)
    dot = functools.partial(jnp.dot, preferred_element_type=F32)
    logits = dot(x_hi, w_hi) + (dot(x_lo, w_hi) + dot(x_hi, w_lo)) + br_ref[...]

    lane = lax.broadcasted_iota(I32, logits.shape, 1)
    neg = -jnp.inf
    first = lambda hit: jnp.min(jnp.where(hit, lane, ROUTE_LANES), axis=-1, keepdims=True)
    gmask = lane < N_EXP_GROUPS
    lg = jnp.where(gmask, logits, neg)
    gmax = jnp.max(lg, axis=-1, keepdims=True)
    gsel = first(lg == gmax)
    psum = jnp.sum(jnp.where(gmask, jnp.exp(logits - gmax), 0.0), axis=-1, keepdims=True)
    pg_sel = 1.0 / psum
    e_lane = lane - N_EXP_GROUPS
    emask = (e_lane >= 0) & (e_lane < N_EXPERTS) & ((e_lane // EXP_PER_GROUP) == gsel)
    le = jnp.where(emask, logits, neg)
    v1 = jnp.max(le, axis=-1, keepdims=True)
    i1 = first(le == v1)
    le2 = jnp.where(lane == i1, neg, le)
    v2 = jnp.max(le2, axis=-1, keepdims=True)
    i2 = first(le2 == v2)
    z = jnp.exp(v2 - v1)
    w1 = pg_sel / (1.0 + z)
    w2 = pg_sel * z / (1.0 + z)
    e1 = i1 - N_EXP_GROUPS
    e2 = i2 - N_EXP_GROUPS
    gw_ref[...] = jnp.where(lane == 0, w1, jnp.where(lane == 1, w2, 0.0))
    oh1 = lane == e1
    oh2 = lane == e2
    hits = jnp.where(oh1 | oh2, 1.0, 0.0).astype(BF16)
    rr = lax.broadcasted_iota(I32, (TMD, TMD), 0)
    cc = lax.broadcasted_iota(I32, (TMD, TMD), 1)
    before = dot(jnp.where(cc < rr, 1.0, 0.0).astype(BF16), hits)
    rank1 = jnp.sum(jnp.where(oh1, before, 0.0), axis=-1, keepdims=True).astype(I32)
    rank2 = jnp.sum(jnp.where(oh2, before, 0.0), axis=-1, keepdims=True).astype(I32)
    route_ref[...] = jnp.where(lane == 0, e1, jnp.where(lane == 1, e2, jnp.where(
        lane == 2, rank1, jnp.where(lane == 3, rank2, 0))))
    cnt_ref[0] = jnp.sum(hits.astype(F32), axis=0, keepdims=True).astype(I32)


def _outproj(m, xp, xs, w_out_bf, g, w_router, b_router):
    tile = lambda n: pl.BlockSpec((TMD, n), lambda i: (i, 0))
    full = lambda r, c: pl.BlockSpec((r, c), lambda i: (0, 0))
    xps, xss = _pair_specs(TMD, D_MODEL, N_PROMPT_TOK // TMD)
    return pl.pallas_call(
        _outproj_kernel,
        grid=(N_DTILES,),
        in_specs=[tile(D_MODEL), xps, xss, full(D_MODEL, D_MODEL), full(1, D_MODEL),
                  full(D_MODEL, ROUTE_LANES), full(1, ROUTE_LANES)],
        out_specs=[tile(D_MODEL), tile(D_MODEL), tile(ROUTE_LANES), tile(ROUTE_LANES),
                   pl.BlockSpec((1, 1, ROUTE_LANES), lambda i: (i, 0, 0))],
        out_shape=(jax.ShapeDtypeStruct((N_TOK, D_MODEL), F32),
                   jax.ShapeDtypeStruct((N_TOK, D_MODEL), F32),
                   jax.ShapeDtypeStruct((N_TOK, ROUTE_LANES), I32),
                   jax.ShapeDtypeStruct((N_TOK, ROUTE_LANES), F32),
                   jax.ShapeDtypeStruct((N_DTILES, 1, ROUTE_LANES), I32)),
        compiler_params=_cparams(("arbitrary",), 48),
        name="out_proj_router",
    )(m, xp, xs, w_out_bf, g, w_router, b_router)


def _expert_kernel(te_ref, nrows_ref, nused_ref, pos_ref, xn_hbm, wg_ref, wu_ref, wd_ref,
                   o_ref, xbuf, wg_s, wu_s, wd_s, tok_s, sem):
    i = pl.program_id(0)
    nused = nused_ref[0]
    slot = i % 2

    def n_issued(tile):
        return pl.multiple_of(((nrows_ref[tile] + SUBLANES - 1) // SUBLANES) * SUBLANES, SUBLANES)

    def row_copy(tile, r, s):
        tok = tok_s[tile * TME + r]
        return pltpu.make_async_copy(xn_hbm.at[pl.ds(tok, 1)], xbuf.at[s, pl.ds(r, 1)], sem.at[s])

    def issue(tile, s):
        def body(q, _):
            row_copy(tile, 2 * q, s).start(priority=0)
            row_copy(tile, 2 * q + 1, s).start(priority=1)
            return 0
        lax.fori_loop(0, n_issued(tile) // 2, body, 0)

    @pl.when(i == 0)
    def _():
        def build(p, _):
            tok_s[pos_ref[p]] = lax.shift_right_logical(p, 1)
            return 0
        lax.fori_loop(0, N_PAIRS, build, 0, unroll=8)

        def pad(tile, _):
            def zero(r, _):
                tok_s[tile * TME + r] = 0
                return 0
            lax.fori_loop(nrows_ref[tile], n_issued(tile), zero, 0)
            return 0
        lax.fori_loop(0, nused, pad, 0)
        xbuf[...] = jnp.zeros_like(xbuf)
        issue(0, 0)

    @pl.when(i + 1 < nused)
    def _():
        issue(i + 1, 1 - slot)

    @pl.when(i < nused)
    def _():
        changed = jnp.logical_or(i == 0, te_ref[i] != te_ref[jnp.maximum(i - 1, 0)])

        @pl.when(changed)
        def _():
            wg_s[...] = wg_ref[0].astype(BF16)
            wu_s[...] = wu_ref[0].astype(BF16)
            wd_s[...] = wd_ref[0].astype(BF16)

        n = n_issued(i)
        pltpu.make_async_copy(xn_hbm.at[pl.ds(0, n)], xbuf.at[slot, pl.ds(0, n)], sem.at[slot]).wait()
        xb = xbuf[slot].astype(BF16)
        hg = jnp.dot(xb, wg_s[...], preferred_element_type=F32)
        hu = jnp.dot(xb, wu_s[...], preferred_element_type=F32)
        hid = jax.nn.silu(hg) * hu
        o_ref[...] = jnp.dot(hid.astype(BF16), wd_s[...], preferred_element_type=F32)

    @pl.when(i >= nused)
    def _():
        o_ref[...] = jnp.zeros_like(o_ref)


def _experts(tile_expert, tile_rows, n_used, pos, xn, wg, wu, wd):
    wspec = lambda r, c: pl.BlockSpec((1, r, c), lambda i, te, nr, nu, pos: (te[i], 0, 0))
    grid_spec = pltpu.PrefetchScalarGridSpec(
        num_scalar_prefetch=4,
        grid=(N_ETILES,),
        in_specs=[pl.BlockSpec(memory_space=pl.ANY), wspec(D_MODEL, D_EXPERT),
                  wspec(D_MODEL, D_EXPERT), wspec(D_EXPERT, D_MODEL)],
        out_specs=pl.BlockSpec((TME, D_MODEL), lambda i, te, nr, nu, pos: (i, 0)),
        scratch_shapes=[
            pltpu.VMEM((2, TME, D_MODEL), F32),
            pltpu.VMEM((D_MODEL, D_EXPERT), BF16),
            pltpu.VMEM((D_MODEL, D_EXPERT), BF16),
            pltpu.VMEM((D_EXPERT, D_MODEL), BF16),
            pltpu.SMEM((N_PAD,), I32),
            pltpu.SemaphoreType.DMA((2,)),
        ],
    )
    return pl.pallas_call(
        _expert_kernel,
        grid_spec=grid_spec,
        out_shape=jax.ShapeDtypeStruct((N_PAD, D_MODEL), F32),
        compiler_params=_cparams(("arbitrary",), 40),
        name="routed_experts",
    )(tile_expert, tile_rows, n_used, pos, xn, wg, wu, wd)


def _combine_kernel(pos_ref, h_ref, gw_ref, o_hbm, g_ref, yp_ref, ys_ref, buf, sem):
    i = pl.program_id(0)
    cur = i % 2

    def row_copy(tile, r, bs, s):
        p = pos_ref[(tile * TM + r) * 2 + s]
        return pltpu.make_async_copy(o_hbm.at[pl.ds(p, 1)], buf.at[bs, s, pl.ds(r, 1)], sem.at[bs, s])

    def issue(tile, bs):
        def body(r, _):
            row_copy(tile, r, bs, 0).start(priority=0)
            row_copy(tile, r, bs, 1).start(priority=1)
            return 0
        lax.fori_loop(0, TM, body, 0, unroll=2)

    @pl.when(i == 0)
    def _():
        issue(0, 0)

    @pl.when(i + 1 < N_TILES)
    def _():
        issue(i + 1, 1 - cur)

    for s in range(2):
        pltpu.make_async_copy(o_hbm.at[pl.ds(0, TM)], buf.at[cur, s], sem.at[cur, s]).wait()
    gw = gw_ref[...]
    y = h_ref[...] + (gw[:, 0:1] * buf[cur, 0] + gw[:, 1:2] * buf[cur, 1])
    y = _rmsnorm_rows(y, g_ref[...])

    @pl.when(i < N_PROMPT_TILES)
    def _():
        yp_ref[...] = y

    @pl.when(i >= N_PROMPT_TILES)
    def _():
        ys_ref[...] = y


def _combine(pos, h, gw, o, g):
    yps, yss = _pair_specs(TM, D_MODEL, N_PROMPT_TILES)
    grid_spec = pltpu.PrefetchScalarGridSpec(
        num_scalar_prefetch=1,
        grid=(N_TILES,),
        in_specs=[
            pl.BlockSpec((TM, D_MODEL), lambda i, pos: (i, 0)),
            pl.BlockSpec((TM, ROUTE_LANES), lambda i, pos: (i, 0)),
            pl.BlockSpec(memory_space=pl.ANY),
            pl.BlockSpec((1, D_MODEL), lambda i, pos: (0, 0)),
        ],
        out_specs=[yps, yss],
        scratch_shapes=[pltpu.VMEM((2, 2, TM, D_MODEL), F32), pltpu.SemaphoreType.DMA((2, 2))],
    )
    return pl.pallas_call(
        _combine_kernel,
        grid_spec=grid_spec,
        out_shape=(jax.ShapeDtypeStruct((N_PROMPT_TOK, D_MODEL), F32),
                   jax.ShapeDtypeStruct((N_SAMPLE_TOK, D_MODEL), F32)),
        compiler_params=_cparams(("arbitrary",), 48),
        name="combine_norm",
    )(pos, h, gw, o, g)


def _dispatch_plan(route, cnt):
    experts = jnp.arange(N_EXPERTS, dtype=I32)
    cnt = cnt[:, 0, :N_EXPERTS]
    counts = jnp.sum(cnt, axis=0)
    before_tile = jnp.cumsum(cnt, axis=0) - cnt
    padded = ((counts + TME - 1) // TME) * TME
    ends = jnp.cumsum(padded)
    starts = ends - padded
    base = jnp.repeat(starts[None, :] + before_tile, TMD, axis=0)
    eid = route[:, 0:2]
    onehot = eid[:, :, None] == experts[None, None, :]
    pos = jnp.sum(jnp.where(onehot, base[:, None, :], 0), axis=-1) + route[:, 2:4]
    n_used = (ends[-1] // TME).astype(I32)
    tile_start = jnp.arange(N_ETILES, dtype=I32) * TME
    te = jnp.sum((ends[None, :] <= tile_start[:, None]).astype(I32), axis=1)
    te = jnp.minimum(te, N_EXPERTS - 1)
    mine = te[:, None] == experts[None, :]
    left = jnp.sum(jnp.where(mine, (starts + counts)[None, :] - tile_start[:, None], 0), axis=1)
    tile_rows = jnp.clip(left, 0, TME)
    last = jnp.sum(jnp.where(jnp.arange(N_ETILES) == n_used - 1, te, 0))
    te = jnp.where(jnp.arange(N_ETILES) < n_used, te, last)
    return te.astype(I32), tile_rows.astype(I32), n_used.reshape(1), pos.reshape(-1).astype(I32)


def kernel(x_prompt, x_sample, state_ssm_re, state_ssm_im, cache_conv, norm_mix_g, w_in, lam_re, lam_im, log_dt, b_re, b_im, c_re, c_im, d_skip, w_ssm_glu, conv_w, conv_b, conv_ln_g, conv_ln_b, w_conv_out, w_out, norm_ffn_g, w_router_group, b_router_group, w_router_expert, b_router_expert, w_exp_gate, w_exp_up, w_exp_down, norm_final_g):
    assert w_in.shape[0] == 1, "single-layer trunk"
    xp = x_prompt.reshape(N_PROMPT_TOK, D_MODEL)
    xs = x_sample.reshape(N_SAMPLE_TOK, D_MODEL)
    w_in_bf = w_in[0].astype(BF16)
    row = lambda a: a.reshape(1, -1)
    g_mix = row(norm_mix_g[0])

    ar_rep, ai_rep, bbr, bbi = _discretise(lam_re[0], lam_im[0], log_dt[0], b_re[0], b_im[0])
    gph = lambda a: a.reshape(N_SSM_GROUPS, SSM_STATE, SSM_GROUP)
    pick = lambda a: gph(a)[:, :, 0].reshape(GROUP_BLOCKS, 1, GB_STATE)
    bm = jnp.concatenate([_block_diag_in(gph(bbr)), _block_diag_in(gph(bbi))], axis=-1).astype(BF16)
    cm = jnp.concatenate([_block_diag_out(c_re[0]), -_block_diag_out(c_im[0])], axis=1).astype(BF16)
    dsk = d_skip[0].reshape(GROUP_BLOCKS, 1, GB_IN)
    s0r = state_ssm_re[0].reshape(N_SAMPLE_SEQ, -1)
    s0i = state_ssm_im[0].reshape(N_SAMPLE_SEQ, -1)

    u, v = _inproj(xp, xs, g_mix, w_in_bf)
    actp, acts, pfr, pfi, sfr, sfi = _scan(u, bm, cm, pick(ar_rep), pick(ai_rep), dsk, s0r, s0i)

    hist_s = cache_conv[0].transpose(1, 0, 2).reshape(HIST_S, D_CONV)
    cvp, cvs, ncp, ncs = _conv(v, hist_s, conv_w[0], row(conv_b[0]), conv_ln_g[0], conv_ln_b[0])

    m = _merge(xp, xs, g_mix, actp.reshape(N_PROMPT_TOK, D_SSM), acts,
               cvp.reshape(N_PROMPT_TOK, D_CONV), cvs,
               w_in_bf, w_ssm_glu[0].astype(BF16), w_conv_out[0].astype(BF16))

    pad_lanes = ROUTE_LANES - N_EXP_GROUPS - N_EXPERTS
    w_router = jnp.concatenate(
        [w_router_group[0], w_router_expert[0], jnp.zeros((D_MODEL, pad_lanes), F32)], axis=1)
    b_router = jnp.concatenate(
        [b_router_group[0], b_router_expert[0], jnp.zeros((pad_lanes,), F32)]).reshape(1, ROUTE_LANES)
    h, xn2, route, gw, cnt = _outproj(m, xp, xs, w_out[0].astype(BF16), row(norm_ffn_g[0]),
                                      w_router, b_router)

    te, tile_rows, n_used, pos = _dispatch_plan(route, cnt)
    o = _experts(te, tile_rows, n_used, pos, xn2, w_exp_gate[0], w_exp_up[0], w_exp_down[0])
    yp, ys = _combine(pos, h, gw, o, row(norm_final_g))

    st = lambda a, n: a.reshape(1, n, N_SSM_GROUPS, SSM_STATE)
    ncp = ncp.reshape(CONV_BUF, N_PROMPT_SEQ, D_CONV).transpose(1, 0, 2)[None]
    ncs = ncs.reshape(CONV_BUF, N_SAMPLE_SEQ, D_CONV).transpose(1, 0, 2)[None]
    return (yp.reshape(N_PROMPT_SEQ, PROMPT_LEN, D_MODEL), ys.reshape(N_SAMPLE_SEQ, SAMPLE_LEN, D_MODEL),
            st(pfr, N_PROMPT_SEQ), st(pfi, N_PROMPT_SEQ), ncp,
            st(sfr, N_SAMPLE_SEQ), st(sfi, N_SAMPLE_SEQ), ncs)
```

```python
import functools

import jax
import jax.numpy as jnp
from jax import lax
from jax.experimental import pallas as pl
from jax.experimental.pallas import tpu as pltpu

F32 = jnp.float32
BF16 = jnp.bfloat16
I32 = jnp.int32

SUBLANES = 8
LANES = 128

D_MODEL = 2048
D_SSM = 1024
D_CONV = 1024
SSM_GROUP = 16
N_SSM_GROUPS = 64
SSM_STATE = 64
CONV_WIDTH = 31
CONV_BUF = CONV_WIDTH - 1
N_EXP_GROUPS = 4
EXP_PER_GROUP = 8
N_EXPERTS = 32
D_EXPERT = 256
EPS = 1e-6

N_PROMPT_SEQ = 4
PROMPT_LEN = 2048
N_SAMPLE_SEQ = 16
SAMPLE_LEN = 32
N_PROMPT_TOK = N_PROMPT_SEQ * PROMPT_LEN
N_SAMPLE_TOK = N_SAMPLE_SEQ * SAMPLE_LEN
N_TOK = N_PROMPT_TOK + N_SAMPLE_TOK

TM = 512
N_PROMPT_TILES = N_PROMPT_TOK // TM
N_TILES = N_TOK // TM
PROMPT_STEPS = TM // N_PROMPT_SEQ
GROUP_BLOCKS = 4
GROUPS_PER_BLOCK = N_SSM_GROUPS // GROUP_BLOCKS
GB_IN = GROUPS_PER_BLOCK * SSM_GROUP
GB_STATE = GROUPS_PER_BLOCK * SSM_STATE
GB_LB = GB_STATE // LANES
HIST_P = CONV_BUF * N_PROMPT_SEQ
HIST_S = CONV_BUF * N_SAMPLE_SEQ
ROUTE_LANES = 128
N_PAIRS = 2 * N_TOK
TME = 256
N_PAD = N_PAIRS + N_EXPERTS * TME
N_ETILES = N_PAD // TME
TMD = 256
N_DTILES = N_TOK // TMD
SLAB_ROWS = D_MODEL // LANES


def _cparams(sem, vmem_mb):
    return pltpu.CompilerParams(dimension_semantics=sem, vmem_limit_bytes=vmem_mb * 1024 * 1024)


def _pair_specs(rows, cols, n_prompt_tiles):
    p = pl.BlockSpec((rows, cols), lambda i, *_: (jnp.minimum(i, n_prompt_tiles - 1), 0))
    s = pl.BlockSpec((rows, cols), lambda i, *_: (jnp.maximum(i - n_prompt_tiles, 0), 0))
    return p, s


def _on_pair(i, n_prompt_tiles, p_ref, s_ref, fn):
    @pl.when(i < n_prompt_tiles)
    def _():
        fn(p_ref)

    @pl.when(i >= n_prompt_tiles)
    def _():
        fn(s_ref)


def _disc_kernel(lr_ref, li_ref, ldt_ref, br_ref, bi_ref, ar_ref, ai_ref, bbr_ref, bbi_ref):
    lr = lr_ref[...]
    li = li_ref[...]
    dt = jnp.exp(ldt_ref[...])
    mag = jnp.exp(lr * dt)
    ar = mag * jnp.cos(li * dt)
    ai = mag * jnp.sin(li * dt)
    den = lr * lr + li * li
    nr = ar - 1.0
    cr = (nr * lr + ai * li) / den
    ci = (ai * lr - nr * li) / den
    br = br_ref[...]
    bi = bi_ref[...]
    ar_ref[...] = ar
    ai_ref[...] = ai
    bbr_ref[...] = cr * br - ci * bi
    bbi_ref[...] = cr * bi + ci * br


def _discretise(lam_re, lam_im, log_dt, b_re, b_im):
    shp = jax.ShapeDtypeStruct((N_SSM_GROUPS, SSM_STATE * SSM_GROUP), F32)
    rep = lambda a: jnp.repeat(a, SSM_GROUP, axis=-1)
    ldt = jnp.broadcast_to(log_dt[:, None], (N_SSM_GROUPS, SSM_STATE * SSM_GROUP))
    return pl.pallas_call(_disc_kernel, out_shape=(shp, shp, shp, shp), name="s5_discretise")(
        rep(lam_re), rep(lam_im), ldt,
        b_re.reshape(N_SSM_GROUPS, -1), b_im.reshape(N_SSM_GROUPS, -1))


def _block_diag_in(bb):
    b4 = bb.reshape(GROUP_BLOCKS, GROUPS_PER_BLOCK, SSM_STATE, SSM_GROUP).transpose(0, 1, 3, 2)
    eye = jnp.eye(GROUPS_PER_BLOCK, dtype=bool)[None, :, None, :, None]
    full = jnp.where(eye, b4[:, :, :, None, :], 0.0)
    return full.reshape(GROUP_BLOCKS, GB_IN, GB_STATE)


def _block_diag_out(c):
    c4 = c.reshape(GROUP_BLOCKS, GROUPS_PER_BLOCK, SSM_GROUP, SSM_STATE).transpose(0, 1, 3, 2)
    eye = jnp.eye(GROUPS_PER_BLOCK, dtype=bool)[None, :, None, :, None]
    full = jnp.where(eye, c4[:, :, :, None, :], 0.0)
    return full.reshape(GROUP_BLOCKS, GB_STATE, GB_IN)


def _rmsnorm_rows(x, g):
    r = lax.rsqrt(jnp.mean(x * x, axis=-1, keepdims=True) + EPS)
    return x * r * g


def _inproj_kernel(xp_ref, xs_ref, g_ref, w_ref, u_ref, v_ref, xn_s, va_s):
    i = pl.program_id(0)
    j = pl.program_id(1)

    @pl.when(j == 0)
    def _():
        def norm(x_ref):
            xn_s[...] = _rmsnorm_rows(x_ref[...], g_ref[...]).astype(BF16)
        _on_pair(i, N_PROMPT_TILES, xp_ref, xs_ref, norm)

    p = jnp.dot(xn_s[...], w_ref[...], preferred_element_type=F32)

    @pl.when(j == 0)
    def _():
        u_ref[...] = p

    @pl.when(j == 1)
    def _():
        va_s[...] = p

    @pl.when(j == 2)
    def _():
        v_ref[...] = va_s[...] * jax.nn.sigmoid(p)


def _inproj(xp, xs, g, w_in_bf):
    out = jax.ShapeDtypeStruct((N_TOK, D_SSM), F32)
    xps, xss = _pair_specs(TM, D_MODEL, N_PROMPT_TILES)
    return pl.pallas_call(
        _inproj_kernel,
        grid=(N_TILES, 3),
        in_specs=[
            xps, xss,
            pl.BlockSpec((1, D_MODEL), lambda i, j: (0, 0)),
            pl.BlockSpec((D_MODEL, D_SSM), lambda i, j: (0, j)),
        ],
        out_specs=[
            pl.BlockSpec((TM, D_SSM), lambda i, j: (i, 0)),
            pl.BlockSpec((TM, D_CONV), lambda i, j: (i, 0)),
        ],
        out_shape=(out, out),
        scratch_shapes=[pltpu.VMEM((TM, D_MODEL), BF16), pltpu.VMEM((TM, D_CONV), F32)],
        compiler_params=_cparams(("arbitrary", "arbitrary"), 48),
        name="in_proj",
    )(xp, xs, g, w_in_bf)


def _seq_tile_specs(cols, col_index):
    blocks_per_seq = PROMPT_LEN // PROMPT_STEPS

    def prompt(b):
        return pl.BlockSpec(
            (PROMPT_STEPS, cols),
            lambda *ids: (b * blocks_per_seq + jnp.minimum(ids[-1], N_PROMPT_TILES - 1), col_index(*ids)))

    sample = pl.BlockSpec((TM, cols), lambda *ids: (N_PROMPT_TOK // TM, col_index(*ids)))
    return [prompt(b) for b in range(N_PROMPT_SEQ)] + [sample]


def _scan_kernel(u0_ref, u1_ref, u2_ref, u3_ref, us_ref, bm_ref, cm_ref, ar_ref, ai_ref, dsk_ref,
                 s0r_ref, s0i_ref, actp_ref, acts_ref, pfr_ref, pfi_ref, sfr_ref, sfi_ref,
                 xs_ref, car_ref):
    t = pl.program_id(1)
    half_cols = GB_LB // 2 * LANES

    def project_in(u, n_seq, steps):
        x = jnp.dot(u.astype(BF16), bm_ref[0], preferred_element_type=F32)
        for c in range(2 * GB_LB):
            for b in range(n_seq):
                xs_ref[c, pl.ds(b, steps, stride=n_seq), :] = (
                    x[b * steps:(b + 1) * steps, c * LANES:(c + 1) * LANES])

    def project_out(u, n_seq, steps):
        rows = []
        for b in range(n_seq):
            rows.append(jnp.concatenate(
                [xs_ref[c, pl.ds(b, steps, stride=n_seq), :] for c in range(2 * GB_LB)], axis=1))
        h = jnp.concatenate(rows, axis=0).astype(BF16)
        y = jnp.dot(h, cm_ref[0], preferred_element_type=F32) + dsk_ref[0] * u
        return jax.nn.gelu(y).astype(BF16)

    def load_half(rows, half, imag):
        c0 = imag * GB_LB + half * (GB_LB // 2)
        return jnp.concatenate([xs_ref[c0 + c, rows, :] for c in range(GB_LB // 2)], axis=1)

    def store_half(rows, half, imag, val):
        c0 = imag * GB_LB + half * (GB_LB // 2)
        for c in range(GB_LB // 2):
            xs_ref[c0 + c, rows, :] = val[:, c * LANES:(c + 1) * LANES]

    @pl.when(t < N_PROMPT_TILES)
    def _prompt():
        u = jnp.concatenate([u0_ref[...], u1_ref[...], u2_ref[...], u3_ref[...]], axis=0)
        project_in(u, N_PROMPT_SEQ, PROMPT_STEPS)

        @pl.when(t == 0)
        def _():
            car_ref[...] = jnp.zeros_like(car_ref)

        shape = (SUBLANES, half_cols)
        top = lax.broadcasted_iota(I32, shape, 0) < N_PROMPT_SEQ
        for half in range(2):
            cols = slice(half * half_cols, (half + 1) * half_cols)
            icols = slice(GB_STATE + half * half_cols, GB_STATE + (half + 1) * half_cols)
            arb = jnp.broadcast_to(ar_ref[0][:, cols], shape)
            aib = jnp.broadcast_to(ai_ref[0][:, cols], shape)
            a1r = jnp.where(top, 0.0, arb)
            a1i = jnp.where(top, 0.0, aib)
            a2r = jnp.where(top, arb, arb * arb - aib * aib)
            a2i = jnp.where(top, aib, 2.0 * arb * aib)

            def body(r, carry):
                hpr, hpi = carry
                rows = pl.ds(pl.multiple_of(r * SUBLANES, SUBLANES), SUBLANES)
                xr = load_half(rows, half, 0)
                xi = load_half(rows, half, 1)
                sxr = pltpu.roll(xr, N_PROMPT_SEQ, axis=0)
                sxi = pltpu.roll(xi, N_PROMPT_SEQ, axis=0)
                tr = xr + a1r * sxr - a1i * sxi
                ti = xi + a1r * sxi + a1i * sxr
                hr = tr + a2r * hpr - a2i * hpi
                hi = ti + a2r * hpi + a2i * hpr
                store_half(rows, half, 0, hr)
                store_half(rows, half, 1, hi)
                nhr = jnp.where(top, pltpu.roll(hr, N_PROMPT_SEQ, axis=0), hr)
                nhi = jnp.where(top, pltpu.roll(hi, N_PROMPT_SEQ, axis=0), hi)
                return nhr, nhi

            hr, hi = lax.fori_loop(0, TM // SUBLANES, body, (car_ref[:, cols], car_ref[:, icols]))
            car_ref[:, cols] = hr
            car_ref[:, icols] = hi

            @pl.when(t == N_PROMPT_TILES - 1)
            def _():
                pfr_ref[:, cols] = hr[N_PROMPT_SEQ:SUBLANES]
                pfi_ref[:, cols] = hi[N_PROMPT_SEQ:SUBLANES]

        y = project_out(u, N_PROMPT_SEQ, PROMPT_STEPS)
        for b in range(N_PROMPT_SEQ):
            actp_ref[b] = y[b * PROMPT_STEPS:(b + 1) * PROMPT_STEPS]

    @pl.when(t == N_PROMPT_TILES)
    def _sample():
        u = us_ref[...]
        project_in(u, N_SAMPLE_SEQ, SAMPLE_LEN)
        shape = (N_SAMPLE_SEQ, half_cols)
        for half in range(2):
            cols = slice(half * half_cols, (half + 1) * half_cols)
            arb = jnp.broadcast_to(ar_ref[0][:, cols], shape)
            aib = jnp.broadcast_to(ai_ref[0][:, cols], shape)

            def body(k, carry):
                hr, hi = carry
                rows = pl.ds(pl.multiple_of(k * N_SAMPLE_SEQ, N_SAMPLE_SEQ), N_SAMPLE_SEQ)
                xr = load_half(rows, half, 0)
                xi = load_half(rows, half, 1)
                nhr = xr + arb * hr - aib * hi
                nhi = xi + arb * hi + aib * hr
                store_half(rows, half, 0, nhr)
                store_half(rows, half, 1, nhi)
                return nhr, nhi

            hr, hi = lax.fori_loop(0, SAMPLE_LEN, body, (s0r_ref[:, cols], s0i_ref[:, cols]))
            sfr_ref[:, cols] = hr
            sfi_ref[:, cols] = hi

        acts_ref[...] = project_out(u, N_SAMPLE_SEQ, SAMPLE_LEN)


def _scan(u, bm, cm, ar, ai, dsk, s0r, s0i):
    gb3 = lambda last: pl.BlockSpec((1,) + last, lambda g, t: (g, 0, 0))
    state = lambda n: pl.BlockSpec((n, GB_STATE), lambda g, t: (0, g))
    pst = jax.ShapeDtypeStruct((N_PROMPT_SEQ, N_SSM_GROUPS * SSM_STATE), F32)
    sst = jax.ShapeDtypeStruct((N_SAMPLE_SEQ, N_SSM_GROUPS * SSM_STATE), F32)
    return pl.pallas_call(
        _scan_kernel,
        grid=(GROUP_BLOCKS, N_TILES),
        in_specs=_seq_tile_specs(GB_IN, lambda g, t: g) + [
            gb3((GB_IN, 2 * GB_STATE)),
            gb3((2 * GB_STATE, GB_IN)),
            gb3((1, GB_STATE)),
            gb3((1, GB_STATE)),
            gb3((1, GB_IN)),
            state(N_SAMPLE_SEQ), state(N_SAMPLE_SEQ),
        ],
        out_specs=[
            pl.BlockSpec((N_PROMPT_SEQ, PROMPT_STEPS, GB_IN),
                         lambda g, t: (0, jnp.minimum(t, N_PROMPT_TILES - 1), g)),
            pl.BlockSpec((TM, GB_IN), lambda g, t: (0, g)),
            state(N_PROMPT_SEQ), state(N_PROMPT_SEQ), state(N_SAMPLE_SEQ), state(N_SAMPLE_SEQ),
        ],
        out_shape=(jax.ShapeDtypeStruct((N_PROMPT_SEQ, PROMPT_LEN, D_SSM), BF16),
                   jax.ShapeDtypeStruct((N_SAMPLE_TOK, D_SSM), BF16), pst, pst, sst, sst),
        scratch_shapes=[pltpu.VMEM((2 * GB_LB, TM, LANES), F32),
                        pltpu.VMEM((SUBLANES, 2 * GB_STATE), F32)],
        compiler_params=_cparams(("arbitrary", "arbitrary"), 40),
        name="s5_scan",
    )(u, u, u, u, u, bm, cm, ar, ai, dsk, s0r, s0i)


CONV_ROWS = 64
EXT_ROWS = HIST_S + TM
CONV_LB = D_CONV // LANES


def _conv_kernel(v0_ref, v1_ref, v2_ref, v3_ref, vs_ref, hist_ref, w_ref, b_ref, lg_ref, lb_ref,
                 cvp_ref, cvs_ref, ncp_ref, ncs_ref, ext_ref, sh_ref, acc_ref):
    t = pl.program_id(0)

    def load_tile(v, n_seq, steps):
        for c in range(CONV_LB):
            for b in range(n_seq):
                ext_ref[c, pl.ds(HIST_S + b, steps, stride=n_seq), :] = (
                    v[b * steps:(b + 1) * steps, c * LANES:(c + 1) * LANES])

    def taps(n_seq):
        half = SUBLANES // 2
        if n_seq % SUBLANES:
            lo = HIST_S - HIST_P - SUBLANES
            sh_ref[:, lo:EXT_ROWS - SUBLANES, :] = ext_ref[:, lo + half:EXT_ROWS - half, :]

        def body(rb, _):
            r0 = pl.multiple_of(rb * CONV_ROWS, CONV_ROWS)
            for cb in range(CONV_LB):
                cols = slice(cb * LANES, (cb + 1) * LANES)
                acc = None
                for j in range(CONV_WIDTH):
                    start = HIST_S - (CONV_BUF - j) * n_seq
                    src = ext_ref
                    if start % SUBLANES:
                        src, start = sh_ref, start - half
                    rows = pl.ds(pl.multiple_of(r0 + start, SUBLANES), CONV_ROWS)
                    term = src[cb, rows, :] * w_ref[j:j + 1, cols]
                    acc = term if acc is None else acc + term
                acc_ref[cb, pl.ds(r0, CONV_ROWS), :] = acc + b_ref[:, cols]
            return 0

        lax.fori_loop(0, TM // CONV_ROWS, body, 0)

    def norm_act():
        y = acc_ref[...]
        mu = jnp.sum(jnp.sum(y, axis=0), axis=-1, keepdims=True) / D_CONV
        yc = y - mu[None]
        var = jnp.sum(jnp.sum(yc * yc, axis=0), axis=-1, keepdims=True) / D_CONV
        z = yc * lax.rsqrt(var + EPS)[None] * lg_ref[...] + lb_ref[...]
        acc_ref[...] = jax.nn.silu(z)

    def rows_of_seq(b, n_seq, steps):
        return jnp.concatenate(
            [acc_ref[c, pl.ds(b, steps, stride=n_seq), :] for c in range(CONV_LB)], axis=1).astype(BF16)

    @pl.when(t < N_PROMPT_TILES)
    def _prompt():
        @pl.when(t == 0)
        def _():
            ext_ref[:, 0:HIST_S, :] = jnp.zeros((CONV_LB, HIST_S, LANES), F32)

        v = jnp.concatenate([v0_ref[...], v1_ref[...], v2_ref[...], v3_ref[...]], axis=0)
        load_tile(v, N_PROMPT_SEQ, PROMPT_STEPS)
        taps(N_PROMPT_SEQ)
        norm_act()
        for b in range(N_PROMPT_SEQ):
            cvp_ref[b] = rows_of_seq(b, N_PROMPT_SEQ, PROMPT_STEPS)

        @pl.when(t == N_PROMPT_TILES - 1)
        def _():
            for c in range(CONV_LB):
                ncp_ref[:, c * LANES:(c + 1) * LANES] = ext_ref[c, EXT_ROWS - HIST_P:EXT_ROWS, :]

        @pl.when(t < N_PROMPT_TILES - 1)
        def _():
            ext_ref[:, HIST_S - HIST_P:HIST_S, :] = ext_ref[:, EXT_ROWS - HIST_P:EXT_ROWS, :]

    @pl.when(t == N_PROMPT_TILES)
    def _sample():
        for c in range(CONV_LB):
            ext_ref[c, 0:HIST_S, :] = hist_ref[:, c * LANES:(c + 1) * LANES]
        load_tile(vs_ref[...], N_SAMPLE_SEQ, SAMPLE_LEN)
        taps(N_SAMPLE_SEQ)
        norm_act()
        cvs_ref[...] = jnp.concatenate(
            [rows_of_seq(b, N_SAMPLE_SEQ, SAMPLE_LEN) for b in range(N_SAMPLE_SEQ)], axis=0)
        for c in range(CONV_LB):
            ncs_ref[:, c * LANES:(c + 1) * LANES] = ext_ref[c, EXT_ROWS - HIST_S:EXT_ROWS, :]


def _conv(v, hist_s, conv_w, conv_b, ln_g, ln_b):
    row = lambda n: pl.BlockSpec((n, D_CONV), lambda t: (0, 0))
    lane3 = pl.BlockSpec((CONV_LB, 1, LANES), lambda t: (0, 0, 0))
    return pl.pallas_call(
        _conv_kernel,
        grid=(N_TILES,),
        in_specs=_seq_tile_specs(D_CONV, lambda t: 0) + [row(HIST_S), row(CONV_WIDTH), row(1), lane3, lane3],
        out_specs=[
            pl.BlockSpec((N_PROMPT_SEQ, PROMPT_STEPS, D_CONV),
                         lambda t: (0, jnp.minimum(t, N_PROMPT_TILES - 1), 0)),
            row(TM), row(HIST_P), row(HIST_S)],
        out_shape=(jax.ShapeDtypeStruct((N_PROMPT_SEQ, PROMPT_LEN, D_CONV), BF16),
                   jax.ShapeDtypeStruct((N_SAMPLE_TOK, D_CONV), BF16),
                   jax.ShapeDtypeStruct((HIST_P, D_CONV), F32),
                   jax.ShapeDtypeStruct((HIST_S, D_CONV), F32)),
        scratch_shapes=[pltpu.VMEM((CONV_LB, EXT_ROWS, LANES), F32),
                        pltpu.VMEM((CONV_LB, EXT_ROWS, LANES), F32),
                        pltpu.VMEM((CONV_LB, TM, LANES), F32)],
        compiler_params=_cparams(("arbitrary",), 48),
        name="conv_branch",
    )(v, v, v, v, v, hist_s, conv_w, conv_b,
      ln_g.reshape(CONV_LB, 1, LANES), ln_b.reshape(CONV_LB, 1, LANES))


MERGE_COLS = 512
N_MERGE = D_MODEL // MERGE_COLS


def _merge_kernel(xp_ref, xs_ref, g_ref, actp_ref, acts_ref, cvp_ref, cvs_ref,
                  wga_ref, wgb_ref, wg1_ref, wg2_ref, wco_ref, m_ref, xn_s, act_s, cv_s):
    i = pl.program_id(0)
    j = pl.program_id(1)

    @pl.when(j == 0)
    def _():
        def norm(x_ref):
            xn_s[...] = _rmsnorm_rows(x_ref[...], g_ref[...]).astype(BF16)

        def copy_act(ref):
            act_s[...] = ref[...]

        def copy_cv(ref):
            cv_s[...] = ref[...]

        _on_pair(i, N_PROMPT_TILES, xp_ref, xs_ref, norm)
        _on_pair(i, N_PROMPT_TILES, actp_ref, acts_ref, copy_act)
        _on_pair(i, N_PROMPT_TILES, cvp_ref, cvs_ref, copy_cv)

    xn = xn_s[...]
    act = act_s[...]
    dot = functools.partial(jnp.dot, preferred_element_type=F32)
    ya = dot(act, wg1_ref[...]) * jax.nn.sigmoid(dot(act, wg2_ref[...]))
    yb = dot(cv_s[...], wco_ref[...])
    m = jax.nn.sigmoid(dot(xn, wga_ref[...])) * ya + jax.nn.sigmoid(dot(xn, wgb_ref[...])) * yb
    m_ref[...] = m.astype(BF16)


def _merge(xp, xs, g, actp, acts, cvp, cvs, w_in_bf, w_glu_bf, w_co_bf):
    ga0 = (D_SSM + 2 * D_CONV) // MERGE_COLS
    gb0 = ga0 + N_MERGE
    pair = lambda n: list(_pair_specs(TM, n, N_PROMPT_TILES))
    return pl.pallas_call(
        _merge_kernel,
        grid=(N_TILES, N_MERGE),
        in_specs=pair(D_MODEL) + [pl.BlockSpec((1, D_MODEL), lambda i, j: (0, 0))]
        + pair(D_SSM) + pair(D_CONV) + [
            pl.BlockSpec((D_MODEL, MERGE_COLS), lambda i, j: (0, ga0 + j)),
            pl.BlockSpec((D_MODEL, MERGE_COLS), lambda i, j: (0, gb0 + j)),
            pl.BlockSpec((D_SSM, MERGE_COLS), lambda i, j: (0, j)),
            pl.BlockSpec((D_SSM, MERGE_COLS), lambda i, j: (0, N_MERGE + j)),
            pl.BlockSpec((D_CONV, MERGE_COLS), lambda i, j: (0, j)),
        ],
        out_specs=pl.BlockSpec((TM, MERGE_COLS), lambda i, j: (i, j)),
        out_shape=jax.ShapeDtypeStruct((N_TOK, D_MODEL), BF16),
        scratch_shapes=[pltpu.VMEM((TM, D_MODEL), BF16), pltpu.VMEM((TM, D_SSM), BF16),
                        pltpu.VMEM((TM, D_CONV), BF16)],
        compiler_params=_cparams(("arbitrary", "arbitrary"), 56),
        name="gated_merge",
    )(xp, xs, g, actp, acts, cvp, cvs, w_in_bf, w_in_bf, w_glu_bf, w_glu_bf, w_co_bf)


def _to_slabs(ref, val):
    rows = val.shape[0]
    for c in range(SLAB_ROWS):
        ref[pl.ds(c, rows, stride=SLAB_ROWS), :] = val[:, c * LANES:(c + 1) * LANES]


def _from_slabs(ref, rows):
    return jnp.concatenate(
        [ref[pl.ds(c, rows, stride=SLAB_ROWS), :] for c in range(SLAB_ROWS)], axis=1)


def _outproj_kernel(m_ref, xp_ref, xs_ref, wo_ref, g_ref, wr_ref, br_ref,
                    h_ref, xn_ref, route_ref, gw_ref, cnt_ref, lg_s):
    i = pl.program_id(0)
    dot = functools.partial(jnp.dot, preferred_element_type=F32)

    @pl.when(i == 0)
    def _():
        lg_s[...] = jnp.zeros_like(lg_s)

    logits = lg_s[(i + 1) % 2]
    _route(logits, route_ref, gw_ref, cnt_ref)

    x = jnp.where(i < N_PROMPT_TOK // TMD, xp_ref[...], xs_ref[...])
    h = x + dot(m_ref[...], wo_ref[...])
    h_ref[...] = h
    xn = _rmsnorm_rows(h, g_ref[...])
    _to_slabs(xn_ref, xn)
    x_hi = xn.astype(BF16)
    x_lo = (xn - x_hi.astype(F32)).astype(BF16)
    w = wr_ref[...]
    w_hi = w.astype(BF16)
    w_lo = (w - w_hi.astype(F32)).astype(BF16)
    lg_s[i % 2] = dot(x_hi, w_hi) + (dot(x_lo, w_hi) + dot(x_hi, w_lo)) + br_ref[...]


def _route(logits, route_ref, gw_ref, cnt_ref):
    dot = functools.partial(jnp.dot, preferred_element_type=F32)
    lane = lax.broadcasted_iota(I32, logits.shape, 1)
    neg = -jnp.inf
    first = lambda hit: jnp.min(jnp.where(hit, lane, ROUTE_LANES), axis=-1, keepdims=True)
    gmask = lane < N_EXP_GROUPS
    lg = jnp.where(gmask, logits, neg)
    gmax = jnp.max(lg, axis=-1, keepdims=True)
    gsel = first(lg == gmax)
    psum = jnp.sum(jnp.where(gmask, jnp.exp(logits - gmax), 0.0), axis=-1, keepdims=True)
    pg_sel = 1.0 / psum
    e_lane = lane - N_EXP_GROUPS
    emask = (e_lane >= 0) & (e_lane < N_EXPERTS) & ((e_lane // EXP_PER_GROUP) == gsel)
    le = jnp.where(emask, logits, neg)
    v1 = jnp.max(le, axis=-1, keepdims=True)
    i1 = first(le == v1)
    le2 = jnp.where(lane == i1, neg, le)
    v2 = jnp.max(le2, axis=-1, keepdims=True)
    i2 = first(le2 == v2)
    z = jnp.exp(v2 - v1)
    w1 = pg_sel / (1.0 + z)
    w2 = pg_sel * z / (1.0 + z)
    e1 = i1 - N_EXP_GROUPS
    e2 = i2 - N_EXP_GROUPS
    gw_ref[...] = jnp.where(lane == 0, w1, jnp.where(lane == 1, w2, 0.0))
    oh1 = lane == e1
    oh2 = lane == e2
    hits = jnp.where(oh1 | oh2, 1.0, 0.0).astype(BF16)
    rr = lax.broadcasted_iota(I32, (TMD, TMD), 0)
    cc = lax.broadcasted_iota(I32, (TMD, TMD), 1)
    before = dot(jnp.where(cc < rr, 1.0, 0.0).astype(BF16), hits)
    rank1 = jnp.sum(jnp.where(oh1, before, 0.0), axis=-1, keepdims=True).astype(I32)
    rank2 = jnp.sum(jnp.where(oh2, before, 0.0), axis=-1, keepdims=True).astype(I32)
    route_ref[...] = jnp.where(lane == 0, e1, jnp.where(lane == 1, e2, jnp.where(
        lane == 2, rank1, jnp.where(lane == 3, rank2, 0))))
    cnt_ref[0] = jnp.sum(hits.astype(F32), axis=0, keepdims=True).astype(I32)


def _outproj(m, xp, xs, w_out_bf, g, w_router, b_router):
    cur = lambda i: jnp.minimum(i, N_DTILES - 1)
    prev = lambda i: jnp.maximum(i - 1, 0)
    tile = lambda n: pl.BlockSpec((TMD, n), lambda i: (cur(i), 0))
    routed = lambda n: pl.BlockSpec((TMD, n), lambda i: (prev(i), 0))
    full = lambda r, c: pl.BlockSpec((r, c), lambda i: (0, 0))
    n_p = N_PROMPT_TOK // TMD
    xps = pl.BlockSpec((TMD, D_MODEL), lambda i: (jnp.minimum(i, n_p - 1), 0))
    xss = pl.BlockSpec((TMD, D_MODEL), lambda i: (jnp.clip(i - n_p, 0, N_DTILES - n_p - 1), 0))
    return pl.pallas_call(
        _outproj_kernel,
        grid=(N_DTILES + 1,),
        in_specs=[tile(D_MODEL), xps, xss, full(D_MODEL, D_MODEL), full(1, D_MODEL),
                  full(D_MODEL, ROUTE_LANES), full(1, ROUTE_LANES)],
        out_specs=[tile(D_MODEL),
                   pl.BlockSpec((TMD * SLAB_ROWS, LANES), lambda i: (cur(i), 0)),
                   routed(ROUTE_LANES), routed(ROUTE_LANES),
                   pl.BlockSpec((1, 1, ROUTE_LANES), lambda i: (prev(i), 0, 0))],
        out_shape=(jax.ShapeDtypeStruct((N_TOK, D_MODEL), F32),
                   jax.ShapeDtypeStruct((N_TOK * SLAB_ROWS, LANES), F32),
                   jax.ShapeDtypeStruct((N_TOK, ROUTE_LANES), I32),
                   jax.ShapeDtypeStruct((N_TOK, ROUTE_LANES), F32),
                   jax.ShapeDtypeStruct((N_DTILES, 1, ROUTE_LANES), I32)),
        scratch_shapes=[pltpu.VMEM((2, TMD, ROUTE_LANES), F32)],
        compiler_params=_cparams(("arbitrary",), 48),
        name="out_proj_router",
    )(m, xp, xs, w_out_bf, g, w_router, b_router)


def _expert_kernel(te_ref, nrows_ref, nused_ref, pos_ref, xn_hbm, wg_ref, wu_ref, wd_ref,
                   o_ref, xbuf, wg_s, wu_s, wd_s, tok_s, sem):
    i = pl.program_id(0)
    nused = nused_ref[0]
    slot = i % 2

    def n_issued(tile):
        return pl.multiple_of(((nrows_ref[tile] + SUBLANES - 1) // SUBLANES) * SUBLANES, SUBLANES)

    def slab(k):
        return pl.ds(pl.multiple_of(k * SLAB_ROWS, SLAB_ROWS), SLAB_ROWS)

    def row_copy(tile, r, s):
        tok = tok_s[tile * TME + r]
        return pltpu.make_async_copy(xn_hbm.at[slab(tok)], xbuf.at[s, slab(r)], sem.at[s])

    def issue(tile, s):
        def body(q, _):
            row_copy(tile, 2 * q, s).start(priority=0)
            row_copy(tile, 2 * q + 1, s).start(priority=1)
            return 0
        lax.fori_loop(0, n_issued(tile) // 2, body, 0)

    @pl.when(i == 0)
    def _():
        def build(p, _):
            tok_s[pos_ref[p]] = lax.shift_right_logical(p, 1)
            return 0
        lax.fori_loop(0, N_PAIRS, build, 0, unroll=8)

        def pad(tile, _):
            def zero(r, _):
                tok_s[tile * TME + r] = 0
                return 0
            lax.fori_loop(nrows_ref[tile], n_issued(tile), zero, 0)
            return 0
        lax.fori_loop(0, nused, pad, 0)
        xbuf[...] = jnp.zeros_like(xbuf)
        issue(0, 0)

    @pl.when(i + 1 < nused)
    def _():
        issue(i + 1, 1 - slot)

    @pl.when(i < nused)
    def _():
        changed = jnp.logical_or(i == 0, te_ref[i] != te_ref[jnp.maximum(i - 1, 0)])

        @pl.when(changed)
        def _():
            wg_s[...] = wg_ref[0].astype(BF16)
            wu_s[...] = wu_ref[0].astype(BF16)
            wd_s[...] = wd_ref[0].astype(BF16)

        n = pl.multiple_of(n_issued(i) * SLAB_ROWS, SUBLANES * SLAB_ROWS)
        pltpu.make_async_copy(xn_hbm.at[pl.ds(0, n)], xbuf.at[slot, pl.ds(0, n)], sem.at[slot]).wait()
        xb = _from_slabs(xbuf.at[slot], TME).astype(BF16)
        hg = jnp.dot(xb, wg_s[...], preferred_element_type=F32)
        hu = jnp.dot(xb, wu_s[...], preferred_element_type=F32)
        hid = jax.nn.silu(hg) * hu
        _to_slabs(o_ref, jnp.dot(hid.astype(BF16), wd_s[...], preferred_element_type=F32))

    @pl.when(i >= nused)
    def _():
        o_ref[...] = jnp.zeros_like(o_ref)


def _experts(tile_expert, tile_rows, n_used, pos, xn, wg, wu, wd):
    wspec = lambda r, c: pl.BlockSpec((1, r, c), lambda i, te, nr, nu, pos: (te[i], 0, 0))
    grid_spec = pltpu.PrefetchScalarGridSpec(
        num_scalar_prefetch=4,
        grid=(N_ETILES,),
        in_specs=[pl.BlockSpec(memory_space=pl.ANY), wspec(D_MODEL, D_EXPERT),
                  wspec(D_MODEL, D_EXPERT), wspec(D_EXPERT, D_MODEL)],
        out_specs=pl.BlockSpec((TME * SLAB_ROWS, LANES), lambda i, te, nr, nu, pos: (i, 0)),
        scratch_shapes=[
            pltpu.VMEM((2, TME * SLAB_ROWS, LANES), F32),
            pltpu.VMEM((D_MODEL, D_EXPERT), BF16),
            pltpu.VMEM((D_MODEL, D_EXPERT), BF16),
            pltpu.VMEM((D_EXPERT, D_MODEL), BF16),
            pltpu.SMEM((N_PAD,), I32),
            pltpu.SemaphoreType.DMA((2,)),
        ],
    )
    return pl.pallas_call(
        _expert_kernel,
        grid_spec=grid_spec,
        out_shape=jax.ShapeDtypeStruct((N_PAD * SLAB_ROWS, LANES), F32),
        compiler_params=_cparams(("arbitrary",), 40),
        name="routed_experts",
    )(tile_expert, tile_rows, n_used, pos, xn, wg, wu, wd)


def _combine_kernel(pos_ref, h_ref, gw_ref, o_hbm, g_ref, yp_ref, ys_ref, buf, sem):
    i = pl.program_id(0)
    cur = i % 2

    def slab(k):
        return pl.ds(pl.multiple_of(k * SLAB_ROWS, SLAB_ROWS), SLAB_ROWS)

    def row_copy(tile, r, bs, s):
        p = pos_ref[(tile * TM + r) * 2 + s]
        return pltpu.make_async_copy(o_hbm.at[slab(p)], buf.at[bs, s, slab(r)], sem.at[bs, s])

    def issue(tile, bs):
        def body(r, _):
            row_copy(tile, r, bs, 0).start(priority=0)
            row_copy(tile, r, bs, 1).start(priority=1)
            return 0
        lax.fori_loop(0, TM, body, 0, unroll=2)

    @pl.when(i == 0)
    def _():
        issue(0, 0)

    @pl.when(i + 1 < N_TILES)
    def _():
        issue(i + 1, 1 - cur)

    for s in range(2):
        pltpu.make_async_copy(o_hbm.at[pl.ds(0, TM * SLAB_ROWS)], buf.at[cur, s], sem.at[cur, s]).wait()
    gw = gw_ref[...]
    y = h_ref[...] + (gw[:, 0:1] * _from_slabs(buf.at[cur, 0], TM)
                      + gw[:, 1:2] * _from_slabs(buf.at[cur, 1], TM))
    y = _rmsnorm_rows(y, g_ref[...])

    @pl.when(i < N_PROMPT_TILES)
    def _():
        yp_ref[...] = y

    @pl.when(i >= N_PROMPT_TILES)
    def _():
        ys_ref[...] = y


def _combine(pos, h, gw, o, g):
    yps, yss = _pair_specs(TM, D_MODEL, N_PROMPT_TILES)
    grid_spec = pltpu.PrefetchScalarGridSpec(
        num_scalar_prefetch=1,
        grid=(N_TILES,),
        in_specs=[
            pl.BlockSpec((TM, D_MODEL), lambda i, pos: (i, 0)),
            pl.BlockSpec((TM, ROUTE_LANES), lambda i, pos: (i, 0)),
            pl.BlockSpec(memory_space=pl.ANY),
            pl.BlockSpec((1, D_MODEL), lambda i, pos: (0, 0)),
        ],
        out_specs=[yps, yss],
        scratch_shapes=[pltpu.VMEM((2, 2, TM * SLAB_ROWS, LANES), F32), pltpu.SemaphoreType.DMA((2, 2))],
    )
    return pl.pallas_call(
        _combine_kernel,
        grid_spec=grid_spec,
        out_shape=(jax.ShapeDtypeStruct((N_PROMPT_TOK, D_MODEL), F32),
                   jax.ShapeDtypeStruct((N_SAMPLE_TOK, D_MODEL), F32)),
        compiler_params=_cparams(("arbitrary",), 48),
        name="combine_norm",
    )(pos, h, gw, o, g)


def _dispatch_plan(route, cnt):
    experts = jnp.arange(N_EXPERTS, dtype=I32)
    cnt = cnt[:, 0, :N_EXPERTS]
    counts = jnp.sum(cnt, axis=0)
    before_tile = jnp.cumsum(cnt, axis=0) - cnt
    padded = ((counts + TME - 1) // TME) * TME
    ends = jnp.cumsum(padded)
    starts = ends - padded
    base = jnp.repeat(starts[None, :] + before_tile, TMD, axis=0)
    eid = route[:, 0:2]
    onehot = eid[:, :, None] == experts[None, None, :]
    pos = jnp.sum(jnp.where(onehot, base[:, None, :], 0), axis=-1) + route[:, 2:4]
    n_used = (ends[-1] // TME).astype(I32)
    tile_start = jnp.arange(N_ETILES, dtype=I32) * TME
    te = jnp.sum((ends[None, :] <= tile_start[:, None]).astype(I32), axis=1)
    te = jnp.minimum(te, N_EXPERTS - 1)
    mine = te[:, None] == experts[None, :]
    left = jnp.sum(jnp.where(mine, (starts + counts)[None, :] - tile_start[:, None], 0), axis=1)
    tile_rows = jnp.clip(left, 0, TME)
    last = jnp.sum(jnp.where(jnp.arange(N_ETILES) == n_used - 1, te, 0))
    te = jnp.where(jnp.arange(N_ETILES) < n_used, te, last)
    return te.astype(I32), tile_rows.astype(I32), n_used.reshape(1), pos.reshape(-1).astype(I32)


def kernel(x_prompt, x_sample, state_ssm_re, state_ssm_im, cache_conv, norm_mix_g, w_in, lam_re, lam_im, log_dt, b_re, b_im, c_re, c_im, d_skip, w_ssm_glu, conv_w, conv_b, conv_ln_g, conv_ln_b, w_conv_out, w_out, norm_ffn_g, w_router_group, b_router_group, w_router_expert, b_router_expert, w_exp_gate, w_exp_up, w_exp_down, norm_final_g):
    assert w_in.shape[0] == 1, "single-layer trunk"
    xp = x_prompt.reshape(N_PROMPT_TOK, D_MODEL)
    xs = x_sample.reshape(N_SAMPLE_TOK, D_MODEL)
    w_in_bf = w_in[0].astype(BF16)
    row = lambda a: a.reshape(1, -1)
    g_mix = row(norm_mix_g[0])

    ar_rep, ai_rep, bbr, bbi = _discretise(lam_re[0], lam_im[0], log_dt[0], b_re[0], b_im[0])
    gph = lambda a: a.reshape(N_SSM_GROUPS, SSM_STATE, SSM_GROUP)
    pick = lambda a: gph(a)[:, :, 0].reshape(GROUP_BLOCKS, 1, GB_STATE)
    bm = jnp.concatenate([_block_diag_in(gph(bbr)), _block_diag_in(gph(bbi))], axis=-1).astype(BF16)
    cm = jnp.concatenate([_block_diag_out(c_re[0]), -_block_diag_out(c_im[0])], axis=1).astype(BF16)
    dsk = d_skip[0].reshape(GROUP_BLOCKS, 1, GB_IN)
    s0r = state_ssm_re[0].reshape(N_SAMPLE_SEQ, -1)
    s0i = state_ssm_im[0].reshape(N_SAMPLE_SEQ, -1)

    u, v = _inproj(xp, xs, g_mix, w_in_bf)
    actp, acts, pfr, pfi, sfr, sfi = _scan(u, bm, cm, pick(ar_rep), pick(ai_rep), dsk, s0r, s0i)

    hist_s = cache_conv[0].transpose(1, 0, 2).reshape(HIST_S, D_CONV)
    cvp, cvs, ncp, ncs = _conv(v, hist_s, conv_w[0], row(conv_b[0]), conv_ln_g[0], conv_ln_b[0])

    m = _merge(xp, xs, g_mix, actp.reshape(N_PROMPT_TOK, D_SSM), acts,
               cvp.reshape(N_PROMPT_TOK, D_CONV), cvs,
               w_in_bf, w_ssm_glu[0].astype(BF16), w_conv_out[0].astype(BF16))

    pad_lanes = ROUTE_LANES - N_EXP_GROUPS - N_EXPERTS
    w_router = jnp.concatenate(
        [w_router_group[0], w_router_expert[0], jnp.zeros((D_MODEL, pad_lanes), F32)], axis=1)
    b_router = jnp.concatenate(
        [b_router_group[0], b_router_expert[0], jnp.zeros((pad_lanes,), F32)]).reshape(1, ROUTE_LANES)
    h, xn2, route, gw, cnt = _outproj(m, xp, xs, w_out[0].astype(BF16), row(norm_ffn_g[0]),
                                      w_router, b_router)

    te, tile_rows, n_used, pos = _dispatch_plan(route, cnt)
    o = _experts(te, tile_rows, n_used, pos, xn2, w_exp_gate[0], w_exp_up[0], w_exp_down[0])
    yp, ys = _combine(pos, h, gw, o, row(norm_final_g))

    st = lambda a, n: a.reshape(1, n, N_SSM_GROUPS, SSM_STATE)
    ncp = ncp.reshape(CONV_BUF, N_PROMPT_SEQ, D_CONV).transpose(1, 0, 2)[None]
    ncs = ncs.reshape(CONV_BUF, N_SAMPLE_SEQ, D_CONV).transpose(1, 0, 2)[None]
    return (yp.reshape(N_PROMPT_SEQ, PROMPT_LEN, D_MODEL), ys.reshape(N_SAMPLE_SEQ, SAMPLE_LEN, D_MODEL),
            st(pfr, N_PROMPT_SEQ), st(pfi, N_PROMPT_SEQ), ncp,
            st(sfr, N_SAMPLE_SEQ), st(sfi, N_SAMPLE_SEQ), ncs)
```

```python
import functools

import jax
import jax.numpy as jnp
from jax import lax
from jax.experimental import pallas as pl
from jax.experimental.pallas import tpu as pltpu

F32 = jnp.float32
BF16 = jnp.bfloat16
I32 = jnp.int32

SUBLANES = 8
LANES = 128

D_MODEL = 2048
D_SSM = 1024
D_CONV = 1024
SSM_GROUP = 16
N_SSM_GROUPS = 64
SSM_STATE = 64
CONV_WIDTH = 31
CONV_BUF = CONV_WIDTH - 1
N_EXP_GROUPS = 4
EXP_PER_GROUP = 8
N_EXPERTS = 32
D_EXPERT = 256
EPS = 1e-6

N_PROMPT_SEQ = 4
PROMPT_LEN = 2048
N_SAMPLE_SEQ = 16
SAMPLE_LEN = 32
N_PROMPT_TOK = N_PROMPT_SEQ * PROMPT_LEN
N_SAMPLE_TOK = N_SAMPLE_SEQ * SAMPLE_LEN
N_TOK = N_PROMPT_TOK + N_SAMPLE_TOK

TM = 512
N_PROMPT_TILES = N_PROMPT_TOK // TM
N_TILES = N_TOK // TM
PROMPT_STEPS = TM // N_PROMPT_SEQ
GROUP_BLOCKS = 4
GROUPS_PER_BLOCK = N_SSM_GROUPS // GROUP_BLOCKS
GB_IN = GROUPS_PER_BLOCK * SSM_GROUP
GB_STATE = GROUPS_PER_BLOCK * SSM_STATE
GB_LB = GB_STATE // LANES
HIST_P = CONV_BUF * N_PROMPT_SEQ
HIST_S = CONV_BUF * N_SAMPLE_SEQ
ROUTE_LANES = 128
N_PAIRS = 2 * N_TOK
TME = 256
N_PAD = N_PAIRS + N_EXPERTS * TME
N_ETILES = N_PAD // TME
SLAB_ROWS = D_MODEL // LANES
MODEL_LB = D_MODEL // LANES
SSM_LB = D_SSM // LANES
CONV_LB = D_CONV // LANES


def _cparams(sem, vmem_mb):
    return pltpu.CompilerParams(dimension_semantics=sem, vmem_limit_bytes=vmem_mb * 1024 * 1024)


def _resident(shape):
    return pl.BlockSpec(shape, lambda *_: (0,) * len(shape), pipeline_mode=pl.Buffered(1))


def _x_tile_specs():
    blocks_per_seq = PROMPT_LEN // PROMPT_STEPS

    def prompt(b):
        return pl.BlockSpec(
            (PROMPT_STEPS, D_MODEL),
            lambda i, *_: (b * blocks_per_seq + jnp.minimum(i, N_PROMPT_TILES - 1), 0))

    return [prompt(b) for b in range(N_PROMPT_SEQ)] + [_resident((N_SAMPLE_TOK, D_MODEL))]


def _to_mixer_order(ref3, val, n_seq, steps):
    for c in range(val.shape[1] // LANES):
        for b in range(n_seq):
            ref3[c, pl.ds(b, steps, stride=n_seq), :] = val[b * steps:(b + 1) * steps,
                                                            c * LANES:(c + 1) * LANES]


def _seq_rows(ref3, b, n_seq, steps):
    return jnp.concatenate(
        [ref3[c, pl.ds(b, steps, stride=n_seq), :] for c in range(ref3.shape[0])], axis=1)


def _tile_to_mixer_order(i, ref3, xs, sample):
    @pl.when(i < N_PROMPT_TILES)
    def _():
        _to_mixer_order(ref3, xs(), N_PROMPT_SEQ, PROMPT_STEPS)

    @pl.when(i >= N_PROMPT_TILES)
    def _():
        _to_mixer_order(ref3, sample(), N_SAMPLE_SEQ, SAMPLE_LEN)


def _lane_blocks(ref3):
    return jnp.concatenate([ref3[c] for c in range(ref3.shape[0])], axis=1)


def _pair_specs(rows, cols, n_prompt_tiles):
    p = pl.BlockSpec((rows, cols), lambda i, *_: (jnp.minimum(i, n_prompt_tiles - 1), 0))
    s = pl.BlockSpec((rows, cols), lambda i, *_: (jnp.maximum(i - n_prompt_tiles, 0), 0))
    return p, s


def _disc_kernel(lr_ref, li_ref, ldt_ref, br_ref, bi_ref, ar_ref, ai_ref, bbr_ref, bbi_ref):
    lr = lr_ref[...]
    li = li_ref[...]
    dt = jnp.exp(ldt_ref[...])
    mag = jnp.exp(lr * dt)
    ar = mag * jnp.cos(li * dt)
    ai = mag * jnp.sin(li * dt)
    den = lr * lr + li * li
    nr = ar - 1.0
    cr = (nr * lr + ai * li) / den
    ci = (ai * lr - nr * li) / den
    br = br_ref[...]
    bi = bi_ref[...]
    ar_ref[...] = ar
    ai_ref[...] = ai
    bbr_ref[...] = cr * br - ci * bi
    bbi_ref[...] = cr * bi + ci * br


def _discretise(lam_re, lam_im, log_dt, b_re, b_im):
    shp = jax.ShapeDtypeStruct((N_SSM_GROUPS, SSM_STATE * SSM_GROUP), F32)
    rep = lambda a: jnp.repeat(a, SSM_GROUP, axis=-1)
    ldt = jnp.broadcast_to(log_dt[:, None], (N_SSM_GROUPS, SSM_STATE * SSM_GROUP))
    return pl.pallas_call(_disc_kernel, out_shape=(shp, shp, shp, shp), name="s5_discretise")(
        rep(lam_re), rep(lam_im), ldt,
        b_re.reshape(N_SSM_GROUPS, -1), b_im.reshape(N_SSM_GROUPS, -1))


def _block_diag_in(bb):
    b4 = bb.reshape(GROUP_BLOCKS, GROUPS_PER_BLOCK, SSM_STATE, SSM_GROUP).transpose(0, 1, 3, 2)
    eye = jnp.eye(GROUPS_PER_BLOCK, dtype=bool)[None, :, None, :, None]
    full = jnp.where(eye, b4[:, :, :, None, :], 0.0)
    return full.reshape(GROUP_BLOCKS, GB_IN, GB_STATE)


def _block_diag_out(c):
    c4 = c.reshape(GROUP_BLOCKS, GROUPS_PER_BLOCK, SSM_GROUP, SSM_STATE).transpose(0, 1, 3, 2)
    eye = jnp.eye(GROUPS_PER_BLOCK, dtype=bool)[None, :, None, :, None]
    full = jnp.where(eye, c4[:, :, :, None, :], 0.0)
    return full.reshape(GROUP_BLOCKS, GB_STATE, GB_IN)


def _rmsnorm_rows(x, g):
    r = lax.rsqrt(jnp.mean(x * x, axis=-1, keepdims=True) + EPS)
    return x * r * g


def _inproj_kernel(x0_ref, x1_ref, x2_ref, x3_ref, xs_ref, g_ref, w_ref,
                   u_ref, v_ref, xn_ref, xn_s, va_s):
    i = pl.program_id(0)
    j = pl.program_id(1)
    prompt_rows = lambda: jnp.concatenate(
        [x0_ref[...], x1_ref[...], x2_ref[...], x3_ref[...]], axis=0)

    @pl.when(j == 0)
    def _():
        def norm(x):
            xn = _rmsnorm_rows(x, g_ref[...])
            xn_s[...] = xn.astype(BF16)
            return xn
        _tile_to_mixer_order(i, xn_ref, lambda: norm(prompt_rows()), lambda: norm(xs_ref[...]))

    p = jnp.dot(xn_s[...], w_ref[...], preferred_element_type=F32)

    @pl.when(j == 0)
    def _():
        _tile_to_mixer_order(i, u_ref, lambda: p, lambda: p)

    @pl.when(j == 1)
    def _():
        va_s[...] = p

    @pl.when(j == 2)
    def _():
        v = va_s[...] * jax.nn.sigmoid(p)
        _tile_to_mixer_order(i, v_ref, lambda: v, lambda: v)


def _inproj(xp, xs, g, w_in_bf):
    blocked = lambda nb: pl.BlockSpec((nb, TM, LANES), lambda i, j: (0, i, 0))
    return pl.pallas_call(
        _inproj_kernel,
        grid=(N_TILES, 3),
        in_specs=_x_tile_specs() + [
            pl.BlockSpec((1, D_MODEL), lambda i, j: (0, 0)),
            pl.BlockSpec((D_MODEL, D_SSM), lambda i, j: (0, j)),
        ],
        out_specs=[blocked(SSM_LB), blocked(CONV_LB), blocked(MODEL_LB)],
        out_shape=(jax.ShapeDtypeStruct((SSM_LB, N_TOK, LANES), F32),
                   jax.ShapeDtypeStruct((CONV_LB, N_TOK, LANES), F32),
                   jax.ShapeDtypeStruct((MODEL_LB, N_TOK, LANES), F32)),
        scratch_shapes=[pltpu.VMEM((TM, D_MODEL), BF16), pltpu.VMEM((TM, D_CONV), F32)],
        compiler_params=_cparams(("arbitrary", "arbitrary"), 52),
        name="in_proj",
    )(xp, xp, xp, xp, xs, g, w_in_bf)


N_SCAN_TILES = GROUP_BLOCKS * N_PROMPT_TILES
SCAN_STAGES = 3


def _scan_prompt_kernel(ua_ref, uc_ref, bm_ref, cm_ref, ar_ref, ai_ref, dsk_ref,
                        act_ref, pfr_ref, pfi_ref, x0, x1, x2, cst, car, fin):
    n = pl.program_id(0)
    q_s = jnp.clip(n - 1, 0, N_SCAN_TILES - 1)
    t_s = q_s % N_PROMPT_TILES
    valid_s = jnp.logical_and(n >= 1, n <= N_SCAN_TILES)

    @pl.when(n == 0)
    def _():
        for buf in (x0, x1, x2):
            buf[...] = jnp.zeros_like(buf)
        car[...] = jnp.zeros_like(car)
        fin[...] = jnp.zeros_like(fin)

    shape = (SUBLANES, GB_STATE)
    top = lax.broadcasted_iota(I32, shape, 0) < N_PROMPT_SEQ
    arb = jnp.broadcast_to(ar_ref[0], shape)
    aib = jnp.broadcast_to(ai_ref[0], shape)
    cst[0] = jnp.where(top, 0.0, arb)
    cst[1] = jnp.where(top, 0.0, aib)
    cst[2] = jnp.where(top, arb, arb * arb - aib * aib)
    cst[3] = jnp.where(top, aib, 2.0 * arb * aib)

    def project_in(xa):
        u = jnp.concatenate([ua_ref[0], ua_ref[1]], axis=1).astype(BF16)
        x = jnp.dot(u, bm_ref[0], preferred_element_type=F32)
        for c in range(2 * GB_LB):
            xa[c] = x[:, c * LANES:(c + 1) * LANES]

    def project_out(xc):
        u = jnp.concatenate([uc_ref[0], uc_ref[1]], axis=1)
        h = _lane_blocks(xc).astype(BF16)
        y = jnp.dot(h, cm_ref[0], preferred_element_type=F32) + dsk_ref[0] * u
        act_ref[...] = jax.nn.gelu(y).astype(BF16)

    def recur(xs):
        top1 = lax.broadcasted_iota(I32, (SUBLANES, LANES), 0) < N_PROMPT_SEQ
        first = t_s == 0
        hr = [jnp.where(first, 0.0, car[:, c * LANES:(c + 1) * LANES]) for c in range(GB_LB)]
        hi = [jnp.where(first, 0.0, car[:, GB_STATE + c * LANES:GB_STATE + (c + 1) * LANES])
              for c in range(GB_LB)]
        for r in range(TM // SUBLANES):
            rows = slice(r * SUBLANES, (r + 1) * SUBLANES)
            for c in range(GB_LB):
                cols = slice(c * LANES, (c + 1) * LANES)
                xr = xs[c, rows, :]
                xi = xs[GB_LB + c, rows, :]
                a1r, a1i, a2r, a2i = cst[0, :, cols], cst[1, :, cols], cst[2, :, cols], cst[3, :, cols]
                sxr = pltpu.roll(xr, N_PROMPT_SEQ, axis=0)
                sxi = pltpu.roll(xi, N_PROMPT_SEQ, axis=0)
                tr = xr + a1r * sxr - a1i * sxi
                ti = xi + a1r * sxi + a1i * sxr
                nr = tr + a2r * hr[c] - a2i * hi[c]
                ni = ti + a2r * hi[c] + a2i * hr[c]
                xs[c, rows, :] = nr
                xs[GB_LB + c, rows, :] = ni
                hr[c] = jnp.where(top1, pltpu.roll(nr, N_PROMPT_SEQ, axis=0), nr)
                hi[c] = jnp.where(top1, pltpu.roll(ni, N_PROMPT_SEQ, axis=0), ni)
        state = jnp.concatenate(hr + hi, axis=1)
        car[...] = state
        last = jnp.logical_and(valid_s, t_s == N_PROMPT_TILES - 1)
        fin[...] = jnp.where(last, state, fin[...])
        pfr_ref[0] = fin[N_PROMPT_SEQ:SUBLANES, 0:GB_STATE]
        pfi_ref[0] = fin[N_PROMPT_SEQ:SUBLANES, GB_STATE:2 * GB_STATE]

    bufs = (x0, x1, x2)
    for r in range(SCAN_STAGES):
        @pl.when(n % SCAN_STAGES == r)
        def _():
            project_out(bufs[(r + 1) % SCAN_STAGES])
            recur(bufs[(r + 2) % SCAN_STAGES])
            project_in(bufs[r])


def _scan_prompt(u3, bm, cm, ar, ai, dsk):
    q_a = lambda n: jnp.minimum(n, N_SCAN_TILES - 1)
    q_s = lambda n: jnp.clip(n - 1, 0, N_SCAN_TILES - 1)
    q_c = lambda n: jnp.clip(n - 2, 0, N_SCAN_TILES - 1)
    gb = lambda q: q // N_PROMPT_TILES
    tt = lambda q: q % N_PROMPT_TILES
    u_spec = lambda q: pl.BlockSpec((GB_IN // LANES, TM, LANES), lambda n: (gb(q(n)), tt(q(n)), 0))
    gb3 = lambda last, q: pl.BlockSpec((1,) + last, lambda n: (gb(q(n)), 0, 0))
    state = jax.ShapeDtypeStruct((GROUP_BLOCKS, N_PROMPT_SEQ, GB_STATE), F32)
    xbuf = pltpu.VMEM((2 * GB_LB, TM, LANES), F32)
    return pl.pallas_call(
        _scan_prompt_kernel,
        grid=(N_SCAN_TILES + SCAN_STAGES - 1,),
        in_specs=[
            u_spec(q_a), u_spec(q_c),
            gb3((GB_IN, 2 * GB_STATE), q_a),
            gb3((2 * GB_STATE, GB_IN), q_c),
            gb3((1, GB_STATE), q_s),
            gb3((1, GB_STATE), q_s),
            gb3((1, GB_IN), q_c),
        ],
        out_specs=[
            pl.BlockSpec((TM, GB_IN), lambda n: (tt(q_c(n)), gb(q_c(n)))),
            gb3((N_PROMPT_SEQ, GB_STATE), q_s),
            gb3((N_PROMPT_SEQ, GB_STATE), q_s),
        ],
        out_shape=(jax.ShapeDtypeStruct((N_PROMPT_TOK, D_SSM), BF16), state, state),
        scratch_shapes=[xbuf, xbuf, xbuf,
                        pltpu.VMEM((4, SUBLANES, GB_STATE), F32),
                        pltpu.VMEM((SUBLANES, 2 * GB_STATE), F32),
                        pltpu.VMEM((SUBLANES, 2 * GB_STATE), F32)],
        compiler_params=_cparams(("arbitrary",), 48),
        name="s5_scan_prompt",
    )(u3, u3, bm, cm, ar, ai, dsk)


def _scan_sample_kernel(u_ref, bm_ref, cm_ref, ar_ref, ai_ref, dsk_ref, s0r_ref, s0i_ref,
                        act_ref, sfr_ref, sfi_ref, xs_ref):
    half_lb = GB_LB // 2
    half_cols = half_lb * LANES
    u = jnp.concatenate([u_ref[0], u_ref[1]], axis=1)
    x = jnp.dot(u.astype(BF16), bm_ref[0], preferred_element_type=F32)
    for c in range(2 * GB_LB):
        xs_ref[c] = x[:, c * LANES:(c + 1) * LANES]

    def load_half(rows, half, imag):
        c0 = imag * GB_LB + half * half_lb
        return jnp.concatenate([xs_ref[c0 + c, rows, :] for c in range(half_lb)], axis=1)

    def store_half(rows, half, imag, val):
        c0 = imag * GB_LB + half * half_lb
        for c in range(half_lb):
            xs_ref[c0 + c, rows, :] = val[:, c * LANES:(c + 1) * LANES]

    shape = (N_SAMPLE_SEQ, half_cols)
    for half in range(2):
        cols = slice(half * half_cols, (half + 1) * half_cols)
        arb = jnp.broadcast_to(ar_ref[0][:, cols], shape)
        aib = jnp.broadcast_to(ai_ref[0][:, cols], shape)

        def body(k, carry):
            hr, hi = carry
            rows = pl.ds(pl.multiple_of(k * N_SAMPLE_SEQ, N_SAMPLE_SEQ), N_SAMPLE_SEQ)
            xr = load_half(rows, half, 0)
            xi = load_half(rows, half, 1)
            nhr = xr + arb * hr - aib * hi
            nhi = xi + arb * hi + aib * hr
            store_half(rows, half, 0, nhr)
            store_half(rows, half, 1, nhi)
            return nhr, nhi

        hr, hi = lax.fori_loop(0, SAMPLE_LEN, body, (s0r_ref[:, cols], s0i_ref[:, cols]))
        sfr_ref[:, cols] = hr
        sfi_ref[:, cols] = hi

    h = _lane_blocks(xs_ref).astype(BF16)
    y = jnp.dot(h, cm_ref[0], preferred_element_type=F32) + dsk_ref[0] * u
    act_ref[...] = jax.nn.gelu(y).astype(BF16)


def _scan_sample(u3, bm, cm, ar, ai, dsk, s0r, s0i):
    gb3 = lambda last: pl.BlockSpec((1,) + last, lambda g: (g, 0, 0))
    state = lambda: pl.BlockSpec((N_SAMPLE_SEQ, GB_STATE), lambda g: (0, g))
    sst = jax.ShapeDtypeStruct((N_SAMPLE_SEQ, N_SSM_GROUPS * SSM_STATE), F32)
    return pl.pallas_call(
        _scan_sample_kernel,
        grid=(GROUP_BLOCKS,),
        in_specs=[
            pl.BlockSpec((GB_IN // LANES, TM, LANES), lambda g: (g, N_PROMPT_TILES, 0)),
            gb3((GB_IN, 2 * GB_STATE)), gb3((2 * GB_STATE, GB_IN)),
            gb3((1, GB_STATE)), gb3((1, GB_STATE)), gb3((1, GB_IN)),
            state(), state(),
        ],
        out_specs=[pl.BlockSpec((TM, GB_IN), lambda g: (0, g)), state(), state()],
        out_shape=(jax.ShapeDtypeStruct((N_SAMPLE_TOK, D_SSM), BF16), sst, sst),
        scratch_shapes=[pltpu.VMEM((2 * GB_LB, TM, LANES), F32)],
        compiler_params=_cparams(("arbitrary",), 32),
        name="s5_scan_sample",
    )(u3, bm, cm, ar, ai, dsk, s0r, s0i)


CONV_ROWS = 64
EXT_ROWS = HIST_S + TM


def _conv_kernel(v_ref, hist_ref, w_ref, b_ref, lg_ref, lb_ref,
                 cv_ref, ncp_ref, ncs_ref, ext_ref, sh_ref, acc_ref):
    t = pl.program_id(0)

    def taps(n_seq):
        half = SUBLANES // 2
        if n_seq % SUBLANES:
            lo = HIST_S - HIST_P - SUBLANES
            sh_ref[:, lo:EXT_ROWS - SUBLANES, :] = ext_ref[:, lo + half:EXT_ROWS - half, :]

        def body(rb, _):
            r0 = pl.multiple_of(rb * CONV_ROWS, CONV_ROWS)
            for cb in range(CONV_LB):
                cols = slice(cb * LANES, (cb + 1) * LANES)
                acc = None
                for j in range(CONV_WIDTH):
                    start = HIST_S - (CONV_BUF - j) * n_seq
                    src = ext_ref
                    if start % SUBLANES:
                        src, start = sh_ref, start - half
                    rows = pl.ds(pl.multiple_of(r0 + start, SUBLANES), CONV_ROWS)
                    term = src[cb, rows, :] * w_ref[j:j + 1, cols]
                    acc = term if acc is None else acc + term
                acc_ref[cb, pl.ds(r0, CONV_ROWS), :] = acc + b_ref[:, cols]
            return 0

        lax.fori_loop(0, TM // CONV_ROWS, body, 0)

    @pl.when(t == 0)
    def _():
        ext_ref[:, 0:HIST_S, :] = jnp.zeros((CONV_LB, HIST_S, LANES), F32)

    @pl.when(t == N_PROMPT_TILES)
    def _():
        for c in range(CONV_LB):
            ext_ref[c, 0:HIST_S, :] = hist_ref[:, c * LANES:(c + 1) * LANES]

    ext_ref[:, HIST_S:EXT_ROWS, :] = v_ref[...]

    @pl.when(t < N_PROMPT_TILES)
    def _():
        taps(N_PROMPT_SEQ)

    @pl.when(t == N_PROMPT_TILES)
    def _():
        taps(N_SAMPLE_SEQ)

    y = acc_ref[...]
    mu = jnp.sum(jnp.sum(y, axis=0), axis=-1, keepdims=True) / D_CONV
    yc = y - mu[None]
    var = jnp.sum(jnp.sum(yc * yc, axis=0), axis=-1, keepdims=True) / D_CONV
    z = jax.nn.silu(yc * lax.rsqrt(var + EPS)[None] * lg_ref[...] + lb_ref[...])
    for c in range(CONV_LB):
        cv_ref[:, c * LANES:(c + 1) * LANES] = z[c].astype(BF16)

    @pl.when(t == N_PROMPT_TILES - 1)
    def _():
        for c in range(CONV_LB):
            ncp_ref[:, c * LANES:(c + 1) * LANES] = ext_ref[c, EXT_ROWS - HIST_P:EXT_ROWS, :]

    @pl.when(t < N_PROMPT_TILES - 1)
    def _():
        ext_ref[:, HIST_S - HIST_P:HIST_S, :] = ext_ref[:, EXT_ROWS - HIST_P:EXT_ROWS, :]

    @pl.when(t == N_PROMPT_TILES)
    def _():
        for c in range(CONV_LB):
            ncs_ref[:, c * LANES:(c + 1) * LANES] = ext_ref[c, EXT_ROWS - HIST_S:EXT_ROWS, :]


def _conv(v3, hist_s, conv_w, conv_b, ln_g, ln_b):
    row = lambda n: pl.BlockSpec((n, D_CONV), lambda t: (0, 0))
    lane3 = pl.BlockSpec((CONV_LB, 1, LANES), lambda t: (0, 0, 0))
    ext = pltpu.VMEM((CONV_LB, EXT_ROWS, LANES), F32)
    return pl.pallas_call(
        _conv_kernel,
        grid=(N_TILES,),
        in_specs=[pl.BlockSpec((CONV_LB, TM, LANES), lambda t: (0, t, 0)),
                  row(HIST_S), row(CONV_WIDTH), row(1), lane3, lane3],
        out_specs=[pl.BlockSpec((TM, D_CONV), lambda t: (t, 0)), row(HIST_P), row(HIST_S)],
        out_shape=(jax.ShapeDtypeStruct((N_TOK, D_CONV), BF16),
                   jax.ShapeDtypeStruct((HIST_P, D_CONV), F32),
                   jax.ShapeDtypeStruct((HIST_S, D_CONV), F32)),
        scratch_shapes=[ext, ext, pltpu.VMEM((CONV_LB, TM, LANES), F32)],
        compiler_params=_cparams(("arbitrary",), 48),
        name="conv_branch",
    )(v3, hist_s, conv_w, conv_b,
      ln_g.reshape(CONV_LB, 1, LANES), ln_b.reshape(CONV_LB, 1, LANES))


MERGE_COLS = 512
N_MERGE = D_MODEL // MERGE_COLS


def _merge_kernel(xn_ref, actp_ref, acts_ref, cv_ref, wga_ref, wgb_ref, wg1_ref, wg2_ref, wco_ref,
                  m_ref, xn_s, act_s):
    i = pl.program_id(0)
    j = pl.program_id(1)

    @pl.when(j == 0)
    def _():
        xn_s[...] = _lane_blocks(xn_ref).astype(BF16)

        @pl.when(i < N_PROMPT_TILES)
        def _():
            act_s[...] = actp_ref[...]

        @pl.when(i >= N_PROMPT_TILES)
        def _():
            act_s[...] = acts_ref[...]

    xn = xn_s[...]
    act = act_s[...]
    dot = functools.partial(jnp.dot, preferred_element_type=F32)
    ya = dot(act, wg1_ref[...]) * jax.nn.sigmoid(dot(act, wg2_ref[...]))
    yb = dot(cv_ref[...], wco_ref[...])
    m = jax.nn.sigmoid(dot(xn, wga_ref[...])) * ya + jax.nn.sigmoid(dot(xn, wgb_ref[...])) * yb
    m_ref[...] = m.astype(BF16)


def _merge(xn3, actp, acts, cv, w_in_bf, w_glu_bf, w_co_bf):
    ga0 = (D_SSM + 2 * D_CONV) // MERGE_COLS
    gb0 = ga0 + N_MERGE
    return pl.pallas_call(
        _merge_kernel,
        grid=(N_TILES, N_MERGE),
        in_specs=[pl.BlockSpec((MODEL_LB, TM, LANES), lambda i, j: (0, i, 0))]
        + list(_pair_specs(TM, D_SSM, N_PROMPT_TILES)) + [
            pl.BlockSpec((TM, D_CONV), lambda i, j: (i, 0)),
            pl.BlockSpec((D_MODEL, MERGE_COLS), lambda i, j: (0, ga0 + j)),
            pl.BlockSpec((D_MODEL, MERGE_COLS), lambda i, j: (0, gb0 + j)),
            pl.BlockSpec((D_SSM, MERGE_COLS), lambda i, j: (0, j)),
            pl.BlockSpec((D_SSM, MERGE_COLS), lambda i, j: (0, N_MERGE + j)),
            pl.BlockSpec((D_CONV, MERGE_COLS), lambda i, j: (0, j)),
        ],
        out_specs=pl.BlockSpec((TM, MERGE_COLS), lambda i, j: (i, j)),
        out_shape=jax.ShapeDtypeStruct((N_TOK, D_MODEL), BF16),
        scratch_shapes=[pltpu.VMEM((TM, D_MODEL), BF16), pltpu.VMEM((TM, D_SSM), BF16)],
        compiler_params=_cparams(("arbitrary", "arbitrary"), 48),
        name="gated_merge",
    )(xn3, actp, acts, cv, w_in_bf, w_in_bf, w_glu_bf, w_glu_bf, w_co_bf)


def _to_slabs(ref, val):
    rows = val.shape[0]
    for c in range(SLAB_ROWS):
        ref[pl.ds(c, rows, stride=SLAB_ROWS), :] = val[:, c * LANES:(c + 1) * LANES]


def _from_slabs(ref, rows):
    return jnp.concatenate(
        [ref[pl.ds(c, rows, stride=SLAB_ROWS), :] for c in range(SLAB_ROWS)], axis=1)


def _outproj_kernel(m_ref, x0_ref, x1_ref, x2_ref, x3_ref, xs_ref, wo_ref, g_ref, wr_ref, br_ref,
                    h_ref, xn_ref, route_ref, gw_ref, cnt_ref, xr_s):
    i = pl.program_id(0)
    dot = functools.partial(jnp.dot, preferred_element_type=F32)
    _tile_to_mixer_order(
        i, xr_s,
        lambda: jnp.concatenate([x0_ref[...], x1_ref[...], x2_ref[...], x3_ref[...]], axis=0),
        lambda: xs_ref[...])
    h = _lane_blocks(xr_s) + dot(m_ref[...], wo_ref[...])
    h_ref[...] = h
    xn = _rmsnorm_rows(h, g_ref[...])
    _to_slabs(xn_ref, xn)
    x_hi = xn.astype(BF16)
    x_lo = (xn - x_hi.astype(F32)).astype(BF16)
    w = wr_ref[...]
    w_hi = w.astype(BF16)
    w_lo = (w - w_hi.astype(F32)).astype(BF16)
    logits = dot(x_hi, w_hi) + (dot(x_lo, w_hi) + dot(x_hi, w_lo)) + br_ref[...]

    lane = lax.broadcasted_iota(I32, logits.shape, 1)
    neg = -jnp.inf
    first = lambda hit: jnp.min(jnp.where(hit, lane, ROUTE_LANES), axis=-1, keepdims=True)
    gmask = lane < N_EXP_GROUPS
    lg = jnp.where(gmask, logits, neg)
    gmax = jnp.max(lg, axis=-1, keepdims=True)
    gsel = first(lg == gmax)
    psum = jnp.sum(jnp.where(gmask, jnp.exp(logits - gmax), 0.0), axis=-1, keepdims=True)
    pg_sel = 1.0 / psum
    e_lane = lane - N_EXP_GROUPS
    emask = (e_lane >= 0) & (e_lane < N_EXPERTS) & ((e_lane // EXP_PER_GROUP) == gsel)
    le = jnp.where(emask, logits, neg)
    v1 = jnp.max(le, axis=-1, keepdims=True)
    i1 = first(le == v1)
    le2 = jnp.where(lane == i1, neg, le)
    v2 = jnp.max(le2, axis=-1, keepdims=True)
    i2 = first(le2 == v2)
    z = jnp.exp(v2 - v1)
    w1 = pg_sel / (1.0 + z)
    w2 = pg_sel * z / (1.0 + z)
    e1 = i1 - N_EXP_GROUPS
    e2 = i2 - N_EXP_GROUPS
    gw_ref[...] = jnp.where(lane == 0, w1, jnp.where(lane == 1, w2, 0.0))
    oh1 = lane == e1
    oh2 = lane == e2
    hits = jnp.where(oh1 | oh2, 1.0, 0.0).astype(BF16)
    rr = lax.broadcasted_iota(I32, (TM, TM), 0)
    cc = lax.broadcasted_iota(I32, (TM, TM), 1)
    before = dot(jnp.where(cc < rr, 1.0, 0.0).astype(BF16), hits)
    rank1 = jnp.sum(jnp.where(oh1, before, 0.0), axis=-1, keepdims=True).astype(I32)
    rank2 = jnp.sum(jnp.where(oh2, before, 0.0), axis=-1, keepdims=True).astype(I32)
    route_ref[...] = jnp.where(lane == 0, e1, jnp.where(lane == 1, e2, jnp.where(
        lane == 2, rank1, jnp.where(lane == 3, rank2, 0))))
    cnt_ref[0] = jnp.sum(hits.astype(F32), axis=0, keepdims=True).astype(I32)


def _outproj(m, xp, xs, w_out_bf, g, w_router, b_router):
    tile = lambda n: pl.BlockSpec((TM, n), lambda i: (i, 0))
    return pl.pallas_call(
        _outproj_kernel,
        grid=(N_TILES,),
        in_specs=[tile(D_MODEL)] + _x_tile_specs() + [
            _resident((D_MODEL, D_MODEL)), _resident((1, D_MODEL)),
            _resident((D_MODEL, ROUTE_LANES)), _resident((1, ROUTE_LANES))],
        out_specs=[tile(D_MODEL),
                   pl.BlockSpec((TM * SLAB_ROWS, LANES), lambda i: (i, 0)),
                   tile(ROUTE_LANES), tile(ROUTE_LANES),
                   pl.BlockSpec((1, 1, ROUTE_LANES), lambda i: (i, 0, 0))],
        out_shape=(jax.ShapeDtypeStruct((N_TOK, D_MODEL), F32),
                   jax.ShapeDtypeStruct((N_TOK * SLAB_ROWS, LANES), F32),
                   jax.ShapeDtypeStruct((N_TOK, ROUTE_LANES), I32),
                   jax.ShapeDtypeStruct((N_TOK, ROUTE_LANES), F32),
                   jax.ShapeDtypeStruct((N_TILES, 1, ROUTE_LANES), I32)),
        scratch_shapes=[pltpu.VMEM((MODEL_LB, TM, LANES), F32)],
        compiler_params=_cparams(("arbitrary",), 56),
        name="out_proj_router",
    )(m, xp, xp, xp, xp, xs, w_out_bf, g, w_router, b_router)


def _expert_kernel(te_ref, nrows_ref, nused_ref, pos_ref, xn_hbm, wg_ref, wu_ref, wd_ref,
                   o_ref, xbuf, wg_s, wu_s, wd_s, tok_s, sem):
    i = pl.program_id(0)
    nused = nused_ref[0]
    slot = i % 2

    def n_issued(tile):
        return pl.multiple_of(((nrows_ref[tile] + SUBLANES - 1) // SUBLANES) * SUBLANES, SUBLANES)

    def slab(k):
        return pl.ds(pl.multiple_of(k * SLAB_ROWS, SLAB_ROWS), SLAB_ROWS)

    def row_copy(tile, r, s):
        tok = tok_s[tile * TME + r]
        return pltpu.make_async_copy(xn_hbm.at[slab(tok)], xbuf.at[s, slab(r)], sem.at[s])

    def issue(tile, s):
        def body(q, _):
            row_copy(tile, 2 * q, s).start(priority=0)
            row_copy(tile, 2 * q + 1, s).start(priority=1)
            return 0
        lax.fori_loop(0, n_issued(tile) // 2, body, 0)

    @pl.when(i == 0)
    def _():
        def build(p, _):
            tok_s[pos_ref[p]] = lax.shift_right_logical(p, 1)
            return 0
        lax.fori_loop(0, N_PAIRS, build, 0, unroll=8)

        def pad(tile, _):
            def zero(r, _):
                tok_s[tile * TME + r] = 0
                return 0
            lax.fori_loop(nrows_ref[tile], n_issued(tile), zero, 0)
            return 0
        lax.fori_loop(0, nused, pad, 0)
        xbuf[...] = jnp.zeros_like(xbuf)
        issue(0, 0)

    @pl.when(i + 1 < nused)
    def _():
        issue(i + 1, 1 - slot)

    @pl.when(i < nused)
    def _():
        changed = jnp.logical_or(i == 0, te_ref[i] != te_ref[jnp.maximum(i - 1, 0)])

        @pl.when(changed)
        def _():
            wg_s[...] = wg_ref[0].astype(BF16)
            wu_s[...] = wu_ref[0].astype(BF16)
            wd_s[...] = wd_ref[0].astype(BF16)

        n = pl.multiple_of(n_issued(i) * SLAB_ROWS, SUBLANES * SLAB_ROWS)
        pltpu.make_async_copy(xn_hbm.at[pl.ds(0, n)], xbuf.at[slot, pl.ds(0, n)], sem.at[slot]).wait()
        xb = _from_slabs(xbuf.at[slot], TME).astype(BF16)
        hg = jnp.dot(xb, wg_s[...], preferred_element_type=F32)
        hu = jnp.dot(xb, wu_s[...], preferred_element_type=F32)
        hid = jax.nn.silu(hg) * hu
        _to_slabs(o_ref, jnp.dot(hid.astype(BF16), wd_s[...], preferred_element_type=F32))

    @pl.when(i >= nused)
    def _():
        o_ref[...] = jnp.zeros_like(o_ref)


def _experts(tile_expert, tile_rows, n_used, pos, xn, wg, wu, wd):
    wspec = lambda r, c: pl.BlockSpec((1, r, c), lambda i, te, nr, nu, pos: (te[i], 0, 0))
    grid_spec = pltpu.PrefetchScalarGridSpec(
        num_scalar_prefetch=4,
        grid=(N_ETILES,),
        in_specs=[pl.BlockSpec(memory_space=pl.ANY), wspec(D_MODEL, D_EXPERT),
                  wspec(D_MODEL, D_EXPERT), wspec(D_EXPERT, D_MODEL)],
        out_specs=pl.BlockSpec((TME * SLAB_ROWS, LANES), lambda i, te, nr, nu, pos: (i, 0)),
        scratch_shapes=[
            pltpu.VMEM((2, TME * SLAB_ROWS, LANES), F32),
            pltpu.VMEM((D_MODEL, D_EXPERT), BF16),
            pltpu.VMEM((D_MODEL, D_EXPERT), BF16),
            pltpu.VMEM((D_EXPERT, D_MODEL), BF16),
            pltpu.SMEM((N_PAD,), I32),
            pltpu.SemaphoreType.DMA((2,)),
        ],
    )
    return pl.pallas_call(
        _expert_kernel,
        grid_spec=grid_spec,
        out_shape=jax.ShapeDtypeStruct((N_PAD * SLAB_ROWS, LANES), F32),
        compiler_params=_cparams(("arbitrary",), 40),
        name="routed_experts",
    )(tile_expert, tile_rows, n_used, pos, xn, wg, wu, wd)


def _combine_kernel(pos_ref, h_ref, gw_ref, o_hbm, g_ref, yp_ref, ys_ref, buf, y_s, sem):
    i = pl.program_id(0)
    cur = i % 2

    def slab(k):
        return pl.ds(pl.multiple_of(k * SLAB_ROWS, SLAB_ROWS), SLAB_ROWS)

    def row_copy(tile, r, bs, s):
        p = pos_ref[(tile * TM + r) * 2 + s]
        return pltpu.make_async_copy(o_hbm.at[slab(p)], buf.at[bs, s, slab(r)], sem.at[bs, s])

    def issue(tile, bs):
        def body(r, _):
            row_copy(tile, r, bs, 0).start(priority=0)
            row_copy(tile, r, bs, 1).start(priority=1)
            return 0
        lax.fori_loop(0, TM, body, 0, unroll=2)

    @pl.when(i == 0)
    def _():
        issue(0, 0)

    @pl.when(i + 1 < N_TILES)
    def _():
        issue(i + 1, 1 - cur)

    for s in range(2):
        pltpu.make_async_copy(o_hbm.at[pl.ds(0, TM * SLAB_ROWS)], buf.at[cur, s], sem.at[cur, s]).wait()
    gw = gw_ref[...]
    y = h_ref[...] + (gw[:, 0:1] * _from_slabs(buf.at[cur, 0], TM)
                      + gw[:, 1:2] * _from_slabs(buf.at[cur, 1], TM))
    y = _rmsnorm_rows(y, g_ref[...])
    for c in range(MODEL_LB):
        y_s[c] = y[:, c * LANES:(c + 1) * LANES]

    @pl.when(i < N_PROMPT_TILES)
    def _():
        for b in range(N_PROMPT_SEQ):
            yp_ref[b] = _seq_rows(y_s, b, N_PROMPT_SEQ, PROMPT_STEPS)

    @pl.when(i >= N_PROMPT_TILES)
    def _():
        for b in range(N_SAMPLE_SEQ):
            ys_ref[b * SAMPLE_LEN:(b + 1) * SAMPLE_LEN, :] = _seq_rows(y_s, b, N_SAMPLE_SEQ, SAMPLE_LEN)


def _combine(pos, h, gw, o, g):
    grid_spec = pltpu.PrefetchScalarGridSpec(
        num_scalar_prefetch=1,
        grid=(N_TILES,),
        in_specs=[
            pl.BlockSpec((TM, D_MODEL), lambda i, pos: (i, 0)),
            pl.BlockSpec((TM, ROUTE_LANES), lambda i, pos: (i, 0)),
            pl.BlockSpec(memory_space=pl.ANY),
            pl.BlockSpec((1, D_MODEL), lambda i, pos: (0, 0)),
        ],
        out_specs=[
            pl.BlockSpec((N_PROMPT_SEQ, PROMPT_STEPS, D_MODEL),
                         lambda i, pos: (0, jnp.minimum(i, N_PROMPT_TILES - 1), 0)),
            pl.BlockSpec((N_SAMPLE_TOK, D_MODEL), lambda i, pos: (0, 0)),
        ],
        scratch_shapes=[pltpu.VMEM((2, 2, TM * SLAB_ROWS, LANES), F32),
                        pltpu.VMEM((MODEL_LB, TM, LANES), F32),
                        pltpu.SemaphoreType.DMA((2, 2))],
    )
    return pl.pallas_call(
        _combine_kernel,
        grid_spec=grid_spec,
        out_shape=(jax.ShapeDtypeStruct((N_PROMPT_SEQ, PROMPT_LEN, D_MODEL), F32),
                   jax.ShapeDtypeStruct((N_SAMPLE_TOK, D_MODEL), F32)),
        compiler_params=_cparams(("arbitrary",), 52),
        name="combine_norm",
    )(pos, h, gw, o, g)


def _dispatch_plan(route, cnt):
    experts = jnp.arange(N_EXPERTS, dtype=I32)
    cnt = cnt[:, 0, :N_EXPERTS]
    counts = jnp.sum(cnt, axis=0)
    before_tile = jnp.cumsum(cnt, axis=0) - cnt
    padded = ((counts + TME - 1) // TME) * TME
    ends = jnp.cumsum(padded)
    starts = ends - padded
    base = jnp.repeat(starts[None, :] + before_tile, TM, axis=0)
    eid = route[:, 0:2]
    onehot = eid[:, :, None] == experts[None, None, :]
    pos = jnp.sum(jnp.where(onehot, base[:, None, :], 0), axis=-1) + route[:, 2:4]
    n_used = (ends[-1] // TME).astype(I32)
    tile_start = jnp.arange(N_ETILES, dtype=I32) * TME
    te = jnp.sum((ends[None, :] <= tile_start[:, None]).astype(I32), axis=1)
    te = jnp.minimum(te, N_EXPERTS - 1)
    mine = te[:, None] == experts[None, :]
    left = jnp.sum(jnp.where(mine, (starts + counts)[None, :] - tile_start[:, None], 0), axis=1)
    tile_rows = jnp.clip(left, 0, TME)
    last = jnp.sum(jnp.where(jnp.arange(N_ETILES) == n_used - 1, te, 0))
    te = jnp.where(jnp.arange(N_ETILES) < n_used, te, last)
    return te.astype(I32), tile_rows.astype(I32), n_used.reshape(1), pos.reshape(-1).astype(I32)


def kernel(x_prompt, x_sample, state_ssm_re, state_ssm_im, cache_conv, norm_mix_g, w_in, lam_re, lam_im, log_dt, b_re, b_im, c_re, c_im, d_skip, w_ssm_glu, conv_w, conv_b, conv_ln_g, conv_ln_b, w_conv_out, w_out, norm_ffn_g, w_router_group, b_router_group, w_router_expert, b_router_expert, w_exp_gate, w_exp_up, w_exp_down, norm_final_g):
    assert w_in.shape[0] == 1, "single-layer trunk"
    xp = x_prompt.reshape(N_PROMPT_TOK, D_MODEL)
    xs = x_sample.reshape(N_SAMPLE_TOK, D_MODEL)
    w_in_bf = w_in[0].astype(BF16)
    row = lambda a: a.reshape(1, -1)

    ar_rep, ai_rep, bbr, bbi = _discretise(lam_re[0], lam_im[0], log_dt[0], b_re[0], b_im[0])
    gph = lambda a: a.reshape(N_SSM_GROUPS, SSM_STATE, SSM_GROUP)
    pick = lambda a: gph(a)[:, :, 0].reshape(GROUP_BLOCKS, 1, GB_STATE)
    bm = jnp.concatenate([_block_diag_in(gph(bbr)), _block_diag_in(gph(bbi))], axis=-1).astype(BF16)
    cm = jnp.concatenate([_block_diag_out(c_re[0]), -_block_diag_out(c_im[0])], axis=1).astype(BF16)
    dsk = d_skip[0].reshape(GROUP_BLOCKS, 1, GB_IN)
    ar, ai = pick(ar_rep), pick(ai_rep)
    s0r = state_ssm_re[0].reshape(N_SAMPLE_SEQ, -1)
    s0i = state_ssm_im[0].reshape(N_SAMPLE_SEQ, -1)

    u3, v3, xn3 = _inproj(xp, xs, row(norm_mix_g[0]), w_in_bf)
    actp, pfr, pfi = _scan_prompt(u3, bm, cm, ar, ai, dsk)
    acts, sfr, sfi = _scan_sample(u3, bm, cm, ar, ai, dsk, s0r, s0i)

    hist_s = cache_conv[0].transpose(1, 0, 2).reshape(HIST_S, D_CONV)
    cv, ncp, ncs = _conv(v3, hist_s, conv_w[0], row(conv_b[0]), conv_ln_g[0], conv_ln_b[0])

    m = _merge(xn3, actp, acts, cv, w_in_bf, w_ssm_glu[0].astype(BF16), w_conv_out[0].astype(BF16))

    pad_lanes = ROUTE_LANES - N_EXP_GROUPS - N_EXPERTS
    w_router = jnp.concatenate(
        [w_router_group[0], w_router_expert[0], jnp.zeros((D_MODEL, pad_lanes), F32)], axis=1)
    b_router = jnp.concatenate(
        [b_router_group[0], b_router_expert[0], jnp.zeros((pad_lanes,), F32)]).reshape(1, ROUTE_LANES)
    h, xn2, route, gw, cnt = _outproj(m, xp, xs, w_out[0].astype(BF16), row(norm_ffn_g[0]),
                                      w_router, b_router)

    te, tile_rows, n_used, pos = _dispatch_plan(route, cnt)
    o = _experts(te, tile_rows, n_used, pos, xn2, w_exp_gate[0], w_exp_up[0], w_exp_down[0])
    yp, ys = _combine(pos, h, gw, o, row(norm_final_g))

    st = lambda a, n: a.reshape(1, n, N_SSM_GROUPS, SSM_STATE)
    pst = lambda a: st(a.transpose(1, 0, 2), N_PROMPT_SEQ)
    ncp = ncp.reshape(CONV_BUF, N_PROMPT_SEQ, D_CONV).transpose(1, 0, 2)[None]
    ncs = ncs.reshape(CONV_BUF, N_SAMPLE_SEQ, D_CONV).transpose(1, 0, 2)[None]
    return (yp, ys.reshape(N_SAMPLE_SEQ, SAMPLE_LEN, D_MODEL), pst(pfr), pst(pfi), ncp,
            st(sfr, N_SAMPLE_SEQ), st(sfi, N_SAMPLE_SEQ), ncs)
```

```python
import functools

import jax
import jax.numpy as jnp
from jax import lax
from jax.experimental import pallas as pl
from jax.experimental.pallas import tpu as pltpu

F32 = jnp.float32
BF16 = jnp.bfloat16
I32 = jnp.int32

SUBLANES = 8
LANES = 128

D_MODEL = 2048
D_SSM = 1024
D_CONV = 1024
SSM_GROUP = 16
N_SSM_GROUPS = 64
SSM_STATE = 64
CONV_WIDTH = 31
CONV_BUF = CONV_WIDTH - 1
N_EXP_GROUPS = 4
EXP_PER_GROUP = 8
N_EXPERTS = 32
D_EXPERT = 256
EPS = 1e-6

N_PROMPT_SEQ = 4
PROMPT_LEN = 2048
N_SAMPLE_SEQ = 16
SAMPLE_LEN = 32
N_PROMPT_TOK = N_PROMPT_SEQ * PROMPT_LEN
N_SAMPLE_TOK = N_SAMPLE_SEQ * SAMPLE_LEN
N_TOK = N_PROMPT_TOK + N_SAMPLE_TOK

TM = 512
N_PROMPT_TILES = N_PROMPT_TOK // TM
N_TILES = N_TOK // TM
PROMPT_STEPS = TM // N_PROMPT_SEQ
GROUP_BLOCKS = 4
GROUPS_PER_BLOCK = N_SSM_GROUPS // GROUP_BLOCKS
GB_IN = GROUPS_PER_BLOCK * SSM_GROUP
GB_STATE = GROUPS_PER_BLOCK * SSM_STATE
GB_LB = GB_STATE // LANES
HIST_P = CONV_BUF * N_PROMPT_SEQ
HIST_S = CONV_BUF * N_SAMPLE_SEQ
ROUTE_LANES = 128
N_PAIRS = 2 * N_TOK
TO = 256
N_OT = N_TOK // TO
N_OT_PROMPT = N_PROMPT_TOK // TO
HALVES = TM // TO
LROWS = 768
TME = 256
N_ETILES = -(-(N_PAIRS + N_OT * N_EXPERTS * (SUBLANES - 1) + N_EXPERTS * (TME - 1)) // TME)
MODEL_LB = D_MODEL // LANES
SSM_LB = D_SSM // LANES
CONV_LB = D_CONV // LANES


def _cparams(sem, vmem_mb):
    return pltpu.CompilerParams(dimension_semantics=sem, vmem_limit_bytes=vmem_mb * 1024 * 1024)


def _resident(shape):
    return pl.BlockSpec(shape, lambda *_: (0,) * len(shape), pipeline_mode=pl.Buffered(1))


def _x_tile_specs(steps=PROMPT_STEPS):
    blocks_per_seq = PROMPT_LEN // steps
    n_prompt = N_PROMPT_SEQ * blocks_per_seq // N_PROMPT_SEQ

    def prompt(b):
        return pl.BlockSpec(
            (steps, D_MODEL),
            lambda i, *_: (b * blocks_per_seq + jnp.minimum(i, n_prompt - 1), 0))

    return [prompt(b) for b in range(N_PROMPT_SEQ)] + [_resident((N_SAMPLE_TOK, D_MODEL))]


def _to_mixer_order(ref3, val, n_seq, steps):
    for c in range(val.shape[1] // LANES):
        for b in range(n_seq):
            ref3[c, pl.ds(b, steps, stride=n_seq), :] = val[b * steps:(b + 1) * steps,
                                                            c * LANES:(c + 1) * LANES]


def _seq_rows(ref3, b, n_seq, steps):
    return jnp.concatenate(
        [ref3[c, pl.ds(b, steps, stride=n_seq), :] for c in range(ref3.shape[0])], axis=1)


def _tile_to_mixer_order(i, ref3, xs, sample):
    @pl.when(i < N_PROMPT_TILES)
    def _():
        _to_mixer_order(ref3, xs(), N_PROMPT_SEQ, PROMPT_STEPS)

    @pl.when(i >= N_PROMPT_TILES)
    def _():
        _to_mixer_order(ref3, sample(), N_SAMPLE_SEQ, SAMPLE_LEN)


def _lane_blocks(ref3):
    return jnp.concatenate([ref3[c] for c in range(ref3.shape[0])], axis=1)


def _pair_specs(rows, cols, n_prompt_tiles):
    p = pl.BlockSpec((rows, cols), lambda i, *_: (jnp.minimum(i, n_prompt_tiles - 1), 0))
    s = pl.BlockSpec((rows, cols), lambda i, *_: (jnp.maximum(i - n_prompt_tiles, 0), 0))
    return p, s


def _disc_kernel(lr_ref, li_ref, ldt_ref, br_ref, bi_ref, ar_ref, ai_ref, bbr_ref, bbi_ref):
    lr = lr_ref[...]
    li = li_ref[...]
    dt = jnp.exp(ldt_ref[...])
    mag = jnp.exp(lr * dt)
    ar = mag * jnp.cos(li * dt)
    ai = mag * jnp.sin(li * dt)
    den = lr * lr + li * li
    nr = ar - 1.0
    cr = (nr * lr + ai * li) / den
    ci = (ai * lr - nr * li) / den
    br = br_ref[...]
    bi = bi_ref[...]
    ar_ref[...] = ar
    ai_ref[...] = ai
    bbr_ref[...] = cr * br - ci * bi
    bbi_ref[...] = cr * bi + ci * br


def _discretise(lam_re, lam_im, log_dt, b_re, b_im):
    shp = jax.ShapeDtypeStruct((N_SSM_GROUPS, SSM_STATE * SSM_GROUP), F32)
    rep = lambda a: jnp.repeat(a, SSM_GROUP, axis=-1)
    ldt = jnp.broadcast_to(log_dt[:, None], (N_SSM_GROUPS, SSM_STATE * SSM_GROUP))
    return pl.pallas_call(_disc_kernel, out_shape=(shp, shp, shp, shp), name="s5_discretise")(
        rep(lam_re), rep(lam_im), ldt,
        b_re.reshape(N_SSM_GROUPS, -1), b_im.reshape(N_SSM_GROUPS, -1))


def _block_diag_in(bb):
    b4 = bb.reshape(GROUP_BLOCKS, GROUPS_PER_BLOCK, SSM_STATE, SSM_GROUP).transpose(0, 1, 3, 2)
    eye = jnp.eye(GROUPS_PER_BLOCK, dtype=bool)[None, :, None, :, None]
    full = jnp.where(eye, b4[:, :, :, None, :], 0.0)
    return full.reshape(GROUP_BLOCKS, GB_IN, GB_STATE)


def _block_diag_out(c):
    c4 = c.reshape(GROUP_BLOCKS, GROUPS_PER_BLOCK, SSM_GROUP, SSM_STATE).transpose(0, 1, 3, 2)
    eye = jnp.eye(GROUPS_PER_BLOCK, dtype=bool)[None, :, None, :, None]
    full = jnp.where(eye, c4[:, :, :, None, :], 0.0)
    return full.reshape(GROUP_BLOCKS, GB_STATE, GB_IN)


def _rmsnorm_rows(x, g):
    r = lax.rsqrt(jnp.mean(x * x, axis=-1, keepdims=True) + EPS)
    return x * r * g


def _inproj_kernel(x0_ref, x1_ref, x2_ref, x3_ref, xs_ref, g_ref, w_ref,
                   u_ref, v_ref, xn_ref, xn_s, va_s):
    i = pl.program_id(0)
    j = pl.program_id(1)
    prompt_rows = lambda: jnp.concatenate(
        [x0_ref[...], x1_ref[...], x2_ref[...], x3_ref[...]], axis=0)

    @pl.when(j == 0)
    def _():
        def norm(x):
            xn = _rmsnorm_rows(x, g_ref[...])
            xn_s[...] = xn.astype(BF16)
            return xn
        _tile_to_mixer_order(i, xn_ref, lambda: norm(prompt_rows()), lambda: norm(xs_ref[...]))

    p = jnp.dot(xn_s[...], w_ref[...], preferred_element_type=F32)

    @pl.when(j == 0)
    def _():
        _tile_to_mixer_order(i, u_ref, lambda: p, lambda: p)

    @pl.when(j == 1)
    def _():
        va_s[...] = p

    @pl.when(j == 2)
    def _():
        v = va_s[...] * jax.nn.sigmoid(p)
        _tile_to_mixer_order(i, v_ref, lambda: v, lambda: v)


def _inproj(xp, xs, g, w_in_bf):
    blocked = lambda nb: pl.BlockSpec((nb, TM, LANES), lambda i, j: (0, i, 0))
    return pl.pallas_call(
        _inproj_kernel,
        grid=(N_TILES, 3),
        in_specs=_x_tile_specs() + [
            pl.BlockSpec((1, D_MODEL), lambda i, j: (0, 0)),
            pl.BlockSpec((D_MODEL, D_SSM), lambda i, j: (0, j)),
        ],
        out_specs=[blocked(SSM_LB), blocked(CONV_LB), blocked(MODEL_LB)],
        out_shape=(jax.ShapeDtypeStruct((SSM_LB, N_TOK, LANES), F32),
                   jax.ShapeDtypeStruct((CONV_LB, N_TOK, LANES), F32),
                   jax.ShapeDtypeStruct((MODEL_LB, N_TOK, LANES), F32)),
        scratch_shapes=[pltpu.VMEM((TM, D_MODEL), BF16), pltpu.VMEM((TM, D_CONV), F32)],
        compiler_params=_cparams(("arbitrary", "arbitrary"), 52),
        name="in_proj",
    )(xp, xp, xp, xp, xs, g, w_in_bf)


N_SCAN_TILES = GROUP_BLOCKS * N_PROMPT_TILES
SCAN_STAGES = 3


def _scan_prompt_kernel(ua_ref, uc_ref, bm_ref, cm_ref, ar_ref, ai_ref, dsk_ref,
                        act_ref, pfr_ref, pfi_ref, x0, x1, x2, cst, car, fin):
    n = pl.program_id(0)
    q_s = jnp.clip(n - 1, 0, N_SCAN_TILES - 1)
    t_s = q_s % N_PROMPT_TILES
    valid_s = jnp.logical_and(n >= 1, n <= N_SCAN_TILES)

    @pl.when(n == 0)
    def _():
        for buf in (x0, x1, x2):
            buf[...] = jnp.zeros_like(buf)
        car[...] = jnp.zeros_like(car)
        fin[...] = jnp.zeros_like(fin)

    shape = (SUBLANES, GB_STATE)
    top = lax.broadcasted_iota(I32, shape, 0) < N_PROMPT_SEQ
    arb = jnp.broadcast_to(ar_ref[0], shape)
    aib = jnp.broadcast_to(ai_ref[0], shape)
    cst[0] = jnp.where(top, 0.0, arb)
    cst[1] = jnp.where(top, 0.0, aib)
    cst[2] = jnp.where(top, arb, arb * arb - aib * aib)
    cst[3] = jnp.where(top, aib, 2.0 * arb * aib)

    def project_in(xa):
        u = jnp.concatenate([ua_ref[0], ua_ref[1]], axis=1).astype(BF16)
        x = jnp.dot(u, bm_ref[0], preferred_element_type=F32)
        for c in range(2 * GB_LB):
            xa[c] = x[:, c * LANES:(c + 1) * LANES]

    def project_out(xc):
        u = jnp.concatenate([uc_ref[0], uc_ref[1]], axis=1)
        h = _lane_blocks(xc).astype(BF16)
        y = jnp.dot(h, cm_ref[0], preferred_element_type=F32) + dsk_ref[0] * u
        act_ref[...] = jax.nn.gelu(y).astype(BF16)

    def recur(xs):
        top1 = lax.broadcasted_iota(I32, (SUBLANES, LANES), 0) < N_PROMPT_SEQ
        first = t_s == 0
        hr = [jnp.where(first, 0.0, car[:, c * LANES:(c + 1) * LANES]) for c in range(GB_LB)]
        hi = [jnp.where(first, 0.0, car[:, GB_STATE + c * LANES:GB_STATE + (c + 1) * LANES])
              for c in range(GB_LB)]
        for r in range(TM // SUBLANES):
            rows = slice(r * SUBLANES, (r + 1) * SUBLANES)
            for c in range(GB_LB):
                cols = slice(c * LANES, (c + 1) * LANES)
                xr = xs[c, rows, :]
                xi = xs[GB_LB + c, rows, :]
                a1r, a1i, a2r, a2i = cst[0, :, cols], cst[1, :, cols], cst[2, :, cols], cst[3, :, cols]
                sxr = pltpu.roll(xr, N_PROMPT_SEQ, axis=0)
                sxi = pltpu.roll(xi, N_PROMPT_SEQ, axis=0)
                tr = xr + a1r * sxr - a1i * sxi
                ti = xi + a1r * sxi + a1i * sxr
                nr = tr + a2r * hr[c] - a2i * hi[c]
                ni = ti + a2r * hi[c] + a2i * hr[c]
                xs[c, rows, :] = nr
                xs[GB_LB + c, rows, :] = ni
                hr[c] = jnp.where(top1, pltpu.roll(nr, N_PROMPT_SEQ, axis=0), nr)
                hi[c] = jnp.where(top1, pltpu.roll(ni, N_PROMPT_SEQ, axis=0), ni)
        state = jnp.concatenate(hr + hi, axis=1)
        car[...] = state
        last = jnp.logical_and(valid_s, t_s == N_PROMPT_TILES - 1)
        fin[...] = jnp.where(last, state, fin[...])
        pfr_ref[0] = fin[N_PROMPT_SEQ:SUBLANES, 0:GB_STATE]
        pfi_ref[0] = fin[N_PROMPT_SEQ:SUBLANES, GB_STATE:2 * GB_STATE]

    bufs = (x0, x1, x2)
    for r in range(SCAN_STAGES):
        @pl.when(n % SCAN_STAGES == r)
        def _():
            project_out(bufs[(r + 1) % SCAN_STAGES])
            recur(bufs[(r + 2) % SCAN_STAGES])
            project_in(bufs[r])


def _scan_prompt(u3, bm, cm, ar, ai, dsk):
    q_a = lambda n: jnp.minimum(n, N_SCAN_TILES - 1)
    q_s = lambda n: jnp.clip(n - 1, 0, N_SCAN_TILES - 1)
    q_c = lambda n: jnp.clip(n - 2, 0, N_SCAN_TILES - 1)
    gb = lambda q: q // N_PROMPT_TILES
    tt = lambda q: q % N_PROMPT_TILES
    u_spec = lambda q: pl.BlockSpec((GB_IN // LANES, TM, LANES), lambda n: (gb(q(n)), tt(q(n)), 0))
    gb3 = lambda last, q: pl.BlockSpec((1,) + last, lambda n: (gb(q(n)), 0, 0))
    state = jax.ShapeDtypeStruct((GROUP_BLOCKS, N_PROMPT_SEQ, GB_STATE), F32)
    xbuf = pltpu.VMEM((2 * GB_LB, TM, LANES), F32)
    return pl.pallas_call(
        _scan_prompt_kernel,
        grid=(N_SCAN_TILES + SCAN_STAGES - 1,),
        in_specs=[
            u_spec(q_a), u_spec(q_c),
            gb3((GB_IN, 2 * GB_STATE), q_a),
            gb3((2 * GB_STATE, GB_IN), q_c),
            gb3((1, GB_STATE), q_s),
            gb3((1, GB_STATE), q_s),
            gb3((1, GB_IN), q_c),
        ],
        out_specs=[
            pl.BlockSpec((TM, GB_IN), lambda n: (tt(q_c(n)), gb(q_c(n)))),
            gb3((N_PROMPT_SEQ, GB_STATE), q_s),
            gb3((N_PROMPT_SEQ, GB_STATE), q_s),
        ],
        out_shape=(jax.ShapeDtypeStruct((N_PROMPT_TOK, D_SSM), BF16), state, state),
        scratch_shapes=[xbuf, xbuf, xbuf,
                        pltpu.VMEM((4, SUBLANES, GB_STATE), F32),
                        pltpu.VMEM((SUBLANES, 2 * GB_STATE), F32),
                        pltpu.VMEM((SUBLANES, 2 * GB_STATE), F32)],
        compiler_params=_cparams(("arbitrary",), 48),
        name="s5_scan_prompt",
    )(u3, u3, bm, cm, ar, ai, dsk)


def _scan_sample_kernel(u_ref, bm_ref, cm_ref, ar_ref, ai_ref, dsk_ref, s0r_ref, s0i_ref,
                        act_ref, sfr_ref, sfi_ref, xs_ref):
    half_lb = GB_LB // 2
    half_cols = half_lb * LANES
    u = jnp.concatenate([u_ref[0], u_ref[1]], axis=1)
    x = jnp.dot(u.astype(BF16), bm_ref[0], preferred_element_type=F32)
    for c in range(2 * GB_LB):
        xs_ref[c] = x[:, c * LANES:(c + 1) * LANES]

    def load_half(rows, half, imag):
        c0 = imag * GB_LB + half * half_lb
        return jnp.concatenate([xs_ref[c0 + c, rows, :] for c in range(half_lb)], axis=1)

    def store_half(rows, half, imag, val):
        c0 = imag * GB_LB + half * half_lb
        for c in range(half_lb):
            xs_ref[c0 + c, rows, :] = val[:, c * LANES:(c + 1) * LANES]

    shape = (N_SAMPLE_SEQ, half_cols)
    for half in range(2):
        cols = slice(half * half_cols, (half + 1) * half_cols)
        arb = jnp.broadcast_to(ar_ref[0][:, cols], shape)
        aib = jnp.broadcast_to(ai_ref[0][:, cols], shape)

        def body(k, carry):
            hr, hi = carry
            rows = pl.ds(pl.multiple_of(k * N_SAMPLE_SEQ, N_SAMPLE_SEQ), N_SAMPLE_SEQ)
            xr = load_half(rows, half, 0)
            xi = load_half(rows, half, 1)
            nhr = xr + arb * hr - aib * hi
            nhi = xi + arb * hi + aib * hr
            store_half(rows, half, 0, nhr)
            store_half(rows, half, 1, nhi)
            return nhr, nhi

        hr, hi = lax.fori_loop(0, SAMPLE_LEN, body, (s0r_ref[:, cols], s0i_ref[:, cols]))
        sfr_ref[:, cols] = hr
        sfi_ref[:, cols] = hi

    h = _lane_blocks(xs_ref).astype(BF16)
    y = jnp.dot(h, cm_ref[0], preferred_element_type=F32) + dsk_ref[0] * u
    act_ref[...] = jax.nn.gelu(y).astype(BF16)


def _scan_sample(u3, bm, cm, ar, ai, dsk, s0r, s0i):
    gb3 = lambda last: pl.BlockSpec((1,) + last, lambda g: (g, 0, 0))
    state = lambda: pl.BlockSpec((N_SAMPLE_SEQ, GB_STATE), lambda g: (0, g))
    sst = jax.ShapeDtypeStruct((N_SAMPLE_SEQ, N_SSM_GROUPS * SSM_STATE), F32)
    return pl.pallas_call(
        _scan_sample_kernel,
        grid=(GROUP_BLOCKS,),
        in_specs=[
            pl.BlockSpec((GB_IN // LANES, TM, LANES), lambda g: (g, N_PROMPT_TILES, 0)),
            gb3((GB_IN, 2 * GB_STATE)), gb3((2 * GB_STATE, GB_IN)),
            gb3((1, GB_STATE)), gb3((1, GB_STATE)), gb3((1, GB_IN)),
            state(), state(),
        ],
        out_specs=[pl.BlockSpec((TM, GB_IN), lambda g: (0, g)), state(), state()],
        out_shape=(jax.ShapeDtypeStruct((N_SAMPLE_TOK, D_SSM), BF16), sst, sst),
        scratch_shapes=[pltpu.VMEM((2 * GB_LB, TM, LANES), F32)],
        compiler_params=_cparams(("arbitrary",), 32),
        name="s5_scan_sample",
    )(u3, bm, cm, ar, ai, dsk, s0r, s0i)


CONV_ROWS = 64
EXT_ROWS = HIST_S + TM


def _conv_kernel(v_ref, hist_ref, w_ref, b_ref, lg_ref, lb_ref,
                 cv_ref, ncp_ref, ncs_ref, ext_ref, sh_ref, acc_ref):
    t = pl.program_id(0)

    def taps(n_seq):
        half = SUBLANES // 2
        if n_seq % SUBLANES:
            lo = HIST_S - HIST_P - SUBLANES
            sh_ref[:, lo:EXT_ROWS - SUBLANES, :] = ext_ref[:, lo + half:EXT_ROWS - half, :]

        def body(rb, _):
            r0 = pl.multiple_of(rb * CONV_ROWS, CONV_ROWS)
            for cb in range(CONV_LB):
                cols = slice(cb * LANES, (cb + 1) * LANES)
                acc = None
                for j in range(CONV_WIDTH):
                    start = HIST_S - (CONV_BUF - j) * n_seq
                    src = ext_ref
                    if start % SUBLANES:
                        src, start = sh_ref, start - half
                    rows = pl.ds(pl.multiple_of(r0 + start, SUBLANES), CONV_ROWS)
                    term = src[cb, rows, :] * w_ref[j:j + 1, cols]
                    acc = term if acc is None else acc + term
                acc_ref[cb, pl.ds(r0, CONV_ROWS), :] = acc + b_ref[:, cols]
            return 0

        lax.fori_loop(0, TM // CONV_ROWS, body, 0)

    @pl.when(t == 0)
    def _():
        ext_ref[:, 0:HIST_S, :] = jnp.zeros((CONV_LB, HIST_S, LANES), F32)

    @pl.when(t == N_PROMPT_TILES)
    def _():
        for c in range(CONV_LB):
            ext_ref[c, 0:HIST_S, :] = hist_ref[:, c * LANES:(c + 1) * LANES]

    ext_ref[:, HIST_S:EXT_ROWS, :] = v_ref[...]

    @pl.when(t < N_PROMPT_TILES)
    def _():
        taps(N_PROMPT_SEQ)

    @pl.when(t == N_PROMPT_TILES)
    def _():
        taps(N_SAMPLE_SEQ)

    y = acc_ref[...]
    mu = jnp.sum(jnp.sum(y, axis=0), axis=-1, keepdims=True) / D_CONV
    yc = y - mu[None]
    var = jnp.sum(jnp.sum(yc * yc, axis=0), axis=-1, keepdims=True) / D_CONV
    z = jax.nn.silu(yc * lax.rsqrt(var + EPS)[None] * lg_ref[...] + lb_ref[...])
    for c in range(CONV_LB):
        cv_ref[:, c * LANES:(c + 1) * LANES] = z[c].astype(BF16)

    @pl.when(t == N_PROMPT_TILES - 1)
    def _():
        for c in range(CONV_LB):
            ncp_ref[:, c * LANES:(c + 1) * LANES] = ext_ref[c, EXT_ROWS - HIST_P:EXT_ROWS, :]

    @pl.when(t < N_PROMPT_TILES - 1)
    def _():
        ext_ref[:, HIST_S - HIST_P:HIST_S, :] = ext_ref[:, EXT_ROWS - HIST_P:EXT_ROWS, :]

    @pl.when(t == N_PROMPT_TILES)
    def _():
        for c in range(CONV_LB):
            ncs_ref[:, c * LANES:(c + 1) * LANES] = ext_ref[c, EXT_ROWS - HIST_S:EXT_ROWS, :]


def _conv(v3, hist_s, conv_w, conv_b, ln_g, ln_b):
    row = lambda n: pl.BlockSpec((n, D_CONV), lambda t: (0, 0))
    lane3 = pl.BlockSpec((CONV_LB, 1, LANES), lambda t: (0, 0, 0))
    ext = pltpu.VMEM((CONV_LB, EXT_ROWS, LANES), F32)
    return pl.pallas_call(
        _conv_kernel,
        grid=(N_TILES,),
        in_specs=[pl.BlockSpec((CONV_LB, TM, LANES), lambda t: (0, t, 0)),
                  row(HIST_S), row(CONV_WIDTH), row(1), lane3, lane3],
        out_specs=[pl.BlockSpec((TM, D_CONV), lambda t: (t, 0)), row(HIST_P), row(HIST_S)],
        out_shape=(jax.ShapeDtypeStruct((N_TOK, D_CONV), BF16),
                   jax.ShapeDtypeStruct((HIST_P, D_CONV), F32),
                   jax.ShapeDtypeStruct((HIST_S, D_CONV), F32)),
        scratch_shapes=[ext, ext, pltpu.VMEM((CONV_LB, TM, LANES), F32)],
        compiler_params=_cparams(("arbitrary",), 48),
        name="conv_branch",
    )(v3, hist_s, conv_w, conv_b,
      ln_g.reshape(CONV_LB, 1, LANES), ln_b.reshape(CONV_LB, 1, LANES))


MERGE_COLS = 512
N_MERGE = D_MODEL // MERGE_COLS


def _merge_kernel(xn_ref, actp_ref, acts_ref, cv_ref, wga_ref, wgb_ref, wg1_ref, wg2_ref, wco_ref,
                  m_ref, xn_s, act_s):
    i = pl.program_id(0)
    j = pl.program_id(1)

    @pl.when(j == 0)
    def _():
        xn_s[...] = _lane_blocks(xn_ref).astype(BF16)

        @pl.when(i < N_PROMPT_TILES)
        def _():
            act_s[...] = actp_ref[...]

        @pl.when(i >= N_PROMPT_TILES)
        def _():
            act_s[...] = acts_ref[...]

    xn = xn_s[...]
    act = act_s[...]
    dot = functools.partial(jnp.dot, preferred_element_type=F32)
    ya = dot(act, wg1_ref[...]) * jax.nn.sigmoid(dot(act, wg2_ref[...]))
    yb = dot(cv_ref[...], wco_ref[...])
    m = jax.nn.sigmoid(dot(xn, wga_ref[...])) * ya + jax.nn.sigmoid(dot(xn, wgb_ref[...])) * yb
    m_ref[...] = m.astype(BF16)


def _merge(xn3, actp, acts, cv, w_in_bf, w_glu_bf, w_co_bf):
    ga0 = (D_SSM + 2 * D_CONV) // MERGE_COLS
    gb0 = ga0 + N_MERGE
    return pl.pallas_call(
        _merge_kernel,
        grid=(N_TILES, N_MERGE),
        in_specs=[pl.BlockSpec((MODEL_LB, TM, LANES), lambda i, j: (0, i, 0))]
        + list(_pair_specs(TM, D_SSM, N_PROMPT_TILES)) + [
            pl.BlockSpec((TM, D_CONV), lambda i, j: (i, 0)),
            pl.BlockSpec((D_MODEL, MERGE_COLS), lambda i, j: (0, ga0 + j)),
            pl.BlockSpec((D_MODEL, MERGE_COLS), lambda i, j: (0, gb0 + j)),
            pl.BlockSpec((D_SSM, MERGE_COLS), lambda i, j: (0, j)),
            pl.BlockSpec((D_SSM, MERGE_COLS), lambda i, j: (0, N_MERGE + j)),
            pl.BlockSpec((D_CONV, MERGE_COLS), lambda i, j: (0, j)),
        ],
        out_specs=pl.BlockSpec((TM, MERGE_COLS), lambda i, j: (i, j)),
        out_shape=jax.ShapeDtypeStruct((N_TOK, D_MODEL), BF16),
        scratch_shapes=[pltpu.VMEM((TM, D_MODEL), BF16), pltpu.VMEM((TM, D_SSM), BF16)],
        compiler_params=_cparams(("arbitrary", "arbitrary"), 48),
        name="gated_merge",
    )(xn3, actp, acts, cv, w_in_bf, w_in_bf, w_glu_bf, w_glu_bf, w_co_bf)


def _outproj_kernel(m_ref, x0_ref, x1_ref, x2_ref, x3_ref, xs_ref, wo_ref, g_ref, wr_ref, br_ref,
                    h_ref, xl_ref, route_ref, gw_ref, cnt_ref, xr_s):
    i = pl.program_id(0)
    dot = functools.partial(jnp.dot, preferred_element_type=F32)
    steps_p = TO // N_PROMPT_SEQ
    steps_s = TO // N_SAMPLE_SEQ

    @pl.when(i < N_OT_PROMPT)
    def _():
        x = jnp.concatenate([x0_ref[...], x1_ref[...], x2_ref[...], x3_ref[...]], axis=0)
        _to_mixer_order(xr_s, x, N_PROMPT_SEQ, steps_p)

    @pl.when(i >= N_OT_PROMPT)
    def _():
        k0 = (i - N_OT_PROMPT) * steps_s
        x = jnp.concatenate(
            [xs_ref[pl.ds(pl.multiple_of(b * SAMPLE_LEN + k0, steps_s), steps_s), :]
             for b in range(N_SAMPLE_SEQ)], axis=0)
        _to_mixer_order(xr_s, x, N_SAMPLE_SEQ, steps_s)

    h = _lane_blocks(xr_s) + dot(m_ref[...], wo_ref[...])
    h_ref[...] = h
    xn = _rmsnorm_rows(h, g_ref[...])
    x_hi = xn.astype(BF16)
    x_lo = (xn - x_hi.astype(F32)).astype(BF16)
    w = wr_ref[...]
    w_hi = w.astype(BF16)
    w_lo = (w - w_hi.astype(F32)).astype(BF16)
    logits = dot(x_hi, w_hi) + (dot(x_lo, w_hi) + dot(x_hi, w_lo)) + br_ref[...]

    lane = lax.broadcasted_iota(I32, logits.shape, 1)
    neg = -jnp.inf
    first = lambda hit: jnp.min(jnp.where(hit, lane, ROUTE_LANES), axis=-1, keepdims=True)
    gmask = lane < N_EXP_GROUPS
    lg = jnp.where(gmask, logits, neg)
    gmax = jnp.max(lg, axis=-1, keepdims=True)
    gsel = first(lg == gmax)
    psum = jnp.sum(jnp.where(gmask, jnp.exp(logits - gmax), 0.0), axis=-1, keepdims=True)
    pg_sel = 1.0 / psum
    e_lane = lane - N_EXP_GROUPS
    emask = (e_lane >= 0) & (e_lane < N_EXPERTS) & ((e_lane // EXP_PER_GROUP) == gsel)
    le = jnp.where(emask, logits, neg)
    v1 = jnp.max(le, axis=-1, keepdims=True)
    i1 = first(le == v1)
    le2 = jnp.where(lane == i1, neg, le)
    v2 = jnp.max(le2, axis=-1, keepdims=True)
    i2 = first(le2 == v2)
    z = jnp.exp(v2 - v1)
    w1 = pg_sel / (1.0 + z)
    w2 = pg_sel * z / (1.0 + z)
    e1 = i1 - N_EXP_GROUPS
    e2 = i2 - N_EXP_GROUPS
    gw_ref[...] = jnp.where(lane == 0, w1, jnp.where(lane == 1, w2, 0.0))
    oh1 = lane == e1
    oh2 = lane == e2
    hits = jnp.where(oh1 | oh2, 1.0, 0.0)
    rr = lax.broadcasted_iota(I32, (TO, TO), 0)
    cc = lax.broadcasted_iota(I32, (TO, TO), 1)
    before = dot(jnp.where(cc < rr, 1.0, 0.0).astype(BF16), hits.astype(BF16))
    counts = jnp.sum(hits, axis=0, keepdims=True)
    groups = jnp.floor((counts + (SUBLANES - 1.0)) * (1.0 / SUBLANES))
    er = lax.broadcasted_iota(I32, (ROUTE_LANES, ROUTE_LANES), 0)
    ec = lax.broadcasted_iota(I32, (ROUTE_LANES, ROUTE_LANES), 1)
    groups_before = dot(jnp.broadcast_to(groups, (SUBLANES, ROUTE_LANES)).astype(BF16),
                        jnp.where(er < ec, 1.0, 0.0).astype(BF16))[0:1]
    local = groups_before * SUBLANES + before
    lpos1 = jnp.sum(jnp.where(oh1, local, 0.0), axis=-1, keepdims=True)
    lpos2 = jnp.sum(jnp.where(oh2, local, 0.0), axis=-1, keepdims=True)
    route_ref[...] = jnp.where(lane == 0, e1, jnp.where(lane == 1, e2, jnp.where(
        lane == 2, lpos1.astype(I32), jnp.where(lane == 3, lpos2.astype(I32), 0))))
    cnt_ref[0] = counts.astype(I32)
    lpos_t = jnp.where(lane == 0, lpos1, jnp.where(lane == 1, lpos2, 0.0)).T.astype(I32)
    jrow = lax.broadcasted_iota(I32, (LROWS, TO), 0)
    sel = (jrow == lpos_t[0:1, :]) | (jrow == lpos_t[1:2, :])
    xl_ref[...] = dot(jnp.where(sel, 1.0, 0.0).astype(BF16), x_hi)


def _outproj(m, xp, xs, w_out_bf, g, w_router, b_router):
    tile = lambda n: pl.BlockSpec((TO, n), lambda i: (i, 0))
    return pl.pallas_call(
        _outproj_kernel,
        grid=(N_OT,),
        in_specs=[tile(D_MODEL)] + _x_tile_specs(TO // N_PROMPT_SEQ) + [
            _resident((D_MODEL, D_MODEL)), _resident((1, D_MODEL)),
            _resident((D_MODEL, ROUTE_LANES)), _resident((1, ROUTE_LANES))],
        out_specs=[tile(D_MODEL),
                   pl.BlockSpec((LROWS, D_MODEL), lambda i: (i, 0)),
                   tile(ROUTE_LANES), tile(ROUTE_LANES),
                   pl.BlockSpec((1, 1, ROUTE_LANES), lambda i: (i, 0, 0))],
        out_shape=(jax.ShapeDtypeStruct((N_TOK, D_MODEL), F32),
                   jax.ShapeDtypeStruct((N_OT * LROWS, D_MODEL), F32),
                   jax.ShapeDtypeStruct((N_TOK, ROUTE_LANES), I32),
                   jax.ShapeDtypeStruct((N_TOK, ROUTE_LANES), F32),
                   jax.ShapeDtypeStruct((N_OT, 1, ROUTE_LANES), I32)),
        scratch_shapes=[pltpu.VMEM((MODEL_LB, TO, LANES), F32)],
        compiler_params=_cparams(("arbitrary",), 56),
        name="out_proj_router",
    )(m, xp, xp, xp, xp, xs, w_out_bf, g, w_router, b_router)


def _expert_kernel(te_ref, nrows_ref, nused_ref, psrc_ref, pdst_ref, plen_ref, tot_ref,
                   xl_hbm, wg_ref, wu_ref, wd_ref, ol_hbm,
                   xbuf, obuf, zbuf, wg_s, wu_s, wd_s, gsem, wsem, zsem):
    i = pl.program_id(0)
    nused = nused_ref[0]
    slot = i % 2
    rows8 = lambda v: pl.multiple_of(v, SUBLANES)

    def for_pieces(tile, fn):
        def body(k, _):
            idx = tile * N_OT + k
            n = plen_ref[idx]

            @pl.when(n > 0)
            def _():
                fn(rows8(psrc_ref[idx]), rows8(pdst_ref[idx]), rows8(n))
            return 0
        lax.fori_loop(0, N_OT, body, 0)

    def gather(tile, s):
        for_pieces(tile, lambda src, dst, n: pltpu.make_async_copy(
            xl_hbm.at[pl.ds(src, n)], xbuf.at[s, pl.ds(dst, n)], gsem.at[s]).start())

    def scatter(tile, s):
        for_pieces(tile, lambda src, dst, n: pltpu.make_async_copy(
            obuf.at[s, pl.ds(dst, n)], ol_hbm.at[pl.ds(src, n)], wsem.at[s]).start())

    def wait_gather(tile, s):
        n = rows8(nrows_ref[tile])
        pltpu.make_async_copy(xl_hbm.at[pl.ds(0, n)], xbuf.at[s, pl.ds(0, n)], gsem.at[s]).wait()

    def wait_scatter(tile, s):
        n = rows8(nrows_ref[tile])
        pltpu.make_async_copy(obuf.at[s, pl.ds(0, n)], ol_hbm.at[pl.ds(0, n)], wsem.at[s]).wait()

    def tail_copy(t):
        used = rows8(tot_ref[t])
        n = rows8(LROWS - tot_ref[t])
        dst = rows8(t * LROWS + used)
        return pltpu.make_async_copy(zbuf.at[pl.ds(0, n)], ol_hbm.at[pl.ds(dst, n)], zsem.at[0])

    @pl.when(i == 0)
    def _():
        zbuf[...] = jnp.zeros_like(zbuf)
        xbuf[...] = jnp.zeros_like(xbuf)

        def start(t, _):
            tail_copy(t).start()
            return 0

        def wait(t, _):
            tail_copy(t).wait()
            return 0
        lax.fori_loop(0, N_OT, start, 0)
        lax.fori_loop(0, N_OT, wait, 0)
        gather(0, 0)

    @pl.when(i + 1 < nused)
    def _():
        gather(i + 1, 1 - slot)

    @pl.when(i < nused)
    def _():
        changed = jnp.logical_or(i == 0, te_ref[i] != te_ref[jnp.maximum(i - 1, 0)])

        @pl.when(changed)
        def _():
            wg_s[...] = wg_ref[0].astype(BF16)
            wu_s[...] = wu_ref[0].astype(BF16)
            wd_s[...] = wd_ref[0].astype(BF16)

        wait_gather(i, slot)

        @pl.when(i >= 2)
        def _():
            wait_scatter(i - 2, slot)

        xb = xbuf[slot].astype(BF16)
        hg = jnp.dot(xb, wg_s[...], preferred_element_type=F32)
        hu = jnp.dot(xb, wu_s[...], preferred_element_type=F32)
        hid = jax.nn.silu(hg) * hu
        obuf[slot] = jnp.dot(hid.astype(BF16), wd_s[...], preferred_element_type=F32)
        scatter(i, slot)

        @pl.when(i == nused - 1)
        def _():
            @pl.when(i >= 1)
            def _():
                wait_scatter(i - 1, 1 - slot)
            wait_scatter(i, slot)


def _experts(plan, xl, wg, wu, wd):
    wspec = lambda r, c: pl.BlockSpec((1, r, c), lambda i, te, *_: (te[i], 0, 0))
    grid_spec = pltpu.PrefetchScalarGridSpec(
        num_scalar_prefetch=len(plan),
        grid=(N_ETILES,),
        in_specs=[pl.BlockSpec(memory_space=pl.ANY), wspec(D_MODEL, D_EXPERT),
                  wspec(D_MODEL, D_EXPERT), wspec(D_EXPERT, D_MODEL)],
        out_specs=pl.BlockSpec(memory_space=pl.ANY),
        scratch_shapes=[
            pltpu.VMEM((2, TME, D_MODEL), F32),
            pltpu.VMEM((2, TME, D_MODEL), F32),
            pltpu.VMEM((LROWS - 2 * TO, D_MODEL), F32),
            pltpu.VMEM((D_MODEL, D_EXPERT), BF16),
            pltpu.VMEM((D_MODEL, D_EXPERT), BF16),
            pltpu.VMEM((D_EXPERT, D_MODEL), BF16),
            pltpu.SemaphoreType.DMA((2,)),
            pltpu.SemaphoreType.DMA((2,)),
            pltpu.SemaphoreType.DMA((1,)),
        ],
    )
    return pl.pallas_call(
        _expert_kernel,
        grid_spec=grid_spec,
        out_shape=jax.ShapeDtypeStruct((N_OT * LROWS, D_MODEL), F32),
        compiler_params=_cparams(("arbitrary",), 40),
        name="routed_experts",
    )(*plan, xl, wg, wu, wd)


def _combine_kernel(h_ref, route_ref, gw_ref, ol_ref, g_ref, yp_ref, ys_ref, y_s):
    i = pl.program_id(0)
    steps_p = TO // N_PROMPT_SEQ
    steps_s = TO // N_SAMPLE_SEQ
    route = route_ref[...]
    gw = gw_ref[...]
    jcol = lax.broadcasted_iota(I32, (TO, LROWS), 1)
    mix = (jnp.where(jcol == route[:, 2:3], gw[:, 0:1], 0.0)
           + jnp.where(jcol == route[:, 3:4], gw[:, 1:2], 0.0))
    moe = jnp.dot(mix.astype(BF16), ol_ref[...].astype(BF16), preferred_element_type=F32)
    y = _rmsnorm_rows(h_ref[...] + moe, g_ref[...])
    for c in range(MODEL_LB):
        y_s[c] = y[:, c * LANES:(c + 1) * LANES]

    @pl.when(i < N_OT_PROMPT)
    def _():
        for b in range(N_PROMPT_SEQ):
            yp_ref[b] = _seq_rows(y_s, b, N_PROMPT_SEQ, steps_p)

    @pl.when(i >= N_OT_PROMPT)
    def _():
        k0 = (i - N_OT_PROMPT) * steps_s
        for b in range(N_SAMPLE_SEQ):
            rows = pl.ds(pl.multiple_of(b * SAMPLE_LEN + k0, steps_s), steps_s)
            ys_ref[rows, :] = _seq_rows(y_s, b, N_SAMPLE_SEQ, steps_s)


def _combine(h, route, gw, ol, g):
    tile = lambda n: pl.BlockSpec((TO, n), lambda i: (i, 0))
    return pl.pallas_call(
        _combine_kernel,
        grid=(N_OT,),
        in_specs=[tile(D_MODEL), tile(ROUTE_LANES), tile(ROUTE_LANES),
                  pl.BlockSpec((LROWS, D_MODEL), lambda i: (i, 0)),
                  pl.BlockSpec((1, D_MODEL), lambda i: (0, 0))],
        out_specs=[
            pl.BlockSpec((N_PROMPT_SEQ, TO // N_PROMPT_SEQ, D_MODEL),
                         lambda i: (0, jnp.minimum(i, N_OT_PROMPT - 1), 0)),
            pl.BlockSpec((N_SAMPLE_TOK, D_MODEL), lambda i: (0, 0)),
        ],
        out_shape=(jax.ShapeDtypeStruct((N_PROMPT_SEQ, PROMPT_LEN, D_MODEL), F32),
                   jax.ShapeDtypeStruct((N_SAMPLE_TOK, D_MODEL), F32)),
        scratch_shapes=[pltpu.VMEM((MODEL_LB, TO, LANES), F32)],
        compiler_params=_cparams(("arbitrary",), 48),
        name="combine_norm",
    )(h, route, gw, ol, g)


def _dispatch_plan(cnt):
    experts = jnp.arange(N_EXPERTS, dtype=I32)
    cnt = cnt[:, 0, :N_EXPERTS]
    run = ((cnt + SUBLANES - 1) // SUBLANES) * SUBLANES
    local_start = jnp.cumsum(run, axis=1) - run
    local_used = jnp.sum(run, axis=1)
    total = jnp.sum(run, axis=0)
    padded = ((total + TME - 1) // TME) * TME
    ends = jnp.cumsum(padded)
    starts = ends - padded
    run_start = starts[None, :] + jnp.cumsum(run, axis=0) - run
    n_used = (ends[-1] // TME).astype(I32)
    tile = jnp.arange(N_ETILES, dtype=I32)
    tile_start = tile * TME
    te = jnp.minimum(jnp.sum((ends[None, :] <= tile_start[:, None]).astype(I32), axis=1), N_EXPERTS - 1)
    mine = (te[:, None] == experts[None, :])[:, None, :]
    of_tile = lambda a: jnp.sum(jnp.where(mine, a[None], 0), axis=-1)
    s0 = of_tile(run_start)
    lo = jnp.maximum(s0, tile_start[:, None])
    hi = jnp.minimum(s0 + of_tile(run), tile_start[:, None] + TME)
    plen = jnp.where((tile < n_used)[:, None], jnp.maximum(hi - lo, 0), 0)
    live = plen > 0
    psrc = jnp.where(live, jnp.arange(N_OT, dtype=I32)[None, :] * LROWS + of_tile(local_start) + lo - s0, 0)
    pdst = jnp.where(live, lo - tile_start[:, None], 0)
    last = jnp.sum(jnp.where(tile == n_used - 1, te, 0))
    te = jnp.where(tile < n_used, te, last)
    flat = lambda a: a.reshape(-1).astype(I32)
    return (te.astype(I32), flat(jnp.sum(plen, axis=1)), n_used.reshape(1),
            flat(psrc), flat(pdst), flat(plen), local_used.astype(I32))


def kernel(x_prompt, x_sample, state_ssm_re, state_ssm_im, cache_conv, norm_mix_g, w_in, lam_re, lam_im, log_dt, b_re, b_im, c_re, c_im, d_skip, w_ssm_glu, conv_w, conv_b, conv_ln_g, conv_ln_b, w_conv_out, w_out, norm_ffn_g, w_router_group, b_router_group, w_router_expert, b_router_expert, w_exp_gate, w_exp_up, w_exp_down, norm_final_g):
    assert w_in.shape[0] == 1, "single-layer trunk"
    xp = x_prompt.reshape(N_PROMPT_TOK, D_MODEL)
    xs = x_sample.reshape(N_SAMPLE_TOK, D_MODEL)
    w_in_bf = w_in[0].astype(BF16)
    row = lambda a: a.reshape(1, -1)

    ar_rep, ai_rep, bbr, bbi = _discretise(lam_re[0], lam_im[0], log_dt[0], b_re[0], b_im[0])
    gph = lambda a: a.reshape(N_SSM_GROUPS, SSM_STATE, SSM_GROUP)
    pick = lambda a: gph(a)[:, :, 0].reshape(GROUP_BLOCKS, 1, GB_STATE)
    bm = jnp.concatenate([_block_diag_in(gph(bbr)), _block_diag_in(gph(bbi))], axis=-1).astype(BF16)
    cm = jnp.concatenate([_block_diag_out(c_re[0]), -_block_diag_out(c_im[0])], axis=1).astype(BF16)
    dsk = d_skip[0].reshape(GROUP_BLOCKS, 1, GB_IN)
    ar, ai = pick(ar_rep), pick(ai_rep)
    s0r = state_ssm_re[0].reshape(N_SAMPLE_SEQ, -1)
    s0i = state_ssm_im[0].reshape(N_SAMPLE_SEQ, -1)

    u3, v3, xn3 = _inproj(xp, xs, row(norm_mix_g[0]), w_in_bf)
    actp, pfr, pfi = _scan_prompt(u3, bm, cm, ar, ai, dsk)
    acts, sfr, sfi = _scan_sample(u3, bm, cm, ar, ai, dsk, s0r, s0i)

    hist_s = cache_conv[0].transpose(1, 0, 2).reshape(HIST_S, D_CONV)
    cv, ncp, ncs = _conv(v3, hist_s, conv_w[0], row(conv_b[0]), conv_ln_g[0], conv_ln_b[0])

    m = _merge(xn3, actp, acts, cv, w_in_bf, w_ssm_glu[0].astype(BF16), w_conv_out[0].astype(BF16))

    pad_lanes = ROUTE_LANES - N_EXP_GROUPS - N_EXPERTS
    w_router = jnp.concatenate(
        [w_router_group[0], w_router_expert[0], jnp.zeros((D_MODEL, pad_lanes), F32)], axis=1)
    b_router = jnp.concatenate(
        [b_router_group[0], b_router_expert[0], jnp.zeros((pad_lanes,), F32)]).reshape(1, ROUTE_LANES)
    h, xl, route, gw, cnt = _outproj(m, xp, xs, w_out[0].astype(BF16), row(norm_ffn_g[0]),
                                     w_router, b_router)

    ol = _experts(_dispatch_plan(cnt), xl, w_exp_gate[0], w_exp_up[0], w_exp_down[0])
    yp, ys = _combine(h, route, gw, ol, row(norm_final_g))

    st = lambda a, n: a.reshape(1, n, N_SSM_GROUPS, SSM_STATE)
    pst = lambda a: st(a.transpose(1, 0, 2), N_PROMPT_SEQ)
    ncp = ncp.reshape(CONV_BUF, N_PROMPT_SEQ, D_CONV).transpose(1, 0, 2)[None]
    ncs = ncs.reshape(CONV_BUF, N_SAMPLE_SEQ, D_CONV).transpose(1, 0, 2)[None]
    return (yp, ys.reshape(N_SAMPLE_SEQ, SAMPLE_LEN, D_MODEL), pst(pfr), pst(pfi), ncp,
            st(sfr, N_SAMPLE_SEQ), st(sfi, N_SAMPLE_SEQ), ncs)
```

```python
import functools

import jax
import jax.numpy as jnp
from jax import lax
from jax.experimental import pallas as pl
from jax.experimental.pallas import tpu as pltpu

F32 = jnp.float32
BF16 = jnp.bfloat16
I32 = jnp.int32

SUBLANES = 8
LANES = 128

D_MODEL = 2048
D_SSM = 1024
D_CONV = 1024
SSM_GROUP = 16
N_SSM_GROUPS = 64
SSM_STATE = 64
CONV_WIDTH = 31
CONV_BUF = CONV_WIDTH - 1
N_EXP_GROUPS = 4
EXP_PER_GROUP = 8
N_EXPERTS = 32
D_EXPERT = 256
EPS = 1e-6

N_PROMPT_SEQ = 4
PROMPT_LEN = 2048
N_SAMPLE_SEQ = 16
SAMPLE_LEN = 32
N_PROMPT_TOK = N_PROMPT_SEQ * PROMPT_LEN
N_SAMPLE_TOK = N_SAMPLE_SEQ * SAMPLE_LEN
N_TOK = N_PROMPT_TOK + N_SAMPLE_TOK

TM = 512
N_PROMPT_TILES = N_PROMPT_TOK // TM
N_TILES = N_TOK // TM
PROMPT_STEPS = TM // N_PROMPT_SEQ
GROUP_BLOCKS = 4
GROUPS_PER_BLOCK = N_SSM_GROUPS // GROUP_BLOCKS
GB_IN = GROUPS_PER_BLOCK * SSM_GROUP
GB_STATE = GROUPS_PER_BLOCK * SSM_STATE
GB_LB = GB_STATE // LANES
HIST_P = CONV_BUF * N_PROMPT_SEQ
HIST_S = CONV_BUF * N_SAMPLE_SEQ
ROUTE_LANES = 128
N_PAIRS = 2 * N_TOK
TO = 256
N_OT = N_TOK // TO
N_OT_PROMPT = N_PROMPT_TOK // TO
HALVES = TM // TO
LROWS = 768
TME = 256
N_ETILES = -(-(N_PAIRS + N_OT * N_EXPERTS * (SUBLANES - 1) + N_EXPERTS * (TME - 1)) // TME)
MODEL_LB = D_MODEL // LANES
SSM_LB = D_SSM // LANES
CONV_LB = D_CONV // LANES


def _cparams(sem, vmem_mb):
    return pltpu.CompilerParams(dimension_semantics=sem, vmem_limit_bytes=vmem_mb * 1024 * 1024)


def _resident(shape):
    return pl.BlockSpec(shape, lambda *_: (0,) * len(shape), pipeline_mode=pl.Buffered(1))


def _x_tile_specs(steps=PROMPT_STEPS):
    blocks_per_seq = PROMPT_LEN // steps
    n_prompt = N_PROMPT_SEQ * blocks_per_seq // N_PROMPT_SEQ

    def prompt(b):
        return pl.BlockSpec(
            (steps, D_MODEL),
            lambda i, *_: (b * blocks_per_seq + jnp.minimum(i, n_prompt - 1), 0))

    return [prompt(b) for b in range(N_PROMPT_SEQ)] + [_resident((N_SAMPLE_TOK, D_MODEL))]


def _to_mixer_order(ref3, val, n_seq, steps):
    for c in range(val.shape[1] // LANES):
        for b in range(n_seq):
            ref3[c, pl.ds(b, steps, stride=n_seq), :] = val[b * steps:(b + 1) * steps,
                                                            c * LANES:(c + 1) * LANES]


def _seq_rows(ref3, b, n_seq, steps):
    return jnp.concatenate(
        [ref3[c, pl.ds(b, steps, stride=n_seq), :] for c in range(ref3.shape[0])], axis=1)


def _tile_to_mixer_order(i, ref3, xs, sample):
    @pl.when(i < N_PROMPT_TILES)
    def _():
        _to_mixer_order(ref3, xs(), N_PROMPT_SEQ, PROMPT_STEPS)

    @pl.when(i >= N_PROMPT_TILES)
    def _():
        _to_mixer_order(ref3, sample(), N_SAMPLE_SEQ, SAMPLE_LEN)


def _lane_blocks(ref3):
    return jnp.concatenate([ref3[c] for c in range(ref3.shape[0])], axis=1)


def _pair_specs(rows, cols, n_prompt_tiles):
    p = pl.BlockSpec((rows, cols), lambda i, *_: (jnp.minimum(i, n_prompt_tiles - 1), 0))
    s = pl.BlockSpec((rows, cols), lambda i, *_: (jnp.maximum(i - n_prompt_tiles, 0), 0))
    return p, s


def _disc_kernel(lr_ref, li_ref, ldt_ref, br_ref, bi_ref, ar_ref, ai_ref, bbr_ref, bbi_ref):
    lr = lr_ref[...]
    li = li_ref[...]
    dt = jnp.exp(ldt_ref[...])
    mag = jnp.exp(lr * dt)
    ar = mag * jnp.cos(li * dt)
    ai = mag * jnp.sin(li * dt)
    den = lr * lr + li * li
    nr = ar - 1.0
    cr = (nr * lr + ai * li) / den
    ci = (ai * lr - nr * li) / den
    br = br_ref[...]
    bi = bi_ref[...]
    ar_ref[...] = ar
    ai_ref[...] = ai
    bbr_ref[...] = cr * br - ci * bi
    bbi_ref[...] = cr * bi + ci * br


def _discretise(lam_re, lam_im, log_dt, b_re, b_im):
    shp = jax.ShapeDtypeStruct((N_SSM_GROUPS, SSM_STATE * SSM_GROUP), F32)
    rep = lambda a: jnp.repeat(a, SSM_GROUP, axis=-1)
    ldt = jnp.broadcast_to(log_dt[:, None], (N_SSM_GROUPS, SSM_STATE * SSM_GROUP))
    return pl.pallas_call(_disc_kernel, out_shape=(shp, shp, shp, shp), name="s5_discretise")(
        rep(lam_re), rep(lam_im), ldt,
        b_re.reshape(N_SSM_GROUPS, -1), b_im.reshape(N_SSM_GROUPS, -1))


def _block_diag_in(bb):
    b4 = bb.reshape(GROUP_BLOCKS, GROUPS_PER_BLOCK, SSM_STATE, SSM_GROUP).transpose(0, 1, 3, 2)
    eye = jnp.eye(GROUPS_PER_BLOCK, dtype=bool)[None, :, None, :, None]
    full = jnp.where(eye, b4[:, :, :, None, :], 0.0)
    return full.reshape(GROUP_BLOCKS, GB_IN, GB_STATE)


def _block_diag_out(c):
    c4 = c.reshape(GROUP_BLOCKS, GROUPS_PER_BLOCK, SSM_GROUP, SSM_STATE).transpose(0, 1, 3, 2)
    eye = jnp.eye(GROUPS_PER_BLOCK, dtype=bool)[None, :, None, :, None]
    full = jnp.where(eye, c4[:, :, :, None, :], 0.0)
    return full.reshape(GROUP_BLOCKS, GB_STATE, GB_IN)


def _rmsnorm_rows(x, g):
    r = lax.rsqrt(jnp.mean(x * x, axis=-1, keepdims=True) + EPS)
    return x * r * g


def _inproj_kernel(x0_ref, x1_ref, x2_ref, x3_ref, xs_ref, g_ref, w_ref,
                   u_ref, v_ref, xn_ref, xn_s, va_s):
    i = pl.program_id(0)
    j = pl.program_id(1)
    prompt_rows = lambda: jnp.concatenate(
        [x0_ref[...], x1_ref[...], x2_ref[...], x3_ref[...]], axis=0)

    @pl.when(j == 0)
    def _():
        def norm(x):
            xn = _rmsnorm_rows(x, g_ref[...])
            xn_s[...] = xn.astype(BF16)
            return xn
        _tile_to_mixer_order(i, xn_ref, lambda: norm(prompt_rows()), lambda: norm(xs_ref[...]))

    p = jnp.dot(xn_s[...], w_ref[...], preferred_element_type=F32)

    @pl.when(j == 0)
    def _():
        _tile_to_mixer_order(i, u_ref, lambda: p, lambda: p)

    @pl.when(j == 1)
    def _():
        va_s[...] = p

    @pl.when(j == 2)
    def _():
        v = va_s[...] * jax.nn.sigmoid(p)
        _tile_to_mixer_order(i, v_ref, lambda: v, lambda: v)


def _inproj(xp, xs, g, w_in_bf):
    blocked = lambda nb: pl.BlockSpec((nb, TM, LANES), lambda i, j: (0, i, 0))
    return pl.pallas_call(
        _inproj_kernel,
        grid=(N_TILES, 3),
        in_specs=_x_tile_specs() + [
            pl.BlockSpec((1, D_MODEL), lambda i, j: (0, 0)),
            pl.BlockSpec((D_MODEL, D_SSM), lambda i, j: (0, j)),
        ],
        out_specs=[blocked(SSM_LB), blocked(CONV_LB), blocked(MODEL_LB)],
        out_shape=(jax.ShapeDtypeStruct((SSM_LB, N_TOK, LANES), F32),
                   jax.ShapeDtypeStruct((CONV_LB, N_TOK, LANES), F32),
                   jax.ShapeDtypeStruct((MODEL_LB, N_TOK, LANES), F32)),
        scratch_shapes=[pltpu.VMEM((TM, D_MODEL), BF16), pltpu.VMEM((TM, D_CONV), F32)],
        compiler_params=_cparams(("arbitrary", "arbitrary"), 52),
        name="in_proj",
    )(xp, xp, xp, xp, xs, g, w_in_bf)


N_SCAN_TILES = GROUP_BLOCKS * N_PROMPT_TILES
SCAN_STAGES = 3


def _scan_prompt_kernel(ua_ref, uc_ref, bm_ref, cm_ref, ar_ref, ai_ref, dsk_ref,
                        act_ref, pfr_ref, pfi_ref, x0, x1, x2, cst, car, fin):
    n = pl.program_id(0)
    q_s = jnp.clip(n - 1, 0, N_SCAN_TILES - 1)
    t_s = q_s % N_PROMPT_TILES
    valid_s = jnp.logical_and(n >= 1, n <= N_SCAN_TILES)

    @pl.when(n == 0)
    def _():
        for buf in (x0, x1, x2):
            buf[...] = jnp.zeros_like(buf)
        car[...] = jnp.zeros_like(car)
        fin[...] = jnp.zeros_like(fin)

    shape = (SUBLANES, GB_STATE)
    top = lax.broadcasted_iota(I32, shape, 0) < N_PROMPT_SEQ
    arb = jnp.broadcast_to(ar_ref[0], shape)
    aib = jnp.broadcast_to(ai_ref[0], shape)
    cst[0] = jnp.where(top, 0.0, arb)
    cst[1] = jnp.where(top, 0.0, aib)
    cst[2] = jnp.where(top, arb, arb * arb - aib * aib)
    cst[3] = jnp.where(top, aib, 2.0 * arb * aib)

    def project_in(xa):
        u = jnp.concatenate([ua_ref[0], ua_ref[1]], axis=1).astype(BF16)
        x = jnp.dot(u, bm_ref[0], preferred_element_type=F32)
        for c in range(2 * GB_LB):
            xa[c] = x[:, c * LANES:(c + 1) * LANES]

    def project_out(xc):
        u = jnp.concatenate([uc_ref[0], uc_ref[1]], axis=1)
        h = _lane_blocks(xc).astype(BF16)
        y = jnp.dot(h, cm_ref[0], preferred_element_type=F32) + dsk_ref[0] * u
        act_ref[...] = jax.nn.gelu(y).astype(BF16)

    def recur(xs):
        top1 = lax.broadcasted_iota(I32, (SUBLANES, LANES), 0) < N_PROMPT_SEQ
        first = t_s == 0
        hr = [jnp.where(first, 0.0, car[:, c * LANES:(c + 1) * LANES]) for c in range(GB_LB)]
        hi = [jnp.where(first, 0.0, car[:, GB_STATE + c * LANES:GB_STATE + (c + 1) * LANES])
              for c in range(GB_LB)]
        for r in range(TM // SUBLANES):
            rows = slice(r * SUBLANES, (r + 1) * SUBLANES)
            for c in range(GB_LB):
                cols = slice(c * LANES, (c + 1) * LANES)
                xr = xs[c, rows, :]
                xi = xs[GB_LB + c, rows, :]
                a1r, a1i, a2r, a2i = cst[0, :, cols], cst[1, :, cols], cst[2, :, cols], cst[3, :, cols]
                sxr = pltpu.roll(xr, N_PROMPT_SEQ, axis=0)
                sxi = pltpu.roll(xi, N_PROMPT_SEQ, axis=0)
                tr = xr + a1r * sxr - a1i * sxi
                ti = xi + a1r * sxi + a1i * sxr
                nr = tr + a2r * hr[c] - a2i * hi[c]
                ni = ti + a2r * hi[c] + a2i * hr[c]
                xs[c, rows, :] = nr
                xs[GB_LB + c, rows, :] = ni
                hr[c] = jnp.where(top1, pltpu.roll(nr, N_PROMPT_SEQ, axis=0), nr)
                hi[c] = jnp.where(top1, pltpu.roll(ni, N_PROMPT_SEQ, axis=0), ni)
        state = jnp.concatenate(hr + hi, axis=1)
        car[...] = state
        last = jnp.logical_and(valid_s, t_s == N_PROMPT_TILES - 1)
        fin[...] = jnp.where(last, state, fin[...])
        pfr_ref[0] = fin[N_PROMPT_SEQ:SUBLANES, 0:GB_STATE]
        pfi_ref[0] = fin[N_PROMPT_SEQ:SUBLANES, GB_STATE:2 * GB_STATE]

    bufs = (x0, x1, x2)
    for r in range(SCAN_STAGES):
        @pl.when(n % SCAN_STAGES == r)
        def _():
            project_out(bufs[(r + 1) % SCAN_STAGES])
            recur(bufs[(r + 2) % SCAN_STAGES])
            project_in(bufs[r])


def _scan_prompt(u3, bm, cm, ar, ai, dsk):
    q_a = lambda n: jnp.minimum(n, N_SCAN_TILES - 1)
    q_s = lambda n: jnp.clip(n - 1, 0, N_SCAN_TILES - 1)
    q_c = lambda n: jnp.clip(n - 2, 0, N_SCAN_TILES - 1)
    gb = lambda q: q // N_PROMPT_TILES
    tt = lambda q: q % N_PROMPT_TILES
    u_spec = lambda q: pl.BlockSpec((GB_IN // LANES, TM, LANES), lambda n: (gb(q(n)), tt(q(n)), 0))
    gb3 = lambda last, q: pl.BlockSpec((1,) + last, lambda n: (gb(q(n)), 0, 0))
    state = jax.ShapeDtypeStruct((GROUP_BLOCKS, N_PROMPT_SEQ, GB_STATE), F32)
    xbuf = pltpu.VMEM((2 * GB_LB, TM, LANES), F32)
    return pl.pallas_call(
        _scan_prompt_kernel,
        grid=(N_SCAN_TILES + SCAN_STAGES - 1,),
        in_specs=[
            u_spec(q_a), u_spec(q_c),
            gb3((GB_IN, 2 * GB_STATE), q_a),
            gb3((2 * GB_STATE, GB_IN), q_c),
            gb3((1, GB_STATE), q_s),
            gb3((1, GB_STATE), q_s),
            gb3((1, GB_IN), q_c),
        ],
        out_specs=[
            pl.BlockSpec((TM, GB_IN), lambda n: (tt(q_c(n)), gb(q_c(n)))),
            gb3((N_PROMPT_SEQ, GB_STATE), q_s),
            gb3((N_PROMPT_SEQ, GB_STATE), q_s),
        ],
        out_shape=(jax.ShapeDtypeStruct((N_PROMPT_TOK, D_SSM), BF16), state, state),
        scratch_shapes=[xbuf, xbuf, xbuf,
                        pltpu.VMEM((4, SUBLANES, GB_STATE), F32),
                        pltpu.VMEM((SUBLANES, 2 * GB_STATE), F32),
                        pltpu.VMEM((SUBLANES, 2 * GB_STATE), F32)],
        compiler_params=_cparams(("arbitrary",), 48),
        name="s5_scan_prompt",
    )(u3, u3, bm, cm, ar, ai, dsk)


def _scan_sample_kernel(u_ref, bm_ref, cm_ref, ar_ref, ai_ref, dsk_ref, s0r_ref, s0i_ref,
                        act_ref, sfr_ref, sfi_ref, xs_ref):
    half_lb = GB_LB // 2
    half_cols = half_lb * LANES
    u = jnp.concatenate([u_ref[0], u_ref[1]], axis=1)
    x = jnp.dot(u.astype(BF16), bm_ref[0], preferred_element_type=F32)
    for c in range(2 * GB_LB):
        xs_ref[c] = x[:, c * LANES:(c + 1) * LANES]

    def load_half(rows, half, imag):
        c0 = imag * GB_LB + half * half_lb
        return jnp.concatenate([xs_ref[c0 + c, rows, :] for c in range(half_lb)], axis=1)

    def store_half(rows, half, imag, val):
        c0 = imag * GB_LB + half * half_lb
        for c in range(half_lb):
            xs_ref[c0 + c, rows, :] = val[:, c * LANES:(c + 1) * LANES]

    shape = (N_SAMPLE_SEQ, half_cols)
    for half in range(2):
        cols = slice(half * half_cols, (half + 1) * half_cols)
        arb = jnp.broadcast_to(ar_ref[0][:, cols], shape)
        aib = jnp.broadcast_to(ai_ref[0][:, cols], shape)

        def body(k, carry):
            hr, hi = carry
            rows = pl.ds(pl.multiple_of(k * N_SAMPLE_SEQ, N_SAMPLE_SEQ), N_SAMPLE_SEQ)
            xr = load_half(rows, half, 0)
            xi = load_half(rows, half, 1)
            nhr = xr + arb * hr - aib * hi
            nhi = xi + arb * hi + aib * hr
            store_half(rows, half, 0, nhr)
            store_half(rows, half, 1, nhi)
            return nhr, nhi

        hr, hi = lax.fori_loop(0, SAMPLE_LEN, body, (s0r_ref[:, cols], s0i_ref[:, cols]))
        sfr_ref[:, cols] = hr
        sfi_ref[:, cols] = hi

    h = _lane_blocks(xs_ref).astype(BF16)
    y = jnp.dot(h, cm_ref[0], preferred_element_type=F32) + dsk_ref[0] * u
    act_ref[...] = jax.nn.gelu(y).astype(BF16)


def _scan_sample(u3, bm, cm, ar, ai, dsk, s0r, s0i):
    gb3 = lambda last: pl.BlockSpec((1,) + last, lambda g: (g, 0, 0))
    state = lambda: pl.BlockSpec((N_SAMPLE_SEQ, GB_STATE), lambda g: (0, g))
    sst = jax.ShapeDtypeStruct((N_SAMPLE_SEQ, N_SSM_GROUPS * SSM_STATE), F32)
    return pl.pallas_call(
        _scan_sample_kernel,
        grid=(GROUP_BLOCKS,),
        in_specs=[
            pl.BlockSpec((GB_IN // LANES, TM, LANES), lambda g: (g, N_PROMPT_TILES, 0)),
            gb3((GB_IN, 2 * GB_STATE)), gb3((2 * GB_STATE, GB_IN)),
            gb3((1, GB_STATE)), gb3((1, GB_STATE)), gb3((1, GB_IN)),
            state(), state(),
        ],
        out_specs=[pl.BlockSpec((TM, GB_IN), lambda g: (0, g)), state(), state()],
        out_shape=(jax.ShapeDtypeStruct((N_SAMPLE_TOK, D_SSM), BF16), sst, sst),
        scratch_shapes=[pltpu.VMEM((2 * GB_LB, TM, LANES), F32)],
        compiler_params=_cparams(("arbitrary",), 32),
        name="s5_scan_sample",
    )(u3, bm, cm, ar, ai, dsk, s0r, s0i)


CONV_ROWS = 64
EXT_ROWS = HIST_S + TM


def _conv_kernel(v_ref, hist_ref, w_ref, b_ref, lg_ref, lb_ref,
                 cv_ref, ncp_ref, ncs_ref, ext_ref, sh_ref, acc_ref):
    t = pl.program_id(0)

    def taps(n_seq):
        half = SUBLANES // 2
        if n_seq % SUBLANES:
            lo = HIST_S - HIST_P - SUBLANES
            sh_ref[:, lo:EXT_ROWS - SUBLANES, :] = ext_ref[:, lo + half:EXT_ROWS - half, :]

        def body(rb, _):
            r0 = pl.multiple_of(rb * CONV_ROWS, CONV_ROWS)
            for cb in range(CONV_LB):
                cols = slice(cb * LANES, (cb + 1) * LANES)
                acc = None
                for j in range(CONV_WIDTH):
                    start = HIST_S - (CONV_BUF - j) * n_seq
                    src = ext_ref
                    if start % SUBLANES:
                        src, start = sh_ref, start - half
                    rows = pl.ds(pl.multiple_of(r0 + start, SUBLANES), CONV_ROWS)
                    term = src[cb, rows, :] * w_ref[j:j + 1, cols]
                    acc = term if acc is None else acc + term
                acc_ref[cb, pl.ds(r0, CONV_ROWS), :] = acc + b_ref[:, cols]
            return 0

        lax.fori_loop(0, TM // CONV_ROWS, body, 0)

    @pl.when(t == 0)
    def _():
        ext_ref[:, 0:HIST_S, :] = jnp.zeros((CONV_LB, HIST_S, LANES), F32)

    @pl.when(t == N_PROMPT_TILES)
    def _():
        for c in range(CONV_LB):
            ext_ref[c, 0:HIST_S, :] = hist_ref[:, c * LANES:(c + 1) * LANES]

    ext_ref[:, HIST_S:EXT_ROWS, :] = v_ref[...]

    @pl.when(t < N_PROMPT_TILES)
    def _():
        taps(N_PROMPT_SEQ)

    @pl.when(t == N_PROMPT_TILES)
    def _():
        taps(N_SAMPLE_SEQ)

    y = acc_ref[...]
    mu = jnp.sum(jnp.sum(y, axis=0), axis=-1, keepdims=True) / D_CONV
    yc = y - mu[None]
    var = jnp.sum(jnp.sum(yc * yc, axis=0), axis=-1, keepdims=True) / D_CONV
    z = jax.nn.silu(yc * lax.rsqrt(var + EPS)[None] * lg_ref[...] + lb_ref[...])
    for c in range(CONV_LB):
        cv_ref[:, c * LANES:(c + 1) * LANES] = z[c].astype(BF16)

    @pl.when(t == N_PROMPT_TILES - 1)
    def _():
        for c in range(CONV_LB):
            ncp_ref[:, c * LANES:(c + 1) * LANES] = ext_ref[c, EXT_ROWS - HIST_P:EXT_ROWS, :]

    @pl.when(t < N_PROMPT_TILES - 1)
    def _():
        ext_ref[:, HIST_S - HIST_P:HIST_S, :] = ext_ref[:, EXT_ROWS - HIST_P:EXT_ROWS, :]

    @pl.when(t == N_PROMPT_TILES)
    def _():
        for c in range(CONV_LB):
            ncs_ref[:, c * LANES:(c + 1) * LANES] = ext_ref[c, EXT_ROWS - HIST_S:EXT_ROWS, :]


def _conv(v3, hist_s, conv_w, conv_b, ln_g, ln_b):
    row = lambda n: pl.BlockSpec((n, D_CONV), lambda t: (0, 0))
    lane3 = pl.BlockSpec((CONV_LB, 1, LANES), lambda t: (0, 0, 0))
    ext = pltpu.VMEM((CONV_LB, EXT_ROWS, LANES), F32)
    return pl.pallas_call(
        _conv_kernel,
        grid=(N_TILES,),
        in_specs=[pl.BlockSpec((CONV_LB, TM, LANES), lambda t: (0, t, 0)),
                  row(HIST_S), row(CONV_WIDTH), row(1), lane3, lane3],
        out_specs=[pl.BlockSpec((TM, D_CONV), lambda t: (t, 0)), row(HIST_P), row(HIST_S)],
        out_shape=(jax.ShapeDtypeStruct((N_TOK, D_CONV), BF16),
                   jax.ShapeDtypeStruct((HIST_P, D_CONV), F32),
                   jax.ShapeDtypeStruct((HIST_S, D_CONV), F32)),
        scratch_shapes=[ext, ext, pltpu.VMEM((CONV_LB, TM, LANES), F32)],
        compiler_params=_cparams(("arbitrary",), 48),
        name="conv_branch",
    )(v3, hist_s, conv_w, conv_b,
      ln_g.reshape(CONV_LB, 1, LANES), ln_b.reshape(CONV_LB, 1, LANES))


MERGE_COLS = 512
N_MERGE = D_MODEL // MERGE_COLS


def _merge_kernel(xn_ref, actp_ref, acts_ref, cv_ref, wga_ref, wgb_ref, wg1_ref, wg2_ref, wco_ref,
                  m_ref, xn_s, act_s):
    i = pl.program_id(0)
    j = pl.program_id(1)

    @pl.when(j == 0)
    def _():
        xn_s[...] = _lane_blocks(xn_ref).astype(BF16)

        @pl.when(i < N_PROMPT_TILES)
        def _():
            act_s[...] = actp_ref[...]

        @pl.when(i >= N_PROMPT_TILES)
        def _():
            act_s[...] = acts_ref[...]

    xn = xn_s[...]
    act = act_s[...]
    dot = functools.partial(jnp.dot, preferred_element_type=F32)
    ya = dot(act, wg1_ref[...]) * jax.nn.sigmoid(dot(act, wg2_ref[...]))
    yb = dot(cv_ref[...], wco_ref[...])
    m = jax.nn.sigmoid(dot(xn, wga_ref[...])) * ya + jax.nn.sigmoid(dot(xn, wgb_ref[...])) * yb
    m_ref[...] = m.astype(BF16)


def _merge(xn3, actp, acts, cv, w_in_bf, w_glu_bf, w_co_bf):
    ga0 = (D_SSM + 2 * D_CONV) // MERGE_COLS
    gb0 = ga0 + N_MERGE
    return pl.pallas_call(
        _merge_kernel,
        grid=(N_TILES, N_MERGE),
        in_specs=[pl.BlockSpec((MODEL_LB, TM, LANES), lambda i, j: (0, i, 0))]
        + list(_pair_specs(TM, D_SSM, N_PROMPT_TILES)) + [
            pl.BlockSpec((TM, D_CONV), lambda i, j: (i, 0)),
            pl.BlockSpec((D_MODEL, MERGE_COLS), lambda i, j: (0, ga0 + j)),
            pl.BlockSpec((D_MODEL, MERGE_COLS), lambda i, j: (0, gb0 + j)),
            pl.BlockSpec((D_SSM, MERGE_COLS), lambda i, j: (0, j)),
            pl.BlockSpec((D_SSM, MERGE_COLS), lambda i, j: (0, N_MERGE + j)),
            pl.BlockSpec((D_CONV, MERGE_COLS), lambda i, j: (0, j)),
        ],
        out_specs=pl.BlockSpec((TM, MERGE_COLS), lambda i, j: (i, j)),
        out_shape=jax.ShapeDtypeStruct((N_TOK, D_MODEL), BF16),
        scratch_shapes=[pltpu.VMEM((TM, D_MODEL), BF16), pltpu.VMEM((TM, D_SSM), BF16)],
        compiler_params=_cparams(("arbitrary", "arbitrary"), 48),
        name="gated_merge",
    )(xn3, actp, acts, cv, w_in_bf, w_in_bf, w_glu_bf, w_glu_bf, w_co_bf)


def _outproj_kernel(m_ref, x0_ref, x1_ref, x2_ref, x3_ref, xs_ref, wo_ref, g_ref, wr_ref, br_ref,
                    h_ref, xl_ref, route_ref, gw_ref, cnt_ref, xr_s, lg0, lg1, xb0, xb1):
    i = pl.program_id(0)
    tile = jnp.minimum(i, N_OT - 1)
    steps_p = TO // N_PROMPT_SEQ
    steps_s = TO // N_SAMPLE_SEQ

    @pl.when(i == 0)
    def _():
        lg1[...] = jnp.zeros_like(lg1)
        xb1[...] = jnp.zeros_like(xb1)

    @pl.when(tile < N_OT_PROMPT)
    def _():
        x = jnp.concatenate([x0_ref[...], x1_ref[...], x2_ref[...], x3_ref[...]], axis=0)
        _to_mixer_order(xr_s, x, N_PROMPT_SEQ, steps_p)

    @pl.when(tile >= N_OT_PROMPT)
    def _():
        k0 = (tile - N_OT_PROMPT) * steps_s
        x = jnp.concatenate(
            [xs_ref[pl.ds(pl.multiple_of(b * SAMPLE_LEN + k0, steps_s), steps_s), :]
             for b in range(N_SAMPLE_SEQ)], axis=0)
        _to_mixer_order(xr_s, x, N_SAMPLE_SEQ, steps_s)

    sets = ((lg0, xb0), (lg1, xb1))
    for r in range(2):
        @pl.when(i % 2 == r)
        def _():
            lg_prev, xb_prev = sets[1 - r]
            _route_and_sort(lg_prev[...], xb_prev[...], xl_ref, route_ref, gw_ref, cnt_ref)
            _project(m_ref, xr_s, wo_ref, g_ref, wr_ref, br_ref, h_ref, *sets[r])


def _project(m_ref, xr_s, wo_ref, g_ref, wr_ref, br_ref, h_ref, lg_s, xb_s):
    dot = functools.partial(jnp.dot, preferred_element_type=F32)
    h = _lane_blocks(xr_s) + dot(m_ref[...], wo_ref[...])
    h_ref[...] = h
    xn = _rmsnorm_rows(h, g_ref[...])
    x_hi = xn.astype(BF16)
    x_lo = (xn - x_hi.astype(F32)).astype(BF16)
    w = wr_ref[...]
    w_hi = w.astype(BF16)
    w_lo = (w - w_hi.astype(F32)).astype(BF16)
    both = dot(x_hi, jnp.concatenate([w_hi, w_lo], axis=1))
    lg_s[...] = (both[:, :ROUTE_LANES] + (dot(x_lo, w_hi) + both[:, ROUTE_LANES:])) + br_ref[...]
    xb_s[...] = x_hi


def _route_and_sort(logits, x_hi, xl_ref, route_ref, gw_ref, cnt_ref):
    dot = functools.partial(jnp.dot, preferred_element_type=F32)
    lane = lax.broadcasted_iota(I32, logits.shape, 1)
    neg = -jnp.inf
    first = lambda hit: jnp.min(jnp.where(hit, lane, ROUTE_LANES), axis=-1, keepdims=True)
    gmask = lane < N_EXP_GROUPS
    lg = jnp.where(gmask, logits, neg)
    gmax = jnp.max(lg, axis=-1, keepdims=True)
    gsel = first(lg == gmax)
    psum = jnp.sum(jnp.where(gmask, jnp.exp(logits - gmax), 0.0), axis=-1, keepdims=True)
    pg_sel = 1.0 / psum
    e_lane = lane - N_EXP_GROUPS
    emask = (e_lane >= 0) & (e_lane < N_EXPERTS) & ((e_lane // EXP_PER_GROUP) == gsel)
    le = jnp.where(emask, logits, neg)
    v1 = jnp.max(le, axis=-1, keepdims=True)
    i1 = first(le == v1)
    le2 = jnp.where(lane == i1, neg, le)
    v2 = jnp.max(le2, axis=-1, keepdims=True)
    i2 = first(le2 == v2)
    z = jnp.exp(v2 - v1)
    w1 = pg_sel / (1.0 + z)
    w2 = pg_sel * z / (1.0 + z)
    e1 = i1 - N_EXP_GROUPS
    e2 = i2 - N_EXP_GROUPS
    gw_ref[...] = jnp.where(lane == 0, w1, jnp.where(lane == 1, w2, 0.0))
    oh1 = lane == e1
    oh2 = lane == e2
    hits = jnp.where(oh1 | oh2, 1.0, 0.0)
    rr = lax.broadcasted_iota(I32, (TO, TO), 0)
    cc = lax.broadcasted_iota(I32, (TO, TO), 1)
    before = dot(jnp.where(cc < rr, 1.0, 0.0).astype(BF16), hits.astype(BF16))
    counts = jnp.sum(hits, axis=0, keepdims=True)
    groups = jnp.floor((counts + (SUBLANES - 1.0)) * (1.0 / SUBLANES))
    er = lax.broadcasted_iota(I32, (ROUTE_LANES, ROUTE_LANES), 0)
    ec = lax.broadcasted_iota(I32, (ROUTE_LANES, ROUTE_LANES), 1)
    groups_before = dot(jnp.broadcast_to(groups, (SUBLANES, ROUTE_LANES)).astype(BF16),
                        jnp.where(er < ec, 1.0, 0.0).astype(BF16))[0:1]
    local = groups_before * SUBLANES + before
    lpos1 = jnp.sum(jnp.where(oh1, local, 0.0), axis=-1, keepdims=True)
    lpos2 = jnp.sum(jnp.where(oh2, local, 0.0), axis=-1, keepdims=True)
    route_ref[...] = jnp.where(lane == 0, e1, jnp.where(lane == 1, e2, jnp.where(
        lane == 2, lpos1.astype(I32), jnp.where(lane == 3, lpos2.astype(I32), 0))))
    cnt_ref[0] = counts.astype(I32)
    lpos_t = jnp.where(lane == 0, lpos1, jnp.where(lane == 1, lpos2, 0.0)).T.astype(I32)
    jrow = lax.broadcasted_iota(I32, (LROWS, TO), 0)
    sel = (jrow == lpos_t[0:1, :]) | (jrow == lpos_t[1:2, :])
    xl_ref[...] = dot(jnp.where(sel, 1.0, 0.0).astype(BF16), x_hi)


def _outproj(m, xp, xs, w_out_bf, g, w_router, b_router):
    cur = lambda i: jnp.minimum(i, N_OT - 1)
    prev = lambda i: jnp.maximum(i - 1, 0)
    tile = lambda n: pl.BlockSpec((TO, n), lambda i: (cur(i), 0))
    routed = lambda n: pl.BlockSpec((TO, n), lambda i: (prev(i), 0))
    return pl.pallas_call(
        _outproj_kernel,
        grid=(N_OT + 1,),
        in_specs=[tile(D_MODEL)] + _x_tile_specs(TO // N_PROMPT_SEQ) + [
            _resident((D_MODEL, D_MODEL)), _resident((1, D_MODEL)),
            _resident((D_MODEL, ROUTE_LANES)), _resident((1, ROUTE_LANES))],
        out_specs=[tile(D_MODEL),
                   pl.BlockSpec((LROWS, D_MODEL), lambda i: (prev(i), 0)),
                   routed(ROUTE_LANES), routed(ROUTE_LANES),
                   pl.BlockSpec((1, 1, ROUTE_LANES), lambda i: (prev(i), 0, 0))],
        out_shape=(jax.ShapeDtypeStruct((N_TOK, D_MODEL), F32),
                   jax.ShapeDtypeStruct((N_OT * LROWS, D_MODEL), F32),
                   jax.ShapeDtypeStruct((N_TOK, ROUTE_LANES), I32),
                   jax.ShapeDtypeStruct((N_TOK, ROUTE_LANES), F32),
                   jax.ShapeDtypeStruct((N_OT, 1, ROUTE_LANES), I32)),
        scratch_shapes=[pltpu.VMEM((MODEL_LB, TO, LANES), F32),
                        pltpu.VMEM((TO, ROUTE_LANES), F32), pltpu.VMEM((TO, ROUTE_LANES), F32),
                        pltpu.VMEM((TO, D_MODEL), BF16), pltpu.VMEM((TO, D_MODEL), BF16)],
        compiler_params=_cparams(("arbitrary",), 56),
        name="out_proj_router",
    )(m, xp, xp, xp, xp, xs, w_out_bf, g, w_router, b_router)


def _expert_kernel(te_ref, nrows_ref, nused_ref, psrc_ref, pdst_ref, plen_ref, tot_ref, kfirst_ref, kend_ref,
                   xl_hbm, wg_ref, wu_ref, wd_ref, ol_hbm,
                   xbuf, obuf, zbuf, wg_s, wu_s, wd_s, gsem, wsem, zsem):
    i = pl.program_id(0)
    nused = nused_ref[0]
    slot = i % 2
    rows8 = lambda v: pl.multiple_of(v, SUBLANES)

    def for_pieces(tile, fn):
        def body(k, _):
            idx = tile * N_OT + k
            n = plen_ref[idx]

            @pl.when(n > 0)
            def _():
                fn(rows8(psrc_ref[idx]), rows8(pdst_ref[idx]), rows8(n))
            return 0
        lax.fori_loop(kfirst_ref[tile], kend_ref[tile], body, 0)

    def gather(tile, s):
        for_pieces(tile, lambda src, dst, n: pltpu.make_async_copy(
            xl_hbm.at[pl.ds(src, n)], xbuf.at[s, pl.ds(dst, n)], gsem.at[s]).start())

    def scatter(tile, s):
        for_pieces(tile, lambda src, dst, n: pltpu.make_async_copy(
            obuf.at[s, pl.ds(dst, n)], ol_hbm.at[pl.ds(src, n)], wsem.at[s]).start())

    def wait_gather(tile, s):
        n = rows8(nrows_ref[tile])
        pltpu.make_async_copy(xl_hbm.at[pl.ds(0, n)], xbuf.at[s, pl.ds(0, n)], gsem.at[s]).wait()

    def wait_scatter(tile, s):
        n = rows8(nrows_ref[tile])
        pltpu.make_async_copy(obuf.at[s, pl.ds(0, n)], ol_hbm.at[pl.ds(0, n)], wsem.at[s]).wait()

    def tail_copy(t):
        used = rows8(tot_ref[t])
        n = rows8(LROWS - tot_ref[t])
        dst = rows8(t * LROWS + used)
        return pltpu.make_async_copy(zbuf.at[pl.ds(0, n)], ol_hbm.at[pl.ds(dst, n)], zsem.at[0])

    @pl.when(i == 0)
    def _():
        zbuf[...] = jnp.zeros_like(zbuf)
        xbuf[...] = jnp.zeros_like(xbuf)

        def start(t, _):
            tail_copy(t).start()
            return 0

        def wait(t, _):
            tail_copy(t).wait()
            return 0
        lax.fori_loop(0, N_OT, start, 0)
        lax.fori_loop(0, N_OT, wait, 0)
        gather(0, 0)

    @pl.when(i + 1 < nused)
    def _():
        gather(i + 1, 1 - slot)

    @pl.when(i < nused)
    def _():
        changed = jnp.logical_or(i == 0, te_ref[i] != te_ref[jnp.maximum(i - 1, 0)])

        @pl.when(changed)
        def _():
            wg_s[...] = wg_ref[0].astype(BF16)
            wu_s[...] = wu_ref[0].astype(BF16)
            wd_s[...] = wd_ref[0].astype(BF16)

        wait_gather(i, slot)

        @pl.when(i >= 2)
        def _():
            wait_scatter(i - 2, slot)

        xb = xbuf[slot].astype(BF16)
        hg = jnp.dot(xb, wg_s[...], preferred_element_type=F32)
        hu = jnp.dot(xb, wu_s[...], preferred_element_type=F32)
        hid = jax.nn.silu(hg) * hu
        obuf[slot] = jnp.dot(hid.astype(BF16), wd_s[...], preferred_element_type=F32)
        scatter(i, slot)

        @pl.when(i == nused - 1)
        def _():
            @pl.when(i >= 1)
            def _():
                wait_scatter(i - 1, 1 - slot)
            wait_scatter(i, slot)


def _experts(plan, xl, wg, wu, wd):
    wspec = lambda r, c: pl.BlockSpec((1, r, c), lambda i, te, *_: (te[i], 0, 0))
    grid_spec = pltpu.PrefetchScalarGridSpec(
        num_scalar_prefetch=len(plan),
        grid=(N_ETILES,),
        in_specs=[pl.BlockSpec(memory_space=pl.ANY), wspec(D_MODEL, D_EXPERT),
                  wspec(D_MODEL, D_EXPERT), wspec(D_EXPERT, D_MODEL)],
        out_specs=pl.BlockSpec(memory_space=pl.ANY),
        scratch_shapes=[
            pltpu.VMEM((2, TME, D_MODEL), F32),
            pltpu.VMEM((2, TME, D_MODEL), F32),
            pltpu.VMEM((LROWS - 2 * TO, D_MODEL), F32),
            pltpu.VMEM((D_MODEL, D_EXPERT), BF16),
            pltpu.VMEM((D_MODEL, D_EXPERT), BF16),
            pltpu.VMEM((D_EXPERT, D_MODEL), BF16),
            pltpu.SemaphoreType.DMA((2,)),
            pltpu.SemaphoreType.DMA((2,)),
            pltpu.SemaphoreType.DMA((1,)),
        ],
    )
    return pl.pallas_call(
        _expert_kernel,
        grid_spec=grid_spec,
        out_shape=jax.ShapeDtypeStruct((N_OT * LROWS, D_MODEL), F32),
        compiler_params=_cparams(("arbitrary",), 40),
        name="routed_experts",
    )(*plan, xl, wg, wu, wd)


def _combine_kernel(h_ref, route_ref, gw_ref, ol_ref, g_ref, yp_ref, ys_ref, y_s):
    i = pl.program_id(0)
    steps_p = TO // N_PROMPT_SEQ
    steps_s = TO // N_SAMPLE_SEQ
    route = route_ref[...]
    gw = gw_ref[...]
    jcol = lax.broadcasted_iota(I32, (TO, LROWS), 1)
    mix = (jnp.where(jcol == route[:, 2:3], gw[:, 0:1], 0.0)
           + jnp.where(jcol == route[:, 3:4], gw[:, 1:2], 0.0))
    moe = jnp.dot(mix.astype(BF16), ol_ref[...].astype(BF16), preferred_element_type=F32)
    y = _rmsnorm_rows(h_ref[...] + moe, g_ref[...])
    for c in range(MODEL_LB):
        y_s[c] = y[:, c * LANES:(c + 1) * LANES]

    @pl.when(i < N_OT_PROMPT)
    def _():
        for b in range(N_PROMPT_SEQ):
            yp_ref[b] = _seq_rows(y_s, b, N_PROMPT_SEQ, steps_p)

    @pl.when(i >= N_OT_PROMPT)
    def _():
        k0 = (i - N_OT_PROMPT) * steps_s
        for b in range(N_SAMPLE_SEQ):
            rows = pl.ds(pl.multiple_of(b * SAMPLE_LEN + k0, steps_s), steps_s)
            ys_ref[rows, :] = _seq_rows(y_s, b, N_SAMPLE_SEQ, steps_s)


def _combine(h, route, gw, ol, g):
    tile = lambda n: pl.BlockSpec((TO, n), lambda i: (i, 0))
    return pl.pallas_call(
        _combine_kernel,
        grid=(N_OT,),
        in_specs=[tile(D_MODEL), tile(ROUTE_LANES), tile(ROUTE_LANES),
                  pl.BlockSpec((LROWS, D_MODEL), lambda i: (i, 0)),
                  pl.BlockSpec((1, D_MODEL), lambda i: (0, 0))],
        out_specs=[
            pl.BlockSpec((N_PROMPT_SEQ, TO // N_PROMPT_SEQ, D_MODEL),
                         lambda i: (0, jnp.minimum(i, N_OT_PROMPT - 1), 0)),
            pl.BlockSpec((N_SAMPLE_TOK, D_MODEL), lambda i: (0, 0)),
        ],
        out_shape=(jax.ShapeDtypeStruct((N_PROMPT_SEQ, PROMPT_LEN, D_MODEL), F32),
                   jax.ShapeDtypeStruct((N_SAMPLE_TOK, D_MODEL), F32)),
        scratch_shapes=[pltpu.VMEM((MODEL_LB, TO, LANES), F32)],
        compiler_params=_cparams(("arbitrary",), 48),
        name="combine_norm",
    )(h, route, gw, ol, g)


def _dispatch_plan(cnt):
    experts = jnp.arange(N_EXPERTS, dtype=I32)
    cnt = cnt[:, 0, :N_EXPERTS]
    run = ((cnt + SUBLANES - 1) // SUBLANES) * SUBLANES
    local_start = jnp.cumsum(run, axis=1) - run
    local_used = jnp.sum(run, axis=1)
    total = jnp.sum(run, axis=0)
    padded = ((total + TME - 1) // TME) * TME
    ends = jnp.cumsum(padded)
    starts = ends - padded
    run_start = starts[None, :] + jnp.cumsum(run, axis=0) - run
    n_used = (ends[-1] // TME).astype(I32)
    tile = jnp.arange(N_ETILES, dtype=I32)
    tile_start = tile * TME
    te = jnp.minimum(jnp.sum((ends[None, :] <= tile_start[:, None]).astype(I32), axis=1), N_EXPERTS - 1)
    mine = (te[:, None] == experts[None, :])[:, None, :]
    of_tile = lambda a: jnp.sum(jnp.where(mine, a[None], 0), axis=-1)
    s0 = of_tile(run_start)
    lo = jnp.maximum(s0, tile_start[:, None])
    hi = jnp.minimum(s0 + of_tile(run), tile_start[:, None] + TME)
    plen = jnp.where((tile < n_used)[:, None], jnp.maximum(hi - lo, 0), 0)
    live = plen > 0
    psrc = jnp.where(live, jnp.arange(N_OT, dtype=I32)[None, :] * LROWS + of_tile(local_start) + lo - s0, 0)
    pdst = jnp.where(live, lo - tile_start[:, None], 0)
    last = jnp.sum(jnp.where(tile == n_used - 1, te, 0))
    te = jnp.where(tile < n_used, te, last)
    flat = lambda a: a.reshape(-1).astype(I32)
    none_yet = lambda a: jnp.sum((jnp.cumsum(a.astype(I32), axis=1) == 0).astype(I32), axis=1)
    k_first = none_yet(live)
    k_end = N_OT - none_yet(live[:, ::-1])
    return (te.astype(I32), flat(jnp.sum(plen, axis=1)), n_used.reshape(1),
            flat(psrc), flat(pdst), flat(plen), local_used.astype(I32), flat(k_first), flat(k_end))


def kernel(x_prompt, x_sample, state_ssm_re, state_ssm_im, cache_conv, norm_mix_g, w_in, lam_re, lam_im, log_dt, b_re, b_im, c_re, c_im, d_skip, w_ssm_glu, conv_w, conv_b, conv_ln_g, conv_ln_b, w_conv_out, w_out, norm_ffn_g, w_router_group, b_router_group, w_router_expert, b_router_expert, w_exp_gate, w_exp_up, w_exp_down, norm_final_g):
    assert w_in.shape[0] == 1, "single-layer trunk"
    xp = x_prompt.reshape(N_PROMPT_TOK, D_MODEL)
    xs = x_sample.reshape(N_SAMPLE_TOK, D_MODEL)
    w_in_bf = w_in[0].astype(BF16)
    row = lambda a: a.reshape(1, -1)

    ar_rep, ai_rep, bbr, bbi = _discretise(lam_re[0], lam_im[0], log_dt[0], b_re[0], b_im[0])
    gph = lambda a: a.reshape(N_SSM_GROUPS, SSM_STATE, SSM_GROUP)
    pick = lambda a: gph(a)[:, :, 0].reshape(GROUP_BLOCKS, 1, GB_STATE)
    bm = jnp.concatenate([_block_diag_in(gph(bbr)), _block_diag_in(gph(bbi))], axis=-1).astype(BF16)
    cm = jnp.concatenate([_block_diag_out(c_re[0]), -_block_diag_out(c_im[0])], axis=1).astype(BF16)
    dsk = d_skip[0].reshape(GROUP_BLOCKS, 1, GB_IN)
    ar, ai = pick(ar_rep), pick(ai_rep)
    s0r = state_ssm_re[0].reshape(N_SAMPLE_SEQ, -1)
    s0i = state_ssm_im[0].reshape(N_SAMPLE_SEQ, -1)

    u3, v3, xn3 = _inproj(xp, xs, row(norm_mix_g[0]), w_in_bf)
    actp, pfr, pfi = _scan_prompt(u3, bm, cm, ar, ai, dsk)
    acts, sfr, sfi = _scan_sample(u3, bm, cm, ar, ai, dsk, s0r, s0i)

    hist_s = cache_conv[0].transpose(1, 0, 2).reshape(HIST_S, D_CONV)
    cv, ncp, ncs = _conv(v3, hist_s, conv_w[0], row(conv_b[0]), conv_ln_g[0], conv_ln_b[0])

    m = _merge(xn3, actp, acts, cv, w_in_bf, w_ssm_glu[0].astype(BF16), w_conv_out[0].astype(BF16))

    pad_lanes = ROUTE_LANES - N_EXP_GROUPS - N_EXPERTS
    w_router = jnp.concatenate(
        [w_router_group[0], w_router_expert[0], jnp.zeros((D_MODEL, pad_lanes), F32)], axis=1)
    b_router = jnp.concatenate(
        [b_router_group[0], b_router_expert[0], jnp.zeros((pad_lanes,), F32)]).reshape(1, ROUTE_LANES)
    h, xl, route, gw, cnt = _outproj(m, xp, xs, w_out[0].astype(BF16), row(norm_ffn_g[0]),
                                     w_router, b_router)

    ol = _experts(_dispatch_plan(cnt), xl, w_exp_gate[0], w_exp_up[0], w_exp_down[0])
    yp, ys = _combine(h, route, gw, ol, row(norm_final_g))

    st = lambda a, n: a.reshape(1, n, N_SSM_GROUPS, SSM_STATE)
    pst = lambda a: st(a.transpose(1, 0, 2), N_PROMPT_SEQ)
    ncp = ncp.reshape(CONV_BUF, N_PROMPT_SEQ, D_CONV).transpose(1, 0, 2)[None]
    ncs = ncs.reshape(CONV_BUF, N_SAMPLE_SEQ, D_CONV).transpose(1, 0, 2)[None]
    return (yp, ys.reshape(N_SAMPLE_SEQ, SAMPLE_LEN, D_MODEL), pst(pfr), pst(pfi), ncp,
            st(sfr, N_SAMPLE_SEQ), st(sfi, N_SAMPLE_SEQ), ncs)
```

```python
import functools

import jax
import jax.numpy as jnp
from jax import lax
from jax.experimental import pallas as pl
from jax.experimental.pallas import tpu as pltpu

F32 = jnp.float32
BF16 = jnp.bfloat16
I32 = jnp.int32

SUBLANES = 8
LANES = 128

D_MODEL = 2048
D_SSM = 1024
D_CONV = 1024
SSM_GROUP = 16
N_SSM_GROUPS = 64
SSM_STATE = 64
CONV_WIDTH = 31
CONV_BUF = CONV_WIDTH - 1
N_EXP_GROUPS = 4
EXP_PER_GROUP = 8
N_EXPERTS = 32
D_EXPERT = 256
EPS = 1e-6

N_PROMPT_SEQ = 4
PROMPT_LEN = 2048
N_SAMPLE_SEQ = 16
SAMPLE_LEN = 32
N_PROMPT_TOK = N_PROMPT_SEQ * PROMPT_LEN
N_SAMPLE_TOK = N_SAMPLE_SEQ * SAMPLE_LEN
N_TOK = N_PROMPT_TOK + N_SAMPLE_TOK

TM = 512
N_PROMPT_TILES = N_PROMPT_TOK // TM
N_TILES = N_TOK // TM
PROMPT_STEPS = TM // N_PROMPT_SEQ
GROUP_BLOCKS = 4
GROUPS_PER_BLOCK = N_SSM_GROUPS // GROUP_BLOCKS
GB_IN = GROUPS_PER_BLOCK * SSM_GROUP
GB_STATE = GROUPS_PER_BLOCK * SSM_STATE
GB_LB = GB_STATE // LANES
HIST_P = CONV_BUF * N_PROMPT_SEQ
HIST_S = CONV_BUF * N_SAMPLE_SEQ
ROUTE_LANES = 128
N_PAIRS = 2 * N_TOK
TO = 256
N_OT = N_TOK // TO
N_OT_PROMPT = N_PROMPT_TOK // TO
HALVES = TM // TO
RUN = 16
LROWS = 1024
TME = 256
N_ETILES = -(-(N_PAIRS + N_OT * N_EXPERTS * (RUN - 1) + N_EXPERTS * (TME - 1)) // TME)
assert 2 * TO + (RUN - 1) * N_EXPERTS <= LROWS
MODEL_LB = D_MODEL // LANES
SSM_LB = D_SSM // LANES
CONV_LB = D_CONV // LANES


def _cparams(sem, vmem_mb):
    return pltpu.CompilerParams(dimension_semantics=sem, vmem_limit_bytes=vmem_mb * 1024 * 1024)


def _resident(shape):
    return pl.BlockSpec(shape, lambda *_: (0,) * len(shape), pipeline_mode=pl.Buffered(1))


def _x_tile_specs(steps=PROMPT_STEPS):
    blocks_per_seq = PROMPT_LEN // steps
    n_prompt = N_PROMPT_SEQ * blocks_per_seq // N_PROMPT_SEQ

    def prompt(b):
        return pl.BlockSpec(
            (steps, D_MODEL),
            lambda i, *_: (b * blocks_per_seq + jnp.minimum(i, n_prompt - 1), 0))

    return [prompt(b) for b in range(N_PROMPT_SEQ)] + [_resident((N_SAMPLE_TOK, D_MODEL))]


def _to_mixer_order(ref3, val, n_seq, steps):
    for c in range(val.shape[1] // LANES):
        for b in range(n_seq):
            ref3[c, pl.ds(b, steps, stride=n_seq), :] = val[b * steps:(b + 1) * steps,
                                                            c * LANES:(c + 1) * LANES]


def _seq_rows(ref3, b, n_seq, steps):
    return jnp.concatenate(
        [ref3[c, pl.ds(b, steps, stride=n_seq), :] for c in range(ref3.shape[0])], axis=1)


def _tile_to_mixer_order(i, ref3, xs, sample):
    @pl.when(i < N_PROMPT_TILES)
    def _():
        _to_mixer_order(ref3, xs(), N_PROMPT_SEQ, PROMPT_STEPS)

    @pl.when(i >= N_PROMPT_TILES)
    def _():
        _to_mixer_order(ref3, sample(), N_SAMPLE_SEQ, SAMPLE_LEN)


def _lane_blocks(ref3):
    return jnp.concatenate([ref3[c] for c in range(ref3.shape[0])], axis=1)


def _pair_specs(rows, cols, n_prompt_tiles):
    p = pl.BlockSpec((rows, cols), lambda i, *_: (jnp.minimum(i, n_prompt_tiles - 1), 0))
    s = pl.BlockSpec((rows, cols), lambda i, *_: (jnp.maximum(i - n_prompt_tiles, 0), 0))
    return p, s


def _disc_kernel(lr_ref, li_ref, ldt_ref, br_ref, bi_ref, ar_ref, ai_ref, bbr_ref, bbi_ref):
    lr = lr_ref[...]
    li = li_ref[...]
    dt = jnp.exp(ldt_ref[...])
    mag = jnp.exp(lr * dt)
    ar = mag * jnp.cos(li * dt)
    ai = mag * jnp.sin(li * dt)
    den = lr * lr + li * li
    nr = ar - 1.0
    cr = (nr * lr + ai * li) / den
    ci = (ai * lr - nr * li) / den
    br = br_ref[...]
    bi = bi_ref[...]
    ar_ref[...] = ar
    ai_ref[...] = ai
    bbr_ref[...] = cr * br - ci * bi
    bbi_ref[...] = cr * bi + ci * br


def _discretise(lam_re, lam_im, log_dt, b_re, b_im):
    shp = jax.ShapeDtypeStruct((N_SSM_GROUPS, SSM_STATE * SSM_GROUP), F32)
    rep = lambda a: jnp.repeat(a, SSM_GROUP, axis=-1)
    ldt = jnp.broadcast_to(log_dt[:, None], (N_SSM_GROUPS, SSM_STATE * SSM_GROUP))
    return pl.pallas_call(_disc_kernel, out_shape=(shp, shp, shp, shp), name="s5_discretise")(
        rep(lam_re), rep(lam_im), ldt,
        b_re.reshape(N_SSM_GROUPS, -1), b_im.reshape(N_SSM_GROUPS, -1))


def _block_diag_in(bb):
    b4 = bb.reshape(GROUP_BLOCKS, GROUPS_PER_BLOCK, SSM_STATE, SSM_GROUP).transpose(0, 1, 3, 2)
    eye = jnp.eye(GROUPS_PER_BLOCK, dtype=bool)[None, :, None, :, None]
    full = jnp.where(eye, b4[:, :, :, None, :], 0.0)
    return full.reshape(GROUP_BLOCKS, GB_IN, GB_STATE)


def _block_diag_out(c):
    c4 = c.reshape(GROUP_BLOCKS, GROUPS_PER_BLOCK, SSM_GROUP, SSM_STATE).transpose(0, 1, 3, 2)
    eye = jnp.eye(GROUPS_PER_BLOCK, dtype=bool)[None, :, None, :, None]
    full = jnp.where(eye, c4[:, :, :, None, :], 0.0)
    return full.reshape(GROUP_BLOCKS, GB_STATE, GB_IN)


def _rmsnorm_rows(x, g):
    r = lax.rsqrt(jnp.mean(x * x, axis=-1, keepdims=True) + EPS)
    return x * r * g


def _inproj_kernel(x0_ref, x1_ref, x2_ref, x3_ref, xs_ref, g_ref, w_ref,
                   u_ref, v_ref, xn_ref, xn_s, va_s):
    i = pl.program_id(0)
    j = pl.program_id(1)
    prompt_rows = lambda: jnp.concatenate(
        [x0_ref[...], x1_ref[...], x2_ref[...], x3_ref[...]], axis=0)

    @pl.when(j == 0)
    def _():
        def norm(x):
            xn = _rmsnorm_rows(x, g_ref[...])
            xn_s[...] = xn.astype(BF16)
            return xn
        _tile_to_mixer_order(i, xn_ref, lambda: norm(prompt_rows()), lambda: norm(xs_ref[...]))

    p = jnp.dot(xn_s[...], w_ref[...], preferred_element_type=F32)

    @pl.when(j == 0)
    def _():
        _tile_to_mixer_order(i, u_ref, lambda: p, lambda: p)

    @pl.when(j == 1)
    def _():
        va_s[...] = p

    @pl.when(j == 2)
    def _():
        v = va_s[...] * jax.nn.sigmoid(p)
        _tile_to_mixer_order(i, v_ref, lambda: v, lambda: v)


def _inproj(xp, xs, g, w_in_bf):
    blocked = lambda nb: pl.BlockSpec((nb, TM, LANES), lambda i, j: (0, i, 0))
    return pl.pallas_call(
        _inproj_kernel,
        grid=(N_TILES, 3),
        in_specs=_x_tile_specs() + [
            pl.BlockSpec((1, D_MODEL), lambda i, j: (0, 0)),
            pl.BlockSpec((D_MODEL, D_SSM), lambda i, j: (0, j)),
        ],
        out_specs=[blocked(SSM_LB), blocked(CONV_LB), blocked(MODEL_LB)],
        out_shape=(jax.ShapeDtypeStruct((SSM_LB, N_TOK, LANES), F32),
                   jax.ShapeDtypeStruct((CONV_LB, N_TOK, LANES), F32),
                   jax.ShapeDtypeStruct((MODEL_LB, N_TOK, LANES), F32)),
        scratch_shapes=[pltpu.VMEM((TM, D_MODEL), BF16), pltpu.VMEM((TM, D_CONV), F32)],
        compiler_params=_cparams(("arbitrary", "arbitrary"), 52),
        name="in_proj",
    )(xp, xp, xp, xp, xs, g, w_in_bf)


N_SCAN_TILES = GROUP_BLOCKS * N_PROMPT_TILES
SCAN_STAGES = 3


def _scan_prompt_kernel(ua_ref, uc_ref, bm_ref, cm_ref, ar_ref, ai_ref, dsk_ref,
                        act_ref, pfr_ref, pfi_ref, x0, x1, x2, cst, car, fin):
    n = pl.program_id(0)
    q_s = jnp.clip(n - 1, 0, N_SCAN_TILES - 1)
    t_s = q_s % N_PROMPT_TILES
    valid_s = jnp.logical_and(n >= 1, n <= N_SCAN_TILES)

    @pl.when(n == 0)
    def _():
        for buf in (x0, x1, x2):
            buf[...] = jnp.zeros_like(buf)
        car[...] = jnp.zeros_like(car)
        fin[...] = jnp.zeros_like(fin)

    shape = (SUBLANES, GB_STATE)
    top = lax.broadcasted_iota(I32, shape, 0) < N_PROMPT_SEQ
    arb = jnp.broadcast_to(ar_ref[0], shape)
    aib = jnp.broadcast_to(ai_ref[0], shape)
    cst[0] = jnp.where(top, 0.0, arb)
    cst[1] = jnp.where(top, 0.0, aib)
    cst[2] = jnp.where(top, arb, arb * arb - aib * aib)
    cst[3] = jnp.where(top, aib, 2.0 * arb * aib)

    def project_in(xa):
        u = jnp.concatenate([ua_ref[0], ua_ref[1]], axis=1).astype(BF16)
        x = jnp.dot(u, bm_ref[0], preferred_element_type=F32)
        for c in range(2 * GB_LB):
            xa[c] = x[:, c * LANES:(c + 1) * LANES]

    def project_out(xc):
        u = jnp.concatenate([uc_ref[0], uc_ref[1]], axis=1)
        h = _lane_blocks(xc).astype(BF16)
        y = jnp.dot(h, cm_ref[0], preferred_element_type=F32) + dsk_ref[0] * u
        act_ref[...] = jax.nn.gelu(y).astype(BF16)

    def recur(xs):
        top1 = lax.broadcasted_iota(I32, (SUBLANES, LANES), 0) < N_PROMPT_SEQ
        first = t_s == 0
        hr = [jnp.where(first, 0.0, car[:, c * LANES:(c + 1) * LANES]) for c in range(GB_LB)]
        hi = [jnp.where(first, 0.0, car[:, GB_STATE + c * LANES:GB_STATE + (c + 1) * LANES])
              for c in range(GB_LB)]
        for r in range(TM // SUBLANES):
            rows = slice(r * SUBLANES, (r + 1) * SUBLANES)
            for c in range(GB_LB):
                cols = slice(c * LANES, (c + 1) * LANES)
                xr = xs[c, rows, :]
                xi = xs[GB_LB + c, rows, :]
                a1r, a1i, a2r, a2i = cst[0, :, cols], cst[1, :, cols], cst[2, :, cols], cst[3, :, cols]
                sxr = pltpu.roll(xr, N_PROMPT_SEQ, axis=0)
                sxi = pltpu.roll(xi, N_PROMPT_SEQ, axis=0)
                tr = xr + a1r * sxr - a1i * sxi
                ti = xi + a1r * sxi + a1i * sxr
                nr = tr + a2r * hr[c] - a2i * hi[c]
                ni = ti + a2r * hi[c] + a2i * hr[c]
                xs[c, rows, :] = nr
                xs[GB_LB + c, rows, :] = ni
                hr[c] = jnp.where(top1, pltpu.roll(nr, N_PROMPT_SEQ, axis=0), nr)
                hi[c] = jnp.where(top1, pltpu.roll(ni, N_PROMPT_SEQ, axis=0), ni)
        state = jnp.concatenate(hr + hi, axis=1)
        car[...] = state
        last = jnp.logical_and(valid_s, t_s == N_PROMPT_TILES - 1)
        fin[...] = jnp.where(last, state, fin[...])
        pfr_ref[0] = fin[N_PROMPT_SEQ:SUBLANES, 0:GB_STATE]
        pfi_ref[0] = fin[N_PROMPT_SEQ:SUBLANES, GB_STATE:2 * GB_STATE]

    bufs = (x0, x1, x2)
    for r in range(SCAN_STAGES):
        @pl.when(n % SCAN_STAGES == r)
        def _():
            project_out(bufs[(r + 1) % SCAN_STAGES])
            recur(bufs[(r + 2) % SCAN_STAGES])
            project_in(bufs[r])


def _scan_prompt(u3, bm, cm, ar, ai, dsk):
    q_a = lambda n: jnp.minimum(n, N_SCAN_TILES - 1)
    q_s = lambda n: jnp.clip(n - 1, 0, N_SCAN_TILES - 1)
    q_c = lambda n: jnp.clip(n - 2, 0, N_SCAN_TILES - 1)
    gb = lambda q: q // N_PROMPT_TILES
    tt = lambda q: q % N_PROMPT_TILES
    u_spec = lambda q: pl.BlockSpec((GB_IN // LANES, TM, LANES), lambda n: (gb(q(n)), tt(q(n)), 0))
    gb3 = lambda last, q: pl.BlockSpec((1,) + last, lambda n: (gb(q(n)), 0, 0))
    state = jax.ShapeDtypeStruct((GROUP_BLOCKS, N_PROMPT_SEQ, GB_STATE), F32)
    xbuf = pltpu.VMEM((2 * GB_LB, TM, LANES), F32)
    return pl.pallas_call(
        _scan_prompt_kernel,
        grid=(N_SCAN_TILES + SCAN_STAGES - 1,),
        in_specs=[
            u_spec(q_a), u_spec(q_c),
            gb3((GB_IN, 2 * GB_STATE), q_a),
            gb3((2 * GB_STATE, GB_IN), q_c),
            gb3((1, GB_STATE), q_s),
            gb3((1, GB_STATE), q_s),
            gb3((1, GB_IN), q_c),
        ],
        out_specs=[
            pl.BlockSpec((TM, GB_IN), lambda n: (tt(q_c(n)), gb(q_c(n)))),
            gb3((N_PROMPT_SEQ, GB_STATE), q_s),
            gb3((N_PROMPT_SEQ, GB_STATE), q_s),
        ],
        out_shape=(jax.ShapeDtypeStruct((N_PROMPT_TOK, D_SSM), BF16), state, state),
        scratch_shapes=[xbuf, xbuf, xbuf,
                        pltpu.VMEM((4, SUBLANES, GB_STATE), F32),
                        pltpu.VMEM((SUBLANES, 2 * GB_STATE), F32),
                        pltpu.VMEM((SUBLANES, 2 * GB_STATE), F32)],
        compiler_params=_cparams(("arbitrary",), 48),
        name="s5_scan_prompt",
    )(u3, u3, bm, cm, ar, ai, dsk)


def _scan_sample_kernel(u_ref, bm_ref, cm_ref, ar_ref, ai_ref, dsk_ref, s0r_ref, s0i_ref,
                        act_ref, sfr_ref, sfi_ref, xs_ref):
    half_lb = GB_LB // 2
    half_cols = half_lb * LANES
    u = jnp.concatenate([u_ref[0], u_ref[1]], axis=1)
    x = jnp.dot(u.astype(BF16), bm_ref[0], preferred_element_type=F32)
    for c in range(2 * GB_LB):
        xs_ref[c] = x[:, c * LANES:(c + 1) * LANES]

    def load_half(rows, half, imag):
        c0 = imag * GB_LB + half * half_lb
        return jnp.concatenate([xs_ref[c0 + c, rows, :] for c in range(half_lb)], axis=1)

    def store_half(rows, half, imag, val):
        c0 = imag * GB_LB + half * half_lb
        for c in range(half_lb):
            xs_ref[c0 + c, rows, :] = val[:, c * LANES:(c + 1) * LANES]

    shape = (N_SAMPLE_SEQ, half_cols)
    for half in range(2):
        cols = slice(half * half_cols, (half + 1) * half_cols)
        arb = jnp.broadcast_to(ar_ref[0][:, cols], shape)
        aib = jnp.broadcast_to(ai_ref[0][:, cols], shape)

        def body(k, carry):
            hr, hi = carry
            rows = pl.ds(pl.multiple_of(k * N_SAMPLE_SEQ, N_SAMPLE_SEQ), N_SAMPLE_SEQ)
            xr = load_half(rows, half, 0)
            xi = load_half(rows, half, 1)
            nhr = xr + arb * hr - aib * hi
            nhi = xi + arb * hi + aib * hr
            store_half(rows, half, 0, nhr)
            store_half(rows, half, 1, nhi)
            return nhr, nhi

        hr, hi = lax.fori_loop(0, SAMPLE_LEN, body, (s0r_ref[:, cols], s0i_ref[:, cols]))
        sfr_ref[:, cols] = hr
        sfi_ref[:, cols] = hi

    h = _lane_blocks(xs_ref).astype(BF16)
    y = jnp.dot(h, cm_ref[0], preferred_element_type=F32) + dsk_ref[0] * u
    act_ref[...] = jax.nn.gelu(y).astype(BF16)


def _scan_sample(u3, bm, cm, ar, ai, dsk, s0r, s0i):
    gb3 = lambda last: pl.BlockSpec((1,) + last, lambda g: (g, 0, 0))
    state = lambda: pl.BlockSpec((N_SAMPLE_SEQ, GB_STATE), lambda g: (0, g))
    sst = jax.ShapeDtypeStruct((N_SAMPLE_SEQ, N_SSM_GROUPS * SSM_STATE), F32)
    return pl.pallas_call(
        _scan_sample_kernel,
        grid=(GROUP_BLOCKS,),
        in_specs=[
            pl.BlockSpec((GB_IN // LANES, TM, LANES), lambda g: (g, N_PROMPT_TILES, 0)),
            gb3((GB_IN, 2 * GB_STATE)), gb3((2 * GB_STATE, GB_IN)),
            gb3((1, GB_STATE)), gb3((1, GB_STATE)), gb3((1, GB_IN)),
            state(), state(),
        ],
        out_specs=[pl.BlockSpec((TM, GB_IN), lambda g: (0, g)), state(), state()],
        out_shape=(jax.ShapeDtypeStruct((N_SAMPLE_TOK, D_SSM), BF16), sst, sst),
        scratch_shapes=[pltpu.VMEM((2 * GB_LB, TM, LANES), F32)],
        compiler_params=_cparams(("arbitrary",), 32),
        name="s5_scan_sample",
    )(u3, bm, cm, ar, ai, dsk, s0r, s0i)


CONV_ROWS = 64
EXT_ROWS = HIST_S + TM


def _conv_kernel(v_ref, hist_ref, w_ref, b_ref, lg_ref, lb_ref,
                 cv_ref, ncp_ref, ncs_ref, ext_ref, sh_ref, acc_ref):
    t = pl.program_id(0)

    def taps(n_seq):
        half = SUBLANES // 2
        if n_seq % SUBLANES:
            lo = HIST_S - HIST_P - SUBLANES
            sh_ref[:, lo:EXT_ROWS - SUBLANES, :] = ext_ref[:, lo + half:EXT_ROWS - half, :]

        def body(rb, _):
            r0 = pl.multiple_of(rb * CONV_ROWS, CONV_ROWS)
            for cb in range(CONV_LB):
                cols = slice(cb * LANES, (cb + 1) * LANES)
                acc = None
                for j in range(CONV_WIDTH):
                    start = HIST_S - (CONV_BUF - j) * n_seq
                    src = ext_ref
                    if start % SUBLANES:
                        src, start = sh_ref, start - half
                    rows = pl.ds(pl.multiple_of(r0 + start, SUBLANES), CONV_ROWS)
                    term = src[cb, rows, :] * w_ref[j:j + 1, cols]
                    acc = term if acc is None else acc + term
                acc_ref[cb, pl.ds(r0, CONV_ROWS), :] = acc + b_ref[:, cols]
            return 0

        lax.fori_loop(0, TM // CONV_ROWS, body, 0)

    @pl.when(t == 0)
    def _():
        ext_ref[:, 0:HIST_S, :] = jnp.zeros((CONV_LB, HIST_S, LANES), F32)

    @pl.when(t == N_PROMPT_TILES)
    def _():
        for c in range(CONV_LB):
            ext_ref[c, 0:HIST_S, :] = hist_ref[:, c * LANES:(c + 1) * LANES]

    ext_ref[:, HIST_S:EXT_ROWS, :] = v_ref[...]

    @pl.when(t < N_PROMPT_TILES)
    def _():
        taps(N_PROMPT_SEQ)

    @pl.when(t == N_PROMPT_TILES)
    def _():
        taps(N_SAMPLE_SEQ)

    y = acc_ref[...]
    mu = jnp.sum(jnp.sum(y, axis=0), axis=-1, keepdims=True) / D_CONV
    yc = y - mu[None]
    var = jnp.sum(jnp.sum(yc * yc, axis=0), axis=-1, keepdims=True) / D_CONV
    z = jax.nn.silu(yc * lax.rsqrt(var + EPS)[None] * lg_ref[...] + lb_ref[...])
    for c in range(CONV_LB):
        cv_ref[:, c * LANES:(c + 1) * LANES] = z[c].astype(BF16)

    @pl.when(t == N_PROMPT_TILES - 1)
    def _():
        for c in range(CONV_LB):
            ncp_ref[:, c * LANES:(c + 1) * LANES] = ext_ref[c, EXT_ROWS - HIST_P:EXT_ROWS, :]

    @pl.when(t < N_PROMPT_TILES - 1)
    def _():
        ext_ref[:, HIST_S - HIST_P:HIST_S, :] = ext_ref[:, EXT_ROWS - HIST_P:EXT_ROWS, :]

    @pl.when(t == N_PROMPT_TILES)
    def _():
        for c in range(CONV_LB):
            ncs_ref[:, c * LANES:(c + 1) * LANES] = ext_ref[c, EXT_ROWS - HIST_S:EXT_ROWS, :]


def _conv(v3, hist_s, conv_w, conv_b, ln_g, ln_b):
    row = lambda n: pl.BlockSpec((n, D_CONV), lambda t: (0, 0))
    lane3 = pl.BlockSpec((CONV_LB, 1, LANES), lambda t: (0, 0, 0))
    ext = pltpu.VMEM((CONV_LB, EXT_ROWS, LANES), F32)
    return pl.pallas_call(
        _conv_kernel,
        grid=(N_TILES,),
        in_specs=[pl.BlockSpec((CONV_LB, TM, LANES), lambda t: (0, t, 0)),
                  row(HIST_S), row(CONV_WIDTH), row(1), lane3, lane3],
        out_specs=[pl.BlockSpec((TM, D_CONV), lambda t: (t, 0)), row(HIST_P), row(HIST_S)],
        out_shape=(jax.ShapeDtypeStruct((N_TOK, D_CONV), BF16),
                   jax.ShapeDtypeStruct((HIST_P, D_CONV), F32),
                   jax.ShapeDtypeStruct((HIST_S, D_CONV), F32)),
        scratch_shapes=[ext, ext, pltpu.VMEM((CONV_LB, TM, LANES), F32)],
        compiler_params=_cparams(("arbitrary",), 48),
        name="conv_branch",
    )(v3, hist_s, conv_w, conv_b,
      ln_g.reshape(CONV_LB, 1, LANES), ln_b.reshape(CONV_LB, 1, LANES))


MERGE_COLS = 512
N_MERGE = D_MODEL // MERGE_COLS


def _merge_kernel(xn_ref, actp_ref, acts_ref, cv_ref, wga_ref, wgb_ref, wg1_ref, wg2_ref, wco_ref,
                  m_ref, xn_s, act_s):
    i = pl.program_id(0)
    j = pl.program_id(1)

    @pl.when(j == 0)
    def _():
        xn_s[...] = _lane_blocks(xn_ref).astype(BF16)

        @pl.when(i < N_PROMPT_TILES)
        def _():
            act_s[...] = actp_ref[...]

        @pl.when(i >= N_PROMPT_TILES)
        def _():
            act_s[...] = acts_ref[...]

    xn = xn_s[...]
    act = act_s[...]
    dot = functools.partial(jnp.dot, preferred_element_type=F32)
    ya = dot(act, wg1_ref[...]) * jax.nn.sigmoid(dot(act, wg2_ref[...]))
    yb = dot(cv_ref[...], wco_ref[...])
    m = jax.nn.sigmoid(dot(xn, wga_ref[...])) * ya + jax.nn.sigmoid(dot(xn, wgb_ref[...])) * yb
    m_ref[...] = m.astype(BF16)


def _merge(xn3, actp, acts, cv, w_in_bf, w_glu_bf, w_co_bf):
    ga0 = (D_SSM + 2 * D_CONV) // MERGE_COLS
    gb0 = ga0 + N_MERGE
    return pl.pallas_call(
        _merge_kernel,
        grid=(N_TILES, N_MERGE),
        in_specs=[pl.BlockSpec((MODEL_LB, TM, LANES), lambda i, j: (0, i, 0))]
        + list(_pair_specs(TM, D_SSM, N_PROMPT_TILES)) + [
            pl.BlockSpec((TM, D_CONV), lambda i, j: (i, 0)),
            pl.BlockSpec((D_MODEL, MERGE_COLS), lambda i, j: (0, ga0 + j)),
            pl.BlockSpec((D_MODEL, MERGE_COLS), lambda i, j: (0, gb0 + j)),
            pl.BlockSpec((D_SSM, MERGE_COLS), lambda i, j: (0, j)),
            pl.BlockSpec((D_SSM, MERGE_COLS), lambda i, j: (0, N_MERGE + j)),
            pl.BlockSpec((D_CONV, MERGE_COLS), lambda i, j: (0, j)),
        ],
        out_specs=pl.BlockSpec((TM, MERGE_COLS), lambda i, j: (i, j)),
        out_shape=jax.ShapeDtypeStruct((N_TOK, D_MODEL), BF16),
        scratch_shapes=[pltpu.VMEM((TM, D_MODEL), BF16), pltpu.VMEM((TM, D_SSM), BF16)],
        compiler_params=_cparams(("arbitrary", "arbitrary"), 48),
        name="gated_merge",
    )(xn3, actp, acts, cv, w_in_bf, w_in_bf, w_glu_bf, w_glu_bf, w_co_bf)


def _outproj_kernel(m_ref, x0_ref, x1_ref, x2_ref, x3_ref, xs_ref, wo_ref, g_ref, wr_ref, br_ref,
                    h_ref, xl_ref, route_ref, gw_ref, cnt_ref, xr_s, lg0, lg1, xb0, xb1):
    i = pl.program_id(0)
    tile = jnp.minimum(i, N_OT - 1)
    steps_p = TO // N_PROMPT_SEQ
    steps_s = TO // N_SAMPLE_SEQ

    @pl.when(i == 0)
    def _():
        lg1[...] = jnp.zeros_like(lg1)
        xb1[...] = jnp.zeros_like(xb1)

    @pl.when(tile < N_OT_PROMPT)
    def _():
        x = jnp.concatenate([x0_ref[...], x1_ref[...], x2_ref[...], x3_ref[...]], axis=0)
        _to_mixer_order(xr_s, x, N_PROMPT_SEQ, steps_p)

    @pl.when(tile >= N_OT_PROMPT)
    def _():
        k0 = (tile - N_OT_PROMPT) * steps_s
        x = jnp.concatenate(
            [xs_ref[pl.ds(pl.multiple_of(b * SAMPLE_LEN + k0, steps_s), steps_s), :]
             for b in range(N_SAMPLE_SEQ)], axis=0)
        _to_mixer_order(xr_s, x, N_SAMPLE_SEQ, steps_s)

    sets = ((lg0, xb0), (lg1, xb1))
    for r in range(2):
        @pl.when(i % 2 == r)
        def _():
            lg_prev, xb_prev = sets[1 - r]
            _route_and_sort(lg_prev[...], xb_prev[...], xl_ref, route_ref, gw_ref, cnt_ref)
            _project(m_ref, xr_s, wo_ref, g_ref, wr_ref, br_ref, h_ref, *sets[r])


def _project(m_ref, xr_s, wo_ref, g_ref, wr_ref, br_ref, h_ref, lg_s, xb_s):
    dot = functools.partial(jnp.dot, preferred_element_type=F32)
    h = _lane_blocks(xr_s) + dot(m_ref[...], wo_ref[...])
    h_ref[...] = h
    xn = _rmsnorm_rows(h, g_ref[...])
    x_hi = xn.astype(BF16)
    x_lo = (xn - x_hi.astype(F32)).astype(BF16)
    w = wr_ref[...]
    w_hi = w.astype(BF16)
    w_lo = (w - w_hi.astype(F32)).astype(BF16)
    both = dot(x_hi, jnp.concatenate([w_hi, w_lo], axis=1))
    lg_s[...] = (both[:, :ROUTE_LANES] + (dot(x_lo, w_hi) + both[:, ROUTE_LANES:])) + br_ref[...]
    xb_s[...] = x_hi


def _route_and_sort(logits, x_hi, xl_ref, route_ref, gw_ref, cnt_ref):
    dot = functools.partial(jnp.dot, preferred_element_type=F32)
    lane = lax.broadcasted_iota(I32, logits.shape, 1)
    neg = -jnp.inf
    first = lambda hit: jnp.min(jnp.where(hit, lane, ROUTE_LANES), axis=-1, keepdims=True)
    gmask = lane < N_EXP_GROUPS
    lg = jnp.where(gmask, logits, neg)
    gmax = jnp.max(lg, axis=-1, keepdims=True)
    gsel = first(lg == gmax)
    psum = jnp.sum(jnp.where(gmask, jnp.exp(logits - gmax), 0.0), axis=-1, keepdims=True)
    pg_sel = 1.0 / psum
    e_lane = lane - N_EXP_GROUPS
    emask = (e_lane >= 0) & (e_lane < N_EXPERTS) & ((e_lane // EXP_PER_GROUP) == gsel)
    le = jnp.where(emask, logits, neg)
    v1 = jnp.max(le, axis=-1, keepdims=True)
    i1 = first(le == v1)
    le2 = jnp.where(lane == i1, neg, le)
    v2 = jnp.max(le2, axis=-1, keepdims=True)
    i2 = first(le2 == v2)
    z = jnp.exp(v2 - v1)
    w1 = pg_sel / (1.0 + z)
    w2 = pg_sel * z / (1.0 + z)
    e1 = i1 - N_EXP_GROUPS
    e2 = i2 - N_EXP_GROUPS
    gw_ref[...] = jnp.where(lane == 0, w1, jnp.where(lane == 1, w2, 0.0))
    oh1 = lane == e1
    oh2 = lane == e2
    hits = jnp.where(oh1 | oh2, 1.0, 0.0)
    rr = lax.broadcasted_iota(I32, (TO, TO), 0)
    cc = lax.broadcasted_iota(I32, (TO, TO), 1)
    before = dot(jnp.where(cc < rr, 1.0, 0.0).astype(BF16), hits.astype(BF16))
    counts = jnp.sum(hits, axis=0, keepdims=True)
    groups = jnp.floor((counts + (RUN - 1.0)) * (1.0 / RUN))
    er = lax.broadcasted_iota(I32, (ROUTE_LANES, ROUTE_LANES), 0)
    ec = lax.broadcasted_iota(I32, (ROUTE_LANES, ROUTE_LANES), 1)
    groups_before = dot(jnp.broadcast_to(groups, (SUBLANES, ROUTE_LANES)).astype(BF16),
                        jnp.where(er < ec, 1.0, 0.0).astype(BF16))[0:1]
    local = groups_before * RUN + before
    lpos1 = jnp.sum(jnp.where(oh1, local, 0.0), axis=-1, keepdims=True)
    lpos2 = jnp.sum(jnp.where(oh2, local, 0.0), axis=-1, keepdims=True)
    route_ref[...] = jnp.where(lane == 0, e1, jnp.where(lane == 1, e2, jnp.where(
        lane == 2, lpos1.astype(I32), jnp.where(lane == 3, lpos2.astype(I32), 0))))
    cnt_ref[0] = counts.astype(I32)
    lpos_t = jnp.where(lane == 0, lpos1, jnp.where(lane == 1, lpos2, 0.0)).T.astype(I32)
    jrow = lax.broadcasted_iota(I32, (LROWS, TO), 0)
    sel = (jrow == lpos_t[0:1, :]) | (jrow == lpos_t[1:2, :])
    xl_ref[...] = dot(jnp.where(sel, 1.0, 0.0).astype(BF16), x_hi).astype(BF16)


def _outproj(m, xp, xs, w_out_bf, g, w_router, b_router):
    cur = lambda i: jnp.minimum(i, N_OT - 1)
    prev = lambda i: jnp.maximum(i - 1, 0)
    tile = lambda n: pl.BlockSpec((TO, n), lambda i: (cur(i), 0))
    routed = lambda n: pl.BlockSpec((TO, n), lambda i: (prev(i), 0))
    return pl.pallas_call(
        _outproj_kernel,
        grid=(N_OT + 1,),
        in_specs=[tile(D_MODEL)] + _x_tile_specs(TO // N_PROMPT_SEQ) + [
            _resident((D_MODEL, D_MODEL)), _resident((1, D_MODEL)),
            _resident((D_MODEL, ROUTE_LANES)), _resident((1, ROUTE_LANES))],
        out_specs=[tile(D_MODEL),
                   pl.BlockSpec((LROWS, D_MODEL), lambda i: (prev(i), 0)),
                   routed(ROUTE_LANES), routed(ROUTE_LANES),
                   pl.BlockSpec((1, 1, ROUTE_LANES), lambda i: (prev(i), 0, 0))],
        out_shape=(jax.ShapeDtypeStruct((N_TOK, D_MODEL), F32),
                   jax.ShapeDtypeStruct((N_OT * LROWS, D_MODEL), BF16),
                   jax.ShapeDtypeStruct((N_TOK, ROUTE_LANES), I32),
                   jax.ShapeDtypeStruct((N_TOK, ROUTE_LANES), F32),
                   jax.ShapeDtypeStruct((N_OT, 1, ROUTE_LANES), I32)),
        scratch_shapes=[pltpu.VMEM((MODEL_LB, TO, LANES), F32),
                        pltpu.VMEM((TO, ROUTE_LANES), F32), pltpu.VMEM((TO, ROUTE_LANES), F32),
                        pltpu.VMEM((TO, D_MODEL), BF16), pltpu.VMEM((TO, D_MODEL), BF16)],
        compiler_params=_cparams(("arbitrary",), 56),
        name="out_proj_router",
    )(m, xp, xp, xp, xp, xs, w_out_bf, g, w_router, b_router)


def _expert_kernel(te_ref, nrows_ref, nused_ref, psrc_ref, pdst_ref, plen_ref, tot_ref, kfirst_ref, kend_ref,
                   xl_hbm, wg_ref, wu_ref, wd_ref, ol_hbm,
                   xbuf, obuf, zbuf, wg_s, wu_s, wd_s, gsem, wsem, zsem):
    i = pl.program_id(0)
    nused = nused_ref[0]
    slot = i % 2
    rows8 = lambda v: pl.multiple_of(v, RUN)

    def for_pieces(tile, fn):
        def body(k, _):
            idx = tile * N_OT + k
            n = plen_ref[idx]

            @pl.when(n > 0)
            def _():
                fn(rows8(psrc_ref[idx]), rows8(pdst_ref[idx]), rows8(n))
            return 0
        lax.fori_loop(kfirst_ref[tile], kend_ref[tile], body, 0)

    def gather(tile, s):
        for_pieces(tile, lambda src, dst, n: pltpu.make_async_copy(
            xl_hbm.at[pl.ds(src, n)], xbuf.at[s, pl.ds(dst, n)], gsem.at[s]).start())

    def scatter(tile, s):
        for_pieces(tile, lambda src, dst, n: pltpu.make_async_copy(
            obuf.at[s, pl.ds(dst, n)], ol_hbm.at[pl.ds(src, n)], wsem.at[s]).start())

    def wait_gather(tile, s):
        n = rows8(nrows_ref[tile])
        pltpu.make_async_copy(xl_hbm.at[pl.ds(0, n)], xbuf.at[s, pl.ds(0, n)], gsem.at[s]).wait()

    def wait_scatter(tile, s):
        n = rows8(nrows_ref[tile])
        pltpu.make_async_copy(obuf.at[s, pl.ds(0, n)], ol_hbm.at[pl.ds(0, n)], wsem.at[s]).wait()

    def tail_copy(t):
        used = rows8(tot_ref[t])
        n = rows8(LROWS - tot_ref[t])
        dst = rows8(t * LROWS + used)
        return pltpu.make_async_copy(zbuf.at[pl.ds(0, n)], ol_hbm.at[pl.ds(dst, n)], zsem.at[0])

    @pl.when(i == 0)
    def _():
        zbuf[...] = jnp.zeros_like(zbuf)
        xbuf[...] = jnp.zeros_like(xbuf)

        def start(t, _):
            tail_copy(t).start()
            return 0

        def wait(t, _):
            tail_copy(t).wait()
            return 0
        lax.fori_loop(0, N_OT, start, 0)
        lax.fori_loop(0, N_OT, wait, 0)
        gather(0, 0)

    @pl.when(i + 1 < nused)
    def _():
        gather(i + 1, 1 - slot)

    @pl.when(i < nused)
    def _():
        changed = jnp.logical_or(i == 0, te_ref[i] != te_ref[jnp.maximum(i - 1, 0)])

        @pl.when(changed)
        def _():
            wg_s[...] = wg_ref[0].astype(BF16)
            wu_s[...] = wu_ref[0].astype(BF16)
            wd_s[...] = wd_ref[0].astype(BF16)

        wait_gather(i, slot)

        @pl.when(i >= 2)
        def _():
            wait_scatter(i - 2, slot)

        xb = xbuf[slot]
        hg = jnp.dot(xb, wg_s[...], preferred_element_type=F32)
        hu = jnp.dot(xb, wu_s[...], preferred_element_type=F32)
        hid = jax.nn.silu(hg) * hu
        obuf[slot] = jnp.dot(hid.astype(BF16), wd_s[...], preferred_element_type=F32).astype(BF16)
        scatter(i, slot)

        @pl.when(i == nused - 1)
        def _():
            @pl.when(i >= 1)
            def _():
                wait_scatter(i - 1, 1 - slot)
            wait_scatter(i, slot)


def _experts(plan, xl, wg, wu, wd):
    wspec = lambda r, c: pl.BlockSpec((1, r, c), lambda i, te, *_: (te[i], 0, 0))
    grid_spec = pltpu.PrefetchScalarGridSpec(
        num_scalar_prefetch=len(plan),
        grid=(N_ETILES,),
        in_specs=[pl.BlockSpec(memory_space=pl.ANY), wspec(D_MODEL, D_EXPERT),
                  wspec(D_MODEL, D_EXPERT), wspec(D_EXPERT, D_MODEL)],
        out_specs=pl.BlockSpec(memory_space=pl.ANY),
        scratch_shapes=[
            pltpu.VMEM((2, TME, D_MODEL), BF16),
            pltpu.VMEM((2, TME, D_MODEL), BF16),
            pltpu.VMEM((LROWS - 2 * TO, D_MODEL), BF16),
            pltpu.VMEM((D_MODEL, D_EXPERT), BF16),
            pltpu.VMEM((D_MODEL, D_EXPERT), BF16),
            pltpu.VMEM((D_EXPERT, D_MODEL), BF16),
            pltpu.SemaphoreType.DMA((2,)),
            pltpu.SemaphoreType.DMA((2,)),
            pltpu.SemaphoreType.DMA((1,)),
        ],
    )
    return pl.pallas_call(
        _expert_kernel,
        grid_spec=grid_spec,
        out_shape=jax.ShapeDtypeStruct((N_OT * LROWS, D_MODEL), BF16),
        compiler_params=_cparams(("arbitrary",), 40),
        name="routed_experts",
    )(*plan, xl, wg, wu, wd)


def _combine_kernel(h_ref, route_ref, gw_ref, ol_ref, g_ref, yp_ref, ys_ref, y_s):
    i = pl.program_id(0)
    steps_p = TO // N_PROMPT_SEQ
    steps_s = TO // N_SAMPLE_SEQ
    route = route_ref[...]
    gw = gw_ref[...]
    jcol = lax.broadcasted_iota(I32, (TO, LROWS), 1)
    mix = (jnp.where(jcol == route[:, 2:3], gw[:, 0:1], 0.0)
           + jnp.where(jcol == route[:, 3:4], gw[:, 1:2], 0.0))
    moe = jnp.dot(mix.astype(BF16), ol_ref[...], preferred_element_type=F32)
    y = _rmsnorm_rows(h_ref[...] + moe, g_ref[...])
    for c in range(MODEL_LB):
        y_s[c] = y[:, c * LANES:(c + 1) * LANES]

    @pl.when(i < N_OT_PROMPT)
    def _():
        for b in range(N_PROMPT_SEQ):
            yp_ref[b] = _seq_rows(y_s, b, N_PROMPT_SEQ, steps_p)

    @pl.when(i >= N_OT_PROMPT)
    def _():
        k0 = (i - N_OT_PROMPT) * steps_s
        for b in range(N_SAMPLE_SEQ):
            rows = pl.ds(pl.multiple_of(b * SAMPLE_LEN + k0, steps_s), steps_s)
            ys_ref[rows, :] = _seq_rows(y_s, b, N_SAMPLE_SEQ, steps_s)


def _combine(h, route, gw, ol, g):
    tile = lambda n: pl.BlockSpec((TO, n), lambda i: (i, 0))
    return pl.pallas_call(
        _combine_kernel,
        grid=(N_OT,),
        in_specs=[tile(D_MODEL), tile(ROUTE_LANES), tile(ROUTE_LANES),
                  pl.BlockSpec((LROWS, D_MODEL), lambda i: (i, 0)),
                  pl.BlockSpec((1, D_MODEL), lambda i: (0, 0))],
        out_specs=[
            pl.BlockSpec((N_PROMPT_SEQ, TO // N_PROMPT_SEQ, D_MODEL),
                         lambda i: (0, jnp.minimum(i, N_OT_PROMPT - 1), 0)),
            pl.BlockSpec((N_SAMPLE_TOK, D_MODEL), lambda i: (0, 0)),
        ],
        out_shape=(jax.ShapeDtypeStruct((N_PROMPT_SEQ, PROMPT_LEN, D_MODEL), F32),
                   jax.ShapeDtypeStruct((N_SAMPLE_TOK, D_MODEL), F32)),
        scratch_shapes=[pltpu.VMEM((MODEL_LB, TO, LANES), F32)],
        compiler_params=_cparams(("arbitrary",), 48),
        name="combine_norm",
    )(h, route, gw, ol, g)


def _dispatch_plan(cnt):
    experts = jnp.arange(N_EXPERTS, dtype=I32)
    cnt = cnt[:, 0, :N_EXPERTS]
    run = ((cnt + RUN - 1) // RUN) * RUN
    local_start = jnp.cumsum(run, axis=1) - run
    local_used = jnp.sum(run, axis=1)
    total = jnp.sum(run, axis=0)
    padded = ((total + TME - 1) // TME) * TME
    ends = jnp.cumsum(padded)
    starts = ends - padded
    run_start = starts[None, :] + jnp.cumsum(run, axis=0) - run
    n_used = (ends[-1] // TME).astype(I32)
    tile = jnp.arange(N_ETILES, dtype=I32)
    tile_start = tile * TME
    te = jnp.minimum(jnp.sum((ends[None, :] <= tile_start[:, None]).astype(I32), axis=1), N_EXPERTS - 1)
    mine = (te[:, None] == experts[None, :])[:, None, :]
    of_tile = lambda a: jnp.sum(jnp.where(mine, a[None], 0), axis=-1)
    s0 = of_tile(run_start)
    lo = jnp.maximum(s0, tile_start[:, None])
    hi = jnp.minimum(s0 + of_tile(run), tile_start[:, None] + TME)
    plen = jnp.where((tile < n_used)[:, None], jnp.maximum(hi - lo, 0), 0)
    live = plen > 0
    psrc = jnp.where(live, jnp.arange(N_OT, dtype=I32)[None, :] * LROWS + of_tile(local_start) + lo - s0, 0)
    pdst = jnp.where(live, lo - tile_start[:, None], 0)
    last = jnp.sum(jnp.where(tile == n_used - 1, te, 0))
    te = jnp.where(tile < n_used, te, last)
    flat = lambda a: a.reshape(-1).astype(I32)
    none_yet = lambda a: jnp.sum((jnp.cumsum(a.astype(I32), axis=1) == 0).astype(I32), axis=1)
    k_first = none_yet(live)
    k_end = N_OT - none_yet(live[:, ::-1])
    return (te.astype(I32), flat(jnp.sum(plen, axis=1)), n_used.reshape(1),
            flat(psrc), flat(pdst), flat(plen), local_used.astype(I32), flat(k_first), flat(k_end))


def kernel(x_prompt, x_sample, state_ssm_re, state_ssm_im, cache_conv, norm_mix_g, w_in, lam_re, lam_im, log_dt, b_re, b_im, c_re, c_im, d_skip, w_ssm_glu, conv_w, conv_b, conv_ln_g, conv_ln_b, w_conv_out, w_out, norm_ffn_g, w_router_group, b_router_group, w_router_expert, b_router_expert, w_exp_gate, w_exp_up, w_exp_down, norm_final_g):
    assert w_in.shape[0] == 1, "single-layer trunk"
    xp = x_prompt.reshape(N_PROMPT_TOK, D_MODEL)
    xs = x_sample.reshape(N_SAMPLE_TOK, D_MODEL)
    w_in_bf = w_in[0].astype(BF16)
    row = lambda a: a.reshape(1, -1)

    ar_rep, ai_rep, bbr, bbi = _discretise(lam_re[0], lam_im[0], log_dt[0], b_re[0], b_im[0])
    gph = lambda a: a.reshape(N_SSM_GROUPS, SSM_STATE, SSM_GROUP)
    pick = lambda a: gph(a)[:, :, 0].reshape(GROUP_BLOCKS, 1, GB_STATE)
    bm = jnp.concatenate([_block_diag_in(gph(bbr)), _block_diag_in(gph(bbi))], axis=-1).astype(BF16)
    cm = jnp.concatenate([_block_diag_out(c_re[0]), -_block_diag_out(c_im[0])], axis=1).astype(BF16)
    dsk = d_skip[0].reshape(GROUP_BLOCKS, 1, GB_IN)
    ar, ai = pick(ar_rep), pick(ai_rep)
    s0r = state_ssm_re[0].reshape(N_SAMPLE_SEQ, -1)
    s0i = state_ssm_im[0].reshape(N_SAMPLE_SEQ, -1)

    u3, v3, xn3 = _inproj(xp, xs, row(norm_mix_g[0]), w_in_bf)
    actp, pfr, pfi = _scan_prompt(u3, bm, cm, ar, ai, dsk)
    acts, sfr, sfi = _scan_sample(u3, bm, cm, ar, ai, dsk, s0r, s0i)

    hist_s = cache_conv[0].transpose(1, 0, 2).reshape(HIST_S, D_CONV)
    cv, ncp, ncs = _conv(v3, hist_s, conv_w[0], row(conv_b[0]), conv_ln_g[0], conv_ln_b[0])

    m = _merge(xn3, actp, acts, cv, w_in_bf, w_ssm_glu[0].astype(BF16), w_conv_out[0].astype(BF16))

    pad_lanes = ROUTE_LANES - N_EXP_GROUPS - N_EXPERTS
    w_router = jnp.concatenate(
        [w_router_group[0], w_router_expert[0], jnp.zeros((D_MODEL, pad_lanes), F32)], axis=1)
    b_router = jnp.concatenate(
        [b_router_group[0], b_router_expert[0], jnp.zeros((pad_lanes,), F32)]).reshape(1, ROUTE_LANES)
    h, xl, route, gw, cnt = _outproj(m, xp, xs, w_out[0].astype(BF16), row(norm_ffn_g[0]),
                                     w_router, b_router)

    ol = _experts(_dispatch_plan(cnt), xl, w_exp_gate[0], w_exp_up[0], w_exp_down[0])
    yp, ys = _combine(h, route, gw, ol, row(norm_final_g))

    st = lambda a, n: a.reshape(1, n, N_SSM_GROUPS, SSM_STATE)
    pst = lambda a: st(a.transpose(1, 0, 2), N_PROMPT_SEQ)
    ncp = ncp.reshape(CONV_BUF, N_PROMPT_SEQ, D_CONV).transpose(1, 0, 2)[None]
    ncs = ncs.reshape(CONV_BUF, N_SAMPLE_SEQ, D_CONV).transpose(1, 0, 2)[None]
    return (yp, ys.reshape(N_SAMPLE_SEQ, SAMPLE_LEN, D_MODEL), pst(pfr), pst(pfi), ncp,
            st(sfr, N_SAMPLE_SEQ), st(sfi, N_SAMPLE_SEQ), ncs)
```

```python
import functools

import jax
import jax.numpy as jnp
from jax import lax
from jax.experimental import pallas as pl
from jax.experimental.pallas import tpu as pltpu

F32 = jnp.float32
BF16 = jnp.bfloat16
I32 = jnp.int32

SUBLANES = 8
LANES = 128

D_MODEL = 2048
D_SSM = 1024
D_CONV = 1024
SSM_GROUP = 16
N_SSM_GROUPS = 64
SSM_STATE = 64
CONV_WIDTH = 31
CONV_BUF = CONV_WIDTH - 1
N_EXP_GROUPS = 4
EXP_PER_GROUP = 8
N_EXPERTS = 32
D_EXPERT = 256
EPS = 1e-6

N_PROMPT_SEQ = 4
PROMPT_LEN = 2048
N_SAMPLE_SEQ = 16
SAMPLE_LEN = 32
N_PROMPT_TOK = N_PROMPT_SEQ * PROMPT_LEN
N_SAMPLE_TOK = N_SAMPLE_SEQ * SAMPLE_LEN
N_TOK = N_PROMPT_TOK + N_SAMPLE_TOK

TM = 512
N_PROMPT_TILES = N_PROMPT_TOK // TM
N_TILES = N_TOK // TM
PROMPT_STEPS = TM // N_PROMPT_SEQ
GROUP_BLOCKS = 4
GROUPS_PER_BLOCK = N_SSM_GROUPS // GROUP_BLOCKS
GB_IN = GROUPS_PER_BLOCK * SSM_GROUP
GB_STATE = GROUPS_PER_BLOCK * SSM_STATE
GB_LB = GB_STATE // LANES
HIST_P = CONV_BUF * N_PROMPT_SEQ
HIST_S = CONV_BUF * N_SAMPLE_SEQ
ROUTE_LANES = 128
N_PAIRS = 2 * N_TOK
TO = 256
N_OT = N_TOK // TO
N_OT_PROMPT = N_PROMPT_TOK // TO
HALVES = TM // TO
RUN = 16
LROWS = 1024
TME = 512
N_ETILES = -(-(N_PAIRS + N_OT * N_EXPERTS * (RUN - 1) + N_EXPERTS * (TME - 1)) // TME)
assert 2 * TO + (RUN - 1) * N_EXPERTS <= LROWS
MODEL_LB = D_MODEL // LANES
SSM_LB = D_SSM // LANES
CONV_LB = D_CONV // LANES


def _cparams(sem, vmem_mb):
    return pltpu.CompilerParams(dimension_semantics=sem, vmem_limit_bytes=vmem_mb * 1024 * 1024)


def _resident(shape):
    return pl.BlockSpec(shape, lambda *_: (0,) * len(shape), pipeline_mode=pl.Buffered(1))


def _x_tile_specs(steps=PROMPT_STEPS):
    blocks_per_seq = PROMPT_LEN // steps
    n_prompt = N_PROMPT_SEQ * blocks_per_seq // N_PROMPT_SEQ

    def prompt(b):
        return pl.BlockSpec(
            (steps, D_MODEL),
            lambda i, *_: (b * blocks_per_seq + jnp.minimum(i, n_prompt - 1), 0))

    return [prompt(b) for b in range(N_PROMPT_SEQ)] + [_resident((N_SAMPLE_TOK, D_MODEL))]


def _to_mixer_order(ref3, val, n_seq, steps):
    for c in range(val.shape[1] // LANES):
        for b in range(n_seq):
            ref3[c, pl.ds(b, steps, stride=n_seq), :] = val[b * steps:(b + 1) * steps,
                                                            c * LANES:(c + 1) * LANES]


def _seq_rows(ref3, b, n_seq, steps):
    return jnp.concatenate(
        [ref3[c, pl.ds(b, steps, stride=n_seq), :] for c in range(ref3.shape[0])], axis=1)


def _tile_to_mixer_order(i, ref3, xs, sample):
    @pl.when(i < N_PROMPT_TILES)
    def _():
        _to_mixer_order(ref3, xs(), N_PROMPT_SEQ, PROMPT_STEPS)

    @pl.when(i >= N_PROMPT_TILES)
    def _():
        _to_mixer_order(ref3, sample(), N_SAMPLE_SEQ, SAMPLE_LEN)


def _lane_blocks(ref3):
    return jnp.concatenate([ref3[c] for c in range(ref3.shape[0])], axis=1)


def _pair_specs(rows, cols, n_prompt_tiles):
    p = pl.BlockSpec((rows, cols), lambda i, *_: (jnp.minimum(i, n_prompt_tiles - 1), 0))
    s = pl.BlockSpec((rows, cols), lambda i, *_: (jnp.maximum(i - n_prompt_tiles, 0), 0))
    return p, s


def _disc_kernel(lr_ref, li_ref, ldt_ref, br_ref, bi_ref, ar_ref, ai_ref, bbr_ref, bbi_ref):
    lr = lr_ref[...]
    li = li_ref[...]
    dt = jnp.exp(ldt_ref[...])
    mag = jnp.exp(lr * dt)
    ar = mag * jnp.cos(li * dt)
    ai = mag * jnp.sin(li * dt)
    den = lr * lr + li * li
    nr = ar - 1.0
    cr = (nr * lr + ai * li) / den
    ci = (ai * lr - nr * li) / den
    br = br_ref[...]
    bi = bi_ref[...]
    ar_ref[...] = ar
    ai_ref[...] = ai
    bbr_ref[...] = cr * br - ci * bi
    bbi_ref[...] = cr * bi + ci * br


def _discretise(lam_re, lam_im, log_dt, b_re, b_im):
    shp = jax.ShapeDtypeStruct((N_SSM_GROUPS, SSM_STATE * SSM_GROUP), F32)
    rep = lambda a: jnp.repeat(a, SSM_GROUP, axis=-1)
    ldt = jnp.broadcast_to(log_dt[:, None], (N_SSM_GROUPS, SSM_STATE * SSM_GROUP))
    return pl.pallas_call(_disc_kernel, out_shape=(shp, shp, shp, shp), name="s5_discretise")(
        rep(lam_re), rep(lam_im), ldt,
        b_re.reshape(N_SSM_GROUPS, -1), b_im.reshape(N_SSM_GROUPS, -1))


def _block_diag_in(bb):
    b4 = bb.reshape(GROUP_BLOCKS, GROUPS_PER_BLOCK, SSM_STATE, SSM_GROUP).transpose(0, 1, 3, 2)
    eye = jnp.eye(GROUPS_PER_BLOCK, dtype=bool)[None, :, None, :, None]
    full = jnp.where(eye, b4[:, :, :, None, :], 0.0)
    return full.reshape(GROUP_BLOCKS, GB_IN, GB_STATE)


def _block_diag_out(c):
    c4 = c.reshape(GROUP_BLOCKS, GROUPS_PER_BLOCK, SSM_GROUP, SSM_STATE).transpose(0, 1, 3, 2)
    eye = jnp.eye(GROUPS_PER_BLOCK, dtype=bool)[None, :, None, :, None]
    full = jnp.where(eye, c4[:, :, :, None, :], 0.0)
    return full.reshape(GROUP_BLOCKS, GB_STATE, GB_IN)


def _rmsnorm_rows(x, g):
    r = lax.rsqrt(jnp.mean(x * x, axis=-1, keepdims=True) + EPS)
    return x * r * g


def _inproj_kernel(x0_ref, x1_ref, x2_ref, x3_ref, xs_ref, g_ref, w_ref,
                   u_ref, v_ref, xn_ref, xn_s, va_s):
    i = pl.program_id(0)
    j = pl.program_id(1)
    prompt_rows = lambda: jnp.concatenate(
        [x0_ref[...], x1_ref[...], x2_ref[...], x3_ref[...]], axis=0)

    @pl.when(j == 0)
    def _():
        def norm(x):
            xn = _rmsnorm_rows(x, g_ref[...])
            xn_s[...] = xn.astype(BF16)
            return xn
        _tile_to_mixer_order(i, xn_ref, lambda: norm(prompt_rows()), lambda: norm(xs_ref[...]))

    p = jnp.dot(xn_s[...], w_ref[...], preferred_element_type=F32)

    @pl.when(j == 0)
    def _():
        _tile_to_mixer_order(i, u_ref, lambda: p, lambda: p)

    @pl.when(j == 1)
    def _():
        va_s[...] = p

    @pl.when(j == 2)
    def _():
        v = va_s[...] * jax.nn.sigmoid(p)
        _tile_to_mixer_order(i, v_ref, lambda: v, lambda: v)


def _inproj(xp, xs, g, w_in_bf):
    blocked = lambda nb: pl.BlockSpec((nb, TM, LANES), lambda i, j: (0, i, 0))
    return pl.pallas_call(
        _inproj_kernel,
        grid=(N_TILES, 3),
        in_specs=_x_tile_specs() + [
            pl.BlockSpec((1, D_MODEL), lambda i, j: (0, 0)),
            pl.BlockSpec((D_MODEL, D_SSM), lambda i, j: (0, j)),
        ],
        out_specs=[blocked(SSM_LB), blocked(CONV_LB), blocked(MODEL_LB)],
        out_shape=(jax.ShapeDtypeStruct((SSM_LB, N_TOK, LANES), F32),
                   jax.ShapeDtypeStruct((CONV_LB, N_TOK, LANES), F32),
                   jax.ShapeDtypeStruct((MODEL_LB, N_TOK, LANES), F32)),
        scratch_shapes=[pltpu.VMEM((TM, D_MODEL), BF16), pltpu.VMEM((TM, D_CONV), F32)],
        compiler_params=_cparams(("arbitrary", "arbitrary"), 52),
        name="in_proj",
    )(xp, xp, xp, xp, xs, g, w_in_bf)


N_SCAN_TILES = GROUP_BLOCKS * N_PROMPT_TILES
SCAN_STAGES = 3


def _scan_prompt_kernel(ua_ref, uc_ref, bm_ref, cm_ref, ar_ref, ai_ref, dsk_ref,
                        act_ref, pfr_ref, pfi_ref, x0, x1, x2, cst, car, fin):
    n = pl.program_id(0)
    q_s = jnp.clip(n - 1, 0, N_SCAN_TILES - 1)
    t_s = q_s % N_PROMPT_TILES
    valid_s = jnp.logical_and(n >= 1, n <= N_SCAN_TILES)

    @pl.when(n == 0)
    def _():
        for buf in (x0, x1, x2):
            buf[...] = jnp.zeros_like(buf)
        car[...] = jnp.zeros_like(car)
        fin[...] = jnp.zeros_like(fin)

    shape = (SUBLANES, GB_STATE)
    top = lax.broadcasted_iota(I32, shape, 0) < N_PROMPT_SEQ
    arb = jnp.broadcast_to(ar_ref[0], shape)
    aib = jnp.broadcast_to(ai_ref[0], shape)
    cst[0] = jnp.where(top, 0.0, arb)
    cst[1] = jnp.where(top, 0.0, aib)
    cst[2] = jnp.where(top, arb, arb * arb - aib * aib)
    cst[3] = jnp.where(top, aib, 2.0 * arb * aib)

    def project_in(xa):
        u = jnp.concatenate([ua_ref[0], ua_ref[1]], axis=1).astype(BF16)
        x = jnp.dot(u, bm_ref[0], preferred_element_type=F32)
        for c in range(2 * GB_LB):
            xa[c] = x[:, c * LANES:(c + 1) * LANES]

    def project_out(xc):
        u = jnp.concatenate([uc_ref[0], uc_ref[1]], axis=1)
        h = _lane_blocks(xc).astype(BF16)
        y = jnp.dot(h, cm_ref[0], preferred_element_type=F32) + dsk_ref[0] * u
        act_ref[...] = jax.nn.gelu(y).astype(BF16)

    def recur(xs):
        top1 = lax.broadcasted_iota(I32, (SUBLANES, LANES), 0) < N_PROMPT_SEQ
        first = t_s == 0
        hr = [jnp.where(first, 0.0, car[:, c * LANES:(c + 1) * LANES]) for c in range(GB_LB)]
        hi = [jnp.where(first, 0.0, car[:, GB_STATE + c * LANES:GB_STATE + (c + 1) * LANES])
              for c in range(GB_LB)]
        for r in range(TM // SUBLANES):
            rows = slice(r * SUBLANES, (r + 1) * SUBLANES)
            for c in range(GB_LB):
                cols = slice(c * LANES, (c + 1) * LANES)
                xr = xs[c, rows, :]
                xi = xs[GB_LB + c, rows, :]
                a1r, a1i, a2r, a2i = cst[0, :, cols], cst[1, :, cols], cst[2, :, cols], cst[3, :, cols]
                sxr = pltpu.roll(xr, N_PROMPT_SEQ, axis=0)
                sxi = pltpu.roll(xi, N_PROMPT_SEQ, axis=0)
                tr = xr + a1r * sxr - a1i * sxi
                ti = xi + a1r * sxi + a1i * sxr
                nr = tr + a2r * hr[c] - a2i * hi[c]
                ni = ti + a2r * hi[c] + a2i * hr[c]
                xs[c, rows, :] = nr
                xs[GB_LB + c, rows, :] = ni
                hr[c] = jnp.where(top1, pltpu.roll(nr, N_PROMPT_SEQ, axis=0), nr)
                hi[c] = jnp.where(top1, pltpu.roll(ni, N_PROMPT_SEQ, axis=0), ni)
        state = jnp.concatenate(hr + hi, axis=1)
        car[...] = state
        last = jnp.logical_and(valid_s, t_s == N_PROMPT_TILES - 1)
        fin[...] = jnp.where(last, state, fin[...])
        pfr_ref[0] = fin[N_PROMPT_SEQ:SUBLANES, 0:GB_STATE]
        pfi_ref[0] = fin[N_PROMPT_SEQ:SUBLANES, GB_STATE:2 * GB_STATE]

    bufs = (x0, x1, x2)
    for r in range(SCAN_STAGES):
        @pl.when(n % SCAN_STAGES == r)
        def _():
            project_out(bufs[(r + 1) % SCAN_STAGES])
            recur(bufs[(r + 2) % SCAN_STAGES])
            project_in(bufs[r])


def _scan_prompt(u3, bm, cm, ar, ai, dsk):
    q_a = lambda n: jnp.minimum(n, N_SCAN_TILES - 1)
    q_s = lambda n: jnp.clip(n - 1, 0, N_SCAN_TILES - 1)
    q_c = lambda n: jnp.clip(n - 2, 0, N_SCAN_TILES - 1)
    gb = lambda q: q // N_PROMPT_TILES
    tt = lambda q: q % N_PROMPT_TILES
    u_spec = lambda q: pl.BlockSpec((GB_IN // LANES, TM, LANES), lambda n: (gb(q(n)), tt(q(n)), 0))
    gb3 = lambda last, q: pl.BlockSpec((1,) + last, lambda n: (gb(q(n)), 0, 0))
    state = jax.ShapeDtypeStruct((GROUP_BLOCKS, N_PROMPT_SEQ, GB_STATE), F32)
    xbuf = pltpu.VMEM((2 * GB_LB, TM, LANES), F32)
    return pl.pallas_call(
        _scan_prompt_kernel,
        grid=(N_SCAN_TILES + SCAN_STAGES - 1,),
        in_specs=[
            u_spec(q_a), u_spec(q_c),
            gb3((GB_IN, 2 * GB_STATE), q_a),
            gb3((2 * GB_STATE, GB_IN), q_c),
            gb3((1, GB_STATE), q_s),
            gb3((1, GB_STATE), q_s),
            gb3((1, GB_IN), q_c),
        ],
        out_specs=[
            pl.BlockSpec((TM, GB_IN), lambda n: (tt(q_c(n)), gb(q_c(n)))),
            gb3((N_PROMPT_SEQ, GB_STATE), q_s),
            gb3((N_PROMPT_SEQ, GB_STATE), q_s),
        ],
        out_shape=(jax.ShapeDtypeStruct((N_PROMPT_TOK, D_SSM), BF16), state, state),
        scratch_shapes=[xbuf, xbuf, xbuf,
                        pltpu.VMEM((4, SUBLANES, GB_STATE), F32),
                        pltpu.VMEM((SUBLANES, 2 * GB_STATE), F32),
                        pltpu.VMEM((SUBLANES, 2 * GB_STATE), F32)],
        compiler_params=_cparams(("arbitrary",), 48),
        name="s5_scan_prompt",
    )(u3, u3, bm, cm, ar, ai, dsk)


def _scan_sample_kernel(u_ref, bm_ref, cm_ref, ar_ref, ai_ref, dsk_ref, s0r_ref, s0i_ref,
                        act_ref, sfr_ref, sfi_ref, xs_ref):
    half_lb = GB_LB // 2
    half_cols = half_lb * LANES
    u = jnp.concatenate([u_ref[0], u_ref[1]], axis=1)
    x = jnp.dot(u.astype(BF16), bm_ref[0], preferred_element_type=F32)
    for c in range(2 * GB_LB):
        xs_ref[c] = x[:, c * LANES:(c + 1) * LANES]

    def load_half(rows, half, imag):
        c0 = imag * GB_LB + half * half_lb
        return jnp.concatenate([xs_ref[c0 + c, rows, :] for c in range(half_lb)], axis=1)

    def store_half(rows, half, imag, val):
        c0 = imag * GB_LB + half * half_lb
        for c in range(half_lb):
            xs_ref[c0 + c, rows, :] = val[:, c * LANES:(c + 1) * LANES]

    shape = (N_SAMPLE_SEQ, half_cols)
    for half in range(2):
        cols = slice(half * half_cols, (half + 1) * half_cols)
        arb = jnp.broadcast_to(ar_ref[0][:, cols], shape)
        aib = jnp.broadcast_to(ai_ref[0][:, cols], shape)

        def body(k, carry):
            hr, hi = carry
            rows = pl.ds(pl.multiple_of(k * N_SAMPLE_SEQ, N_SAMPLE_SEQ), N_SAMPLE_SEQ)
            xr = load_half(rows, half, 0)
            xi = load_half(rows, half, 1)
            nhr = xr + arb * hr - aib * hi
            nhi = xi + arb * hi + aib * hr
            store_half(rows, half, 0, nhr)
            store_half(rows, half, 1, nhi)
            return nhr, nhi

        hr, hi = lax.fori_loop(0, SAMPLE_LEN, body, (s0r_ref[:, cols], s0i_ref[:, cols]))
        sfr_ref[:, cols] = hr
        sfi_ref[:, cols] = hi

    h = _lane_blocks(xs_ref).astype(BF16)
    y = jnp.dot(h, cm_ref[0], preferred_element_type=F32) + dsk_ref[0] * u
    act_ref[...] = jax.nn.gelu(y).astype(BF16)


def _scan_sample(u3, bm, cm, ar, ai, dsk, s0r, s0i):
    gb3 = lambda last: pl.BlockSpec((1,) + last, lambda g: (g, 0, 0))
    state = lambda: pl.BlockSpec((N_SAMPLE_SEQ, GB_STATE), lambda g: (0, g))
    sst = jax.ShapeDtypeStruct((N_SAMPLE_SEQ, N_SSM_GROUPS * SSM_STATE), F32)
    return pl.pallas_call(
        _scan_sample_kernel,
        grid=(GROUP_BLOCKS,),
        in_specs=[
            pl.BlockSpec((GB_IN // LANES, TM, LANES), lambda g: (g, N_PROMPT_TILES, 0)),
            gb3((GB_IN, 2 * GB_STATE)), gb3((2 * GB_STATE, GB_IN)),
            gb3((1, GB_STATE)), gb3((1, GB_STATE)), gb3((1, GB_IN)),
            state(), state(),
        ],
        out_specs=[pl.BlockSpec((TM, GB_IN), lambda g: (0, g)), state(), state()],
        out_shape=(jax.ShapeDtypeStruct((N_SAMPLE_TOK, D_SSM), BF16), sst, sst),
        scratch_shapes=[pltpu.VMEM((2 * GB_LB, TM, LANES), F32)],
        compiler_params=_cparams(("arbitrary",), 32),
        name="s5_scan_sample",
    )(u3, bm, cm, ar, ai, dsk, s0r, s0i)


CONV_ROWS = 64
EXT_ROWS = HIST_S + TM


def _conv_kernel(v_ref, hist_ref, w_ref, b_ref, lg_ref, lb_ref,
                 cv_ref, ncp_ref, ncs_ref, ext_ref, sh_ref, acc_ref):
    t = pl.program_id(0)

    def taps(n_seq):
        half = SUBLANES // 2
        if n_seq % SUBLANES:
            lo = HIST_S - HIST_P - SUBLANES
            sh_ref[:, lo:EXT_ROWS - SUBLANES, :] = ext_ref[:, lo + half:EXT_ROWS - half, :]

        def body(rb, _):
            r0 = pl.multiple_of(rb * CONV_ROWS, CONV_ROWS)
            for cb in range(CONV_LB):
                cols = slice(cb * LANES, (cb + 1) * LANES)
                acc = None
                for j in range(CONV_WIDTH):
                    start = HIST_S - (CONV_BUF - j) * n_seq
                    src = ext_ref
                    if start % SUBLANES:
                        src, start = sh_ref, start - half
                    rows = pl.ds(pl.multiple_of(r0 + start, SUBLANES), CONV_ROWS)
                    term = src[cb, rows, :] * w_ref[j:j + 1, cols]
                    acc = term if acc is None else acc + term
                acc_ref[cb, pl.ds(r0, CONV_ROWS), :] = acc + b_ref[:, cols]
            return 0

        lax.fori_loop(0, TM // CONV_ROWS, body, 0)

    @pl.when(t == 0)
    def _():
        ext_ref[:, 0:HIST_S, :] = jnp.zeros((CONV_LB, HIST_S, LANES), F32)

    @pl.when(t == N_PROMPT_TILES)
    def _():
        for c in range(CONV_LB):
            ext_ref[c, 0:HIST_S, :] = hist_ref[:, c * LANES:(c + 1) * LANES]

    ext_ref[:, HIST_S:EXT_ROWS, :] = v_ref[...]

    @pl.when(t < N_PROMPT_TILES)
    def _():
        taps(N_PROMPT_SEQ)

    @pl.when(t == N_PROMPT_TILES)
    def _():
        taps(N_SAMPLE_SEQ)

    y = acc_ref[...]
    mu = jnp.sum(jnp.sum(y, axis=0), axis=-1, keepdims=True) / D_CONV
    yc = y - mu[None]
    var = jnp.sum(jnp.sum(yc * yc, axis=0), axis=-1, keepdims=True) / D_CONV
    z = jax.nn.silu(yc * lax.rsqrt(var + EPS)[None] * lg_ref[...] + lb_ref[...])
    for c in range(CONV_LB):
        cv_ref[:, c * LANES:(c + 1) * LANES] = z[c].astype(BF16)

    @pl.when(t == N_PROMPT_TILES - 1)
    def _():
        for c in range(CONV_LB):
            ncp_ref[:, c * LANES:(c + 1) * LANES] = ext_ref[c, EXT_ROWS - HIST_P:EXT_ROWS, :]

    @pl.when(t < N_PROMPT_TILES - 1)
    def _():
        ext_ref[:, HIST_S - HIST_P:HIST_S, :] = ext_ref[:, EXT_ROWS - HIST_P:EXT_ROWS, :]

    @pl.when(t == N_PROMPT_TILES)
    def _():
        for c in range(CONV_LB):
            ncs_ref[:, c * LANES:(c + 1) * LANES] = ext_ref[c, EXT_ROWS - HIST_S:EXT_ROWS, :]


def _conv(v3, hist_s, conv_w, conv_b, ln_g, ln_b):
    row = lambda n: pl.BlockSpec((n, D_CONV), lambda t: (0, 0))
    lane3 = pl.BlockSpec((CONV_LB, 1, LANES), lambda t: (0, 0, 0))
    ext = pltpu.VMEM((CONV_LB, EXT_ROWS, LANES), F32)
    return pl.pallas_call(
        _conv_kernel,
        grid=(N_TILES,),
        in_specs=[pl.BlockSpec((CONV_LB, TM, LANES), lambda t: (0, t, 0)),
                  row(HIST_S), row(CONV_WIDTH), row(1), lane3, lane3],
        out_specs=[pl.BlockSpec((TM, D_CONV), lambda t: (t, 0)), row(HIST_P), row(HIST_S)],
        out_shape=(jax.ShapeDtypeStruct((N_TOK, D_CONV), BF16),
                   jax.ShapeDtypeStruct((HIST_P, D_CONV), F32),
                   jax.ShapeDtypeStruct((HIST_S, D_CONV), F32)),
        scratch_shapes=[ext, ext, pltpu.VMEM((CONV_LB, TM, LANES), F32)],
        compiler_params=_cparams(("arbitrary",), 48),
        name="conv_branch",
    )(v3, hist_s, conv_w, conv_b,
      ln_g.reshape(CONV_LB, 1, LANES), ln_b.reshape(CONV_LB, 1, LANES))


MERGE_COLS = 512
N_MERGE = D_MODEL // MERGE_COLS


def _merge_kernel(xn_ref, actp_ref, acts_ref, cv_ref, wga_ref, wgb_ref, wg1_ref, wg2_ref, wco_ref,
                  m_ref, xn_s, act_s):
    i = pl.program_id(0)
    j = pl.program_id(1)

    @pl.when(j == 0)
    def _():
        xn_s[...] = _lane_blocks(xn_ref).astype(BF16)

        @pl.when(i < N_PROMPT_TILES)
        def _():
            act_s[...] = actp_ref[...]

        @pl.when(i >= N_PROMPT_TILES)
        def _():
            act_s[...] = acts_ref[...]

    xn = xn_s[...]
    act = act_s[...]
    dot = functools.partial(jnp.dot, preferred_element_type=F32)
    ya = dot(act, wg1_ref[...]) * jax.nn.sigmoid(dot(act, wg2_ref[...]))
    yb = dot(cv_ref[...], wco_ref[...])
    m = jax.nn.sigmoid(dot(xn, wga_ref[...])) * ya + jax.nn.sigmoid(dot(xn, wgb_ref[...])) * yb
    m_ref[...] = m.astype(BF16)


def _merge(xn3, actp, acts, cv, w_in_bf, w_glu_bf, w_co_bf):
    ga0 = (D_SSM + 2 * D_CONV) // MERGE_COLS
    gb0 = ga0 + N_MERGE
    return pl.pallas_call(
        _merge_kernel,
        grid=(N_TILES, N_MERGE),
        in_specs=[pl.BlockSpec((MODEL_LB, TM, LANES), lambda i, j: (0, i, 0))]
        + list(_pair_specs(TM, D_SSM, N_PROMPT_TILES)) + [
            pl.BlockSpec((TM, D_CONV), lambda i, j: (i, 0)),
            pl.BlockSpec((D_MODEL, MERGE_COLS), lambda i, j: (0, ga0 + j)),
            pl.BlockSpec((D_MODEL, MERGE_COLS), lambda i, j: (0, gb0 + j)),
            pl.BlockSpec((D_SSM, MERGE_COLS), lambda i, j: (0, j)),
            pl.BlockSpec((D_SSM, MERGE_COLS), lambda i, j: (0, N_MERGE + j)),
            pl.BlockSpec((D_CONV, MERGE_COLS), lambda i, j: (0, j)),
        ],
        out_specs=pl.BlockSpec((TM, MERGE_COLS), lambda i, j: (i, j)),
        out_shape=jax.ShapeDtypeStruct((N_TOK, D_MODEL), BF16),
        scratch_shapes=[pltpu.VMEM((TM, D_MODEL), BF16), pltpu.VMEM((TM, D_SSM), BF16)],
        compiler_params=_cparams(("arbitrary", "arbitrary"), 48),
        name="gated_merge",
    )(xn3, actp, acts, cv, w_in_bf, w_in_bf, w_glu_bf, w_glu_bf, w_co_bf)


def _outproj_kernel(m_ref, x0_ref, x1_ref, x2_ref, x3_ref, xs_ref, wo_ref, g_ref, wr_ref, br_ref,
                    h_ref, xl_ref, route_ref, gw_ref, cnt_ref, xr_s, lg0, lg1, xb0, xb1):
    i = pl.program_id(0)
    tile = jnp.minimum(i, N_OT - 1)
    steps_p = TO // N_PROMPT_SEQ
    steps_s = TO // N_SAMPLE_SEQ

    @pl.when(i == 0)
    def _():
        lg1[...] = jnp.zeros_like(lg1)
        xb1[...] = jnp.zeros_like(xb1)

    @pl.when(tile < N_OT_PROMPT)
    def _():
        x = jnp.concatenate([x0_ref[...], x1_ref[...], x2_ref[...], x3_ref[...]], axis=0)
        _to_mixer_order(xr_s, x, N_PROMPT_SEQ, steps_p)

    @pl.when(tile >= N_OT_PROMPT)
    def _():
        k0 = (tile - N_OT_PROMPT) * steps_s
        x = jnp.concatenate(
            [xs_ref[pl.ds(pl.multiple_of(b * SAMPLE_LEN + k0, steps_s), steps_s), :]
             for b in range(N_SAMPLE_SEQ)], axis=0)
        _to_mixer_order(xr_s, x, N_SAMPLE_SEQ, steps_s)

    sets = ((lg0, xb0), (lg1, xb1))
    for r in range(2):
        @pl.when(i % 2 == r)
        def _():
            lg_prev, xb_prev = sets[1 - r]
            _route_and_sort(lg_prev[...], xb_prev[...], xl_ref, route_ref, gw_ref, cnt_ref)
            _project(m_ref, xr_s, wo_ref, g_ref, wr_ref, br_ref, h_ref, *sets[r])


def _project(m_ref, xr_s, wo_ref, g_ref, wr_ref, br_ref, h_ref, lg_s, xb_s):
    dot = functools.partial(jnp.dot, preferred_element_type=F32)
    h = _lane_blocks(xr_s) + dot(m_ref[...], wo_ref[...])
    h_ref[...] = h
    xn = _rmsnorm_rows(h, g_ref[...])
    x_hi = xn.astype(BF16)
    x_lo = (xn - x_hi.astype(F32)).astype(BF16)
    w = wr_ref[...]
    w_hi = w.astype(BF16)
    w_lo = (w - w_hi.astype(F32)).astype(BF16)
    both = dot(x_hi, jnp.concatenate([w_hi, w_lo], axis=1))
    lg_s[...] = (both[:, :ROUTE_LANES] + (dot(x_lo, w_hi) + both[:, ROUTE_LANES:])) + br_ref[...]
    xb_s[...] = x_hi


def _route_and_sort(logits, x_hi, xl_ref, route_ref, gw_ref, cnt_ref):
    dot = functools.partial(jnp.dot, preferred_element_type=F32)
    lane = lax.broadcasted_iota(I32, logits.shape, 1)
    neg = -jnp.inf
    first = lambda hit: jnp.min(jnp.where(hit, lane, ROUTE_LANES), axis=-1, keepdims=True)
    gmask = lane < N_EXP_GROUPS
    lg = jnp.where(gmask, logits, neg)
    gmax = jnp.max(lg, axis=-1, keepdims=True)
    gsel = first(lg == gmax)
    psum = jnp.sum(jnp.where(gmask, jnp.exp(logits - gmax), 0.0), axis=-1, keepdims=True)
    pg_sel = 1.0 / psum
    e_lane = lane - N_EXP_GROUPS
    emask = (e_lane >= 0) & (e_lane < N_EXPERTS) & ((e_lane // EXP_PER_GROUP) == gsel)
    le = jnp.where(emask, logits, neg)
    v1 = jnp.max(le, axis=-1, keepdims=True)
    i1 = first(le == v1)
    le2 = jnp.where(lane == i1, neg, le)
    v2 = jnp.max(le2, axis=-1, keepdims=True)
    i2 = first(le2 == v2)
    z = jnp.exp(v2 - v1)
    w1 = pg_sel / (1.0 + z)
    w2 = pg_sel * z / (1.0 + z)
    e1 = i1 - N_EXP_GROUPS
    e2 = i2 - N_EXP_GROUPS
    gw_ref[...] = jnp.where(lane == 0, w1, jnp.where(lane == 1, w2, 0.0))
    oh1 = lane == e1
    oh2 = lane == e2
    hits = jnp.where(oh1 | oh2, 1.0, 0.0)
    rr = lax.broadcasted_iota(I32, (TO, TO), 0)
    cc = lax.broadcasted_iota(I32, (TO, TO), 1)
    before = dot(jnp.where(cc < rr, 1.0, 0.0).astype(BF16), hits.astype(BF16))
    counts = jnp.sum(hits, axis=0, keepdims=True)
    groups = jnp.floor((counts + (RUN - 1.0)) * (1.0 / RUN))
    er = lax.broadcasted_iota(I32, (ROUTE_LANES, ROUTE_LANES), 0)
    ec = lax.broadcasted_iota(I32, (ROUTE_LANES, ROUTE_LANES), 1)
    groups_before = dot(jnp.broadcast_to(groups, (SUBLANES, ROUTE_LANES)).astype(BF16),
                        jnp.where(er < ec, 1.0, 0.0).astype(BF16))[0:1]
    local = groups_before * RUN + before
    lpos1 = jnp.sum(jnp.where(oh1, local, 0.0), axis=-1, keepdims=True)
    lpos2 = jnp.sum(jnp.where(oh2, local, 0.0), axis=-1, keepdims=True)
    route_ref[...] = jnp.where(lane == 0, e1, jnp.where(lane == 1, e2, jnp.where(
        lane == 2, lpos1.astype(I32), jnp.where(lane == 3, lpos2.astype(I32), 0))))
    cnt_ref[0] = counts.astype(I32)
    lpos_t = jnp.where(lane == 0, lpos1, jnp.where(lane == 1, lpos2, 0.0)).T.astype(I32)
    jrow = lax.broadcasted_iota(I32, (LROWS, TO), 0)
    sel = (jrow == lpos_t[0:1, :]) | (jrow == lpos_t[1:2, :])
    xl_ref[...] = dot(jnp.where(sel, 1.0, 0.0).astype(BF16), x_hi).astype(BF16)


def _outproj(m, xp, xs, w_out_bf, g, w_router, b_router):
    cur = lambda i: jnp.minimum(i, N_OT - 1)
    prev = lambda i: jnp.maximum(i - 1, 0)
    tile = lambda n: pl.BlockSpec((TO, n), lambda i: (cur(i), 0))
    routed = lambda n: pl.BlockSpec((TO, n), lambda i: (prev(i), 0))
    return pl.pallas_call(
        _outproj_kernel,
        grid=(N_OT + 1,),
        in_specs=[tile(D_MODEL)] + _x_tile_specs(TO // N_PROMPT_SEQ) + [
            _resident((D_MODEL, D_MODEL)), _resident((1, D_MODEL)),
            _resident((D_MODEL, ROUTE_LANES)), _resident((1, ROUTE_LANES))],
        out_specs=[tile(D_MODEL),
                   pl.BlockSpec((LROWS, D_MODEL), lambda i: (prev(i), 0)),
                   routed(ROUTE_LANES), routed(ROUTE_LANES),
                   pl.BlockSpec((1, 1, ROUTE_LANES), lambda i: (prev(i), 0, 0))],
        out_shape=(jax.ShapeDtypeStruct((N_TOK, D_MODEL), F32),
                   jax.ShapeDtypeStruct((N_OT * LROWS, D_MODEL), BF16),
                   jax.ShapeDtypeStruct((N_TOK, ROUTE_LANES), I32),
                   jax.ShapeDtypeStruct((N_TOK, ROUTE_LANES), F32),
                   jax.ShapeDtypeStruct((N_OT, 1, ROUTE_LANES), I32)),
        scratch_shapes=[pltpu.VMEM((MODEL_LB, TO, LANES), F32),
                        pltpu.VMEM((TO, ROUTE_LANES), F32), pltpu.VMEM((TO, ROUTE_LANES), F32),
                        pltpu.VMEM((TO, D_MODEL), BF16), pltpu.VMEM((TO, D_MODEL), BF16)],
        compiler_params=_cparams(("arbitrary",), 56),
        name="out_proj_router",
    )(m, xp, xp, xp, xp, xs, w_out_bf, g, w_router, b_router)


def _expert_kernel(te_ref, nrows_ref, nused_ref, psrc_ref, pdst_ref, plen_ref, tot_ref, kfirst_ref, kend_ref,
                   first_ref, par_ref, nexte_ref,
                   xl_hbm, wg_hbm, wu_hbm, wd_hbm, ol_hbm,
                   xbuf, obuf, zbuf, wg_s, wu_s, wd_s, wg_st, wu_st, wd_st, gsem, wsem, zsem, esem):
    i = pl.program_id(0)
    nused = nused_ref[0]
    slot = i % 2
    rows8 = lambda v: pl.multiple_of(v, RUN)

    def for_pieces(tile, fn):
        def body(k, _):
            idx = tile * N_OT + k
            n = plen_ref[idx]

            @pl.when(n > 0)
            def _():
                fn(rows8(psrc_ref[idx]), rows8(pdst_ref[idx]), rows8(n))
            return 0
        lax.fori_loop(kfirst_ref[tile], kend_ref[tile], body, 0)

    def gather(tile, s):
        for_pieces(tile, lambda src, dst, n: pltpu.make_async_copy(
            xl_hbm.at[pl.ds(src, n)], xbuf.at[s, pl.ds(dst, n)], gsem.at[s]).start())

    def scatter(tile, s):
        for_pieces(tile, lambda src, dst, n: pltpu.make_async_copy(
            obuf.at[s, pl.ds(dst, n)], ol_hbm.at[pl.ds(src, n)], wsem.at[s]).start())

    def wait_gather(tile, s):
        n = rows8(nrows_ref[tile])
        pltpu.make_async_copy(xl_hbm.at[pl.ds(0, n)], xbuf.at[s, pl.ds(0, n)], gsem.at[s]).wait()

    def wait_scatter(tile, s):
        n = rows8(nrows_ref[tile])
        pltpu.make_async_copy(obuf.at[s, pl.ds(0, n)], ol_hbm.at[pl.ds(0, n)], wsem.at[s]).wait()

    def tail_copy(t):
        used = rows8(tot_ref[t])
        n = rows8(LROWS - tot_ref[t])
        dst = rows8(t * LROWS + used)
        return pltpu.make_async_copy(zbuf.at[pl.ds(0, n)], ol_hbm.at[pl.ds(dst, n)], zsem.at[0])

    def weight_copies(e, b):
        return [pltpu.make_async_copy(src.at[e], dst.at[b], esem.at[b, k])
                for k, (src, dst) in enumerate(((wg_hbm, wg_st), (wu_hbm, wu_st), (wd_hbm, wd_st)))]

    @pl.when(i == 0)
    def _():
        for c in weight_copies(te_ref[0], par_ref[0]):
            c.start()
        zbuf[...] = jnp.zeros_like(zbuf)
        xbuf[...] = jnp.zeros_like(xbuf)

        def start(t, _):
            tail_copy(t).start()
            return 0

        def wait(t, _):
            tail_copy(t).wait()
            return 0
        lax.fori_loop(0, N_OT, start, 0)
        lax.fori_loop(0, N_OT, wait, 0)
        gather(0, 0)

    @pl.when(i + 1 < nused)
    def _():
        gather(i + 1, 1 - slot)

    @pl.when(i < nused)
    def _():
        @pl.when(first_ref[i] == 1)
        def _():
            b = par_ref[i]
            for c in weight_copies(te_ref[i], b):
                c.wait()
            wg_s[...] = wg_st[b].astype(BF16)
            wu_s[...] = wu_st[b].astype(BF16)
            wd_s[...] = wd_st[b].astype(BF16)

            @pl.when(nexte_ref[i] >= 0)
            def _():
                for c in weight_copies(nexte_ref[i], 1 - b):
                    c.start()

        wait_gather(i, slot)

        @pl.when(i >= 2)
        def _():
            wait_scatter(i - 2, slot)

        xb = xbuf[slot]
        hg = jnp.dot(xb, wg_s[...], preferred_element_type=F32)
        hu = jnp.dot(xb, wu_s[...], preferred_element_type=F32)
        hid = jax.nn.silu(hg) * hu
        obuf[slot] = jnp.dot(hid.astype(BF16), wd_s[...], preferred_element_type=F32).astype(BF16)
        scatter(i, slot)

        @pl.when(i == nused - 1)
        def _():
            @pl.when(i >= 1)
            def _():
                wait_scatter(i - 1, 1 - slot)
            wait_scatter(i, slot)


def _experts(plan, xl, wg, wu, wd):
    anyspec = pl.BlockSpec(memory_space=pl.ANY)
    grid_spec = pltpu.PrefetchScalarGridSpec(
        num_scalar_prefetch=len(plan),
        grid=(N_ETILES,),
        in_specs=[anyspec, anyspec, anyspec, anyspec],
        out_specs=anyspec,
        scratch_shapes=[
            pltpu.VMEM((2, TME, D_MODEL), BF16),
            pltpu.VMEM((2, TME, D_MODEL), BF16),
            pltpu.VMEM((LROWS - 2 * TO, D_MODEL), BF16),
            pltpu.VMEM((D_MODEL, D_EXPERT), BF16),
            pltpu.VMEM((D_MODEL, D_EXPERT), BF16),
            pltpu.VMEM((D_EXPERT, D_MODEL), BF16),
            pltpu.VMEM((2, D_MODEL, D_EXPERT), F32),
            pltpu.VMEM((2, D_MODEL, D_EXPERT), F32),
            pltpu.VMEM((2, D_EXPERT, D_MODEL), F32),
            pltpu.SemaphoreType.DMA((2,)),
            pltpu.SemaphoreType.DMA((2,)),
            pltpu.SemaphoreType.DMA((1,)),
            pltpu.SemaphoreType.DMA((2, 3)),
        ],
    )
    return pl.pallas_call(
        _expert_kernel,
        grid_spec=grid_spec,
        out_shape=jax.ShapeDtypeStruct((N_OT * LROWS, D_MODEL), BF16),
        compiler_params=_cparams(("arbitrary",), 40),
        name="routed_experts",
    )(*plan, xl, wg, wu, wd)


def _combine_kernel(h_ref, route_ref, gw_ref, ol_ref, g_ref, yp_ref, ys_ref, y_s):
    i = pl.program_id(0)
    steps_p = TO // N_PROMPT_SEQ
    steps_s = TO // N_SAMPLE_SEQ
    route = route_ref[...]
    gw = gw_ref[...]
    jcol = lax.broadcasted_iota(I32, (TO, LROWS), 1)
    mix = (jnp.where(jcol == route[:, 2:3], gw[:, 0:1], 0.0)
           + jnp.where(jcol == route[:, 3:4], gw[:, 1:2], 0.0))
    moe = jnp.dot(mix.astype(BF16), ol_ref[...], preferred_element_type=F32)
    y = _rmsnorm_rows(h_ref[...] + moe, g_ref[...])
    for c in range(MODEL_LB):
        y_s[c] = y[:, c * LANES:(c + 1) * LANES]

    @pl.when(i < N_OT_PROMPT)
    def _():
        for b in range(N_PROMPT_SEQ):
            yp_ref[b] = _seq_rows(y_s, b, N_PROMPT_SEQ, steps_p)

    @pl.when(i >= N_OT_PROMPT)
    def _():
        k0 = (i - N_OT_PROMPT) * steps_s
        for b in range(N_SAMPLE_SEQ):
            rows = pl.ds(pl.multiple_of(b * SAMPLE_LEN + k0, steps_s), steps_s)
            ys_ref[rows, :] = _seq_rows(y_s, b, N_SAMPLE_SEQ, steps_s)


def _combine(h, route, gw, ol, g):
    tile = lambda n: pl.BlockSpec((TO, n), lambda i: (i, 0))
    return pl.pallas_call(
        _combine_kernel,
        grid=(N_OT,),
        in_specs=[tile(D_MODEL), tile(ROUTE_LANES), tile(ROUTE_LANES),
                  pl.BlockSpec((LROWS, D_MODEL), lambda i: (i, 0)),
                  pl.BlockSpec((1, D_MODEL), lambda i: (0, 0))],
        out_specs=[
            pl.BlockSpec((N_PROMPT_SEQ, TO // N_PROMPT_SEQ, D_MODEL),
                         lambda i: (0, jnp.minimum(i, N_OT_PROMPT - 1), 0)),
            pl.BlockSpec((N_SAMPLE_TOK, D_MODEL), lambda i: (0, 0)),
        ],
        out_shape=(jax.ShapeDtypeStruct((N_PROMPT_SEQ, PROMPT_LEN, D_MODEL), F32),
                   jax.ShapeDtypeStruct((N_SAMPLE_TOK, D_MODEL), F32)),
        scratch_shapes=[pltpu.VMEM((MODEL_LB, TO, LANES), F32)],
        compiler_params=_cparams(("arbitrary",), 48),
        name="combine_norm",
    )(h, route, gw, ol, g)


def _dispatch_plan(cnt):
    experts = jnp.arange(N_EXPERTS, dtype=I32)
    cnt = cnt[:, 0, :N_EXPERTS]
    run = ((cnt + RUN - 1) // RUN) * RUN
    local_start = jnp.cumsum(run, axis=1) - run
    local_used = jnp.sum(run, axis=1)
    total = jnp.sum(run, axis=0)
    padded = ((total + TME - 1) // TME) * TME
    ends = jnp.cumsum(padded)
    starts = ends - padded
    run_start = starts[None, :] + jnp.cumsum(run, axis=0) - run
    n_used = (ends[-1] // TME).astype(I32)
    tile = jnp.arange(N_ETILES, dtype=I32)
    tile_start = tile * TME
    te = jnp.minimum(jnp.sum((ends[None, :] <= tile_start[:, None]).astype(I32), axis=1), N_EXPERTS - 1)
    mine = (te[:, None] == experts[None, :])[:, None, :]
    of_tile = lambda a: jnp.sum(jnp.where(mine, a[None], 0), axis=-1)
    s0 = of_tile(run_start)
    lo = jnp.maximum(s0, tile_start[:, None])
    hi = jnp.minimum(s0 + of_tile(run), tile_start[:, None] + TME)
    plen = jnp.where((tile < n_used)[:, None], jnp.maximum(hi - lo, 0), 0)
    live = plen > 0
    psrc = jnp.where(live, jnp.arange(N_OT, dtype=I32)[None, :] * LROWS + of_tile(local_start) + lo - s0, 0)
    pdst = jnp.where(live, lo - tile_start[:, None], 0)
    last = jnp.sum(jnp.where(tile == n_used - 1, te, 0))
    te = jnp.where(tile < n_used, te, last)
    flat = lambda a: a.reshape(-1).astype(I32)
    none_yet = lambda a: jnp.sum((jnp.cumsum(a.astype(I32), axis=1) == 0).astype(I32), axis=1)
    k_first = none_yet(live)
    k_end = N_OT - none_yet(live[:, ::-1])
    first = jnp.logical_and(tile < n_used, jnp.concatenate([jnp.array([True]), te[1:] != te[:-1]]))
    parity = (jnp.cumsum(first.astype(I32)) - 1) % 2
    next_first = lax.cummin(jnp.where(first, tile, N_ETILES), axis=0, reverse=True)
    next_first = jnp.concatenate([next_first[1:], jnp.array([N_ETILES], I32)])
    next_expert = jnp.where(next_first < N_ETILES, te[jnp.minimum(next_first, N_ETILES - 1)], -1)
    return (te.astype(I32), flat(jnp.sum(plen, axis=1)), n_used.reshape(1),
            flat(psrc), flat(pdst), flat(plen), local_used.astype(I32), flat(k_first), flat(k_end),
            flat(first), flat(parity), flat(next_expert))


def kernel(x_prompt, x_sample, state_ssm_re, state_ssm_im, cache_conv, norm_mix_g, w_in, lam_re, lam_im, log_dt, b_re, b_im, c_re, c_im, d_skip, w_ssm_glu, conv_w, conv_b, conv_ln_g, conv_ln_b, w_conv_out, w_out, norm_ffn_g, w_router_group, b_router_group, w_router_expert, b_router_expert, w_exp_gate, w_exp_up, w_exp_down, norm_final_g):
    assert w_in.shape[0] == 1, "single-layer trunk"
    xp = x_prompt.reshape(N_PROMPT_TOK, D_MODEL)
    xs = x_sample.reshape(N_SAMPLE_TOK, D_MODEL)
    w_in_bf = w_in[0].astype(BF16)
    row = lambda a: a.reshape(1, -1)

    ar_rep, ai_rep, bbr, bbi = _discretise(lam_re[0], lam_im[0], log_dt[0], b_re[0], b_im[0])
    gph = lambda a: a.reshape(N_SSM_GROUPS, SSM_STATE, SSM_GROUP)
    pick = lambda a: gph(a)[:, :, 0].reshape(GROUP_BLOCKS, 1, GB_STATE)
    bm = jnp.concatenate([_block_diag_in(gph(bbr)), _block_diag_in(gph(bbi))], axis=-1).astype(BF16)
    cm = jnp.concatenate([_block_diag_out(c_re[0]), -_block_diag_out(c_im[0])], axis=1).astype(BF16)
    dsk = d_skip[0].reshape(GROUP_BLOCKS, 1, GB_IN)
    ar, ai = pick(ar_rep), pick(ai_rep)
    s0r = state_ssm_re[0].reshape(N_SAMPLE_SEQ, -1)
    s0i = state_ssm_im[0].reshape(N_SAMPLE_SEQ, -1)

    u3, v3, xn3 = _inproj(xp, xs, row(norm_mix_g[0]), w_in_bf)
    actp, pfr, pfi = _scan_prompt(u3, bm, cm, ar, ai, dsk)
    acts, sfr, sfi = _scan_sample(u3, bm, cm, ar, ai, dsk, s0r, s0i)

    hist_s = cache_conv[0].transpose(1, 0, 2).reshape(HIST_S, D_CONV)
    cv, ncp, ncs = _conv(v3, hist_s, conv_w[0], row(conv_b[0]), conv_ln_g[0], conv_ln_b[0])

    m = _merge(xn3, actp, acts, cv, w_in_bf, w_ssm_glu[0].astype(BF16), w_conv_out[0].astype(BF16))

    pad_lanes = ROUTE_LANES - N_EXP_GROUPS - N_EXPERTS
    w_router = jnp.concatenate(
        [w_router_group[0], w_router_expert[0], jnp.zeros((D_MODEL, pad_lanes), F32)], axis=1)
    b_router = jnp.concatenate(
        [b_router_group[0], b_router_expert[0], jnp.zeros((pad_lanes,), F32)]).reshape(1, ROUTE_LANES)
    h, xl, route, gw, cnt = _outproj(m, xp, xs, w_out[0].astype(BF16), row(norm_ffn_g[0]),
                                     w_router, b_router)

    ol = _experts(_dispatch_plan(cnt), xl, w_exp_gate[0], w_exp_up[0], w_exp_down[0])
    yp, ys = _combine(h, route, gw, ol, row(norm_final_g))

    st = lambda a, n: a.reshape(1, n, N_SSM_GROUPS, SSM_STATE)
    pst = lambda a: st(a.transpose(1, 0, 2), N_PROMPT_SEQ)
    ncp = ncp.reshape(CONV_BUF, N_PROMPT_SEQ, D_CONV).transpose(1, 0, 2)[None]
    ncs = ncs.reshape(CONV_BUF, N_SAMPLE_SEQ, D_CONV).transpose(1, 0, 2)[None]
    return (yp, ys.reshape(N_SAMPLE_SEQ, SAMPLE_LEN, D_MODEL), pst(pfr), pst(pfi), ncp,
            st(sfr, N_SAMPLE_SEQ), st(sfi, N_SAMPLE_SEQ), ncs)
```

```python
import functools

import jax
import jax.numpy as jnp
from jax import lax
from jax.experimental import pallas as pl
from jax.experimental.pallas import tpu as pltpu

F32 = jnp.float32
BF16 = jnp.bfloat16
I32 = jnp.int32

SUBLANES = 8
LANES = 128

D_MODEL = 2048
D_SSM = 1024
D_CONV = 1024
SSM_GROUP = 16
N_SSM_GROUPS = 64
SSM_STATE = 64
CONV_WIDTH = 31
CONV_BUF = CONV_WIDTH - 1
N_EXP_GROUPS = 4
EXP_PER_GROUP = 8
N_EXPERTS = 32
D_EXPERT = 256
EPS = 1e-6

N_PROMPT_SEQ = 4
PROMPT_LEN = 2048
N_SAMPLE_SEQ = 16
SAMPLE_LEN = 32
N_PROMPT_TOK = N_PROMPT_SEQ * PROMPT_LEN
N_SAMPLE_TOK = N_SAMPLE_SEQ * SAMPLE_LEN
N_TOK = N_PROMPT_TOK + N_SAMPLE_TOK

TM = 512
N_PROMPT_TILES = N_PROMPT_TOK // TM
N_TILES = N_TOK // TM
PROMPT_STEPS = TM // N_PROMPT_SEQ
GROUP_BLOCKS = 4
GROUPS_PER_BLOCK = N_SSM_GROUPS // GROUP_BLOCKS
GB_IN = GROUPS_PER_BLOCK * SSM_GROUP
GB_STATE = GROUPS_PER_BLOCK * SSM_STATE
GB_LB = GB_STATE // LANES
HIST_P = CONV_BUF * N_PROMPT_SEQ
HIST_S = CONV_BUF * N_SAMPLE_SEQ
ROUTE_LANES = 128
N_PAIRS = 2 * N_TOK
TO = 256
N_OT = N_TOK // TO
N_OT_PROMPT = N_PROMPT_TOK // TO
HALVES = TM // TO
RUN = 16
LROWS = 1024
TME = 512
N_ETILES = -(-(N_PAIRS + N_OT * N_EXPERTS * (RUN - 1) + N_EXPERTS * (TME - 1)) // TME)
assert 2 * TO + (RUN - 1) * N_EXPERTS <= LROWS
MODEL_LB = D_MODEL // LANES
SSM_LB = D_SSM // LANES
CONV_LB = D_CONV // LANES


def _cparams(sem, vmem_mb):
    return pltpu.CompilerParams(dimension_semantics=sem, vmem_limit_bytes=vmem_mb * 1024 * 1024)


def _resident(shape):
    return pl.BlockSpec(shape, lambda *_: (0,) * len(shape), pipeline_mode=pl.Buffered(1))


def _x_tile_specs(steps=PROMPT_STEPS):
    blocks_per_seq = PROMPT_LEN // steps
    n_prompt = N_PROMPT_SEQ * blocks_per_seq // N_PROMPT_SEQ

    def prompt(b):
        return pl.BlockSpec(
            (steps, D_MODEL),
            lambda i, *_: (b * blocks_per_seq + jnp.minimum(i, n_prompt - 1), 0))

    return [prompt(b) for b in range(N_PROMPT_SEQ)] + [_resident((N_SAMPLE_TOK, D_MODEL))]


def _to_mixer_order(ref3, val, n_seq, steps):
    for c in range(val.shape[1] // LANES):
        for b in range(n_seq):
            ref3[c, pl.ds(b, steps, stride=n_seq), :] = val[b * steps:(b + 1) * steps,
                                                            c * LANES:(c + 1) * LANES]


def _seq_rows(ref3, b, n_seq, steps):
    return jnp.concatenate(
        [ref3[c, pl.ds(b, steps, stride=n_seq), :] for c in range(ref3.shape[0])], axis=1)


def _tile_to_mixer_order(i, ref3, xs, sample):
    @pl.when(i < N_PROMPT_TILES)
    def _():
        _to_mixer_order(ref3, xs(), N_PROMPT_SEQ, PROMPT_STEPS)

    @pl.when(i >= N_PROMPT_TILES)
    def _():
        _to_mixer_order(ref3, sample(), N_SAMPLE_SEQ, SAMPLE_LEN)


def _lane_blocks(ref3):
    return jnp.concatenate([ref3[c] for c in range(ref3.shape[0])], axis=1)


def _pair_specs(rows, cols, n_prompt_tiles):
    p = pl.BlockSpec((rows, cols), lambda i, *_: (jnp.minimum(i, n_prompt_tiles - 1), 0))
    s = pl.BlockSpec((rows, cols), lambda i, *_: (jnp.maximum(i - n_prompt_tiles, 0), 0))
    return p, s


def _disc_kernel(lr_ref, li_ref, ldt_ref, br_ref, bi_ref, ar_ref, ai_ref, bbr_ref, bbi_ref):
    lr = lr_ref[...]
    li = li_ref[...]
    dt = jnp.exp(ldt_ref[...])
    mag = jnp.exp(lr * dt)
    ar = mag * jnp.cos(li * dt)
    ai = mag * jnp.sin(li * dt)
    den = lr * lr + li * li
    nr = ar - 1.0
    cr = (nr * lr + ai * li) / den
    ci = (ai * lr - nr * li) / den
    br = br_ref[...]
    bi = bi_ref[...]
    ar_ref[...] = ar
    ai_ref[...] = ai
    bbr_ref[...] = cr * br - ci * bi
    bbi_ref[...] = cr * bi + ci * br


def _discretise(lam_re, lam_im, log_dt, b_re, b_im):
    shp = jax.ShapeDtypeStruct((N_SSM_GROUPS, SSM_STATE * SSM_GROUP), F32)
    rep = lambda a: jnp.repeat(a, SSM_GROUP, axis=-1)
    ldt = jnp.broadcast_to(log_dt[:, None], (N_SSM_GROUPS, SSM_STATE * SSM_GROUP))
    return pl.pallas_call(_disc_kernel, out_shape=(shp, shp, shp, shp), name="s5_discretise")(
        rep(lam_re), rep(lam_im), ldt,
        b_re.reshape(N_SSM_GROUPS, -1), b_im.reshape(N_SSM_GROUPS, -1))


def _block_diag_in(bb):
    b4 = bb.reshape(GROUP_BLOCKS, GROUPS_PER_BLOCK, SSM_STATE, SSM_GROUP).transpose(0, 1, 3, 2)
    eye = jnp.eye(GROUPS_PER_BLOCK, dtype=bool)[None, :, None, :, None]
    full = jnp.where(eye, b4[:, :, :, None, :], 0.0)
    return full.reshape(GROUP_BLOCKS, GB_IN, GB_STATE)


def _block_diag_out(c):
    c4 = c.reshape(GROUP_BLOCKS, GROUPS_PER_BLOCK, SSM_GROUP, SSM_STATE).transpose(0, 1, 3, 2)
    eye = jnp.eye(GROUPS_PER_BLOCK, dtype=bool)[None, :, None, :, None]
    full = jnp.where(eye, c4[:, :, :, None, :], 0.0)
    return full.reshape(GROUP_BLOCKS, GB_STATE, GB_IN)


def _rmsnorm_rows(x, g):
    r = lax.rsqrt(jnp.mean(x * x, axis=-1, keepdims=True) + EPS)
    return x * r * g


def _inproj_kernel(x0_ref, x1_ref, x2_ref, x3_ref, xs_ref, g_ref, wu_ref, wa_ref, wb_ref,
                   u_ref, v_ref, xn_ref):
    i = pl.program_id(0)
    dot = functools.partial(jnp.dot, preferred_element_type=F32)

    def tile(x, n_seq, steps):
        xn = _rmsnorm_rows(x, g_ref[...])
        _to_mixer_order(xn_ref, xn, n_seq, steps)
        xb = xn.astype(BF16)
        _to_mixer_order(u_ref, dot(xb, wu_ref[...]), n_seq, steps)
        v = dot(xb, wa_ref[...]) * jax.nn.sigmoid(dot(xb, wb_ref[...]))
        _to_mixer_order(v_ref, v, n_seq, steps)

    @pl.when(i < N_PROMPT_TILES)
    def _():
        x = jnp.concatenate([x0_ref[...], x1_ref[...], x2_ref[...], x3_ref[...]], axis=0)
        tile(x, N_PROMPT_SEQ, PROMPT_STEPS)

    @pl.when(i >= N_PROMPT_TILES)
    def _():
        tile(xs_ref[...], N_SAMPLE_SEQ, SAMPLE_LEN)


def _inproj(xp, xs, g, w_in_bf):
    blocked = lambda nb: pl.BlockSpec((nb, TM, LANES), lambda i: (0, i, 0))
    wcols = lambda j: pl.BlockSpec((D_MODEL, D_SSM), lambda i: (0, j), pipeline_mode=pl.Buffered(1))
    return pl.pallas_call(
        _inproj_kernel,
        grid=(N_TILES,),
        in_specs=_x_tile_specs() + [_resident((1, D_MODEL)), wcols(0), wcols(1), wcols(2)],
        out_specs=[blocked(SSM_LB), blocked(CONV_LB), blocked(MODEL_LB)],
        out_shape=(jax.ShapeDtypeStruct((SSM_LB, N_TOK, LANES), F32),
                   jax.ShapeDtypeStruct((CONV_LB, N_TOK, LANES), F32),
                   jax.ShapeDtypeStruct((MODEL_LB, N_TOK, LANES), F32)),
        compiler_params=_cparams(("arbitrary",), 56),
        name="in_proj",
    )(xp, xp, xp, xp, xs, g, w_in_bf, w_in_bf, w_in_bf)


N_SCAN_TILES = GROUP_BLOCKS * N_PROMPT_TILES
SCAN_STAGES = 3


def _scan_prompt_kernel(ua_ref, uc_ref, bm_ref, cm_ref, ar_ref, ai_ref, dsk_ref,
                        act_ref, pfr_ref, pfi_ref, x0, x1, x2, cst, car, fin):
    n = pl.program_id(0)
    q_s = jnp.clip(n - 1, 0, N_SCAN_TILES - 1)
    t_s = q_s % N_PROMPT_TILES
    valid_s = jnp.logical_and(n >= 1, n <= N_SCAN_TILES)

    @pl.when(n == 0)
    def _():
        for buf in (x0, x1, x2):
            buf[...] = jnp.zeros_like(buf)
        car[...] = jnp.zeros_like(car)
        fin[...] = jnp.zeros_like(fin)

    shape = (SUBLANES, GB_STATE)
    top = lax.broadcasted_iota(I32, shape, 0) < N_PROMPT_SEQ
    arb = jnp.broadcast_to(ar_ref[0], shape)
    aib = jnp.broadcast_to(ai_ref[0], shape)
    cst[0] = jnp.where(top, 0.0, arb)
    cst[1] = jnp.where(top, 0.0, aib)
    cst[2] = jnp.where(top, arb, arb * arb - aib * aib)
    cst[3] = jnp.where(top, aib, 2.0 * arb * aib)

    def project_in(xa):
        u = jnp.concatenate([ua_ref[0], ua_ref[1]], axis=1).astype(BF16)
        x = jnp.dot(u, bm_ref[0], preferred_element_type=F32)
        for c in range(2 * GB_LB):
            xa[c] = x[:, c * LANES:(c + 1) * LANES]

    def project_out(xc):
        u = jnp.concatenate([uc_ref[0], uc_ref[1]], axis=1)
        h = _lane_blocks(xc).astype(BF16)
        y = jnp.dot(h, cm_ref[0], preferred_element_type=F32) + dsk_ref[0] * u
        act_ref[...] = jax.nn.gelu(y).astype(BF16)

    def recur(xs):
        top1 = lax.broadcasted_iota(I32, (SUBLANES, LANES), 0) < N_PROMPT_SEQ
        first = t_s == 0
        hr = [jnp.where(first, 0.0, car[:, c * LANES:(c + 1) * LANES]) for c in range(GB_LB)]
        hi = [jnp.where(first, 0.0, car[:, GB_STATE + c * LANES:GB_STATE + (c + 1) * LANES])
              for c in range(GB_LB)]
        for r in range(TM // SUBLANES):
            rows = slice(r * SUBLANES, (r + 1) * SUBLANES)
            for c in range(GB_LB):
                cols = slice(c * LANES, (c + 1) * LANES)
                xr = xs[c, rows, :]
                xi = xs[GB_LB + c, rows, :]
                a1r, a1i, a2r, a2i = cst[0, :, cols], cst[1, :, cols], cst[2, :, cols], cst[3, :, cols]
                sxr = pltpu.roll(xr, N_PROMPT_SEQ, axis=0)
                sxi = pltpu.roll(xi, N_PROMPT_SEQ, axis=0)
                tr = xr + a1r * sxr - a1i * sxi
                ti = xi + a1r * sxi + a1i * sxr
                nr = tr + a2r * hr[c] - a2i * hi[c]
                ni = ti + a2r * hi[c] + a2i * hr[c]
                xs[c, rows, :] = nr
                xs[GB_LB + c, rows, :] = ni
                hr[c] = jnp.where(top1, pltpu.roll(nr, N_PROMPT_SEQ, axis=0), nr)
                hi[c] = jnp.where(top1, pltpu.roll(ni, N_PROMPT_SEQ, axis=0), ni)
        state = jnp.concatenate(hr + hi, axis=1)
        car[...] = state
        last = jnp.logical_and(valid_s, t_s == N_PROMPT_TILES - 1)
        fin[...] = jnp.where(last, state, fin[...])
        pfr_ref[0] = fin[N_PROMPT_SEQ:SUBLANES, 0:GB_STATE]
        pfi_ref[0] = fin[N_PROMPT_SEQ:SUBLANES, GB_STATE:2 * GB_STATE]

    bufs = (x0, x1, x2)
    for r in range(SCAN_STAGES):
        @pl.when(n % SCAN_STAGES == r)
        def _():
            project_out(bufs[(r + 1) % SCAN_STAGES])
            recur(bufs[(r + 2) % SCAN_STAGES])
            project_in(bufs[r])


def _scan_prompt(u3, bm, cm, ar, ai, dsk):
    q_a = lambda n: jnp.minimum(n, N_SCAN_TILES - 1)
    q_s = lambda n: jnp.clip(n - 1, 0, N_SCAN_TILES - 1)
    q_c = lambda n: jnp.clip(n - 2, 0, N_SCAN_TILES - 1)
    gb = lambda q: q // N_PROMPT_TILES
    tt = lambda q: q % N_PROMPT_TILES
    u_spec = lambda q: pl.BlockSpec((GB_IN // LANES, TM, LANES), lambda n: (gb(q(n)), tt(q(n)), 0))
    gb3 = lambda last, q: pl.BlockSpec((1,) + last, lambda n: (gb(q(n)), 0, 0))
    state = jax.ShapeDtypeStruct((GROUP_BLOCKS, N_PROMPT_SEQ, GB_STATE), F32)
    xbuf = pltpu.VMEM((2 * GB_LB, TM, LANES), F32)
    return pl.pallas_call(
        _scan_prompt_kernel,
        grid=(N_SCAN_TILES + SCAN_STAGES - 1,),
        in_specs=[
            u_spec(q_a), u_spec(q_c),
            gb3((GB_IN, 2 * GB_STATE), q_a),
            gb3((2 * GB_STATE, GB_IN), q_c),
            gb3((1, GB_STATE), q_s),
            gb3((1, GB_STATE), q_s),
            gb3((1, GB_IN), q_c),
        ],
        out_specs=[
            pl.BlockSpec((TM, GB_IN), lambda n: (tt(q_c(n)), gb(q_c(n)))),
            gb3((N_PROMPT_SEQ, GB_STATE), q_s),
            gb3((N_PROMPT_SEQ, GB_STATE), q_s),
        ],
        out_shape=(jax.ShapeDtypeStruct((N_PROMPT_TOK, D_SSM), BF16), state, state),
        scratch_shapes=[xbuf, xbuf, xbuf,
                        pltpu.VMEM((4, SUBLANES, GB_STATE), F32),
                        pltpu.VMEM((SUBLANES, 2 * GB_STATE), F32),
                        pltpu.VMEM((SUBLANES, 2 * GB_STATE), F32)],
        compiler_params=_cparams(("arbitrary",), 48),
        name="s5_scan_prompt",
    )(u3, u3, bm, cm, ar, ai, dsk)


def _scan_sample_kernel(u_ref, bm_ref, cm_ref, ar_ref, ai_ref, dsk_ref, s0r_ref, s0i_ref,
                        act_ref, sfr_ref, sfi_ref, xs_ref):
    half_lb = GB_LB // 2
    half_cols = half_lb * LANES
    u = jnp.concatenate([u_ref[0], u_ref[1]], axis=1)
    x = jnp.dot(u.astype(BF16), bm_ref[0], preferred_element_type=F32)
    for c in range(2 * GB_LB):
        xs_ref[c] = x[:, c * LANES:(c + 1) * LANES]

    def load_half(rows, half, imag):
        c0 = imag * GB_LB + half * half_lb
        return jnp.concatenate([xs_ref[c0 + c, rows, :] for c in range(half_lb)], axis=1)

    def store_half(rows, half, imag, val):
        c0 = imag * GB_LB + half * half_lb
        for c in range(half_lb):
            xs_ref[c0 + c, rows, :] = val[:, c * LANES:(c + 1) * LANES]

    shape = (N_SAMPLE_SEQ, half_cols)
    for half in range(2):
        cols = slice(half * half_cols, (half + 1) * half_cols)
        arb = jnp.broadcast_to(ar_ref[0][:, cols], shape)
        aib = jnp.broadcast_to(ai_ref[0][:, cols], shape)

        def body(k, carry):
            hr, hi = carry
            rows = pl.ds(pl.multiple_of(k * N_SAMPLE_SEQ, N_SAMPLE_SEQ), N_SAMPLE_SEQ)
            xr = load_half(rows, half, 0)
            xi = load_half(rows, half, 1)
            nhr = xr + arb * hr - aib * hi
            nhi = xi + arb * hi + aib * hr
            store_half(rows, half, 0, nhr)
            store_half(rows, half, 1, nhi)
            return nhr, nhi

        hr, hi = lax.fori_loop(0, SAMPLE_LEN, body, (s0r_ref[:, cols], s0i_ref[:, cols]))
        sfr_ref[:, cols] = hr
        sfi_ref[:, cols] = hi

    h = _lane_blocks(xs_ref).astype(BF16)
    y = jnp.dot(h, cm_ref[0], preferred_element_type=F32) + dsk_ref[0] * u
    act_ref[...] = jax.nn.gelu(y).astype(BF16)


def _scan_sample(u3, bm, cm, ar, ai, dsk, s0r, s0i):
    gb3 = lambda last: pl.BlockSpec((1,) + last, lambda g: (g, 0, 0))
    state = lambda: pl.BlockSpec((N_SAMPLE_SEQ, GB_STATE), lambda g: (0, g))
    sst = jax.ShapeDtypeStruct((N_SAMPLE_SEQ, N_SSM_GROUPS * SSM_STATE), F32)
    return pl.pallas_call(
        _scan_sample_kernel,
        grid=(GROUP_BLOCKS,),
        in_specs=[
            pl.BlockSpec((GB_IN // LANES, TM, LANES), lambda g: (g, N_PROMPT_TILES, 0)),
            gb3((GB_IN, 2 * GB_STATE)), gb3((2 * GB_STATE, GB_IN)),
            gb3((1, GB_STATE)), gb3((1, GB_STATE)), gb3((1, GB_IN)),
            state(), state(),
        ],
        out_specs=[pl.BlockSpec((TM, GB_IN), lambda g: (0, g)), state(), state()],
        out_shape=(jax.ShapeDtypeStruct((N_SAMPLE_TOK, D_SSM), BF16), sst, sst),
        scratch_shapes=[pltpu.VMEM((2 * GB_LB, TM, LANES), F32)],
        compiler_params=_cparams(("arbitrary",), 32),
        name="s5_scan_sample",
    )(u3, bm, cm, ar, ai, dsk, s0r, s0i)


CONV_ROWS = 64
EXT_ROWS = HIST_S + TM


def _conv_kernel(v_ref, hist_ref, w_ref, b_ref, lg_ref, lb_ref,
                 cv_ref, ncp_ref, ncs_ref, ext_ref, sh_ref, acc_ref):
    t = pl.program_id(0)

    def taps(n_seq):
        half = SUBLANES // 2
        if n_seq % SUBLANES:
            lo = HIST_S - HIST_P - SUBLANES
            sh_ref[:, lo:EXT_ROWS - SUBLANES, :] = ext_ref[:, lo + half:EXT_ROWS - half, :]

        def body(rb, _):
            r0 = pl.multiple_of(rb * CONV_ROWS, CONV_ROWS)
            for cb in range(CONV_LB):
                cols = slice(cb * LANES, (cb + 1) * LANES)
                acc = None
                for j in range(CONV_WIDTH):
                    start = HIST_S - (CONV_BUF - j) * n_seq
                    src = ext_ref
                    if start % SUBLANES:
                        src, start = sh_ref, start - half
                    rows = pl.ds(pl.multiple_of(r0 + start, SUBLANES), CONV_ROWS)
                    term = src[cb, rows, :] * w_ref[j:j + 1, cols]
                    acc = term if acc is None else acc + term
                acc_ref[cb, pl.ds(r0, CONV_ROWS), :] = acc + b_ref[:, cols]
            return 0

        lax.fori_loop(0, TM // CONV_ROWS, body, 0)

    @pl.when(t == 0)
    def _():
        ext_ref[:, 0:HIST_S, :] = jnp.zeros((CONV_LB, HIST_S, LANES), F32)

    @pl.when(t == N_PROMPT_TILES)
    def _():
        for c in range(CONV_LB):
            ext_ref[c, 0:HIST_S, :] = hist_ref[:, c * LANES:(c + 1) * LANES]

    ext_ref[:, HIST_S:EXT_ROWS, :] = v_ref[...]

    @pl.when(t < N_PROMPT_TILES)
    def _():
        taps(N_PROMPT_SEQ)

    @pl.when(t == N_PROMPT_TILES)
    def _():
        taps(N_SAMPLE_SEQ)

    y = acc_ref[...]
    mu = jnp.sum(jnp.sum(y, axis=0), axis=-1, keepdims=True) / D_CONV
    yc = y - mu[None]
    var = jnp.sum(jnp.sum(yc * yc, axis=0), axis=-1, keepdims=True) / D_CONV
    z = jax.nn.silu(yc * lax.rsqrt(var + EPS)[None] * lg_ref[...] + lb_ref[...])
    for c in range(CONV_LB):
        cv_ref[:, c * LANES:(c + 1) * LANES] = z[c].astype(BF16)

    @pl.when(t == N_PROMPT_TILES - 1)
    def _():
        for c in range(CONV_LB):
            ncp_ref[:, c * LANES:(c + 1) * LANES] = ext_ref[c, EXT_ROWS - HIST_P:EXT_ROWS, :]

    @pl.when(t < N_PROMPT_TILES - 1)
    def _():
        ext_ref[:, HIST_S - HIST_P:HIST_S, :] = ext_ref[:, EXT_ROWS - HIST_P:EXT_ROWS, :]

    @pl.when(t == N_PROMPT_TILES)
    def _():
        for c in range(CONV_LB):
            ncs_ref[:, c * LANES:(c + 1) * LANES] = ext_ref[c, EXT_ROWS - HIST_S:EXT_ROWS, :]


def _conv(v3, hist_s, conv_w, conv_b, ln_g, ln_b):
    row = lambda n: pl.BlockSpec((n, D_CONV), lambda t: (0, 0))
    lane3 = pl.BlockSpec((CONV_LB, 1, LANES), lambda t: (0, 0, 0))
    ext = pltpu.VMEM((CONV_LB, EXT_ROWS, LANES), F32)
    return pl.pallas_call(
        _conv_kernel,
        grid=(N_TILES,),
        in_specs=[pl.BlockSpec((CONV_LB, TM, LANES), lambda t: (0, t, 0)),
                  row(HIST_S), row(CONV_WIDTH), row(1), lane3, lane3],
        out_specs=[pl.BlockSpec((TM, D_CONV), lambda t: (t, 0)), row(HIST_P), row(HIST_S)],
        out_shape=(jax.ShapeDtypeStruct((N_TOK, D_CONV), BF16),
                   jax.ShapeDtypeStruct((HIST_P, D_CONV), F32),
                   jax.ShapeDtypeStruct((HIST_S, D_CONV), F32)),
        scratch_shapes=[ext, ext, pltpu.VMEM((CONV_LB, TM, LANES), F32)],
        compiler_params=_cparams(("arbitrary",), 48),
        name="conv_branch",
    )(v3, hist_s, conv_w, conv_b,
      ln_g.reshape(CONV_LB, 1, LANES), ln_b.reshape(CONV_LB, 1, LANES))


MERGE_COLS = 512
N_MERGE = D_MODEL // MERGE_COLS


def _merge_kernel(xn_ref, actp_ref, acts_ref, cv_ref, wga0_ref, wga1_ref, wgb0_ref, wgb1_ref,
                  wglu_ref, wco_ref, m_ref):
    i = pl.program_id(0)
    dot = functools.partial(jnp.dot, preferred_element_type=F32)
    xn = _lane_blocks(xn_ref).astype(BF16)
    act = jnp.where(i < N_PROMPT_TILES, actp_ref[...], acts_ref[...])
    cv = cv_ref[...]
    per_block = D_SSM // MERGE_COLS
    for c in range(N_MERGE):
        cols = pl.ds(c * MERGE_COLS, MERGE_COLS)
        in_block = pl.ds((c % per_block) * MERGE_COLS, MERGE_COLS)
        wga = (wga0_ref, wga1_ref)[c // per_block]
        wgb = (wgb0_ref, wgb1_ref)[c // per_block]
        ya = dot(act, wglu_ref[:, cols]) * jax.nn.sigmoid(
            dot(act, wglu_ref[:, pl.ds(D_MODEL + c * MERGE_COLS, MERGE_COLS)]))
        yb = dot(cv, wco_ref[:, cols])
        m = (jax.nn.sigmoid(dot(xn, wga[:, in_block])) * ya
             + jax.nn.sigmoid(dot(xn, wgb[:, in_block])) * yb)
        m_ref[:, cols] = m.astype(BF16)


def _merge(xn3, actp, acts, cv, w_in_bf, w_glu_bf, w_co_bf):
    ga0 = (D_SSM + 2 * D_CONV) // D_SSM
    wcols = lambda j: pl.BlockSpec((D_MODEL, D_SSM), lambda i: (0, j), pipeline_mode=pl.Buffered(1))
    return pl.pallas_call(
        _merge_kernel,
        grid=(N_TILES,),
        in_specs=[pl.BlockSpec((MODEL_LB, TM, LANES), lambda i: (0, i, 0)),
                  pl.BlockSpec((TM, D_SSM), lambda i: (jnp.minimum(i, N_PROMPT_TILES - 1), 0)),
                  _resident((N_SAMPLE_TOK, D_SSM)),
                  pl.BlockSpec((TM, D_CONV), lambda i: (i, 0)),
                  wcols(ga0), wcols(ga0 + 1), wcols(ga0 + 2), wcols(ga0 + 3),
                  _resident((D_SSM, 2 * D_MODEL)), _resident((D_CONV, D_MODEL))],
        out_specs=pl.BlockSpec((TM, D_MODEL), lambda i: (i, 0)),
        out_shape=jax.ShapeDtypeStruct((N_TOK, D_MODEL), BF16),
        compiler_params=_cparams(("arbitrary",), 58),
        name="gated_merge",
    )(xn3, actp, acts, cv, w_in_bf, w_in_bf, w_in_bf, w_in_bf, w_glu_bf, w_co_bf)


def _outproj_kernel(m_ref, x0_ref, x1_ref, x2_ref, x3_ref, xs_ref, wo_ref, g_ref, wr_ref, br_ref,
                    h_ref, xl_ref, route_ref, gw_ref, cnt_ref, xr_s, lg0, lg1, xb0, xb1):
    i = pl.program_id(0)
    tile = jnp.minimum(i, N_OT - 1)
    steps_p = TO // N_PROMPT_SEQ
    steps_s = TO // N_SAMPLE_SEQ

    @pl.when(i == 0)
    def _():
        lg1[...] = jnp.zeros_like(lg1)
        xb1[...] = jnp.zeros_like(xb1)

    @pl.when(tile < N_OT_PROMPT)
    def _():
        x = jnp.concatenate([x0_ref[...], x1_ref[...], x2_ref[...], x3_ref[...]], axis=0)
        _to_mixer_order(xr_s, x, N_PROMPT_SEQ, steps_p)

    @pl.when(tile >= N_OT_PROMPT)
    def _():
        k0 = (tile - N_OT_PROMPT) * steps_s
        x = jnp.concatenate(
            [xs_ref[pl.ds(pl.multiple_of(b * SAMPLE_LEN + k0, steps_s), steps_s), :]
             for b in range(N_SAMPLE_SEQ)], axis=0)
        _to_mixer_order(xr_s, x, N_SAMPLE_SEQ, steps_s)

    sets = ((lg0, xb0), (lg1, xb1))
    for r in range(2):
        @pl.when(i % 2 == r)
        def _():
            lg_prev, xb_prev = sets[1 - r]
            _route_and_sort(lg_prev[...], xb_prev[...], xl_ref, route_ref, gw_ref, cnt_ref)
            _project(m_ref, xr_s, wo_ref, g_ref, wr_ref, br_ref, h_ref, *sets[r])


def _project(m_ref, xr_s, wo_ref, g_ref, wr_ref, br_ref, h_ref, lg_s, xb_s):
    dot = functools.partial(jnp.dot, preferred_element_type=F32)
    h = _lane_blocks(xr_s) + dot(m_ref[...], wo_ref[...])
    h_ref[...] = h
    xn = _rmsnorm_rows(h, g_ref[...])
    x_hi = xn.astype(BF16)
    x_lo = (xn - x_hi.astype(F32)).astype(BF16)
    w = wr_ref[...]
    w_hi = w.astype(BF16)
    w_lo = (w - w_hi.astype(F32)).astype(BF16)
    both = dot(x_hi, jnp.concatenate([w_hi, w_lo], axis=1))
    lg_s[...] = (both[:, :ROUTE_LANES] + (dot(x_lo, w_hi) + both[:, ROUTE_LANES:])) + br_ref[...]
    xb_s[...] = x_hi


def _route_and_sort(logits, x_hi, xl_ref, route_ref, gw_ref, cnt_ref):
    dot = functools.partial(jnp.dot, preferred_element_type=F32)
    lane = lax.broadcasted_iota(I32, logits.shape, 1)
    neg = -jnp.inf
    first = lambda hit: jnp.min(jnp.where(hit, lane, ROUTE_LANES), axis=-1, keepdims=True)
    gmask = lane < N_EXP_GROUPS
    lg = jnp.where(gmask, logits, neg)
    gmax = jnp.max(lg, axis=-1, keepdims=True)
    gsel = first(lg == gmax)
    psum = jnp.sum(jnp.where(gmask, jnp.exp(logits - gmax), 0.0), axis=-1, keepdims=True)
    pg_sel = 1.0 / psum
    e_lane = lane - N_EXP_GROUPS
    emask = (e_lane >= 0) & (e_lane < N_EXPERTS) & ((e_lane // EXP_PER_GROUP) == gsel)
    le = jnp.where(emask, logits, neg)
    v1 = jnp.max(le, axis=-1, keepdims=True)
    i1 = first(le == v1)
    le2 = jnp.where(lane == i1, neg, le)
    v2 = jnp.max(le2, axis=-1, keepdims=True)
    i2 = first(le2 == v2)
    z = jnp.exp(v2 - v1)
    w1 = pg_sel / (1.0 + z)
    w2 = pg_sel * z / (1.0 + z)
    e1 = i1 - N_EXP_GROUPS
    e2 = i2 - N_EXP_GROUPS
    gw_ref[...] = jnp.where(lane == 0, w1, jnp.where(lane == 1, w2, 0.0))
    oh1 = lane == e1
    oh2 = lane == e2
    hits = jnp.where(oh1 | oh2, 1.0, 0.0)
    rr = lax.broadcasted_iota(I32, (TO, TO), 0)
    cc = lax.broadcasted_iota(I32, (TO, TO), 1)
    before = dot(jnp.where(cc < rr, 1.0, 0.0).astype(BF16), hits.astype(BF16))
    counts = jnp.sum(hits, axis=0, keepdims=True)
    groups = jnp.floor((counts + (RUN - 1.0)) * (1.0 / RUN))
    er = lax.broadcasted_iota(I32, (ROUTE_LANES, ROUTE_LANES), 0)
    ec = lax.broadcasted_iota(I32, (ROUTE_LANES, ROUTE_LANES), 1)
    groups_before = dot(jnp.broadcast_to(groups, (SUBLANES, ROUTE_LANES)).astype(BF16),
                        jnp.where(er < ec, 1.0, 0.0).astype(BF16))[0:1]
    local = groups_before * RUN + before
    lpos1 = jnp.sum(jnp.where(oh1, local, 0.0), axis=-1, keepdims=True)
    lpos2 = jnp.sum(jnp.where(oh2, local, 0.0), axis=-1, keepdims=True)
    route_ref[...] = jnp.where(lane == 0, e1, jnp.where(lane == 1, e2, jnp.where(
        lane == 2, lpos1.astype(I32), jnp.where(lane == 3, lpos2.astype(I32), 0))))
    cnt_ref[0] = counts.astype(I32)
    lpos_t = jnp.where(lane == 0, lpos1, jnp.where(lane == 1, lpos2, 0.0)).T.astype(I32)
    jrow = lax.broadcasted_iota(I32, (LROWS, TO), 0)
    sel = (jrow == lpos_t[0:1, :]) | (jrow == lpos_t[1:2, :])
    xl_ref[...] = dot(jnp.where(sel, 1.0, 0.0).astype(BF16), x_hi).astype(BF16)


def _outproj(m, xp, xs, w_out_bf, g, w_router, b_router):
    cur = lambda i: jnp.minimum(i, N_OT - 1)
    prev = lambda i: jnp.maximum(i - 1, 0)
    tile = lambda n: pl.BlockSpec((TO, n), lambda i: (cur(i), 0))
    routed = lambda n: pl.BlockSpec((TO, n), lambda i: (prev(i), 0))
    return pl.pallas_call(
        _outproj_kernel,
        grid=(N_OT + 1,),
        in_specs=[tile(D_MODEL)] + _x_tile_specs(TO // N_PROMPT_SEQ) + [
            _resident((D_MODEL, D_MODEL)), _resident((1, D_MODEL)),
            _resident((D_MODEL, ROUTE_LANES)), _resident((1, ROUTE_LANES))],
        out_specs=[tile(D_MODEL),
                   pl.BlockSpec((LROWS, D_MODEL), lambda i: (prev(i), 0)),
                   routed(ROUTE_LANES), routed(ROUTE_LANES),
                   pl.BlockSpec((1, 1, ROUTE_LANES), lambda i: (prev(i), 0, 0))],
        out_shape=(jax.ShapeDtypeStruct((N_TOK, D_MODEL), F32),
                   jax.ShapeDtypeStruct((N_OT * LROWS, D_MODEL), BF16),
                   jax.ShapeDtypeStruct((N_TOK, ROUTE_LANES), I32),
                   jax.ShapeDtypeStruct((N_TOK, ROUTE_LANES), F32),
                   jax.ShapeDtypeStruct((N_OT, 1, ROUTE_LANES), I32)),
        scratch_shapes=[pltpu.VMEM((MODEL_LB, TO, LANES), F32),
                        pltpu.VMEM((TO, ROUTE_LANES), F32), pltpu.VMEM((TO, ROUTE_LANES), F32),
                        pltpu.VMEM((TO, D_MODEL), BF16), pltpu.VMEM((TO, D_MODEL), BF16)],
        compiler_params=_cparams(("arbitrary",), 56),
        name="out_proj_router",
    )(m, xp, xp, xp, xp, xs, w_out_bf, g, w_router, b_router)


def _expert_kernel(te_ref, nrows_ref, nused_ref, psrc_ref, pdst_ref, plen_ref, tot_ref, kfirst_ref, kend_ref,
                   first_ref, par_ref, nexte_ref,
                   xl_hbm, wg_hbm, wu_hbm, wd_hbm, ol_hbm,
                   xbuf, obuf, zbuf, wg_s, wu_s, wd_s, wg_st, wu_st, wd_st, gsem, wsem, zsem, esem):
    i = pl.program_id(0)
    nused = nused_ref[0]
    slot = i % 2
    rows8 = lambda v: pl.multiple_of(v, RUN)

    def for_pieces(tile, fn):
        def body(k, _):
            idx = tile * N_OT + k
            n = plen_ref[idx]

            @pl.when(n > 0)
            def _():
                fn(rows8(psrc_ref[idx]), rows8(pdst_ref[idx]), rows8(n))
            return 0
        lax.fori_loop(kfirst_ref[tile], kend_ref[tile], body, 0)

    def gather(tile, s):
        for_pieces(tile, lambda src, dst, n: pltpu.make_async_copy(
            xl_hbm.at[pl.ds(src, n)], xbuf.at[s, pl.ds(dst, n)], gsem.at[s]).start())

    def scatter(tile, s):
        for_pieces(tile, lambda src, dst, n: pltpu.make_async_copy(
            obuf.at[s, pl.ds(dst, n)], ol_hbm.at[pl.ds(src, n)], wsem.at[s]).start())

    def wait_gather(tile, s):
        n = rows8(nrows_ref[tile])
        pltpu.make_async_copy(xl_hbm.at[pl.ds(0, n)], xbuf.at[s, pl.ds(0, n)], gsem.at[s]).wait()

    def wait_scatter(tile, s):
        n = rows8(nrows_ref[tile])
        pltpu.make_async_copy(obuf.at[s, pl.ds(0, n)], ol_hbm.at[pl.ds(0, n)], wsem.at[s]).wait()

    def tail_copy(t):
        used = rows8(tot_ref[t])
        n = rows8(LROWS - tot_ref[t])
        dst = rows8(t * LROWS + used)
        return pltpu.make_async_copy(zbuf.at[pl.ds(0, n)], ol_hbm.at[pl.ds(dst, n)], zsem.at[0])

    def weight_copies(e, b):
        return [pltpu.make_async_copy(src.at[e], dst.at[b], esem.at[b, k])
                for k, (src, dst) in enumerate(((wg_hbm, wg_st), (wu_hbm, wu_st), (wd_hbm, wd_st)))]

    @pl.when(i == 0)
    def _():
        for c in weight_copies(te_ref[0], par_ref[0]):
            c.start()
        zbuf[...] = jnp.zeros_like(zbuf)
        xbuf[...] = jnp.zeros_like(xbuf)

        def start(t, _):
            tail_copy(t).start()
            return 0

        def wait(t, _):
            tail_copy(t).wait()
            return 0
        lax.fori_loop(0, N_OT, start, 0)
        lax.fori_loop(0, N_OT, wait, 0)
        gather(0, 0)

    @pl.when(i + 1 < nused)
    def _():
        gather(i + 1, 1 - slot)

    @pl.when(i < nused)
    def _():
        @pl.when(first_ref[i] == 1)
        def _():
            b = par_ref[i]
            for c in weight_copies(te_ref[i], b):
                c.wait()
            wg_s[...] = wg_st[b].astype(BF16)
            wu_s[...] = wu_st[b].astype(BF16)
            wd_s[...] = wd_st[b].astype(BF16)

            @pl.when(nexte_ref[i] >= 0)
            def _():
                for c in weight_copies(nexte_ref[i], 1 - b):
                    c.start()

        wait_gather(i, slot)

        @pl.when(i >= 2)
        def _():
            wait_scatter(i - 2, slot)

        xb = xbuf[slot]
        hg = jnp.dot(xb, wg_s[...], preferred_element_type=F32)
        hu = jnp.dot(xb, wu_s[...], preferred_element_type=F32)
        hid = jax.nn.silu(hg) * hu
        obuf[slot] = jnp.dot(hid.astype(BF16), wd_s[...], preferred_element_type=F32).astype(BF16)
        scatter(i, slot)

        @pl.when(i == nused - 1)
        def _():
            @pl.when(i >= 1)
            def _():
                wait_scatter(i - 1, 1 - slot)
            wait_scatter(i, slot)


def _experts(plan, xl, wg, wu, wd):
    anyspec = pl.BlockSpec(memory_space=pl.ANY)
    grid_spec = pltpu.PrefetchScalarGridSpec(
        num_scalar_prefetch=len(plan),
        grid=(N_ETILES,),
        in_specs=[anyspec, anyspec, anyspec, anyspec],
        out_specs=anyspec,
        scratch_shapes=[
            pltpu.VMEM((2, TME, D_MODEL), BF16),
            pltpu.VMEM((2, TME, D_MODEL), BF16),
            pltpu.VMEM((LROWS - 2 * TO, D_MODEL), BF16),
            pltpu.VMEM((D_MODEL, D_EXPERT), BF16),
            pltpu.VMEM((D_MODEL, D_EXPERT), BF16),
            pltpu.VMEM((D_EXPERT, D_MODEL), BF16),
            pltpu.VMEM((2, D_MODEL, D_EXPERT), F32),
            pltpu.VMEM((2, D_MODEL, D_EXPERT), F32),
            pltpu.VMEM((2, D_EXPERT, D_MODEL), F32),
            pltpu.SemaphoreType.DMA((2,)),
            pltpu.SemaphoreType.DMA((2,)),
            pltpu.SemaphoreType.DMA((1,)),
            pltpu.SemaphoreType.DMA((2, 3)),
        ],
    )
    return pl.pallas_call(
        _expert_kernel,
        grid_spec=grid_spec,
        out_shape=jax.ShapeDtypeStruct((N_OT * LROWS, D_MODEL), BF16),
        compiler_params=_cparams(("arbitrary",), 40),
        name="routed_experts",
    )(*plan, xl, wg, wu, wd)


def _combine_kernel(h_ref, route_ref, gw_ref, ol_ref, g_ref, yp_ref, ys_ref, y_s):
    i = pl.program_id(0)
    steps_p = TO // N_PROMPT_SEQ
    steps_s = TO // N_SAMPLE_SEQ
    route = route_ref[...]
    gw = gw_ref[...]
    jcol = lax.broadcasted_iota(I32, (TO, LROWS), 1)
    mix = (jnp.where(jcol == route[:, 2:3], gw[:, 0:1], 0.0)
           + jnp.where(jcol == route[:, 3:4], gw[:, 1:2], 0.0))
    moe = jnp.dot(mix.astype(BF16), ol_ref[...], preferred_element_type=F32)
    y = _rmsnorm_rows(h_ref[...] + moe, g_ref[...])
    for c in range(MODEL_LB):
        y_s[c] = y[:, c * LANES:(c + 1) * LANES]

    @pl.when(i < N_OT_PROMPT)
    def _():
        for b in range(N_PROMPT_SEQ):
            yp_ref[b] = _seq_rows(y_s, b, N_PROMPT_SEQ, steps_p)

    @pl.when(i >= N_OT_PROMPT)
    def _():
        k0 = (i - N_OT_PROMPT) * steps_s
        for b in range(N_SAMPLE_SEQ):
            rows = pl.ds(pl.multiple_of(b * SAMPLE_LEN + k0, steps_s), steps_s)
            ys_ref[rows, :] = _seq_rows(y_s, b, N_SAMPLE_SEQ, steps_s)


def _combine(h, route, gw, ol, g):
    tile = lambda n: pl.BlockSpec((TO, n), lambda i: (i, 0))
    return pl.pallas_call(
        _combine_kernel,
        grid=(N_OT,),
        in_specs=[tile(D_MODEL), tile(ROUTE_LANES), tile(ROUTE_LANES),
                  pl.BlockSpec((LROWS, D_MODEL), lambda i: (i, 0)),
                  pl.BlockSpec((1, D_MODEL), lambda i: (0, 0))],
        out_specs=[
            pl.BlockSpec((N_PROMPT_SEQ, TO // N_PROMPT_SEQ, D_MODEL),
                         lambda i: (0, jnp.minimum(i, N_OT_PROMPT - 1), 0)),
            pl.BlockSpec((N_SAMPLE_TOK, D_MODEL), lambda i: (0, 0)),
        ],
        out_shape=(jax.ShapeDtypeStruct((N_PROMPT_SEQ, PROMPT_LEN, D_MODEL), F32),
                   jax.ShapeDtypeStruct((N_SAMPLE_TOK, D_MODEL), F32)),
        scratch_shapes=[pltpu.VMEM((MODEL_LB, TO, LANES), F32)],
        compiler_params=_cparams(("arbitrary",), 48),
        name="combine_norm",
    )(h, route, gw, ol, g)


def _dispatch_plan(cnt):
    experts = jnp.arange(N_EXPERTS, dtype=I32)
    cnt = cnt[:, 0, :N_EXPERTS]
    run = ((cnt + RUN - 1) // RUN) * RUN
    local_start = jnp.cumsum(run, axis=1) - run
    local_used = jnp.sum(run, axis=1)
    total = jnp.sum(run, axis=0)
    padded = ((total + TME - 1) // TME) * TME
    ends = jnp.cumsum(padded)
    starts = ends - padded
    run_start = starts[None, :] + jnp.cumsum(run, axis=0) - run
    n_used = (ends[-1] // TME).astype(I32)
    tile = jnp.arange(N_ETILES, dtype=I32)
    tile_start = tile * TME
    te = jnp.minimum(jnp.sum((ends[None, :] <= tile_start[:, None]).astype(I32), axis=1), N_EXPERTS - 1)
    mine = (te[:, None] == experts[None, :])[:, None, :]
    of_tile = lambda a: jnp.sum(jnp.where(mine, a[None], 0), axis=-1)
    s0 = of_tile(run_start)
    lo = jnp.maximum(s0, tile_start[:, None])
    hi = jnp.minimum(s0 + of_tile(run), tile_start[:, None] + TME)
    plen = jnp.where((tile < n_used)[:, None], jnp.maximum(hi - lo, 0), 0)
    live = plen > 0
    psrc = jnp.where(live, jnp.arange(N_OT, dtype=I32)[None, :] * LROWS + of_tile(local_start) + lo - s0, 0)
    pdst = jnp.where(live, lo - tile_start[:, None], 0)
    last = jnp.sum(jnp.where(tile == n_used - 1, te, 0))
    te = jnp.where(tile < n_used, te, last)
    flat = lambda a: a.reshape(-1).astype(I32)
    none_yet = lambda a: jnp.sum((jnp.cumsum(a.astype(I32), axis=1) == 0).astype(I32), axis=1)
    k_first = none_yet(live)
    k_end = N_OT - none_yet(live[:, ::-1])
    first = jnp.logical_and(tile < n_used, jnp.concatenate([jnp.array([True]), te[1:] != te[:-1]]))
    parity = (jnp.cumsum(first.astype(I32)) - 1) % 2
    next_first = lax.cummin(jnp.where(first, tile, N_ETILES), axis=0, reverse=True)
    next_first = jnp.concatenate([next_first[1:], jnp.array([N_ETILES], I32)])
    next_expert = jnp.where(next_first < N_ETILES, te[jnp.minimum(next_first, N_ETILES - 1)], -1)
    return (te.astype(I32), flat(jnp.sum(plen, axis=1)), n_used.reshape(1),
            flat(psrc), flat(pdst), flat(plen), local_used.astype(I32), flat(k_first), flat(k_end),
            flat(first), flat(parity), flat(next_expert))


def kernel(x_prompt, x_sample, state_ssm_re, state_ssm_im, cache_conv, norm_mix_g, w_in, lam_re, lam_im, log_dt, b_re, b_im, c_re, c_im, d_skip, w_ssm_glu, conv_w, conv_b, conv_ln_g, conv_ln_b, w_conv_out, w_out, norm_ffn_g, w_router_group, b_router_group, w_router_expert, b_router_expert, w_exp_gate, w_exp_up, w_exp_down, norm_final_g):
    assert w_in.shape[0] == 1, "single-layer trunk"
    xp = x_prompt.reshape(N_PROMPT_TOK, D_MODEL)
    xs = x_sample.reshape(N_SAMPLE_TOK, D_MODEL)
    w_in_bf = w_in[0].astype(BF16)
    row = lambda a: a.reshape(1, -1)

    ar_rep, ai_rep, bbr, bbi = _discretise(lam_re[0], lam_im[0], log_dt[0], b_re[0], b_im[0])
    gph = lambda a: a.reshape(N_SSM_GROUPS, SSM_STATE, SSM_GROUP)
    pick = lambda a: gph(a)[:, :, 0].reshape(GROUP_BLOCKS, 1, GB_STATE)
    bm = jnp.concatenate([_block_diag_in(gph(bbr)), _block_diag_in(gph(bbi))], axis=-1).astype(BF16)
    cm = jnp.concatenate([_block_diag_out(c_re[0]), -_block_diag_out(c_im[0])], axis=1).astype(BF16)
    dsk = d_skip[0].reshape(GROUP_BLOCKS, 1, GB_IN)
    ar, ai = pick(ar_rep), pick(ai_rep)
    s0r = state_ssm_re[0].reshape(N_SAMPLE_SEQ, -1)
    s0i = state_ssm_im[0].reshape(N_SAMPLE_SEQ, -1)

    u3, v3, xn3 = _inproj(xp, xs, row(norm_mix_g[0]), w_in_bf)
    actp, pfr, pfi = _scan_prompt(u3, bm, cm, ar, ai, dsk)
    acts, sfr, sfi = _scan_sample(u3, bm, cm, ar, ai, dsk, s0r, s0i)

    hist_s = cache_conv[0].transpose(1, 0, 2).reshape(HIST_S, D_CONV)
    cv, ncp, ncs = _conv(v3, hist_s, conv_w[0], row(conv_b[0]), conv_ln_g[0], conv_ln_b[0])

    m = _merge(xn3, actp, acts, cv, w_in_bf, w_ssm_glu[0].astype(BF16), w_conv_out[0].astype(BF16))

    pad_lanes = ROUTE_LANES - N_EXP_GROUPS - N_EXPERTS
    w_router = jnp.concatenate(
        [w_router_group[0], w_router_expert[0], jnp.zeros((D_MODEL, pad_lanes), F32)], axis=1)
    b_router = jnp.concatenate(
        [b_router_group[0], b_router_expert[0], jnp.zeros((pad_lanes,), F32)]).reshape(1, ROUTE_LANES)
    h, xl, route, gw, cnt = _outproj(m, xp, xs, w_out[0].astype(BF16), row(norm_ffn_g[0]),
                                     w_router, b_router)

    ol = _experts(_dispatch_plan(cnt), xl, w_exp_gate[0], w_exp_up[0], w_exp_down[0])
    yp, ys = _combine(h, route, gw, ol, row(norm_final_g))

    st = lambda a, n: a.reshape(1, n, N_SSM_GROUPS, SSM_STATE)
    pst = lambda a: st(a.transpose(1, 0, 2), N_PROMPT_SEQ)
    ncp = ncp.reshape(CONV_BUF, N_PROMPT_SEQ, D_CONV).transpose(1, 0, 2)[None]
    ncs = ncs.reshape(CONV_BUF, N_SAMPLE_SEQ, D_CONV).transpose(1, 0, 2)[None]
    return (yp, ys.reshape(N_SAMPLE_SEQ, SAMPLE_LEN, D_MODEL), pst(pfr), pst(pfi), ncp,
            st(sfr, N_SAMPLE_SEQ), st(sfi, N_SAMPLE_SEQ), ncs)
```

```python
import functools

import jax
import jax.numpy as jnp
from jax import lax
from jax.experimental import pallas as pl
from jax.experimental.pallas import tpu as pltpu

F32 = jnp.float32
BF16 = jnp.bfloat16
I32 = jnp.int32

SUBLANES = 8
LANES = 128

D_MODEL = 2048
D_SSM = 1024
D_CONV = 1024
SSM_GROUP = 16
N_SSM_GROUPS = 64
SSM_STATE = 64
CONV_WIDTH = 31
CONV_BUF = CONV_WIDTH - 1
N_EXP_GROUPS = 4
EXP_PER_GROUP = 8
N_EXPERTS = 32
D_EXPERT = 256
EPS = 1e-6

N_PROMPT_SEQ = 4
PROMPT_LEN = 2048
N_SAMPLE_SEQ = 16
SAMPLE_LEN = 32
N_PROMPT_TOK = N_PROMPT_SEQ * PROMPT_LEN
N_SAMPLE_TOK = N_SAMPLE_SEQ * SAMPLE_LEN
N_TOK = N_PROMPT_TOK + N_SAMPLE_TOK

TM = 512
N_PROMPT_TILES = N_PROMPT_TOK // TM
N_TILES = N_TOK // TM
PROMPT_STEPS = TM // N_PROMPT_SEQ
GROUP_BLOCKS = 4
GROUPS_PER_BLOCK = N_SSM_GROUPS // GROUP_BLOCKS
GB_IN = GROUPS_PER_BLOCK * SSM_GROUP
GB_STATE = GROUPS_PER_BLOCK * SSM_STATE
GB_LB = GB_STATE // LANES
HIST_P = CONV_BUF * N_PROMPT_SEQ
HIST_S = CONV_BUF * N_SAMPLE_SEQ
ROUTE_LANES = 128
N_PAIRS = 2 * N_TOK
TO = 256
N_OT = N_TOK // TO
N_OT_PROMPT = N_PROMPT_TOK // TO
HALVES = TM // TO
RUN = 16
LROWS = 1024
TME = 512
N_ETILES = -(-(N_PAIRS + N_OT * N_EXPERTS * (RUN - 1) + N_EXPERTS * (TME - 1)) // TME)
assert 2 * TO + (RUN - 1) * N_EXPERTS <= LROWS
MODEL_LB = D_MODEL // LANES
SSM_LB = D_SSM // LANES
CONV_LB = D_CONV // LANES


def _cparams(sem, vmem_mb):
    return pltpu.CompilerParams(dimension_semantics=sem, vmem_limit_bytes=vmem_mb * 1024 * 1024)


def _resident(shape):
    return pl.BlockSpec(shape, lambda *_: (0,) * len(shape), pipeline_mode=pl.Buffered(1))


def _x_tile_specs(steps=PROMPT_STEPS):
    blocks_per_seq = PROMPT_LEN // steps
    n_prompt = N_PROMPT_SEQ * blocks_per_seq // N_PROMPT_SEQ

    def prompt(b):
        return pl.BlockSpec(
            (steps, D_MODEL),
            lambda i, *_: (b * blocks_per_seq + jnp.minimum(i, n_prompt - 1), 0))

    return [prompt(b) for b in range(N_PROMPT_SEQ)] + [_resident((N_SAMPLE_TOK, D_MODEL))]


def _to_mixer_order(ref3, val, n_seq, steps):
    for c in range(val.shape[1] // LANES):
        for b in range(n_seq):
            ref3[c, pl.ds(b, steps, stride=n_seq), :] = val[b * steps:(b + 1) * steps,
                                                            c * LANES:(c + 1) * LANES]


def _seq_rows(ref3, b, n_seq, steps):
    return jnp.concatenate(
        [ref3[c, pl.ds(b, steps, stride=n_seq), :] for c in range(ref3.shape[0])], axis=1)


def _tile_to_mixer_order(i, ref3, xs, sample):
    @pl.when(i < N_PROMPT_TILES)
    def _():
        _to_mixer_order(ref3, xs(), N_PROMPT_SEQ, PROMPT_STEPS)

    @pl.when(i >= N_PROMPT_TILES)
    def _():
        _to_mixer_order(ref3, sample(), N_SAMPLE_SEQ, SAMPLE_LEN)


def _lane_blocks(ref3):
    return jnp.concatenate([ref3[c] for c in range(ref3.shape[0])], axis=1)


def _pair_specs(rows, cols, n_prompt_tiles):
    p = pl.BlockSpec((rows, cols), lambda i, *_: (jnp.minimum(i, n_prompt_tiles - 1), 0))
    s = pl.BlockSpec((rows, cols), lambda i, *_: (jnp.maximum(i - n_prompt_tiles, 0), 0))
    return p, s


def _disc_kernel(lr_ref, li_ref, ldt_ref, br_ref, bi_ref, ar_ref, ai_ref, bbr_ref, bbi_ref):
    lr = lr_ref[...]
    li = li_ref[...]
    dt = jnp.exp(ldt_ref[...])
    mag = jnp.exp(lr * dt)
    ar = mag * jnp.cos(li * dt)
    ai = mag * jnp.sin(li * dt)
    den = lr * lr + li * li
    nr = ar - 1.0
    cr = (nr * lr + ai * li) / den
    ci = (ai * lr - nr * li) / den
    br = br_ref[...]
    bi = bi_ref[...]
    ar_ref[...] = ar
    ai_ref[...] = ai
    bbr_ref[...] = cr * br - ci * bi
    bbi_ref[...] = cr * bi + ci * br


def _discretise(lam_re, lam_im, log_dt, b_re, b_im):
    shp = jax.ShapeDtypeStruct((N_SSM_GROUPS, SSM_STATE * SSM_GROUP), F32)
    rep = lambda a: jnp.tile(a, (1, SSM_GROUP))
    ldt = jnp.broadcast_to(log_dt[:, None], (N_SSM_GROUPS, SSM_STATE * SSM_GROUP))
    hp = lambda b: b.transpose(0, 2, 1).reshape(N_SSM_GROUPS, -1)
    return pl.pallas_call(_disc_kernel, out_shape=(shp, shp, shp, shp), name="s5_discretise")(
        rep(lam_re), rep(lam_im), ldt, hp(b_re), hp(b_im))


def _block_diag(blocks, rows_per_group, cols_per_group):
    wide = jnp.tile(blocks, (1, 1, GROUPS_PER_BLOCK))
    r = lax.broadcasted_iota(I32, wide.shape, 1) // rows_per_group
    c = lax.broadcasted_iota(I32, wide.shape, 2) // cols_per_group
    return jnp.where(r == c, wide, 0.0)


def _block_diag_in(bb):
    return _block_diag(bb.reshape(GROUP_BLOCKS, GB_IN, SSM_STATE), SSM_GROUP, SSM_STATE)


def _block_diag_out(c):
    return _block_diag(c.transpose(0, 2, 1).reshape(GROUP_BLOCKS, GB_STATE, SSM_GROUP),
                       SSM_STATE, SSM_GROUP)


def _rmsnorm_rows(x, g):
    r = lax.rsqrt(jnp.mean(x * x, axis=-1, keepdims=True) + EPS)
    return x * r * g


def _inproj_kernel(x0_ref, x1_ref, x2_ref, x3_ref, xs_ref, g_ref, wu_ref, wa_ref, wb_ref,
                   u_ref, v_ref, xn_ref):
    i = pl.program_id(0)
    dot = functools.partial(jnp.dot, preferred_element_type=F32)

    def tile(x, n_seq, steps):
        xn = _rmsnorm_rows(x, g_ref[...])
        _to_mixer_order(xn_ref, xn, n_seq, steps)
        xb = xn.astype(BF16)
        _to_mixer_order(u_ref, dot(xb, wu_ref[...]), n_seq, steps)
        v = dot(xb, wa_ref[...]) * jax.nn.sigmoid(dot(xb, wb_ref[...]))
        _to_mixer_order(v_ref, v, n_seq, steps)

    @pl.when(i < N_PROMPT_TILES)
    def _():
        x = jnp.concatenate([x0_ref[...], x1_ref[...], x2_ref[...], x3_ref[...]], axis=0)
        tile(x, N_PROMPT_SEQ, PROMPT_STEPS)

    @pl.when(i >= N_PROMPT_TILES)
    def _():
        tile(xs_ref[...], N_SAMPLE_SEQ, SAMPLE_LEN)


def _inproj(xp, xs, g, w_in_bf):
    blocked = lambda nb: pl.BlockSpec((nb, TM, LANES), lambda i: (0, i, 0))
    wcols = lambda j: pl.BlockSpec((D_MODEL, D_SSM), lambda i: (0, j), pipeline_mode=pl.Buffered(1))
    return pl.pallas_call(
        _inproj_kernel,
        grid=(N_TILES,),
        in_specs=_x_tile_specs() + [_resident((1, D_MODEL)), wcols(0), wcols(1), wcols(2)],
        out_specs=[blocked(SSM_LB), blocked(CONV_LB), blocked(MODEL_LB)],
        out_shape=(jax.ShapeDtypeStruct((SSM_LB, N_TOK, LANES), F32),
                   jax.ShapeDtypeStruct((CONV_LB, N_TOK, LANES), F32),
                   jax.ShapeDtypeStruct((MODEL_LB, N_TOK, LANES), F32)),
        compiler_params=_cparams(("arbitrary",), 56),
        name="in_proj",
    )(xp, xp, xp, xp, xs, g, w_in_bf, w_in_bf, w_in_bf)


N_SCAN_TILES = GROUP_BLOCKS * N_PROMPT_TILES
SCAN_STAGES = 3


def _scan_prompt_kernel(ua_ref, uc_ref, bm_ref, cm_ref, ar_ref, ai_ref, dsk_ref,
                        act_ref, pfr_ref, pfi_ref, x0, x1, x2, cst, car, fin):
    n = pl.program_id(0)
    q_s = jnp.clip(n - 1, 0, N_SCAN_TILES - 1)
    t_s = q_s % N_PROMPT_TILES
    valid_s = jnp.logical_and(n >= 1, n <= N_SCAN_TILES)

    @pl.when(n == 0)
    def _():
        for buf in (x0, x1, x2):
            buf[...] = jnp.zeros_like(buf)
        car[...] = jnp.zeros_like(car)
        fin[...] = jnp.zeros_like(fin)

    shape = (SUBLANES, GB_STATE)
    top = lax.broadcasted_iota(I32, shape, 0) < N_PROMPT_SEQ
    arb = jnp.broadcast_to(ar_ref[0], shape)
    aib = jnp.broadcast_to(ai_ref[0], shape)
    cst[0] = jnp.where(top, 0.0, arb)
    cst[1] = jnp.where(top, 0.0, aib)
    cst[2] = jnp.where(top, arb, arb * arb - aib * aib)
    cst[3] = jnp.where(top, aib, 2.0 * arb * aib)

    def project_in(xa):
        u = jnp.concatenate([ua_ref[0], ua_ref[1]], axis=1).astype(BF16)
        x = jnp.dot(u, bm_ref[0], preferred_element_type=F32)
        for c in range(2 * GB_LB):
            xa[c] = x[:, c * LANES:(c + 1) * LANES]

    def project_out(xc):
        u = jnp.concatenate([uc_ref[0], uc_ref[1]], axis=1)
        h = _lane_blocks(xc).astype(BF16)
        y = jnp.dot(h, cm_ref[0], preferred_element_type=F32) + dsk_ref[0] * u
        act_ref[...] = jax.nn.gelu(y).astype(BF16)

    def recur(xs):
        top1 = lax.broadcasted_iota(I32, (SUBLANES, LANES), 0) < N_PROMPT_SEQ
        first = t_s == 0
        hr = [jnp.where(first, 0.0, car[:, c * LANES:(c + 1) * LANES]) for c in range(GB_LB)]
        hi = [jnp.where(first, 0.0, car[:, GB_STATE + c * LANES:GB_STATE + (c + 1) * LANES])
              for c in range(GB_LB)]
        for r in range(TM // SUBLANES):
            rows = slice(r * SUBLANES, (r + 1) * SUBLANES)
            for c in range(GB_LB):
                cols = slice(c * LANES, (c + 1) * LANES)
                xr = xs[c, rows, :]
                xi = xs[GB_LB + c, rows, :]
                a1r, a1i, a2r, a2i = cst[0, :, cols], cst[1, :, cols], cst[2, :, cols], cst[3, :, cols]
                sxr = pltpu.roll(xr, N_PROMPT_SEQ, axis=0)
                sxi = pltpu.roll(xi, N_PROMPT_SEQ, axis=0)
                tr = xr + a1r * sxr - a1i * sxi
                ti = xi + a1r * sxi + a1i * sxr
                nr = tr + a2r * hr[c] - a2i * hi[c]
                ni = ti + a2r * hi[c] + a2i * hr[c]
                xs[c, rows, :] = nr
                xs[GB_LB + c, rows, :] = ni
                hr[c] = jnp.where(top1, pltpu.roll(nr, N_PROMPT_SEQ, axis=0), nr)
                hi[c] = jnp.where(top1, pltpu.roll(ni, N_PROMPT_SEQ, axis=0), ni)
        state = jnp.concatenate(hr + hi, axis=1)
        car[...] = state
        last = jnp.logical_and(valid_s, t_s == N_PROMPT_TILES - 1)
        fin[...] = jnp.where(last, state, fin[...])
        pfr_ref[0] = fin[N_PROMPT_SEQ:SUBLANES, 0:GB_STATE]
        pfi_ref[0] = fin[N_PROMPT_SEQ:SUBLANES, GB_STATE:2 * GB_STATE]

    bufs = (x0, x1, x2)
    for r in range(SCAN_STAGES):
        @pl.when(n % SCAN_STAGES == r)
        def _():
            project_out(bufs[(r + 1) % SCAN_STAGES])
            recur(bufs[(r + 2) % SCAN_STAGES])
            project_in(bufs[r])


def _scan_prompt(u3, bm, cm, ar, ai, dsk):
    q_a = lambda n: jnp.minimum(n, N_SCAN_TILES - 1)
    q_s = lambda n: jnp.clip(n - 1, 0, N_SCAN_TILES - 1)
    q_c = lambda n: jnp.clip(n - 2, 0, N_SCAN_TILES - 1)
    gb = lambda q: q // N_PROMPT_TILES
    tt = lambda q: q % N_PROMPT_TILES
    u_spec = lambda q: pl.BlockSpec((GB_IN // LANES, TM, LANES), lambda n: (gb(q(n)), tt(q(n)), 0))
    gb3 = lambda last, q: pl.BlockSpec((1,) + last, lambda n: (gb(q(n)), 0, 0))
    state = jax.ShapeDtypeStruct((GROUP_BLOCKS, N_PROMPT_SEQ, GB_STATE), F32)
    xbuf = pltpu.VMEM((2 * GB_LB, TM, LANES), F32)
    return pl.pallas_call(
        _scan_prompt_kernel,
        grid=(N_SCAN_TILES + SCAN_STAGES - 1,),
        in_specs=[
            u_spec(q_a), u_spec(q_c),
            gb3((GB_IN, 2 * GB_STATE), q_a),
            gb3((2 * GB_STATE, GB_IN), q_c),
            gb3((1, GB_STATE), q_s),
            gb3((1, GB_STATE), q_s),
            gb3((1, GB_IN), q_c),
        ],
        out_specs=[
            pl.BlockSpec((TM, GB_IN), lambda n: (tt(q_c(n)), gb(q_c(n)))),
            gb3((N_PROMPT_SEQ, GB_STATE), q_s),
            gb3((N_PROMPT_SEQ, GB_STATE), q_s),
        ],
        out_shape=(jax.ShapeDtypeStruct((N_PROMPT_TOK, D_SSM), BF16), state, state),
        scratch_shapes=[xbuf, xbuf, xbuf,
                        pltpu.VMEM((4, SUBLANES, GB_STATE), F32),
                        pltpu.VMEM((SUBLANES, 2 * GB_STATE), F32),
                        pltpu.VMEM((SUBLANES, 2 * GB_STATE), F32)],
        compiler_params=_cparams(("arbitrary",), 48),
        name="s5_scan_prompt",
    )(u3, u3, bm, cm, ar, ai, dsk)


def _scan_sample_kernel(u_ref, bm_ref, cm_ref, ar_ref, ai_ref, dsk_ref, s0r_ref, s0i_ref,
                        act_ref, sfr_ref, sfi_ref, xs_ref):
    half_lb = GB_LB // 2
    half_cols = half_lb * LANES
    u = jnp.concatenate([u_ref[0], u_ref[1]], axis=1)
    x = jnp.dot(u.astype(BF16), bm_ref[0], preferred_element_type=F32)
    for c in range(2 * GB_LB):
        xs_ref[c] = x[:, c * LANES:(c + 1) * LANES]

    def load_half(rows, half, imag):
        c0 = imag * GB_LB + half * half_lb
        return jnp.concatenate([xs_ref[c0 + c, rows, :] for c in range(half_lb)], axis=1)

    def store_half(rows, half, imag, val):
        c0 = imag * GB_LB + half * half_lb
        for c in range(half_lb):
            xs_ref[c0 + c, rows, :] = val[:, c * LANES:(c + 1) * LANES]

    shape = (N_SAMPLE_SEQ, half_cols)
    for half in range(2):
        cols = slice(half * half_cols, (half + 1) * half_cols)
        arb = jnp.broadcast_to(ar_ref[0][:, cols], shape)
        aib = jnp.broadcast_to(ai_ref[0][:, cols], shape)

        def body(k, carry):
            hr, hi = carry
            rows = pl.ds(pl.multiple_of(k * N_SAMPLE_SEQ, N_SAMPLE_SEQ), N_SAMPLE_SEQ)
            xr = load_half(rows, half, 0)
            xi = load_half(rows, half, 1)
            nhr = xr + arb * hr - aib * hi
            nhi = xi + arb * hi + aib * hr
            store_half(rows, half, 0, nhr)
            store_half(rows, half, 1, nhi)
            return nhr, nhi

        hr, hi = lax.fori_loop(0, SAMPLE_LEN, body, (s0r_ref[:, cols], s0i_ref[:, cols]))
        sfr_ref[:, cols] = hr
        sfi_ref[:, cols] = hi

    h = _lane_blocks(xs_ref).astype(BF16)
    y = jnp.dot(h, cm_ref[0], preferred_element_type=F32) + dsk_ref[0] * u
    act_ref[...] = jax.nn.gelu(y).astype(BF16)


def _scan_sample(u3, bm, cm, ar, ai, dsk, s0r, s0i):
    gb3 = lambda last: pl.BlockSpec((1,) + last, lambda g: (g, 0, 0))
    state = lambda: pl.BlockSpec((N_SAMPLE_SEQ, GB_STATE), lambda g: (0, g))
    sst = jax.ShapeDtypeStruct((N_SAMPLE_SEQ, N_SSM_GROUPS * SSM_STATE), F32)
    return pl.pallas_call(
        _scan_sample_kernel,
        grid=(GROUP_BLOCKS,),
        in_specs=[
            pl.BlockSpec((GB_IN // LANES, TM, LANES), lambda g: (g, N_PROMPT_TILES, 0)),
            gb3((GB_IN, 2 * GB_STATE)), gb3((2 * GB_STATE, GB_IN)),
            gb3((1, GB_STATE)), gb3((1, GB_STATE)), gb3((1, GB_IN)),
            state(), state(),
        ],
        out_specs=[pl.BlockSpec((TM, GB_IN), lambda g: (0, g)), state(), state()],
        out_shape=(jax.ShapeDtypeStruct((N_SAMPLE_TOK, D_SSM), BF16), sst, sst),
        scratch_shapes=[pltpu.VMEM((2 * GB_LB, TM, LANES), F32)],
        compiler_params=_cparams(("arbitrary",), 32),
        name="s5_scan_sample",
    )(u3, bm, cm, ar, ai, dsk, s0r, s0i)


CONV_ROWS = 64
EXT_ROWS = HIST_S + TM


def _conv_kernel(v_ref, hist_ref, w_ref, b_ref, lg_ref, lb_ref,
                 cv_ref, ncp_ref, ncs_ref, ext_ref, sh_ref, acc_ref):
    t = pl.program_id(0)

    def taps(n_seq):
        half = SUBLANES // 2
        if n_seq % SUBLANES:
            lo = HIST_S - HIST_P - SUBLANES
            sh_ref[:, lo:EXT_ROWS - SUBLANES, :] = ext_ref[:, lo + half:EXT_ROWS - half, :]

        def body(rb, _):
            r0 = pl.multiple_of(rb * CONV_ROWS, CONV_ROWS)
            for cb in range(CONV_LB):
                cols = slice(cb * LANES, (cb + 1) * LANES)
                acc = None
                for j in range(CONV_WIDTH):
                    start = HIST_S - (CONV_BUF - j) * n_seq
                    src = ext_ref
                    if start % SUBLANES:
                        src, start = sh_ref, start - half
                    rows = pl.ds(pl.multiple_of(r0 + start, SUBLANES), CONV_ROWS)
                    term = src[cb, rows, :] * w_ref[j:j + 1, cols]
                    acc = term if acc is None else acc + term
                acc_ref[cb, pl.ds(r0, CONV_ROWS), :] = acc + b_ref[:, cols]
            return 0

        lax.fori_loop(0, TM // CONV_ROWS, body, 0)

    @pl.when(t == 0)
    def _():
        ext_ref[:, 0:HIST_S, :] = jnp.zeros((CONV_LB, HIST_S, LANES), F32)

    @pl.when(t == N_PROMPT_TILES)
    def _():
        for c in range(CONV_LB):
            ext_ref[c, 0:HIST_S, :] = hist_ref[:, c * LANES:(c + 1) * LANES]

    ext_ref[:, HIST_S:EXT_ROWS, :] = v_ref[...]

    @pl.when(t < N_PROMPT_TILES)
    def _():
        taps(N_PROMPT_SEQ)

    @pl.when(t == N_PROMPT_TILES)
    def _():
        taps(N_SAMPLE_SEQ)

    y = acc_ref[...]
    mu = jnp.sum(jnp.sum(y, axis=0), axis=-1, keepdims=True) / D_CONV
    yc = y - mu[None]
    var = jnp.sum(jnp.sum(yc * yc, axis=0), axis=-1, keepdims=True) / D_CONV
    z = jax.nn.silu(yc * lax.rsqrt(var + EPS)[None] * lg_ref[...] + lb_ref[...])
    for c in range(CONV_LB):
        cv_ref[:, c * LANES:(c + 1) * LANES] = z[c].astype(BF16)

    @pl.when(t == N_PROMPT_TILES - 1)
    def _():
        for c in range(CONV_LB):
            ncp_ref[:, c * LANES:(c + 1) * LANES] = ext_ref[c, EXT_ROWS - HIST_P:EXT_ROWS, :]

    @pl.when(t < N_PROMPT_TILES - 1)
    def _():
        ext_ref[:, HIST_S - HIST_P:HIST_S, :] = ext_ref[:, EXT_ROWS - HIST_P:EXT_ROWS, :]

    @pl.when(t == N_PROMPT_TILES)
    def _():
        for c in range(CONV_LB):
            ncs_ref[:, c * LANES:(c + 1) * LANES] = ext_ref[c, EXT_ROWS - HIST_S:EXT_ROWS, :]


def _conv(v3, hist_s, conv_w, conv_b, ln_g, ln_b):
    row = lambda n: pl.BlockSpec((n, D_CONV), lambda t: (0, 0))
    lane3 = pl.BlockSpec((CONV_LB, 1, LANES), lambda t: (0, 0, 0))
    ext = pltpu.VMEM((CONV_LB, EXT_ROWS, LANES), F32)
    return pl.pallas_call(
        _conv_kernel,
        grid=(N_TILES,),
        in_specs=[pl.BlockSpec((CONV_LB, TM, LANES), lambda t: (0, t, 0)),
                  row(HIST_S), row(CONV_WIDTH), row(1), lane3, lane3],
        out_specs=[pl.BlockSpec((TM, D_CONV), lambda t: (t, 0)), row(HIST_P), row(HIST_S)],
        out_shape=(jax.ShapeDtypeStruct((N_TOK, D_CONV), BF16),
                   jax.ShapeDtypeStruct((HIST_P, D_CONV), F32),
                   jax.ShapeDtypeStruct((HIST_S, D_CONV), F32)),
        scratch_shapes=[ext, ext, pltpu.VMEM((CONV_LB, TM, LANES), F32)],
        compiler_params=_cparams(("arbitrary",), 48),
        name="conv_branch",
    )(v3, hist_s, conv_w, conv_b,
      ln_g.reshape(CONV_LB, 1, LANES), ln_b.reshape(CONV_LB, 1, LANES))


MERGE_COLS = 512
N_MERGE = D_MODEL // MERGE_COLS


def _merge_kernel(xn_ref, actp_ref, acts_ref, cv_ref, wga0_ref, wga1_ref, wgb0_ref, wgb1_ref,
                  wglu_ref, wco_ref, m_ref):
    i = pl.program_id(0)
    dot = functools.partial(jnp.dot, preferred_element_type=F32)
    xn = _lane_blocks(xn_ref).astype(BF16)
    act = jnp.where(i < N_PROMPT_TILES, actp_ref[...], acts_ref[...])
    cv = cv_ref[...]
    per_block = D_SSM // MERGE_COLS
    for c in range(N_MERGE):
        cols = pl.ds(c * MERGE_COLS, MERGE_COLS)
        in_block = pl.ds((c % per_block) * MERGE_COLS, MERGE_COLS)
        wga = (wga0_ref, wga1_ref)[c // per_block]
        wgb = (wgb0_ref, wgb1_ref)[c // per_block]
        ya = dot(act, wglu_ref[:, cols]) * jax.nn.sigmoid(
            dot(act, wglu_ref[:, pl.ds(D_MODEL + c * MERGE_COLS, MERGE_COLS)]))
        yb = dot(cv, wco_ref[:, cols])
        m = (jax.nn.sigmoid(dot(xn, wga[:, in_block])) * ya
             + jax.nn.sigmoid(dot(xn, wgb[:, in_block])) * yb)
        m_ref[:, cols] = m.astype(BF16)


def _merge(xn3, actp, acts, cv, w_in_bf, w_glu_bf, w_co_bf):
    ga0 = (D_SSM + 2 * D_CONV) // D_SSM
    wcols = lambda j: pl.BlockSpec((D_MODEL, D_SSM), lambda i: (0, j), pipeline_mode=pl.Buffered(1))
    return pl.pallas_call(
        _merge_kernel,
        grid=(N_TILES,),
        in_specs=[pl.BlockSpec((MODEL_LB, TM, LANES), lambda i: (0, i, 0)),
                  pl.BlockSpec((TM, D_SSM), lambda i: (jnp.minimum(i, N_PROMPT_TILES - 1), 0)),
                  _resident((N_SAMPLE_TOK, D_SSM)),
                  pl.BlockSpec((TM, D_CONV), lambda i: (i, 0)),
                  wcols(ga0), wcols(ga0 + 1), wcols(ga0 + 2), wcols(ga0 + 3),
                  _resident((D_SSM, 2 * D_MODEL)), _resident((D_CONV, D_MODEL))],
        out_specs=pl.BlockSpec((TM, D_MODEL), lambda i: (i, 0)),
        out_shape=jax.ShapeDtypeStruct((N_TOK, D_MODEL), BF16),
        compiler_params=_cparams(("arbitrary",), 58),
        name="gated_merge",
    )(xn3, actp, acts, cv, w_in_bf, w_in_bf, w_in_bf, w_in_bf, w_glu_bf, w_co_bf)


def _outproj_kernel(m_ref, x0_ref, x1_ref, x2_ref, x3_ref, xs_ref, wo_ref, g_ref, wr_ref, br_ref,
                    h_ref, xl_ref, route_ref, gw_ref, cnt_ref, xr_s, lg0, lg1, xb0, xb1):
    i = pl.program_id(0)
    tile = jnp.minimum(i, N_OT - 1)
    steps_p = TO // N_PROMPT_SEQ
    steps_s = TO // N_SAMPLE_SEQ

    @pl.when(i == 0)
    def _():
        lg1[...] = jnp.zeros_like(lg1)
        xb1[...] = jnp.zeros_like(xb1)

    @pl.when(tile < N_OT_PROMPT)
    def _():
        x = jnp.concatenate([x0_ref[...], x1_ref[...], x2_ref[...], x3_ref[...]], axis=0)
        _to_mixer_order(xr_s, x, N_PROMPT_SEQ, steps_p)

    @pl.when(tile >= N_OT_PROMPT)
    def _():
        k0 = (tile - N_OT_PROMPT) * steps_s
        x = jnp.concatenate(
            [xs_ref[pl.ds(pl.multiple_of(b * SAMPLE_LEN + k0, steps_s), steps_s), :]
             for b in range(N_SAMPLE_SEQ)], axis=0)
        _to_mixer_order(xr_s, x, N_SAMPLE_SEQ, steps_s)

    sets = ((lg0, xb0), (lg1, xb1))
    for r in range(2):
        @pl.when(i % 2 == r)
        def _():
            lg_prev, xb_prev = sets[1 - r]
            _route_and_sort(lg_prev[...], xb_prev[...], xl_ref, route_ref, gw_ref, cnt_ref)
            _project(m_ref, xr_s, wo_ref, g_ref, wr_ref, br_ref, h_ref, *sets[r])


def _project(m_ref, xr_s, wo_ref, g_ref, wr_ref, br_ref, h_ref, lg_s, xb_s):
    dot = functools.partial(jnp.dot, preferred_element_type=F32)
    h = _lane_blocks(xr_s) + dot(m_ref[...], wo_ref[...])
    h_ref[...] = h
    xn = _rmsnorm_rows(h, g_ref[...])
    x_hi = xn.astype(BF16)
    x_lo = (xn - x_hi.astype(F32)).astype(BF16)
    w = wr_ref[...]
    w_hi = w.astype(BF16)
    w_lo = (w - w_hi.astype(F32)).astype(BF16)
    both = dot(x_hi, jnp.concatenate([w_hi, w_lo], axis=1))
    lg_s[...] = (both[:, :ROUTE_LANES] + (dot(x_lo, w_hi) + both[:, ROUTE_LANES:])) + br_ref[...]
    xb_s[...] = x_hi


def _route_and_sort(logits, x_hi, xl_ref, route_ref, gw_ref, cnt_ref):
    dot = functools.partial(jnp.dot, preferred_element_type=F32)
    lane = lax.broadcasted_iota(I32, logits.shape, 1)
    neg = -jnp.inf
    first = lambda hit: jnp.min(jnp.where(hit, lane, ROUTE_LANES), axis=-1, keepdims=True)
    gmask = lane < N_EXP_GROUPS
    lg = jnp.where(gmask, logits, neg)
    gmax = jnp.max(lg, axis=-1, keepdims=True)
    gsel = first(lg == gmax)
    psum = jnp.sum(jnp.where(gmask, jnp.exp(logits - gmax), 0.0), axis=-1, keepdims=True)
    pg_sel = 1.0 / psum
    e_lane = lane - N_EXP_GROUPS
    emask = (e_lane >= 0) & (e_lane < N_EXPERTS) & ((e_lane // EXP_PER_GROUP) == gsel)
    le = jnp.where(emask, logits, neg)
    v1 = jnp.max(le, axis=-1, keepdims=True)
    i1 = first(le == v1)
    le2 = jnp.where(lane == i1, neg, le)
    v2 = jnp.max(le2, axis=-1, keepdims=True)
    i2 = first(le2 == v2)
    z = jnp.exp(v2 - v1)
    w1 = pg_sel / (1.0 + z)
    w2 = pg_sel * z / (1.0 + z)
    e1 = i1 - N_EXP_GROUPS
    e2 = i2 - N_EXP_GROUPS
    gw_ref[...] = jnp.where(lane == 0, w1, jnp.where(lane == 1, w2, 0.0))
    oh1 = lane == e1
    oh2 = lane == e2
    hits = jnp.where(oh1 | oh2, 1.0, 0.0)
    rr = lax.broadcasted_iota(I32, (TO, TO), 0)
    cc = lax.broadcasted_iota(I32, (TO, TO), 1)
    before = dot(jnp.where(cc < rr, 1.0, 0.0).astype(BF16), hits.astype(BF16))
    counts = jnp.sum(hits, axis=0, keepdims=True)
    groups = jnp.floor((counts + (RUN - 1.0)) * (1.0 / RUN))
    er = lax.broadcasted_iota(I32, (ROUTE_LANES, ROUTE_LANES), 0)
    ec = lax.broadcasted_iota(I32, (ROUTE_LANES, ROUTE_LANES), 1)
    groups_before = dot(jnp.broadcast_to(groups, (SUBLANES, ROUTE_LANES)).astype(BF16),
                        jnp.where(er < ec, 1.0, 0.0).astype(BF16))[0:1]
    local = groups_before * RUN + before
    lpos1 = jnp.sum(jnp.where(oh1, local, 0.0), axis=-1, keepdims=True)
    lpos2 = jnp.sum(jnp.where(oh2, local, 0.0), axis=-1, keepdims=True)
    route_ref[...] = jnp.where(lane == 0, e1, jnp.where(lane == 1, e2, jnp.where(
        lane == 2, lpos1.astype(I32), jnp.where(lane == 3, lpos2.astype(I32), 0))))
    cnt_ref[0] = counts.astype(I32)
    lpos_t = jnp.where(lane == 0, lpos1, jnp.where(lane == 1, lpos2, 0.0)).T.astype(I32)
    jrow = lax.broadcasted_iota(I32, (LROWS, TO), 0)
    sel = (jrow == lpos_t[0:1, :]) | (jrow == lpos_t[1:2, :])
    xl_ref[...] = dot(jnp.where(sel, 1.0, 0.0).astype(BF16), x_hi).astype(BF16)


def _outproj(m, xp, xs, w_out_bf, g, w_router, b_router):
    cur = lambda i: jnp.minimum(i, N_OT - 1)
    prev = lambda i: jnp.maximum(i - 1, 0)
    tile = lambda n: pl.BlockSpec((TO, n), lambda i: (cur(i), 0))
    routed = lambda n: pl.BlockSpec((TO, n), lambda i: (prev(i), 0))
    return pl.pallas_call(
        _outproj_kernel,
        grid=(N_OT + 1,),
        in_specs=[tile(D_MODEL)] + _x_tile_specs(TO // N_PROMPT_SEQ) + [
            _resident((D_MODEL, D_MODEL)), _resident((1, D_MODEL)),
            _resident((D_MODEL, ROUTE_LANES)), _resident((1, ROUTE_LANES))],
        out_specs=[tile(D_MODEL),
                   pl.BlockSpec((LROWS, D_MODEL), lambda i: (prev(i), 0)),
                   routed(ROUTE_LANES), routed(ROUTE_LANES),
                   pl.BlockSpec((1, 1, ROUTE_LANES), lambda i: (prev(i), 0, 0))],
        out_shape=(jax.ShapeDtypeStruct((N_TOK, D_MODEL), F32),
                   jax.ShapeDtypeStruct((N_OT * LROWS, D_MODEL), BF16),
                   jax.ShapeDtypeStruct((N_TOK, ROUTE_LANES), I32),
                   jax.ShapeDtypeStruct((N_TOK, ROUTE_LANES), F32),
                   jax.ShapeDtypeStruct((N_OT, 1, ROUTE_LANES), I32)),
        scratch_shapes=[pltpu.VMEM((MODEL_LB, TO, LANES), F32),
                        pltpu.VMEM((TO, ROUTE_LANES), F32), pltpu.VMEM((TO, ROUTE_LANES), F32),
                        pltpu.VMEM((TO, D_MODEL), BF16), pltpu.VMEM((TO, D_MODEL), BF16)],
        compiler_params=_cparams(("arbitrary",), 56),
        name="out_proj_router",
    )(m, xp, xp, xp, xp, xs, w_out_bf, g, w_router, b_router)


def _expert_kernel(te_ref, nrows_ref, nused_ref, psrc_ref, pdst_ref, plen_ref, tot_ref, kfirst_ref, kend_ref,
                   first_ref, par_ref, nexte_ref,
                   xl_hbm, wg_hbm, wu_hbm, wd_hbm, ol_hbm,
                   xbuf, obuf, zbuf, wg_s, wu_s, wd_s, wg_st, wu_st, wd_st, gsem, wsem, zsem, esem):
    i = pl.program_id(0)
    nused = nused_ref[0]
    slot = i % 2
    rows8 = lambda v: pl.multiple_of(v, RUN)

    def for_pieces(tile, fn):
        def body(k, _):
            idx = tile * N_OT + k
            n = plen_ref[idx]

            @pl.when(n > 0)
            def _():
                fn(rows8(psrc_ref[idx]), rows8(pdst_ref[idx]), rows8(n))
            return 0
        lax.fori_loop(kfirst_ref[tile], kend_ref[tile], body, 0)

    def gather(tile, s):
        for_pieces(tile, lambda src, dst, n: pltpu.make_async_copy(
            xl_hbm.at[pl.ds(src, n)], xbuf.at[s, pl.ds(dst, n)], gsem.at[s]).start())

    def scatter(tile, s):
        for_pieces(tile, lambda src, dst, n: pltpu.make_async_copy(
            obuf.at[s, pl.ds(dst, n)], ol_hbm.at[pl.ds(src, n)], wsem.at[s]).start())

    def wait_gather(tile, s):
        n = rows8(nrows_ref[tile])
        pltpu.make_async_copy(xl_hbm.at[pl.ds(0, n)], xbuf.at[s, pl.ds(0, n)], gsem.at[s]).wait()

    def wait_scatter(tile, s):
        n = rows8(nrows_ref[tile])
        pltpu.make_async_copy(obuf.at[s, pl.ds(0, n)], ol_hbm.at[pl.ds(0, n)], wsem.at[s]).wait()

    def tail_copy(t):
        used = rows8(tot_ref[t])
        n = rows8(LROWS - tot_ref[t])
        dst = rows8(t * LROWS + used)
        return pltpu.make_async_copy(zbuf.at[pl.ds(0, n)], ol_hbm.at[pl.ds(dst, n)], zsem.at[0])

    def weight_copies(e, b):
        return [pltpu.make_async_copy(src.at[e], dst.at[b], esem.at[b, k])
                for k, (src, dst) in enumerate(((wg_hbm, wg_st), (wu_hbm, wu_st), (wd_hbm, wd_st)))]

    @pl.when(i == 0)
    def _():
        for c in weight_copies(te_ref[0], par_ref[0]):
            c.start()
        zbuf[...] = jnp.zeros_like(zbuf)
        xbuf[...] = jnp.zeros_like(xbuf)

        def start(t, _):
            tail_copy(t).start()
            return 0

        def wait(t, _):
            tail_copy(t).wait()
            return 0
        lax.fori_loop(0, N_OT, start, 0)
        lax.fori_loop(0, N_OT, wait, 0)
        gather(0, 0)

    @pl.when(i + 1 < nused)
    def _():
        gather(i + 1, 1 - slot)

    @pl.when(i < nused)
    def _():
        @pl.when(first_ref[i] == 1)
        def _():
            b = par_ref[i]
            for c in weight_copies(te_ref[i], b):
                c.wait()
            wg_s[...] = wg_st[b].astype(BF16)
            wu_s[...] = wu_st[b].astype(BF16)
            wd_s[...] = wd_st[b].astype(BF16)

            @pl.when(nexte_ref[i] >= 0)
            def _():
                for c in weight_copies(nexte_ref[i], 1 - b):
                    c.start()

        wait_gather(i, slot)

        @pl.when(i >= 2)
        def _():
            wait_scatter(i - 2, slot)

        xb = xbuf[slot]
        hg = jnp.dot(xb, wg_s[...], preferred_element_type=F32)
        hu = jnp.dot(xb, wu_s[...], preferred_element_type=F32)
        hid = jax.nn.silu(hg) * hu
        obuf[slot] = jnp.dot(hid.astype(BF16), wd_s[...], preferred_element_type=F32).astype(BF16)
        scatter(i, slot)

        @pl.when(i == nused - 1)
        def _():
            @pl.when(i >= 1)
            def _():
                wait_scatter(i - 1, 1 - slot)
            wait_scatter(i, slot)


def _experts(plan, xl, wg, wu, wd):
    anyspec = pl.BlockSpec(memory_space=pl.ANY)
    grid_spec = pltpu.PrefetchScalarGridSpec(
        num_scalar_prefetch=len(plan),
        grid=(N_ETILES,),
        in_specs=[anyspec, anyspec, anyspec, anyspec],
        out_specs=anyspec,
        scratch_shapes=[
            pltpu.VMEM((2, TME, D_MODEL), BF16),
            pltpu.VMEM((2, TME, D_MODEL), BF16),
            pltpu.VMEM((LROWS - 2 * TO, D_MODEL), BF16),
            pltpu.VMEM((D_MODEL, D_EXPERT), BF16),
            pltpu.VMEM((D_MODEL, D_EXPERT), BF16),
            pltpu.VMEM((D_EXPERT, D_MODEL), BF16),
            pltpu.VMEM((2, D_MODEL, D_EXPERT), F32),
            pltpu.VMEM((2, D_MODEL, D_EXPERT), F32),
            pltpu.VMEM((2, D_EXPERT, D_MODEL), F32),
            pltpu.SemaphoreType.DMA((2,)),
            pltpu.SemaphoreType.DMA((2,)),
            pltpu.SemaphoreType.DMA((1,)),
            pltpu.SemaphoreType.DMA((2, 3)),
        ],
    )
    return pl.pallas_call(
        _expert_kernel,
        grid_spec=grid_spec,
        out_shape=jax.ShapeDtypeStruct((N_OT * LROWS, D_MODEL), BF16),
        compiler_params=_cparams(("arbitrary",), 40),
        name="routed_experts",
    )(*plan, xl, wg, wu, wd)


def _combine_kernel(h_ref, route_ref, gw_ref, ol_ref, g_ref, yp_ref, ys_ref, y_s):
    i = pl.program_id(0)
    steps_p = TO // N_PROMPT_SEQ
    steps_s = TO // N_SAMPLE_SEQ
    route = route_ref[...]
    gw = gw_ref[...]
    jcol = lax.broadcasted_iota(I32, (TO, LROWS), 1)
    mix = (jnp.where(jcol == route[:, 2:3], gw[:, 0:1], 0.0)
           + jnp.where(jcol == route[:, 3:4], gw[:, 1:2], 0.0))
    moe = jnp.dot(mix.astype(BF16), ol_ref[...], preferred_element_type=F32)
    y = _rmsnorm_rows(h_ref[...] + moe, g_ref[...])
    for c in range(MODEL_LB):
        y_s[c] = y[:, c * LANES:(c + 1) * LANES]

    @pl.when(i < N_OT_PROMPT)
    def _():
        for b in range(N_PROMPT_SEQ):
            yp_ref[b] = _seq_rows(y_s, b, N_PROMPT_SEQ, steps_p)

    @pl.when(i >= N_OT_PROMPT)
    def _():
        k0 = (i - N_OT_PROMPT) * steps_s
        for b in range(N_SAMPLE_SEQ):
            rows = pl.ds(pl.multiple_of(b * SAMPLE_LEN + k0, steps_s), steps_s)
            ys_ref[rows, :] = _seq_rows(y_s, b, N_SAMPLE_SEQ, steps_s)


def _combine(h, route, gw, ol, g):
    tile = lambda n: pl.BlockSpec((TO, n), lambda i: (i, 0))
    return pl.pallas_call(
        _combine_kernel,
        grid=(N_OT,),
        in_specs=[tile(D_MODEL), tile(ROUTE_LANES), tile(ROUTE_LANES),
                  pl.BlockSpec((LROWS, D_MODEL), lambda i: (i, 0)),
                  pl.BlockSpec((1, D_MODEL), lambda i: (0, 0))],
        out_specs=[
            pl.BlockSpec((N_PROMPT_SEQ, TO // N_PROMPT_SEQ, D_MODEL),
                         lambda i: (0, jnp.minimum(i, N_OT_PROMPT - 1), 0)),
            pl.BlockSpec((N_SAMPLE_TOK, D_MODEL), lambda i: (0, 0)),
        ],
        out_shape=(jax.ShapeDtypeStruct((N_PROMPT_SEQ, PROMPT_LEN, D_MODEL), F32),
                   jax.ShapeDtypeStruct((N_SAMPLE_TOK, D_MODEL), F32)),
        scratch_shapes=[pltpu.VMEM((MODEL_LB, TO, LANES), F32)],
        compiler_params=_cparams(("arbitrary",), 48),
        name="combine_norm",
    )(h, route, gw, ol, g)


def _dispatch_plan(cnt):
    experts = jnp.arange(N_EXPERTS, dtype=I32)
    cnt = cnt[:, 0, :N_EXPERTS]
    run = ((cnt + RUN - 1) // RUN) * RUN
    local_start = jnp.cumsum(run, axis=1) - run
    local_used = jnp.sum(run, axis=1)
    total = jnp.sum(run, axis=0)
    padded = ((total + TME - 1) // TME) * TME
    ends = jnp.cumsum(padded)
    starts = ends - padded
    run_start = starts[None, :] + jnp.cumsum(run, axis=0) - run
    n_used = (ends[-1] // TME).astype(I32)
    tile = jnp.arange(N_ETILES, dtype=I32)
    tile_start = tile * TME
    te = jnp.minimum(jnp.sum((ends[None, :] <= tile_start[:, None]).astype(I32), axis=1), N_EXPERTS - 1)
    mine = (te[:, None] == experts[None, :])[:, None, :]
    of_tile = lambda a: jnp.sum(jnp.where(mine, a[None], 0), axis=-1)
    s0 = of_tile(run_start)
    lo = jnp.maximum(s0, tile_start[:, None])
    hi = jnp.minimum(s0 + of_tile(run), tile_start[:, None] + TME)
    plen = jnp.where((tile < n_used)[:, None], jnp.maximum(hi - lo, 0), 0)
    live = plen > 0
    psrc = jnp.where(live, jnp.arange(N_OT, dtype=I32)[None, :] * LROWS + of_tile(local_start) + lo - s0, 0)
    pdst = jnp.where(live, lo - tile_start[:, None], 0)
    last = jnp.sum(jnp.where(tile == n_used - 1, te, 0))
    te = jnp.where(tile < n_used, te, last)
    flat = lambda a: a.reshape(-1).astype(I32)
    none_yet = lambda a: jnp.sum((jnp.cumsum(a.astype(I32), axis=1) == 0).astype(I32), axis=1)
    k_first = none_yet(live)
    k_end = N_OT - none_yet(live[:, ::-1])
    first = jnp.logical_and(tile < n_used, jnp.concatenate([jnp.array([True]), te[1:] != te[:-1]]))
    parity = (jnp.cumsum(first.astype(I32)) - 1) % 2
    next_first = lax.cummin(jnp.where(first, tile, N_ETILES), axis=0, reverse=True)
    next_first = jnp.concatenate([next_first[1:], jnp.array([N_ETILES], I32)])
    next_expert = jnp.where(next_first < N_ETILES, te[jnp.minimum(next_first, N_ETILES - 1)], -1)
    return (te.astype(I32), flat(jnp.sum(plen, axis=1)), n_used.reshape(1),
            flat(psrc), flat(pdst), flat(plen), local_used.astype(I32), flat(k_first), flat(k_end),
            flat(first), flat(parity), flat(next_expert))


def kernel(x_prompt, x_sample, state_ssm_re, state_ssm_im, cache_conv, norm_mix_g, w_in, lam_re, lam_im, log_dt, b_re, b_im, c_re, c_im, d_skip, w_ssm_glu, conv_w, conv_b, conv_ln_g, conv_ln_b, w_conv_out, w_out, norm_ffn_g, w_router_group, b_router_group, w_router_expert, b_router_expert, w_exp_gate, w_exp_up, w_exp_down, norm_final_g):
    assert w_in.shape[0] == 1, "single-layer trunk"
    xp = x_prompt.reshape(N_PROMPT_TOK, D_MODEL)
    xs = x_sample.reshape(N_SAMPLE_TOK, D_MODEL)
    w_in_bf = w_in[0].astype(BF16)
    row = lambda a: a.reshape(1, -1)

    ar_rep, ai_rep, bbr, bbi = _discretise(lam_re[0], lam_im[0], log_dt[0], b_re[0], b_im[0])
    pick = lambda a: a[:, :SSM_STATE].reshape(GROUP_BLOCKS, 1, GB_STATE)
    bm = jnp.concatenate([_block_diag_in(bbr), _block_diag_in(bbi)], axis=-1).astype(BF16)
    cm = jnp.concatenate([_block_diag_out(c_re[0]), -_block_diag_out(c_im[0])], axis=1).astype(BF16)
    dsk = d_skip[0].reshape(GROUP_BLOCKS, 1, GB_IN)
    ar, ai = pick(ar_rep), pick(ai_rep)
    s0r = state_ssm_re[0].reshape(N_SAMPLE_SEQ, -1)
    s0i = state_ssm_im[0].reshape(N_SAMPLE_SEQ, -1)

    u3, v3, xn3 = _inproj(xp, xs, row(norm_mix_g[0]), w_in_bf)
    actp, pfr, pfi = _scan_prompt(u3, bm, cm, ar, ai, dsk)
    acts, sfr, sfi = _scan_sample(u3, bm, cm, ar, ai, dsk, s0r, s0i)

    hist_s = cache_conv[0].transpose(1, 0, 2).reshape(HIST_S, D_CONV)
    cv, ncp, ncs = _conv(v3, hist_s, conv_w[0], row(conv_b[0]), conv_ln_g[0], conv_ln_b[0])

    m = _merge(xn3, actp, acts, cv, w_in_bf, w_ssm_glu[0].astype(BF16), w_conv_out[0].astype(BF16))

    pad_lanes = ROUTE_LANES - N_EXP_GROUPS - N_EXPERTS
    w_router = jnp.concatenate(
        [w_router_group[0], w_router_expert[0], jnp.zeros((D_MODEL, pad_lanes), F32)], axis=1)
    b_router = jnp.concatenate(
        [b_router_group[0], b_router_expert[0], jnp.zeros((pad_lanes,), F32)]).reshape(1, ROUTE_LANES)
    h, xl, route, gw, cnt = _outproj(m, xp, xs, w_out[0].astype(BF16), row(norm_ffn_g[0]),
                                     w_router, b_router)

    ol = _experts(_dispatch_plan(cnt), xl, w_exp_gate[0], w_exp_up[0], w_exp_down[0])
    yp, ys = _combine(h, route, gw, ol, row(norm_final_g))

    st = lambda a, n: a.reshape(1, n, N_SSM_GROUPS, SSM_STATE)
    pst = lambda a: st(a.transpose(1, 0, 2), N_PROMPT_SEQ)
    ncp = ncp.reshape(CONV_BUF, N_PROMPT_SEQ, D_CONV).transpose(1, 0, 2)[None]
    ncs = ncs.reshape(CONV_BUF, N_SAMPLE_SEQ, D_CONV).transpose(1, 0, 2)[None]
    return (yp, ys.reshape(N_SAMPLE_SEQ, SAMPLE_LEN, D_MODEL), pst(pfr), pst(pfi), ncp,
            st(sfr, N_SAMPLE_SEQ), st(sfi, N_SAMPLE_SEQ), ncs)
```

```python
import functools

import jax
import jax.numpy as jnp
from jax import lax
from jax.experimental import pallas as pl
from jax.experimental.pallas import tpu as pltpu

F32 = jnp.float32
BF16 = jnp.bfloat16
I32 = jnp.int32

SUBLANES = 8
LANES = 128

D_MODEL = 2048
D_SSM = 1024
D_CONV = 1024
SSM_GROUP = 16
N_SSM_GROUPS = 64
SSM_STATE = 64
CONV_WIDTH = 31
CONV_BUF = CONV_WIDTH - 1
N_EXP_GROUPS = 4
EXP_PER_GROUP = 8
N_EXPERTS = 32
D_EXPERT = 256
EPS = 1e-6

N_PROMPT_SEQ = 4
PROMPT_LEN = 2048
N_SAMPLE_SEQ = 16
SAMPLE_LEN = 32
N_PROMPT_TOK = N_PROMPT_SEQ * PROMPT_LEN
N_SAMPLE_TOK = N_SAMPLE_SEQ * SAMPLE_LEN
N_TOK = N_PROMPT_TOK + N_SAMPLE_TOK

TM = 512
N_PROMPT_TILES = N_PROMPT_TOK // TM
N_TILES = N_TOK // TM
PROMPT_STEPS = TM // N_PROMPT_SEQ
GROUP_BLOCKS = 4
GROUPS_PER_BLOCK = N_SSM_GROUPS // GROUP_BLOCKS
GB_IN = GROUPS_PER_BLOCK * SSM_GROUP
GB_STATE = GROUPS_PER_BLOCK * SSM_STATE
GB_LB = GB_STATE // LANES
HIST_P = CONV_BUF * N_PROMPT_SEQ
HIST_S = CONV_BUF * N_SAMPLE_SEQ
ROUTE_LANES = 128
N_PAIRS = 2 * N_TOK
TO = 256
N_OT = N_TOK // TO
N_OT_PROMPT = N_PROMPT_TOK // TO
HALVES = TM // TO
RUN = 16
LROWS = 1024
TME = 512
N_ETILES = -(-(N_PAIRS + N_OT * N_EXPERTS * (RUN - 1) + N_EXPERTS * (TME - 1)) // TME)
assert 2 * TO + (RUN - 1) * N_EXPERTS <= LROWS
MODEL_LB = D_MODEL // LANES
SSM_LB = D_SSM // LANES
CONV_LB = D_CONV // LANES


def _cparams(sem, vmem_mb):
    return pltpu.CompilerParams(dimension_semantics=sem, vmem_limit_bytes=vmem_mb * 1024 * 1024)


def _resident(shape):
    return pl.BlockSpec(shape, lambda *_: (0,) * len(shape), pipeline_mode=pl.Buffered(1))


def _x_tile_specs(steps=PROMPT_STEPS):
    blocks_per_seq = PROMPT_LEN // steps
    n_prompt = N_PROMPT_SEQ * blocks_per_seq // N_PROMPT_SEQ

    def prompt(b):
        return pl.BlockSpec(
            (steps, D_MODEL),
            lambda i, *_: (b * blocks_per_seq + jnp.minimum(i, n_prompt - 1), 0))

    return [prompt(b) for b in range(N_PROMPT_SEQ)] + [_resident((N_SAMPLE_TOK, D_MODEL))]


def _to_mixer_order(ref3, val, n_seq, steps):
    for c in range(val.shape[1] // LANES):
        for b in range(n_seq):
            ref3[c, pl.ds(b, steps, stride=n_seq), :] = val[b * steps:(b + 1) * steps,
                                                            c * LANES:(c + 1) * LANES]


def _seq_rows(ref3, b, n_seq, steps):
    return jnp.concatenate(
        [ref3[c, pl.ds(b, steps, stride=n_seq), :] for c in range(ref3.shape[0])], axis=1)


def _tile_to_mixer_order(i, ref3, xs, sample):
    @pl.when(i < N_PROMPT_TILES)
    def _():
        _to_mixer_order(ref3, xs(), N_PROMPT_SEQ, PROMPT_STEPS)

    @pl.when(i >= N_PROMPT_TILES)
    def _():
        _to_mixer_order(ref3, sample(), N_SAMPLE_SEQ, SAMPLE_LEN)


def _lane_blocks(ref3):
    return jnp.concatenate([ref3[c] for c in range(ref3.shape[0])], axis=1)


def _pair_specs(rows, cols, n_prompt_tiles):
    p = pl.BlockSpec((rows, cols), lambda i, *_: (jnp.minimum(i, n_prompt_tiles - 1), 0))
    s = pl.BlockSpec((rows, cols), lambda i, *_: (jnp.maximum(i - n_prompt_tiles, 0), 0))
    return p, s


def _disc_kernel(lr_ref, li_ref, ldt_ref, br_ref, bi_ref, ar_ref, ai_ref, bbr_ref, bbi_ref):
    lr = lr_ref[...]
    li = li_ref[...]
    dt = jnp.exp(ldt_ref[...])
    mag = jnp.exp(lr * dt)
    ar = mag * jnp.cos(li * dt)
    ai = mag * jnp.sin(li * dt)
    den = lr * lr + li * li
    nr = ar - 1.0
    cr = (nr * lr + ai * li) / den
    ci = (ai * lr - nr * li) / den
    br = br_ref[...]
    bi = bi_ref[...]
    ar_ref[...] = ar
    ai_ref[...] = ai
    bbr_ref[...] = cr * br - ci * bi
    bbi_ref[...] = cr * bi + ci * br


def _discretise(lam_re, lam_im, log_dt, b_re, b_im):
    shp = jax.ShapeDtypeStruct((N_SSM_GROUPS, SSM_STATE * SSM_GROUP), F32)
    rep = lambda a: jnp.tile(a, (1, SSM_GROUP))
    ldt = jnp.broadcast_to(log_dt[:, None], (N_SSM_GROUPS, SSM_STATE * SSM_GROUP))
    hp = lambda b: b.transpose(0, 2, 1).reshape(N_SSM_GROUPS, -1)
    return pl.pallas_call(_disc_kernel, out_shape=(shp, shp, shp, shp), name="s5_discretise")(
        rep(lam_re), rep(lam_im), ldt, hp(b_re), hp(b_im))


def _block_diag(blocks, rows_per_group, cols_per_group):
    wide = jnp.tile(blocks, (1, 1, GROUPS_PER_BLOCK))
    r = lax.broadcasted_iota(I32, wide.shape, 1) // rows_per_group
    c = lax.broadcasted_iota(I32, wide.shape, 2) // cols_per_group
    return jnp.where(r == c, wide, 0.0)


def _block_diag_in(bb):
    return _block_diag(bb.reshape(GROUP_BLOCKS, GB_IN, SSM_STATE), SSM_GROUP, SSM_STATE)


def _block_diag_out(c):
    return _block_diag(c.transpose(0, 2, 1).reshape(GROUP_BLOCKS, GB_STATE, SSM_GROUP),
                       SSM_STATE, SSM_GROUP)


def _rmsnorm_rows(x, g):
    r = lax.rsqrt(jnp.mean(x * x, axis=-1, keepdims=True) + EPS)
    return x * r * g


def _inproj_kernel(x0_ref, x1_ref, x2_ref, x3_ref, xs_ref, g_ref, wu_ref, wa_ref, wb_ref,
                   u_ref, v_ref, xn_ref):
    i = pl.program_id(0)
    dot = functools.partial(jnp.dot, preferred_element_type=F32)

    def tile(x, n_seq, steps):
        xn = _rmsnorm_rows(x, g_ref[...])
        _to_mixer_order(xn_ref, xn, n_seq, steps)
        xb = xn.astype(BF16)
        _to_mixer_order(u_ref, dot(xb, wu_ref[...]), n_seq, steps)
        v = dot(xb, wa_ref[...]) * jax.nn.sigmoid(dot(xb, wb_ref[...]))
        _to_mixer_order(v_ref, v, n_seq, steps)

    @pl.when(i < N_PROMPT_TILES)
    def _():
        x = jnp.concatenate([x0_ref[...], x1_ref[...], x2_ref[...], x3_ref[...]], axis=0)
        tile(x, N_PROMPT_SEQ, PROMPT_STEPS)

    @pl.when(i >= N_PROMPT_TILES)
    def _():
        tile(xs_ref[...], N_SAMPLE_SEQ, SAMPLE_LEN)


def _inproj(xp, xs, g, w_in_bf):
    blocked = lambda nb: pl.BlockSpec((nb, TM, LANES), lambda i: (0, i, 0))
    wcols = lambda j: pl.BlockSpec((D_MODEL, D_SSM), lambda i: (0, j), pipeline_mode=pl.Buffered(1))
    return pl.pallas_call(
        _inproj_kernel,
        grid=(N_TILES,),
        in_specs=_x_tile_specs() + [_resident((1, D_MODEL)), wcols(0), wcols(1), wcols(2)],
        out_specs=[blocked(SSM_LB), blocked(CONV_LB), blocked(MODEL_LB)],
        out_shape=(jax.ShapeDtypeStruct((SSM_LB, N_TOK, LANES), F32),
                   jax.ShapeDtypeStruct((CONV_LB, N_TOK, LANES), F32),
                   jax.ShapeDtypeStruct((MODEL_LB, N_TOK, LANES), F32)),
        compiler_params=_cparams(("arbitrary",), 56),
        name="in_proj",
    )(xp, xp, xp, xp, xs, g, w_in_bf, w_in_bf, w_in_bf)


N_SCAN_TILES = GROUP_BLOCKS * N_PROMPT_TILES
SCAN_STAGES = 3


def _scan_prompt_kernel(ua_ref, uc_ref, bm_ref, cm_ref, ar_ref, ai_ref, dsk_ref,
                        act_ref, pfr_ref, pfi_ref, x0, x1, x2, cst, car, fin):
    n = pl.program_id(0)
    q_s = jnp.clip(n - 1, 0, N_SCAN_TILES - 1)
    t_s = q_s % N_PROMPT_TILES
    valid_s = jnp.logical_and(n >= 1, n <= N_SCAN_TILES)

    @pl.when(n == 0)
    def _():
        for buf in (x0, x1, x2):
            buf[...] = jnp.zeros_like(buf)
        car[...] = jnp.zeros_like(car)
        fin[...] = jnp.zeros_like(fin)

    shape = (SUBLANES, GB_STATE)
    top = lax.broadcasted_iota(I32, shape, 0) < N_PROMPT_SEQ
    arb = jnp.broadcast_to(ar_ref[0], shape)
    aib = jnp.broadcast_to(ai_ref[0], shape)
    cst[0] = jnp.where(top, 0.0, arb)
    cst[1] = jnp.where(top, 0.0, aib)
    cst[2] = jnp.where(top, arb, arb * arb - aib * aib)
    cst[3] = jnp.where(top, aib, 2.0 * arb * aib)

    def project_in_pieces(xa, n_pieces=8):
        u = jnp.concatenate([ua_ref[0], ua_ref[1]], axis=1).astype(BF16)
        lb = 2 * GB_LB // n_pieces

        def piece(k):
            x = jnp.dot(u, bm_ref[0, :, k * lb * LANES:(k + 1) * lb * LANES],
                        preferred_element_type=F32)
            for c in range(lb):
                xa[k * lb + c] = x[:, c * LANES:(c + 1) * LANES]
        return [functools.partial(piece, k) for k in range(n_pieces)]

    def project_out_pieces(xc, n_pieces=8):
        lb = 2 * GB_LB // n_pieces
        acc = []

        def piece(k):
            h = jnp.concatenate([xc[k * lb + c] for c in range(lb)], axis=1).astype(BF16)
            part = jnp.dot(h, cm_ref[0, k * lb * LANES:(k + 1) * lb * LANES, :],
                           preferred_element_type=F32)
            acc[:] = [part if not acc else acc[0] + part]

        def final():
            u = jnp.concatenate([uc_ref[0], uc_ref[1]], axis=1)
            act_ref[...] = jax.nn.gelu(acc[0] + dsk_ref[0] * u).astype(BF16)
        return [functools.partial(piece, k) for k in range(n_pieces)], final

    def recur(xs, between=()):
        every = (TM // SUBLANES) // max(len(between), 1)
        top1 = lax.broadcasted_iota(I32, (SUBLANES, LANES), 0) < N_PROMPT_SEQ
        first = t_s == 0
        hr = [jnp.where(first, 0.0, car[:, c * LANES:(c + 1) * LANES]) for c in range(GB_LB)]
        hi = [jnp.where(first, 0.0, car[:, GB_STATE + c * LANES:GB_STATE + (c + 1) * LANES])
              for c in range(GB_LB)]
        for r in range(TM // SUBLANES):
            if between and r % every == 0:
                between[r // every]()
            rows = slice(r * SUBLANES, (r + 1) * SUBLANES)
            for c in range(GB_LB):
                cols = slice(c * LANES, (c + 1) * LANES)
                xr = xs[c, rows, :]
                xi = xs[GB_LB + c, rows, :]
                a1r, a1i, a2r, a2i = cst[0, :, cols], cst[1, :, cols], cst[2, :, cols], cst[3, :, cols]
                sxr = pltpu.roll(xr, N_PROMPT_SEQ, axis=0)
                sxi = pltpu.roll(xi, N_PROMPT_SEQ, axis=0)
                tr = xr + a1r * sxr - a1i * sxi
                ti = xi + a1r * sxi + a1i * sxr
                nr = tr + a2r * hr[c] - a2i * hi[c]
                ni = ti + a2r * hi[c] + a2i * hr[c]
                xs[c, rows, :] = nr
                xs[GB_LB + c, rows, :] = ni
                hr[c] = jnp.where(top1, pltpu.roll(nr, N_PROMPT_SEQ, axis=0), nr)
                hi[c] = jnp.where(top1, pltpu.roll(ni, N_PROMPT_SEQ, axis=0), ni)
        state = jnp.concatenate(hr + hi, axis=1)
        car[...] = state
        last = jnp.logical_and(valid_s, t_s == N_PROMPT_TILES - 1)
        fin[...] = jnp.where(last, state, fin[...])
        pfr_ref[0] = fin[N_PROMPT_SEQ:SUBLANES, 0:GB_STATE]
        pfi_ref[0] = fin[N_PROMPT_SEQ:SUBLANES, GB_STATE:2 * GB_STATE]

    bufs = (x0, x1, x2)
    for r in range(SCAN_STAGES):
        @pl.when(n % SCAN_STAGES == r)
        def _():
            ins = project_in_pieces(bufs[r])
            outs, finish_out = project_out_pieces(bufs[(r + 1) % SCAN_STAGES])
            recur(bufs[(r + 2) % SCAN_STAGES], [p for pair in zip(outs, ins) for p in pair])
            finish_out()


def _scan_prompt(u3, bm, cm, ar, ai, dsk):
    q_a = lambda n: jnp.minimum(n, N_SCAN_TILES - 1)
    q_s = lambda n: jnp.clip(n - 1, 0, N_SCAN_TILES - 1)
    q_c = lambda n: jnp.clip(n - 2, 0, N_SCAN_TILES - 1)
    gb = lambda q: q // N_PROMPT_TILES
    tt = lambda q: q % N_PROMPT_TILES
    u_spec = lambda q: pl.BlockSpec((GB_IN // LANES, TM, LANES), lambda n: (gb(q(n)), tt(q(n)), 0))
    gb3 = lambda last, q: pl.BlockSpec((1,) + last, lambda n: (gb(q(n)), 0, 0))
    state = jax.ShapeDtypeStruct((GROUP_BLOCKS, N_PROMPT_SEQ, GB_STATE), F32)
    xbuf = pltpu.VMEM((2 * GB_LB, TM, LANES), F32)
    return pl.pallas_call(
        _scan_prompt_kernel,
        grid=(N_SCAN_TILES + SCAN_STAGES - 1,),
        in_specs=[
            u_spec(q_a), u_spec(q_c),
            gb3((GB_IN, 2 * GB_STATE), q_a),
            gb3((2 * GB_STATE, GB_IN), q_c),
            gb3((1, GB_STATE), q_s),
            gb3((1, GB_STATE), q_s),
            gb3((1, GB_IN), q_c),
        ],
        out_specs=[
            pl.BlockSpec((TM, GB_IN), lambda n: (tt(q_c(n)), gb(q_c(n)))),
            gb3((N_PROMPT_SEQ, GB_STATE), q_s),
            gb3((N_PROMPT_SEQ, GB_STATE), q_s),
        ],
        out_shape=(jax.ShapeDtypeStruct((N_PROMPT_TOK, D_SSM), BF16), state, state),
        scratch_shapes=[xbuf, xbuf, xbuf,
                        pltpu.VMEM((4, SUBLANES, GB_STATE), F32),
                        pltpu.VMEM((SUBLANES, 2 * GB_STATE), F32),
                        pltpu.VMEM((SUBLANES, 2 * GB_STATE), F32)],
        compiler_params=_cparams(("arbitrary",), 48),
        name="s5_scan_prompt",
    )(u3, u3, bm, cm, ar, ai, dsk)


def _scan_sample_kernel(u_ref, bm_ref, cm_ref, ar_ref, ai_ref, dsk_ref, s0r_ref, s0i_ref,
                        act_ref, sfr_ref, sfi_ref, xs_ref):
    half_lb = GB_LB // 2
    half_cols = half_lb * LANES
    u = jnp.concatenate([u_ref[0], u_ref[1]], axis=1)
    x = jnp.dot(u.astype(BF16), bm_ref[0], preferred_element_type=F32)
    for c in range(2 * GB_LB):
        xs_ref[c] = x[:, c * LANES:(c + 1) * LANES]

    def load_half(rows, half, imag):
        c0 = imag * GB_LB + half * half_lb
        return jnp.concatenate([xs_ref[c0 + c, rows, :] for c in range(half_lb)], axis=1)

    def store_half(rows, half, imag, val):
        c0 = imag * GB_LB + half * half_lb
        for c in range(half_lb):
            xs_ref[c0 + c, rows, :] = val[:, c * LANES:(c + 1) * LANES]

    shape = (N_SAMPLE_SEQ, half_cols)
    for half in range(2):
        cols = slice(half * half_cols, (half + 1) * half_cols)
        arb = jnp.broadcast_to(ar_ref[0][:, cols], shape)
        aib = jnp.broadcast_to(ai_ref[0][:, cols], shape)

        def body(k, carry):
            hr, hi = carry
            rows = pl.ds(pl.multiple_of(k * N_SAMPLE_SEQ, N_SAMPLE_SEQ), N_SAMPLE_SEQ)
            xr = load_half(rows, half, 0)
            xi = load_half(rows, half, 1)
            nhr = xr + arb * hr - aib * hi
            nhi = xi + arb * hi + aib * hr
            store_half(rows, half, 0, nhr)
            store_half(rows, half, 1, nhi)
            return nhr, nhi

        hr, hi = lax.fori_loop(0, SAMPLE_LEN, body, (s0r_ref[:, cols], s0i_ref[:, cols]))
        sfr_ref[:, cols] = hr
        sfi_ref[:, cols] = hi

    h = _lane_blocks(xs_ref).astype(BF16)
    y = jnp.dot(h, cm_ref[0], preferred_element_type=F32) + dsk_ref[0] * u
    act_ref[...] = jax.nn.gelu(y).astype(BF16)


def _scan_sample(u3, bm, cm, ar, ai, dsk, s0r, s0i):
    gb3 = lambda last: pl.BlockSpec((1,) + last, lambda g: (g, 0, 0))
    state = lambda: pl.BlockSpec((N_SAMPLE_SEQ, GB_STATE), lambda g: (0, g))
    sst = jax.ShapeDtypeStruct((N_SAMPLE_SEQ, N_SSM_GROUPS * SSM_STATE), F32)
    return pl.pallas_call(
        _scan_sample_kernel,
        grid=(GROUP_BLOCKS,),
        in_specs=[
            pl.BlockSpec((GB_IN // LANES, TM, LANES), lambda g: (g, N_PROMPT_TILES, 0)),
            gb3((GB_IN, 2 * GB_STATE)), gb3((2 * GB_STATE, GB_IN)),
            gb3((1, GB_STATE)), gb3((1, GB_STATE)), gb3((1, GB_IN)),
            state(), state(),
        ],
        out_specs=[pl.BlockSpec((TM, GB_IN), lambda g: (0, g)), state(), state()],
        out_shape=(jax.ShapeDtypeStruct((N_SAMPLE_TOK, D_SSM), BF16), sst, sst),
        scratch_shapes=[pltpu.VMEM((2 * GB_LB, TM, LANES), F32)],
        compiler_params=_cparams(("arbitrary",), 32),
        name="s5_scan_sample",
    )(u3, bm, cm, ar, ai, dsk, s0r, s0i)


CONV_ROWS = 64
EXT_ROWS = HIST_S + TM


def _conv_kernel(v_ref, hist_ref, w_ref, b_ref, lg_ref, lb_ref,
                 cv_ref, ncp_ref, ncs_ref, ext_ref, sh_ref, acc_ref):
    t = pl.program_id(0)

    def taps(n_seq):
        half = SUBLANES // 2
        if n_seq % SUBLANES:
            lo = HIST_S - HIST_P - SUBLANES
            sh_ref[:, lo:EXT_ROWS - SUBLANES, :] = ext_ref[:, lo + half:EXT_ROWS - half, :]

        def body(rb, _):
            r0 = pl.multiple_of(rb * CONV_ROWS, CONV_ROWS)
            for cb in range(CONV_LB):
                cols = slice(cb * LANES, (cb + 1) * LANES)
                acc = None
                for j in range(CONV_WIDTH):
                    start = HIST_S - (CONV_BUF - j) * n_seq
                    src = ext_ref
                    if start % SUBLANES:
                        src, start = sh_ref, start - half
                    rows = pl.ds(pl.multiple_of(r0 + start, SUBLANES), CONV_ROWS)
                    term = src[cb, rows, :] * w_ref[j:j + 1, cols]
                    acc = term if acc is None else acc + term
                acc_ref[cb, pl.ds(r0, CONV_ROWS), :] = acc + b_ref[:, cols]
            return 0

        lax.fori_loop(0, TM // CONV_ROWS, body, 0)

    @pl.when(t == 0)
    def _():
        ext_ref[:, 0:HIST_S, :] = jnp.zeros((CONV_LB, HIST_S, LANES), F32)

    @pl.when(t == N_PROMPT_TILES)
    def _():
        for c in range(CONV_LB):
            ext_ref[c, 0:HIST_S, :] = hist_ref[:, c * LANES:(c + 1) * LANES]

    ext_ref[:, HIST_S:EXT_ROWS, :] = v_ref[...]

    @pl.when(t < N_PROMPT_TILES)
    def _():
        taps(N_PROMPT_SEQ)

    @pl.when(t == N_PROMPT_TILES)
    def _():
        taps(N_SAMPLE_SEQ)

    y = acc_ref[...]
    mu = jnp.sum(jnp.sum(y, axis=0), axis=-1, keepdims=True) / D_CONV
    yc = y - mu[None]
    var = jnp.sum(jnp.sum(yc * yc, axis=0), axis=-1, keepdims=True) / D_CONV
    z = jax.nn.silu(yc * lax.rsqrt(var + EPS)[None] * lg_ref[...] + lb_ref[...])
    for c in range(CONV_LB):
        cv_ref[:, c * LANES:(c + 1) * LANES] = z[c].astype(BF16)

    @pl.when(t == N_PROMPT_TILES - 1)
    def _():
        for c in range(CONV_LB):
            ncp_ref[:, c * LANES:(c + 1) * LANES] = ext_ref[c, EXT_ROWS - HIST_P:EXT_ROWS, :]

    @pl.when(t < N_PROMPT_TILES - 1)
    def _():
        ext_ref[:, HIST_S - HIST_P:HIST_S, :] = ext_ref[:, EXT_ROWS - HIST_P:EXT_ROWS, :]

    @pl.when(t == N_PROMPT_TILES)
    def _():
        for c in range(CONV_LB):
            ncs_ref[:, c * LANES:(c + 1) * LANES] = ext_ref[c, EXT_ROWS - HIST_S:EXT_ROWS, :]


def _conv(v3, hist_s, conv_w, conv_b, ln_g, ln_b):
    row = lambda n: pl.BlockSpec((n, D_CONV), lambda t: (0, 0))
    lane3 = pl.BlockSpec((CONV_LB, 1, LANES), lambda t: (0, 0, 0))
    ext = pltpu.VMEM((CONV_LB, EXT_ROWS, LANES), F32)
    return pl.pallas_call(
        _conv_kernel,
        grid=(N_TILES,),
        in_specs=[pl.BlockSpec((CONV_LB, TM, LANES), lambda t: (0, t, 0)),
                  row(HIST_S), row(CONV_WIDTH), row(1), lane3, lane3],
        out_specs=[pl.BlockSpec((TM, D_CONV), lambda t: (t, 0)), row(HIST_P), row(HIST_S)],
        out_shape=(jax.ShapeDtypeStruct((N_TOK, D_CONV), BF16),
                   jax.ShapeDtypeStruct((HIST_P, D_CONV), F32),
                   jax.ShapeDtypeStruct((HIST_S, D_CONV), F32)),
        scratch_shapes=[ext, ext, pltpu.VMEM((CONV_LB, TM, LANES), F32)],
        compiler_params=_cparams(("arbitrary",), 48),
        name="conv_branch",
    )(v3, hist_s, conv_w, conv_b,
      ln_g.reshape(CONV_LB, 1, LANES), ln_b.reshape(CONV_LB, 1, LANES))


MERGE_COLS = 512
N_MERGE = D_MODEL // MERGE_COLS


def _merge_kernel(xn_ref, actp_ref, acts_ref, cv_ref, wga0_ref, wga1_ref, wgb0_ref, wgb1_ref,
                  wglu_ref, wco_ref, m_ref):
    i = pl.program_id(0)
    dot = functools.partial(jnp.dot, preferred_element_type=F32)
    xn = _lane_blocks(xn_ref).astype(BF16)
    act = jnp.where(i < N_PROMPT_TILES, actp_ref[...], acts_ref[...])
    cv = cv_ref[...]
    per_block = D_SSM // MERGE_COLS
    for c in range(N_MERGE):
        cols = pl.ds(c * MERGE_COLS, MERGE_COLS)
        in_block = pl.ds((c % per_block) * MERGE_COLS, MERGE_COLS)
        wga = (wga0_ref, wga1_ref)[c // per_block]
        wgb = (wgb0_ref, wgb1_ref)[c // per_block]
        ya = dot(act, wglu_ref[:, cols]) * jax.nn.sigmoid(
            dot(act, wglu_ref[:, pl.ds(D_MODEL + c * MERGE_COLS, MERGE_COLS)]))
        yb = dot(cv, wco_ref[:, cols])
        m = (jax.nn.sigmoid(dot(xn, wga[:, in_block])) * ya
             + jax.nn.sigmoid(dot(xn, wgb[:, in_block])) * yb)
        m_ref[:, cols] = m.astype(BF16)


def _merge(xn3, actp, acts, cv, w_in_bf, w_glu_bf, w_co_bf):
    ga0 = (D_SSM + 2 * D_CONV) // D_SSM
    wcols = lambda j: pl.BlockSpec((D_MODEL, D_SSM), lambda i: (0, j), pipeline_mode=pl.Buffered(1))
    return pl.pallas_call(
        _merge_kernel,
        grid=(N_TILES,),
        in_specs=[pl.BlockSpec((MODEL_LB, TM, LANES), lambda i: (0, i, 0)),
                  pl.BlockSpec((TM, D_SSM), lambda i: (jnp.minimum(i, N_PROMPT_TILES - 1), 0)),
                  _resident((N_SAMPLE_TOK, D_SSM)),
                  pl.BlockSpec((TM, D_CONV), lambda i: (i, 0)),
                  wcols(ga0), wcols(ga0 + 1), wcols(ga0 + 2), wcols(ga0 + 3),
                  _resident((D_SSM, 2 * D_MODEL)), _resident((D_CONV, D_MODEL))],
        out_specs=pl.BlockSpec((TM, D_MODEL), lambda i: (i, 0)),
        out_shape=jax.ShapeDtypeStruct((N_TOK, D_MODEL), BF16),
        compiler_params=_cparams(("arbitrary",), 58),
        name="gated_merge",
    )(xn3, actp, acts, cv, w_in_bf, w_in_bf, w_in_bf, w_in_bf, w_glu_bf, w_co_bf)


def _outproj_kernel(m_ref, x0_ref, x1_ref, x2_ref, x3_ref, xs_ref, wo_ref, g_ref, wr_ref, br_ref,
                    h_ref, xl_ref, route_ref, gw_ref, cnt_ref, xr_s, lg0, lg1, xb0, xb1):
    i = pl.program_id(0)
    tile = jnp.minimum(i, N_OT - 1)
    steps_p = TO // N_PROMPT_SEQ
    steps_s = TO // N_SAMPLE_SEQ

    @pl.when(i == 0)
    def _():
        lg1[...] = jnp.zeros_like(lg1)
        xb1[...] = jnp.zeros_like(xb1)

    @pl.when(tile < N_OT_PROMPT)
    def _():
        x = jnp.concatenate([x0_ref[...], x1_ref[...], x2_ref[...], x3_ref[...]], axis=0)
        _to_mixer_order(xr_s, x, N_PROMPT_SEQ, steps_p)

    @pl.when(tile >= N_OT_PROMPT)
    def _():
        k0 = (tile - N_OT_PROMPT) * steps_s
        x = jnp.concatenate(
            [xs_ref[pl.ds(pl.multiple_of(b * SAMPLE_LEN + k0, steps_s), steps_s), :]
             for b in range(N_SAMPLE_SEQ)], axis=0)
        _to_mixer_order(xr_s, x, N_SAMPLE_SEQ, steps_s)

    sets = ((lg0, xb0), (lg1, xb1))
    for r in range(2):
        @pl.when(i % 2 == r)
        def _():
            lg_prev, xb_prev = sets[1 - r]
            _alternate(
                _route_and_sort(lg_prev[...], xb_prev[...], xl_ref, route_ref, gw_ref, cnt_ref),
                _project(m_ref, xr_s, wo_ref, g_ref, wr_ref, br_ref, h_ref, *sets[r]))


def _alternate(*chains):
    chains = list(chains)
    while chains:
        for chain in list(chains):
            try:
                next(chain)
            except StopIteration:
                chains.remove(chain)


def _project(m_ref, xr_s, wo_ref, g_ref, wr_ref, br_ref, h_ref, lg_s, xb_s):
    dot = functools.partial(jnp.dot, preferred_element_type=F32)
    n_chunks = 4
    lb = MODEL_LB // n_chunks
    parts = []
    for c in range(n_chunks):
        cols = slice(c * lb * LANES, (c + 1) * lb * LANES)
        x = jnp.concatenate([xr_s[c * lb + k] for k in range(lb)], axis=1)
        part = x + dot(m_ref[...], wo_ref[:, cols])
        h_ref[:, cols] = part
        parts.append(part)
        yield
    h = jnp.concatenate(parts, axis=1)
    xn = _rmsnorm_rows(h, g_ref[...])
    x_hi = xn.astype(BF16)
    x_lo = (xn - x_hi.astype(F32)).astype(BF16)
    w = wr_ref[...]
    w_hi = w.astype(BF16)
    w_lo = (w - w_hi.astype(F32)).astype(BF16)
    xb_s[...] = x_hi
    yield
    both = dot(x_hi, jnp.concatenate([w_hi, w_lo], axis=1))
    yield
    lg_s[...] = (both[:, :ROUTE_LANES] + (dot(x_lo, w_hi) + both[:, ROUTE_LANES:])) + br_ref[...]


def _route_and_sort(logits, x_hi, xl_ref, route_ref, gw_ref, cnt_ref):
    dot = functools.partial(jnp.dot, preferred_element_type=F32)
    lane = lax.broadcasted_iota(I32, logits.shape, 1)
    neg = -jnp.inf
    first = lambda hit: jnp.min(jnp.where(hit, lane, ROUTE_LANES), axis=-1, keepdims=True)
    gmask = lane < N_EXP_GROUPS
    lg = jnp.where(gmask, logits, neg)
    gmax = jnp.max(lg, axis=-1, keepdims=True)
    gsel = first(lg == gmax)
    psum = jnp.sum(jnp.where(gmask, jnp.exp(logits - gmax), 0.0), axis=-1, keepdims=True)
    pg_sel = 1.0 / psum
    e_lane = lane - N_EXP_GROUPS
    emask = (e_lane >= 0) & (e_lane < N_EXPERTS) & ((e_lane // EXP_PER_GROUP) == gsel)
    le = jnp.where(emask, logits, neg)
    v1 = jnp.max(le, axis=-1, keepdims=True)
    i1 = first(le == v1)
    le2 = jnp.where(lane == i1, neg, le)
    v2 = jnp.max(le2, axis=-1, keepdims=True)
    i2 = first(le2 == v2)
    z = jnp.exp(v2 - v1)
    w1 = pg_sel / (1.0 + z)
    w2 = pg_sel * z / (1.0 + z)
    e1 = i1 - N_EXP_GROUPS
    e2 = i2 - N_EXP_GROUPS
    gw_ref[...] = jnp.where(lane == 0, w1, jnp.where(lane == 1, w2, 0.0))
    yield
    oh1 = lane == e1
    oh2 = lane == e2
    hits = jnp.where(oh1 | oh2, 1.0, 0.0)
    rr = lax.broadcasted_iota(I32, (TO, TO), 0)
    cc = lax.broadcasted_iota(I32, (TO, TO), 1)
    before = dot(jnp.where(cc < rr, 1.0, 0.0).astype(BF16), hits.astype(BF16))
    counts = jnp.sum(hits, axis=0, keepdims=True)
    groups = jnp.floor((counts + (RUN - 1.0)) * (1.0 / RUN))
    er = lax.broadcasted_iota(I32, (ROUTE_LANES, ROUTE_LANES), 0)
    ec = lax.broadcasted_iota(I32, (ROUTE_LANES, ROUTE_LANES), 1)
    groups_before = dot(jnp.broadcast_to(groups, (SUBLANES, ROUTE_LANES)).astype(BF16),
                        jnp.where(er < ec, 1.0, 0.0).astype(BF16))[0:1]
    local = groups_before * RUN + before
    lpos1 = jnp.sum(jnp.where(oh1, local, 0.0), axis=-1, keepdims=True)
    lpos2 = jnp.sum(jnp.where(oh2, local, 0.0), axis=-1, keepdims=True)
    route_ref[...] = jnp.where(lane == 0, e1, jnp.where(lane == 1, e2, jnp.where(
        lane == 2, lpos1.astype(I32), jnp.where(lane == 3, lpos2.astype(I32), 0))))
    cnt_ref[0] = counts.astype(I32)
    yield
    lpos_t = jnp.where(lane == 0, lpos1, jnp.where(lane == 1, lpos2, 0.0)).T.astype(I32)
    n_chunks = 4
    rows = LROWS // n_chunks
    for q in range(n_chunks):
        jrow = q * rows + lax.broadcasted_iota(I32, (rows, TO), 0)
        sel = (jrow == lpos_t[0:1, :]) | (jrow == lpos_t[1:2, :])
        xl_ref[q * rows:(q + 1) * rows, :] = dot(jnp.where(sel, 1.0, 0.0).astype(BF16), x_hi).astype(BF16)
        yield


def _outproj(m, xp, xs, w_out_bf, g, w_router, b_router):
    cur = lambda i: jnp.minimum(i, N_OT - 1)
    prev = lambda i: jnp.maximum(i - 1, 0)
    tile = lambda n: pl.BlockSpec((TO, n), lambda i: (cur(i), 0))
    routed = lambda n: pl.BlockSpec((TO, n), lambda i: (prev(i), 0))
    return pl.pallas_call(
        _outproj_kernel,
        grid=(N_OT + 1,),
        in_specs=[tile(D_MODEL)] + _x_tile_specs(TO // N_PROMPT_SEQ) + [
            _resident((D_MODEL, D_MODEL)), _resident((1, D_MODEL)),
            _resident((D_MODEL, ROUTE_LANES)), _resident((1, ROUTE_LANES))],
        out_specs=[tile(D_MODEL),
                   pl.BlockSpec((LROWS, D_MODEL), lambda i: (prev(i), 0)),
                   routed(ROUTE_LANES), routed(ROUTE_LANES),
                   pl.BlockSpec((1, 1, ROUTE_LANES), lambda i: (prev(i), 0, 0))],
        out_shape=(jax.ShapeDtypeStruct((N_TOK, D_MODEL), F32),
                   jax.ShapeDtypeStruct((N_OT * LROWS, D_MODEL), BF16),
                   jax.ShapeDtypeStruct((N_TOK, ROUTE_LANES), I32),
                   jax.ShapeDtypeStruct((N_TOK, ROUTE_LANES), F32),
                   jax.ShapeDtypeStruct((N_OT, 1, ROUTE_LANES), I32)),
        scratch_shapes=[pltpu.VMEM((MODEL_LB, TO, LANES), F32),
                        pltpu.VMEM((TO, ROUTE_LANES), F32), pltpu.VMEM((TO, ROUTE_LANES), F32),
                        pltpu.VMEM((TO, D_MODEL), BF16), pltpu.VMEM((TO, D_MODEL), BF16)],
        compiler_params=_cparams(("arbitrary",), 56),
        name="out_proj_router",
    )(m, xp, xp, xp, xp, xs, w_out_bf, g, w_router, b_router)


def _expert_kernel(te_ref, nrows_ref, nused_ref, psrc_ref, pdst_ref, plen_ref, tot_ref, kfirst_ref, kend_ref,
                   first_ref, par_ref, nexte_ref,
                   xl_hbm, wg_hbm, wu_hbm, wd_hbm, ol_hbm,
                   xbuf, obuf, zbuf, wg_s, wu_s, wd_s, wg_st, wu_st, wd_st, gsem, wsem, zsem, esem):
    i = pl.program_id(0)
    nused = nused_ref[0]
    slot = i % 2
    rows8 = lambda v: pl.multiple_of(v, RUN)

    def for_pieces(tile, fn):
        def body(k, _):
            idx = tile * N_OT + k
            n = plen_ref[idx]

            @pl.when(n > 0)
            def _():
                fn(rows8(psrc_ref[idx]), rows8(pdst_ref[idx]), rows8(n))
            return 0
        lax.fori_loop(kfirst_ref[tile], kend_ref[tile], body, 0)

    def gather(tile, s):
        for_pieces(tile, lambda src, dst, n: pltpu.make_async_copy(
            xl_hbm.at[pl.ds(src, n)], xbuf.at[s, pl.ds(dst, n)], gsem.at[s]).start())

    def scatter(tile, s):
        for_pieces(tile, lambda src, dst, n: pltpu.make_async_copy(
            obuf.at[s, pl.ds(dst, n)], ol_hbm.at[pl.ds(src, n)], wsem.at[s]).start())

    def wait_gather(tile, s):
        n = rows8(nrows_ref[tile])
        pltpu.make_async_copy(xl_hbm.at[pl.ds(0, n)], xbuf.at[s, pl.ds(0, n)], gsem.at[s]).wait()

    def wait_scatter(tile, s):
        n = rows8(nrows_ref[tile])
        pltpu.make_async_copy(obuf.at[s, pl.ds(0, n)], ol_hbm.at[pl.ds(0, n)], wsem.at[s]).wait()

    def tail_copy(t):
        used = rows8(tot_ref[t])
        n = rows8(LROWS - tot_ref[t])
        dst = rows8(t * LROWS + used)
        return pltpu.make_async_copy(zbuf.at[pl.ds(0, n)], ol_hbm.at[pl.ds(dst, n)], zsem.at[0])

    def weight_copies(e, b):
        return [pltpu.make_async_copy(src.at[e], dst.at[b], esem.at[b, k])
                for k, (src, dst) in enumerate(((wg_hbm, wg_st), (wu_hbm, wu_st), (wd_hbm, wd_st)))]

    @pl.when(i == 0)
    def _():
        for c in weight_copies(te_ref[0], par_ref[0]):
            c.start()
        zbuf[...] = jnp.zeros_like(zbuf)
        xbuf[...] = jnp.zeros_like(xbuf)

        def start(t, _):
            tail_copy(t).start()
            return 0

        def wait(t, _):
            tail_copy(t).wait()
            return 0
        lax.fori_loop(0, N_OT, start, 0)
        lax.fori_loop(0, N_OT, wait, 0)
        gather(0, 0)

    @pl.when(i + 1 < nused)
    def _():
        gather(i + 1, 1 - slot)

    @pl.when(i < nused)
    def _():
        @pl.when(first_ref[i] == 1)
        def _():
            b = par_ref[i]
            for c in weight_copies(te_ref[i], b):
                c.wait()
            wg_s[...] = wg_st[b].astype(BF16)
            wu_s[...] = wu_st[b].astype(BF16)
            wd_s[...] = wd_st[b].astype(BF16)

            @pl.when(nexte_ref[i] >= 0)
            def _():
                for c in weight_copies(nexte_ref[i], 1 - b):
                    c.start()

        wait_gather(i, slot)

        @pl.when(i >= 2)
        def _():
            wait_scatter(i - 2, slot)

        xb = xbuf[slot]
        hg = jnp.dot(xb, wg_s[...], preferred_element_type=F32)
        hu = jnp.dot(xb, wu_s[...], preferred_element_type=F32)
        hid = jax.nn.silu(hg) * hu
        obuf[slot] = jnp.dot(hid.astype(BF16), wd_s[...], preferred_element_type=F32).astype(BF16)
        scatter(i, slot)

        @pl.when(i == nused - 1)
        def _():
            @pl.when(i >= 1)
            def _():
                wait_scatter(i - 1, 1 - slot)
            wait_scatter(i, slot)


def _experts(plan, xl, wg, wu, wd):
    anyspec = pl.BlockSpec(memory_space=pl.ANY)
    grid_spec = pltpu.PrefetchScalarGridSpec(
        num_scalar_prefetch=len(plan),
        grid=(N_ETILES,),
        in_specs=[anyspec, anyspec, anyspec, anyspec],
        out_specs=anyspec,
        scratch_shapes=[
            pltpu.VMEM((2, TME, D_MODEL), BF16),
            pltpu.VMEM((2, TME, D_MODEL), BF16),
            pltpu.VMEM((LROWS - 2 * TO, D_MODEL), BF16),
            pltpu.VMEM((D_MODEL, D_EXPERT), BF16),
            pltpu.VMEM((D_MODEL, D_EXPERT), BF16),
            pltpu.VMEM((D_EXPERT, D_MODEL), BF16),
            pltpu.VMEM((2, D_MODEL, D_EXPERT), F32),
            pltpu.VMEM((2, D_MODEL, D_EXPERT), F32),
            pltpu.VMEM((2, D_EXPERT, D_MODEL), F32),
            pltpu.SemaphoreType.DMA((2,)),
            pltpu.SemaphoreType.DMA((2,)),
            pltpu.SemaphoreType.DMA((1,)),
            pltpu.SemaphoreType.DMA((2, 3)),
        ],
    )
    return pl.pallas_call(
        _expert_kernel,
        grid_spec=grid_spec,
        out_shape=jax.ShapeDtypeStruct((N_OT * LROWS, D_MODEL), BF16),
        compiler_params=_cparams(("arbitrary",), 40),
        name="routed_experts",
    )(*plan, xl, wg, wu, wd)


def _combine_kernel(h_ref, route_ref, gw_ref, ol_ref, g_ref, yp_ref, ys_ref, y_s):
    i = pl.program_id(0)
    steps_p = TO // N_PROMPT_SEQ
    steps_s = TO // N_SAMPLE_SEQ
    route = route_ref[...]
    gw = gw_ref[...]
    jcol = lax.broadcasted_iota(I32, (TO, LROWS), 1)
    mix = (jnp.where(jcol == route[:, 2:3], gw[:, 0:1], 0.0)
           + jnp.where(jcol == route[:, 3:4], gw[:, 1:2], 0.0))
    moe = jnp.dot(mix.astype(BF16), ol_ref[...], preferred_element_type=F32)
    y = _rmsnorm_rows(h_ref[...] + moe, g_ref[...])
    for c in range(MODEL_LB):
        y_s[c] = y[:, c * LANES:(c + 1) * LANES]

    @pl.when(i < N_OT_PROMPT)
    def _():
        for b in range(N_PROMPT_SEQ):
            yp_ref[b] = _seq_rows(y_s, b, N_PROMPT_SEQ, steps_p)

    @pl.when(i >= N_OT_PROMPT)
    def _():
        k0 = (i - N_OT_PROMPT) * steps_s
        for b in range(N_SAMPLE_SEQ):
            rows = pl.ds(pl.multiple_of(b * SAMPLE_LEN + k0, steps_s), steps_s)
            ys_ref[rows, :] = _seq_rows(y_s, b, N_SAMPLE_SEQ, steps_s)


def _combine(h, route, gw, ol, g):
    tile = lambda n: pl.BlockSpec((TO, n), lambda i: (i, 0))
    return pl.pallas_call(
        _combine_kernel,
        grid=(N_OT,),
        in_specs=[tile(D_MODEL), tile(ROUTE_LANES), tile(ROUTE_LANES),
                  pl.BlockSpec((LROWS, D_MODEL), lambda i: (i, 0)),
                  pl.BlockSpec((1, D_MODEL), lambda i: (0, 0))],
        out_specs=[
            pl.BlockSpec((N_PROMPT_SEQ, TO // N_PROMPT_SEQ, D_MODEL),
                         lambda i: (0, jnp.minimum(i, N_OT_PROMPT - 1), 0)),
            pl.BlockSpec((N_SAMPLE_TOK, D_MODEL), lambda i: (0, 0)),
        ],
        out_shape=(jax.ShapeDtypeStruct((N_PROMPT_SEQ, PROMPT_LEN, D_MODEL), F32),
                   jax.ShapeDtypeStruct((N_SAMPLE_TOK, D_MODEL), F32)),
        scratch_shapes=[pltpu.VMEM((MODEL_LB, TO, LANES), F32)],
        compiler_params=_cparams(("arbitrary",), 48),
        name="combine_norm",
    )(h, route, gw, ol, g)


def _dispatch_plan(cnt):
    experts = jnp.arange(N_EXPERTS, dtype=I32)
    cnt = cnt[:, 0, :N_EXPERTS]
    run = ((cnt + RUN - 1) // RUN) * RUN
    local_start = jnp.cumsum(run, axis=1) - run
    local_used = jnp.sum(run, axis=1)
    total = jnp.sum(run, axis=0)
    padded = ((total + TME - 1) // TME) * TME
    ends = jnp.cumsum(padded)
    starts = ends - padded
    run_start = starts[None, :] + jnp.cumsum(run, axis=0) - run
    n_used = (ends[-1] // TME).astype(I32)
    tile = jnp.arange(N_ETILES, dtype=I32)
    tile_start = tile * TME
    te = jnp.minimum(jnp.sum((ends[None, :] <= tile_start[:, None]).astype(I32), axis=1), N_EXPERTS - 1)
    mine = (te[:, None] == experts[None, :])[:, None, :]
    of_tile = lambda a: jnp.sum(jnp.where(mine, a[None], 0), axis=-1)
    s0 = of_tile(run_start)
    lo = jnp.maximum(s0, tile_start[:, None])
    hi = jnp.minimum(s0 + of_tile(run), tile_start[:, None] + TME)
    plen = jnp.where((tile < n_used)[:, None], jnp.maximum(hi - lo, 0), 0)
    live = plen > 0
    psrc = jnp.where(live, jnp.arange(N_OT, dtype=I32)[None, :] * LROWS + of_tile(local_start) + lo - s0, 0)
    pdst = jnp.where(live, lo - tile_start[:, None], 0)
    last = jnp.sum(jnp.where(tile == n_used - 1, te, 0))
    te = jnp.where(tile < n_used, te, last)
    flat = lambda a: a.reshape(-1).astype(I32)
    none_yet = lambda a: jnp.sum((jnp.cumsum(a.astype(I32), axis=1) == 0).astype(I32), axis=1)
    k_first = none_yet(live)
    k_end = N_OT - none_yet(live[:, ::-1])
    first = jnp.logical_and(tile < n_used, jnp.concatenate([jnp.array([True]), te[1:] != te[:-1]]))
    parity = (jnp.cumsum(first.astype(I32)) - 1) % 2
    next_first = lax.cummin(jnp.where(first, tile, N_ETILES), axis=0, reverse=True)
    next_first = jnp.concatenate([next_first[1:], jnp.array([N_ETILES], I32)])
    next_expert = jnp.where(next_first < N_ETILES, te[jnp.minimum(next_first, N_ETILES - 1)], -1)
    return (te.astype(I32), flat(jnp.sum(plen, axis=1)), n_used.reshape(1),
            flat(psrc), flat(pdst), flat(plen), local_used.astype(I32), flat(k_first), flat(k_end),
            flat(first), flat(parity), flat(next_expert))


def kernel(x_prompt, x_sample, state_ssm_re, state_ssm_im, cache_conv, norm_mix_g, w_in, lam_re, lam_im, log_dt, b_re, b_im, c_re, c_im, d_skip, w_ssm_glu, conv_w, conv_b, conv_ln_g, conv_ln_b, w_conv_out, w_out, norm_ffn_g, w_router_group, b_router_group, w_router_expert, b_router_expert, w_exp_gate, w_exp_up, w_exp_down, norm_final_g):
    assert w_in.shape[0] == 1, "single-layer trunk"
    xp = x_prompt.reshape(N_PROMPT_TOK, D_MODEL)
    xs = x_sample.reshape(N_SAMPLE_TOK, D_MODEL)
    w_in_bf = w_in[0].astype(BF16)
    row = lambda a: a.reshape(1, -1)

    ar_rep, ai_rep, bbr, bbi = _discretise(lam_re[0], lam_im[0], log_dt[0], b_re[0], b_im[0])
    pick = lambda a: a[:, :SSM_STATE].reshape(GROUP_BLOCKS, 1, GB_STATE)
    bm = jnp.concatenate([_block_diag_in(bbr), _block_diag_in(bbi)], axis=-1).astype(BF16)
    cm = jnp.concatenate([_block_diag_out(c_re[0]), -_block_diag_out(c_im[0])], axis=1).astype(BF16)
    dsk = d_skip[0].reshape(GROUP_BLOCKS, 1, GB_IN)
    ar, ai = pick(ar_rep), pick(ai_rep)
    s0r = state_ssm_re[0].reshape(N_SAMPLE_SEQ, -1)
    s0i = state_ssm_im[0].reshape(N_SAMPLE_SEQ, -1)

    u3, v3, xn3 = _inproj(xp, xs, row(norm_mix_g[0]), w_in_bf)
    actp, pfr, pfi = _scan_prompt(u3, bm, cm, ar, ai, dsk)
    acts, sfr, sfi = _scan_sample(u3, bm, cm, ar, ai, dsk, s0r, s0i)

    hist_s = cache_conv[0].transpose(1, 0, 2).reshape(HIST_S, D_CONV)
    cv, ncp, ncs = _conv(v3, hist_s, conv_w[0], row(conv_b[0]), conv_ln_g[0], conv_ln_b[0])

    m = _merge(xn3, actp, acts, cv, w_in_bf, w_ssm_glu[0].astype(BF16), w_conv_out[0].astype(BF16))

    pad_lanes = ROUTE_LANES - N_EXP_GROUPS - N_EXPERTS
    w_router = jnp.concatenate(
        [w_router_group[0], w_router_expert[0], jnp.zeros((D_MODEL, pad_lanes), F32)], axis=1)
    b_router = jnp.concatenate(
        [b_router_group[0], b_router_expert[0], jnp.zeros((pad_lanes,), F32)]).reshape(1, ROUTE_LANES)
    h, xl, route, gw, cnt = _outproj(m, xp, xs, w_out[0].astype(BF16), row(norm_ffn_g[0]),
                                     w_router, b_router)

    ol = _experts(_dispatch_plan(cnt), xl, w_exp_gate[0], w_exp_up[0], w_exp_down[0])
    yp, ys = _combine(h, route, gw, ol, row(norm_final_g))

    st = lambda a, n: a.reshape(1, n, N_SSM_GROUPS, SSM_STATE)
    pst = lambda a: st(a.transpose(1, 0, 2), N_PROMPT_SEQ)
    ncp = ncp.reshape(CONV_BUF, N_PROMPT_SEQ, D_CONV).transpose(1, 0, 2)[None]
    ncs = ncs.reshape(CONV_BUF, N_SAMPLE_SEQ, D_CONV).transpose(1, 0, 2)[None]
    return (yp, ys.reshape(N_SAMPLE_SEQ, SAMPLE_LEN, D_MODEL), pst(pfr), pst(pfi), ncp,
            st(sfr, N_SAMPLE_SEQ), st(sfi, N_SAMPLE_SEQ), ncs)
```

```python
import functools

import jax
import jax.numpy as jnp
from jax import lax
from jax.experimental import pallas as pl
from jax.experimental.pallas import tpu as pltpu

F32 = jnp.float32
BF16 = jnp.bfloat16
I32 = jnp.int32

SUBLANES = 8
LANES = 128

D_MODEL = 2048
D_SSM = 1024
D_CONV = 1024
SSM_GROUP = 16
N_SSM_GROUPS = 64
SSM_STATE = 64
CONV_WIDTH = 31
CONV_BUF = CONV_WIDTH - 1
N_EXP_GROUPS = 4
EXP_PER_GROUP = 8
N_EXPERTS = 32
D_EXPERT = 256
EPS = 1e-6

N_PROMPT_SEQ = 4
PROMPT_LEN = 2048
N_SAMPLE_SEQ = 16
SAMPLE_LEN = 32
N_PROMPT_TOK = N_PROMPT_SEQ * PROMPT_LEN
N_SAMPLE_TOK = N_SAMPLE_SEQ * SAMPLE_LEN
N_TOK = N_PROMPT_TOK + N_SAMPLE_TOK

TM = 512
N_PROMPT_TILES = N_PROMPT_TOK // TM
N_TILES = N_TOK // TM
PROMPT_STEPS = TM // N_PROMPT_SEQ
GROUP_BLOCKS = 4
GROUPS_PER_BLOCK = N_SSM_GROUPS // GROUP_BLOCKS
GB_IN = GROUPS_PER_BLOCK * SSM_GROUP
GB_STATE = GROUPS_PER_BLOCK * SSM_STATE
GB_LB = GB_STATE // LANES
HIST_P = CONV_BUF * N_PROMPT_SEQ
HIST_S = CONV_BUF * N_SAMPLE_SEQ
ROUTE_LANES = 128
N_PAIRS = 2 * N_TOK
TO = 256
N_OT = N_TOK // TO
N_OT_PROMPT = N_PROMPT_TOK // TO
HALVES = TM // TO
RUN = 16
LROWS = 1024
TME = 512
N_ETILES = -(-(N_PAIRS + N_OT * N_EXPERTS * (RUN - 1) + N_EXPERTS * (TME - 1)) // TME)
assert 2 * TO + (RUN - 1) * N_EXPERTS <= LROWS
MODEL_LB = D_MODEL // LANES
SSM_LB = D_SSM // LANES
CONV_LB = D_CONV // LANES


def _cparams(sem, vmem_mb):
    return pltpu.CompilerParams(dimension_semantics=sem, vmem_limit_bytes=vmem_mb * 1024 * 1024)


def _resident(shape):
    return pl.BlockSpec(shape, lambda *_: (0,) * len(shape), pipeline_mode=pl.Buffered(1))


def _x_tile_specs(steps=PROMPT_STEPS):
    blocks_per_seq = PROMPT_LEN // steps
    n_prompt = N_PROMPT_SEQ * blocks_per_seq // N_PROMPT_SEQ

    def prompt(b):
        return pl.BlockSpec(
            (steps, D_MODEL),
            lambda i, *_: (b * blocks_per_seq + jnp.minimum(i, n_prompt - 1), 0))

    return [prompt(b) for b in range(N_PROMPT_SEQ)] + [_resident((N_SAMPLE_TOK, D_MODEL))]


def _to_mixer_order(ref3, val, n_seq, steps, at=0):
    for c in range(val.shape[1] // LANES):
        for b in range(n_seq):
            ref3[at + c, pl.ds(b, steps, stride=n_seq), :] = val[b * steps:(b + 1) * steps,
                                                                 c * LANES:(c + 1) * LANES]


def _seq_rows(ref3, b, n_seq, steps):
    return jnp.concatenate(
        [ref3[c, pl.ds(b, steps, stride=n_seq), :] for c in range(ref3.shape[0])], axis=1)


def _tile_to_mixer_order(i, ref3, xs, sample):
    @pl.when(i < N_PROMPT_TILES)
    def _():
        _to_mixer_order(ref3, xs(), N_PROMPT_SEQ, PROMPT_STEPS)

    @pl.when(i >= N_PROMPT_TILES)
    def _():
        _to_mixer_order(ref3, sample(), N_SAMPLE_SEQ, SAMPLE_LEN)


def _lane_blocks(ref3):
    return jnp.concatenate([ref3[c] for c in range(ref3.shape[0])], axis=1)


def _pair_specs(rows, cols, n_prompt_tiles):
    p = pl.BlockSpec((rows, cols), lambda i, *_: (jnp.minimum(i, n_prompt_tiles - 1), 0))
    s = pl.BlockSpec((rows, cols), lambda i, *_: (jnp.maximum(i - n_prompt_tiles, 0), 0))
    return p, s


def _disc_kernel(lr_ref, li_ref, ldt_ref, br_ref, bi_ref, ar_ref, ai_ref, bbr_ref, bbi_ref):
    lr = lr_ref[...]
    li = li_ref[...]
    dt = jnp.exp(ldt_ref[...])
    mag = jnp.exp(lr * dt)
    ar = mag * jnp.cos(li * dt)
    ai = mag * jnp.sin(li * dt)
    den = lr * lr + li * li
    nr = ar - 1.0
    cr = (nr * lr + ai * li) / den
    ci = (ai * lr - nr * li) / den
    br = br_ref[...]
    bi = bi_ref[...]
    ar_ref[...] = ar
    ai_ref[...] = ai
    bbr_ref[...] = cr * br - ci * bi
    bbi_ref[...] = cr * bi + ci * br


def _discretise(lam_re, lam_im, log_dt, b_re, b_im):
    shp = jax.ShapeDtypeStruct((N_SSM_GROUPS, SSM_STATE * SSM_GROUP), F32)
    rep = lambda a: jnp.tile(a, (1, SSM_GROUP))
    ldt = jnp.broadcast_to(log_dt[:, None], (N_SSM_GROUPS, SSM_STATE * SSM_GROUP))
    hp = lambda b: b.transpose(0, 2, 1).reshape(N_SSM_GROUPS, -1)
    return pl.pallas_call(_disc_kernel, out_shape=(shp, shp, shp, shp), name="s5_discretise")(
        rep(lam_re), rep(lam_im), ldt, hp(b_re), hp(b_im))


def _block_diag(blocks, rows_per_group, cols_per_group):
    wide = jnp.tile(blocks, (1, 1, GROUPS_PER_BLOCK))
    r = lax.broadcasted_iota(I32, wide.shape, 1) // rows_per_group
    c = lax.broadcasted_iota(I32, wide.shape, 2) // cols_per_group
    return jnp.where(r == c, wide, 0.0)


def _block_diag_in(bb):
    return _block_diag(bb.reshape(GROUP_BLOCKS, GB_IN, SSM_STATE), SSM_GROUP, SSM_STATE)


def _block_diag_out(c):
    return _block_diag(c.transpose(0, 2, 1).reshape(GROUP_BLOCKS, GB_STATE, SSM_GROUP),
                       SSM_STATE, SSM_GROUP)


def _rmsnorm_rows(x, g):
    r = lax.rsqrt(jnp.mean(x * x, axis=-1, keepdims=True) + EPS)
    return x * r * g


def _inproj_kernel(x0_ref, x1_ref, x2_ref, x3_ref, xs_ref, g_ref, wu_ref, wa_ref, wb_ref,
                   u_ref, v_ref, xn_ref):
    i = pl.program_id(0)
    dot = functools.partial(jnp.dot, preferred_element_type=F32)

    def tile(x, n_seq, steps):
        xn = _rmsnorm_rows(x, g_ref[...])
        xb = xn.astype(BF16)
        n_pieces = 4
        wide = D_SSM // n_pieces
        for k in range(n_pieces):
            cols = slice(k * wide, (k + 1) * wide)
            at = k * (wide // LANES)
            _to_mixer_order(u_ref, dot(xb, wu_ref[:, cols]), n_seq, steps, at)
            xcols = slice(k * D_MODEL // n_pieces, (k + 1) * D_MODEL // n_pieces)
            _to_mixer_order(xn_ref, xn[:, xcols], n_seq, steps, k * (MODEL_LB // n_pieces))
            v = dot(xb, wa_ref[:, cols]) * jax.nn.sigmoid(dot(xb, wb_ref[:, cols]))
            _to_mixer_order(v_ref, v, n_seq, steps, at)

    @pl.when(i < N_PROMPT_TILES)
    def _():
        x = jnp.concatenate([x0_ref[...], x1_ref[...], x2_ref[...], x3_ref[...]], axis=0)
        tile(x, N_PROMPT_SEQ, PROMPT_STEPS)

    @pl.when(i >= N_PROMPT_TILES)
    def _():
        tile(xs_ref[...], N_SAMPLE_SEQ, SAMPLE_LEN)


def _inproj(xp, xs, g, w_in_bf):
    blocked = lambda nb: pl.BlockSpec((nb, TM, LANES), lambda i: (0, i, 0))
    wcols = lambda j: pl.BlockSpec((D_MODEL, D_SSM), lambda i: (0, j), pipeline_mode=pl.Buffered(1))
    return pl.pallas_call(
        _inproj_kernel,
        grid=(N_TILES,),
        in_specs=_x_tile_specs() + [_resident((1, D_MODEL)), wcols(0), wcols(1), wcols(2)],
        out_specs=[blocked(SSM_LB), blocked(CONV_LB), blocked(MODEL_LB)],
        out_shape=(jax.ShapeDtypeStruct((SSM_LB, N_TOK, LANES), F32),
                   jax.ShapeDtypeStruct((CONV_LB, N_TOK, LANES), F32),
                   jax.ShapeDtypeStruct((MODEL_LB, N_TOK, LANES), F32)),
        compiler_params=_cparams(("arbitrary",), 56),
        name="in_proj",
    )(xp, xp, xp, xp, xs, g, w_in_bf, w_in_bf, w_in_bf)


N_SCAN_TILES = GROUP_BLOCKS * N_PROMPT_TILES
SCAN_STAGES = 3


def _scan_prompt_kernel(ua_ref, uc_ref, bm_ref, cm_ref, ar_ref, ai_ref, dsk_ref,
                        act_ref, pfr_ref, pfi_ref, x0, x1, x2, cst, car, fin):
    n = pl.program_id(0)
    q_s = jnp.clip(n - 1, 0, N_SCAN_TILES - 1)
    t_s = q_s % N_PROMPT_TILES
    valid_s = jnp.logical_and(n >= 1, n <= N_SCAN_TILES)

    @pl.when(n == 0)
    def _():
        for buf in (x0, x1, x2):
            buf[...] = jnp.zeros_like(buf)
        car[...] = jnp.zeros_like(car)
        fin[...] = jnp.zeros_like(fin)

    shape = (SUBLANES, GB_STATE)
    top = lax.broadcasted_iota(I32, shape, 0) < N_PROMPT_SEQ
    arb = jnp.broadcast_to(ar_ref[0], shape)
    aib = jnp.broadcast_to(ai_ref[0], shape)
    cst[0] = jnp.where(top, 0.0, arb)
    cst[1] = jnp.where(top, 0.0, aib)
    cst[2] = jnp.where(top, arb, arb * arb - aib * aib)
    cst[3] = jnp.where(top, aib, 2.0 * arb * aib)

    def project_in_pieces(xa, n_pieces=8):
        u = jnp.concatenate([ua_ref[0], ua_ref[1]], axis=1).astype(BF16)
        lb = 2 * GB_LB // n_pieces

        def piece(k):
            x = jnp.dot(u, bm_ref[0, :, k * lb * LANES:(k + 1) * lb * LANES],
                        preferred_element_type=F32)
            for c in range(lb):
                xa[k * lb + c] = x[:, c * LANES:(c + 1) * LANES]
        return [functools.partial(piece, k) for k in range(n_pieces)]

    def project_out_pieces(xc, n_pieces=8):
        lb = 2 * GB_LB // n_pieces
        acc = []

        def piece(k):
            h = jnp.concatenate([xc[k * lb + c] for c in range(lb)], axis=1).astype(BF16)
            part = jnp.dot(h, cm_ref[0, k * lb * LANES:(k + 1) * lb * LANES, :],
                           preferred_element_type=F32)
            acc[:] = [part if not acc else acc[0] + part]

        def final():
            u = jnp.concatenate([uc_ref[0], uc_ref[1]], axis=1)
            act_ref[...] = jax.nn.gelu(acc[0] + dsk_ref[0] * u).astype(BF16)
        return [functools.partial(piece, k) for k in range(n_pieces)], final

    def recur(xs, between=()):
        every = (TM // SUBLANES) // max(len(between), 1)
        top1 = lax.broadcasted_iota(I32, (SUBLANES, LANES), 0) < N_PROMPT_SEQ
        first = t_s == 0
        hr = [jnp.where(first, 0.0, car[:, c * LANES:(c + 1) * LANES]) for c in range(GB_LB)]
        hi = [jnp.where(first, 0.0, car[:, GB_STATE + c * LANES:GB_STATE + (c + 1) * LANES])
              for c in range(GB_LB)]
        for r in range(TM // SUBLANES):
            if between and r % every == 0:
                between[r // every]()
            rows = slice(r * SUBLANES, (r + 1) * SUBLANES)
            for c in range(GB_LB):
                cols = slice(c * LANES, (c + 1) * LANES)
                xr = xs[c, rows, :]
                xi = xs[GB_LB + c, rows, :]
                a1r, a1i, a2r, a2i = cst[0, :, cols], cst[1, :, cols], cst[2, :, cols], cst[3, :, cols]
                sxr = pltpu.roll(xr, N_PROMPT_SEQ, axis=0)
                sxi = pltpu.roll(xi, N_PROMPT_SEQ, axis=0)
                tr = xr + a1r * sxr - a1i * sxi
                ti = xi + a1r * sxi + a1i * sxr
                nr = tr + a2r * hr[c] - a2i * hi[c]
                ni = ti + a2r * hi[c] + a2i * hr[c]
                xs[c, rows, :] = nr
                xs[GB_LB + c, rows, :] = ni
                hr[c] = jnp.where(top1, pltpu.roll(nr, N_PROMPT_SEQ, axis=0), nr)
                hi[c] = jnp.where(top1, pltpu.roll(ni, N_PROMPT_SEQ, axis=0), ni)
        state = jnp.concatenate(hr + hi, axis=1)
        car[...] = state
        last = jnp.logical_and(valid_s, t_s == N_PROMPT_TILES - 1)
        fin[...] = jnp.where(last, state, fin[...])
        pfr_ref[0] = fin[N_PROMPT_SEQ:SUBLANES, 0:GB_STATE]
        pfi_ref[0] = fin[N_PROMPT_SEQ:SUBLANES, GB_STATE:2 * GB_STATE]

    bufs = (x0, x1, x2)
    for r in range(SCAN_STAGES):
        @pl.when(n % SCAN_STAGES == r)
        def _():
            ins = project_in_pieces(bufs[r])
            outs, finish_out = project_out_pieces(bufs[(r + 1) % SCAN_STAGES])
            recur(bufs[(r + 2) % SCAN_STAGES], [p for pair in zip(outs, ins) for p in pair])
            finish_out()


def _scan_prompt(u3, bm, cm, ar, ai, dsk):
    q_a = lambda n: jnp.minimum(n, N_SCAN_TILES - 1)
    q_s = lambda n: jnp.clip(n - 1, 0, N_SCAN_TILES - 1)
    q_c = lambda n: jnp.clip(n - 2, 0, N_SCAN_TILES - 1)
    gb = lambda q: q // N_PROMPT_TILES
    tt = lambda q: q % N_PROMPT_TILES
    u_spec = lambda q: pl.BlockSpec((GB_IN // LANES, TM, LANES), lambda n: (gb(q(n)), tt(q(n)), 0))
    gb3 = lambda last, q: pl.BlockSpec((1,) + last, lambda n: (gb(q(n)), 0, 0))
    state = jax.ShapeDtypeStruct((GROUP_BLOCKS, N_PROMPT_SEQ, GB_STATE), F32)
    xbuf = pltpu.VMEM((2 * GB_LB, TM, LANES), F32)
    return pl.pallas_call(
        _scan_prompt_kernel,
        grid=(N_SCAN_TILES + SCAN_STAGES - 1,),
        in_specs=[
            u_spec(q_a), u_spec(q_c),
            gb3((GB_IN, 2 * GB_STATE), q_a),
            gb3((2 * GB_STATE, GB_IN), q_c),
            gb3((1, GB_STATE), q_s),
            gb3((1, GB_STATE), q_s),
            gb3((1, GB_IN), q_c),
        ],
        out_specs=[
            pl.BlockSpec((TM, GB_IN), lambda n: (tt(q_c(n)), gb(q_c(n)))),
            gb3((N_PROMPT_SEQ, GB_STATE), q_s),
            gb3((N_PROMPT_SEQ, GB_STATE), q_s),
        ],
        out_shape=(jax.ShapeDtypeStruct((N_PROMPT_TOK, D_SSM), BF16), state, state),
        scratch_shapes=[xbuf, xbuf, xbuf,
                        pltpu.VMEM((4, SUBLANES, GB_STATE), F32),
                        pltpu.VMEM((SUBLANES, 2 * GB_STATE), F32),
                        pltpu.VMEM((SUBLANES, 2 * GB_STATE), F32)],
        compiler_params=_cparams(("arbitrary",), 48),
        name="s5_scan_prompt",
    )(u3, u3, bm, cm, ar, ai, dsk)


def _scan_sample_kernel(u_ref, bm_ref, cm_ref, ar_ref, ai_ref, dsk_ref, s0r_ref, s0i_ref,
                        act_ref, sfr_ref, sfi_ref, xs_ref):
    half_lb = GB_LB // 2
    half_cols = half_lb * LANES
    u = jnp.concatenate([u_ref[0], u_ref[1]], axis=1)
    x = jnp.dot(u.astype(BF16), bm_ref[0], preferred_element_type=F32)
    for c in range(2 * GB_LB):
        xs_ref[c] = x[:, c * LANES:(c + 1) * LANES]

    def load_half(rows, half, imag):
        c0 = imag * GB_LB + half * half_lb
        return jnp.concatenate([xs_ref[c0 + c, rows, :] for c in range(half_lb)], axis=1)

    def store_half(rows, half, imag, val):
        c0 = imag * GB_LB + half * half_lb
        for c in range(half_lb):
            xs_ref[c0 + c, rows, :] = val[:, c * LANES:(c + 1) * LANES]

    shape = (N_SAMPLE_SEQ, half_cols)
    for half in range(2):
        cols = slice(half * half_cols, (half + 1) * half_cols)
        arb = jnp.broadcast_to(ar_ref[0][:, cols], shape)
        aib = jnp.broadcast_to(ai_ref[0][:, cols], shape)

        def body(k, carry):
            hr, hi = carry
            rows = pl.ds(pl.multiple_of(k * N_SAMPLE_SEQ, N_SAMPLE_SEQ), N_SAMPLE_SEQ)
            xr = load_half(rows, half, 0)
            xi = load_half(rows, half, 1)
            nhr = xr + arb * hr - aib * hi
            nhi = xi + arb * hi + aib * hr
            store_half(rows, half, 0, nhr)
            store_half(rows, half, 1, nhi)
            return nhr, nhi

        hr, hi = lax.fori_loop(0, SAMPLE_LEN, body, (s0r_ref[:, cols], s0i_ref[:, cols]))
        sfr_ref[:, cols] = hr
        sfi_ref[:, cols] = hi

    h = _lane_blocks(xs_ref).astype(BF16)
    y = jnp.dot(h, cm_ref[0], preferred_element_type=F32) + dsk_ref[0] * u
    act_ref[...] = jax.nn.gelu(y).astype(BF16)


def _scan_sample(u3, bm, cm, ar, ai, dsk, s0r, s0i):
    gb3 = lambda last: pl.BlockSpec((1,) + last, lambda g: (g, 0, 0))
    state = lambda: pl.BlockSpec((N_SAMPLE_SEQ, GB_STATE), lambda g: (0, g))
    sst = jax.ShapeDtypeStruct((N_SAMPLE_SEQ, N_SSM_GROUPS * SSM_STATE), F32)
    return pl.pallas_call(
        _scan_sample_kernel,
        grid=(GROUP_BLOCKS,),
        in_specs=[
            pl.BlockSpec((GB_IN // LANES, TM, LANES), lambda g: (g, N_PROMPT_TILES, 0)),
            gb3((GB_IN, 2 * GB_STATE)), gb3((2 * GB_STATE, GB_IN)),
            gb3((1, GB_STATE)), gb3((1, GB_STATE)), gb3((1, GB_IN)),
            state(), state(),
        ],
        out_specs=[pl.BlockSpec((TM, GB_IN), lambda g: (0, g)), state(), state()],
        out_shape=(jax.ShapeDtypeStruct((N_SAMPLE_TOK, D_SSM), BF16), sst, sst),
        scratch_shapes=[pltpu.VMEM((2 * GB_LB, TM, LANES), F32)],
        compiler_params=_cparams(("arbitrary",), 32),
        name="s5_scan_sample",
    )(u3, bm, cm, ar, ai, dsk, s0r, s0i)


CONV_ROWS = 64
EXT_ROWS = HIST_S + TM


def _conv_kernel(v_ref, hist_ref, w_ref, b_ref, lg_ref, lb_ref,
                 cv_ref, ncp_ref, ncs_ref, ext_ref, sh_ref, acc_ref):
    t = pl.program_id(0)

    def taps(n_seq):
        half = SUBLANES // 2
        if n_seq % SUBLANES:
            lo = HIST_S - HIST_P - SUBLANES
            sh_ref[:, lo:EXT_ROWS - SUBLANES, :] = ext_ref[:, lo + half:EXT_ROWS - half, :]

        def body(rb, _):
            r0 = pl.multiple_of(rb * CONV_ROWS, CONV_ROWS)
            for cb in range(CONV_LB):
                cols = slice(cb * LANES, (cb + 1) * LANES)
                acc = None
                for j in range(CONV_WIDTH):
                    start = HIST_S - (CONV_BUF - j) * n_seq
                    src = ext_ref
                    if start % SUBLANES:
                        src, start = sh_ref, start - half
                    rows = pl.ds(pl.multiple_of(r0 + start, SUBLANES), CONV_ROWS)
                    term = src[cb, rows, :] * w_ref[j:j + 1, cols]
                    acc = term if acc is None else acc + term
                acc_ref[cb, pl.ds(r0, CONV_ROWS), :] = acc + b_ref[:, cols]
            return 0

        lax.fori_loop(0, TM // CONV_ROWS, body, 0)

    @pl.when(t == 0)
    def _():
        ext_ref[:, 0:HIST_S, :] = jnp.zeros((CONV_LB, HIST_S, LANES), F32)

    @pl.when(t == N_PROMPT_TILES)
    def _():
        for c in range(CONV_LB):
            ext_ref[c, 0:HIST_S, :] = hist_ref[:, c * LANES:(c + 1) * LANES]

    ext_ref[:, HIST_S:EXT_ROWS, :] = v_ref[...]

    @pl.when(t < N_PROMPT_TILES)
    def _():
        taps(N_PROMPT_SEQ)

    @pl.when(t == N_PROMPT_TILES)
    def _():
        taps(N_SAMPLE_SEQ)

    y = acc_ref[...]
    mu = jnp.sum(jnp.sum(y, axis=0), axis=-1, keepdims=True) / D_CONV
    yc = y - mu[None]
    var = jnp.sum(jnp.sum(yc * yc, axis=0), axis=-1, keepdims=True) / D_CONV
    z = jax.nn.silu(yc * lax.rsqrt(var + EPS)[None] * lg_ref[...] + lb_ref[...])
    for c in range(CONV_LB):
        cv_ref[:, c * LANES:(c + 1) * LANES] = z[c].astype(BF16)

    @pl.when(t == N_PROMPT_TILES - 1)
    def _():
        for c in range(CONV_LB):
            ncp_ref[:, c * LANES:(c + 1) * LANES] = ext_ref[c, EXT_ROWS - HIST_P:EXT_ROWS, :]

    @pl.when(t < N_PROMPT_TILES - 1)
    def _():
        ext_ref[:, HIST_S - HIST_P:HIST_S, :] = ext_ref[:, EXT_ROWS - HIST_P:EXT_ROWS, :]

    @pl.when(t == N_PROMPT_TILES)
    def _():
        for c in range(CONV_LB):
            ncs_ref[:, c * LANES:(c + 1) * LANES] = ext_ref[c, EXT_ROWS - HIST_S:EXT_ROWS, :]


def _conv(v3, hist_s, conv_w, conv_b, ln_g, ln_b):
    row = lambda n: pl.BlockSpec((n, D_CONV), lambda t: (0, 0))
    lane3 = pl.BlockSpec((CONV_LB, 1, LANES), lambda t: (0, 0, 0))
    ext = pltpu.VMEM((CONV_LB, EXT_ROWS, LANES), F32)
    return pl.pallas_call(
        _conv_kernel,
        grid=(N_TILES,),
        in_specs=[pl.BlockSpec((CONV_LB, TM, LANES), lambda t: (0, t, 0)),
                  row(HIST_S), row(CONV_WIDTH), row(1), lane3, lane3],
        out_specs=[pl.BlockSpec((TM, D_CONV), lambda t: (t, 0)), row(HIST_P), row(HIST_S)],
        out_shape=(jax.ShapeDtypeStruct((N_TOK, D_CONV), BF16),
                   jax.ShapeDtypeStruct((HIST_P, D_CONV), F32),
                   jax.ShapeDtypeStruct((HIST_S, D_CONV), F32)),
        scratch_shapes=[ext, ext, pltpu.VMEM((CONV_LB, TM, LANES), F32)],
        compiler_params=_cparams(("arbitrary",), 48),
        name="conv_branch",
    )(v3, hist_s, conv_w, conv_b,
      ln_g.reshape(CONV_LB, 1, LANES), ln_b.reshape(CONV_LB, 1, LANES))


MERGE_COLS = 512
N_MERGE = D_MODEL // MERGE_COLS


def _merge_kernel(xn_ref, actp_ref, acts_ref, cv_ref, wga0_ref, wga1_ref, wgb0_ref, wgb1_ref,
                  wglu_ref, wco_ref, m_ref):
    i = pl.program_id(0)
    dot = functools.partial(jnp.dot, preferred_element_type=F32)
    xn = _lane_blocks(xn_ref).astype(BF16)
    act = jnp.where(i < N_PROMPT_TILES, actp_ref[...], acts_ref[...])
    cv = cv_ref[...]
    per_block = D_SSM // MERGE_COLS
    for c in range(N_MERGE):
        cols = pl.ds(c * MERGE_COLS, MERGE_COLS)
        in_block = pl.ds((c % per_block) * MERGE_COLS, MERGE_COLS)
        wga = (wga0_ref, wga1_ref)[c // per_block]
        wgb = (wgb0_ref, wgb1_ref)[c // per_block]
        ya = dot(act, wglu_ref[:, cols]) * jax.nn.sigmoid(
            dot(act, wglu_ref[:, pl.ds(D_MODEL + c * MERGE_COLS, MERGE_COLS)]))
        yb = dot(cv, wco_ref[:, cols])
        m = (jax.nn.sigmoid(dot(xn, wga[:, in_block])) * ya
             + jax.nn.sigmoid(dot(xn, wgb[:, in_block])) * yb)
        m_ref[:, cols] = m.astype(BF16)


def _merge(xn3, actp, acts, cv, w_in_bf, w_glu_bf, w_co_bf):
    ga0 = (D_SSM + 2 * D_CONV) // D_SSM
    wcols = lambda j: pl.BlockSpec((D_MODEL, D_SSM), lambda i: (0, j), pipeline_mode=pl.Buffered(1))
    return pl.pallas_call(
        _merge_kernel,
        grid=(N_TILES,),
        in_specs=[pl.BlockSpec((MODEL_LB, TM, LANES), lambda i: (0, i, 0)),
                  pl.BlockSpec((TM, D_SSM), lambda i: (jnp.minimum(i, N_PROMPT_TILES - 1), 0)),
                  _resident((N_SAMPLE_TOK, D_SSM)),
                  pl.BlockSpec((TM, D_CONV), lambda i: (i, 0)),
                  wcols(ga0), wcols(ga0 + 1), wcols(ga0 + 2), wcols(ga0 + 3),
                  _resident((D_SSM, 2 * D_MODEL)), _resident((D_CONV, D_MODEL))],
        out_specs=pl.BlockSpec((TM, D_MODEL), lambda i: (i, 0)),
        out_shape=jax.ShapeDtypeStruct((N_TOK, D_MODEL), BF16),
        compiler_params=_cparams(("arbitrary",), 58),
        name="gated_merge",
    )(xn3, actp, acts, cv, w_in_bf, w_in_bf, w_in_bf, w_in_bf, w_glu_bf, w_co_bf)


def _outproj_kernel(m_ref, x0_ref, x1_ref, x2_ref, x3_ref, xs_ref, wo_ref, g_ref, wr_ref, br_ref,
                    h_ref, xl_ref, route_ref, gw_ref, cnt_ref, xr_s, lg0, lg1, xb0, xb1, wo_s):
    i = pl.program_id(0)
    tile = jnp.minimum(i, N_OT - 1)
    steps_p = TO // N_PROMPT_SEQ
    steps_s = TO // N_SAMPLE_SEQ

    @pl.when(i == 0)
    def _():
        lg1[...] = jnp.zeros_like(lg1)
        xb1[...] = jnp.zeros_like(xb1)
        quarter = D_MODEL // 4
        for k in range(4):
            wo_s[k * quarter:(k + 1) * quarter, :] = wo_ref[k * quarter:(k + 1) * quarter, :].astype(BF16)

    @pl.when(tile < N_OT_PROMPT)
    def _():
        x = jnp.concatenate([x0_ref[...], x1_ref[...], x2_ref[...], x3_ref[...]], axis=0)
        _to_mixer_order(xr_s, x, N_PROMPT_SEQ, steps_p)

    @pl.when(tile >= N_OT_PROMPT)
    def _():
        k0 = (tile - N_OT_PROMPT) * steps_s
        x = jnp.concatenate(
            [xs_ref[pl.ds(pl.multiple_of(b * SAMPLE_LEN + k0, steps_s), steps_s), :]
             for b in range(N_SAMPLE_SEQ)], axis=0)
        _to_mixer_order(xr_s, x, N_SAMPLE_SEQ, steps_s)

    sets = ((lg0, xb0), (lg1, xb1))
    for r in range(2):
        @pl.when(i % 2 == r)
        def _():
            lg_prev, xb_prev = sets[1 - r]
            _alternate(
                _route_and_sort(lg_prev[...], xb_prev[...], xl_ref, route_ref, gw_ref, cnt_ref),
                _project(m_ref, xr_s, wo_s, g_ref, wr_ref, br_ref, h_ref, *sets[r]))


def _alternate(*chains):
    chains = list(chains)
    while chains:
        for chain in list(chains):
            try:
                next(chain)
            except StopIteration:
                chains.remove(chain)


def _project(m_ref, xr_s, wo_ref, g_ref, wr_ref, br_ref, h_ref, lg_s, xb_s):
    dot = functools.partial(jnp.dot, preferred_element_type=F32)
    n_chunks = 4
    lb = MODEL_LB // n_chunks
    parts = []
    for c in range(n_chunks):
        cols = slice(c * lb * LANES, (c + 1) * lb * LANES)
        x = jnp.concatenate([xr_s[c * lb + k] for k in range(lb)], axis=1)
        part = x + dot(m_ref[...], wo_ref[:, cols])
        h_ref[:, cols] = part
        parts.append(part)
        yield
    h = jnp.concatenate(parts, axis=1)
    xn = _rmsnorm_rows(h, g_ref[...])
    x_hi = xn.astype(BF16)
    x_lo = (xn - x_hi.astype(F32)).astype(BF16)
    w = wr_ref[...]
    w_hi = w.astype(BF16)
    w_lo = (w - w_hi.astype(F32)).astype(BF16)
    xb_s[...] = x_hi
    yield
    both = dot(x_hi, jnp.concatenate([w_hi, w_lo], axis=1))
    yield
    lg_s[...] = (both[:, :ROUTE_LANES] + (dot(x_lo, w_hi) + both[:, ROUTE_LANES:])) + br_ref[...]


def _route_and_sort(logits, x_hi, xl_ref, route_ref, gw_ref, cnt_ref):
    dot = functools.partial(jnp.dot, preferred_element_type=F32)
    lane = lax.broadcasted_iota(I32, logits.shape, 1)
    neg = -jnp.inf
    first = lambda hit: jnp.min(jnp.where(hit, lane, ROUTE_LANES), axis=-1, keepdims=True)
    gmask = lane < N_EXP_GROUPS
    lg = jnp.where(gmask, logits, neg)
    gmax = jnp.max(lg, axis=-1, keepdims=True)
    gsel = first(lg == gmax)
    psum = jnp.sum(jnp.where(gmask, jnp.exp(logits - gmax), 0.0), axis=-1, keepdims=True)
    pg_sel = 1.0 / psum
    e_lane = lane - N_EXP_GROUPS
    emask = (e_lane >= 0) & (e_lane < N_EXPERTS) & ((e_lane // EXP_PER_GROUP) == gsel)
    le = jnp.where(emask, logits, neg)
    v1 = jnp.max(le, axis=-1, keepdims=True)
    i1 = first(le == v1)
    le2 = jnp.where(lane == i1, neg, le)
    v2 = jnp.max(le2, axis=-1, keepdims=True)
    i2 = first(le2 == v2)
    z = jnp.exp(v2 - v1)
    w1 = pg_sel / (1.0 + z)
    w2 = pg_sel * z / (1.0 + z)
    e1 = i1 - N_EXP_GROUPS
    e2 = i2 - N_EXP_GROUPS
    gw_ref[...] = jnp.where(lane == 0, w1, jnp.where(lane == 1, w2, 0.0))
    yield
    oh1 = lane == e1
    oh2 = lane == e2
    hits = jnp.where(oh1 | oh2, 1.0, 0.0)
    rr = lax.broadcasted_iota(I32, (TO, TO), 0)
    cc = lax.broadcasted_iota(I32, (TO, TO), 1)
    before = dot(jnp.where(cc < rr, 1.0, 0.0).astype(BF16), hits.astype(BF16))
    counts = jnp.sum(hits, axis=0, keepdims=True)
    groups = jnp.floor((counts + (RUN - 1.0)) * (1.0 / RUN))
    er = lax.broadcasted_iota(I32, (ROUTE_LANES, ROUTE_LANES), 0)
    ec = lax.broadcasted_iota(I32, (ROUTE_LANES, ROUTE_LANES), 1)
    groups_before = dot(jnp.broadcast_to(groups, (SUBLANES, ROUTE_LANES)).astype(BF16),
                        jnp.where(er < ec, 1.0, 0.0).astype(BF16))[0:1]
    local = groups_before * RUN + before
    lpos1 = jnp.sum(jnp.where(oh1, local, 0.0), axis=-1, keepdims=True)
    lpos2 = jnp.sum(jnp.where(oh2, local, 0.0), axis=-1, keepdims=True)
    route_ref[...] = jnp.where(lane == 0, e1, jnp.where(lane == 1, e2, jnp.where(
        lane == 2, lpos1.astype(I32), jnp.where(lane == 3, lpos2.astype(I32), 0))))
    cnt_ref[0] = counts.astype(I32)
    yield
    lpos_t = jnp.where(lane == 0, lpos1, jnp.where(lane == 1, lpos2, 0.0)).T.astype(I32)
    n_chunks = 4
    rows = LROWS // n_chunks
    for q in range(n_chunks):
        jrow = q * rows + lax.broadcasted_iota(I32, (rows, TO), 0)
        sel = (jrow == lpos_t[0:1, :]) | (jrow == lpos_t[1:2, :])
        xl_ref[q * rows:(q + 1) * rows, :] = dot(jnp.where(sel, 1.0, 0.0).astype(BF16), x_hi).astype(BF16)
        yield


def _outproj(m, xp, xs, w_out_bf, g, w_router, b_router):
    cur = lambda i: jnp.minimum(i, N_OT - 1)
    prev = lambda i: jnp.maximum(i - 1, 0)
    tile = lambda n: pl.BlockSpec((TO, n), lambda i: (cur(i), 0))
    routed = lambda n: pl.BlockSpec((TO, n), lambda i: (prev(i), 0))
    return pl.pallas_call(
        _outproj_kernel,
        grid=(N_OT + 1,),
        in_specs=[tile(D_MODEL)] + _x_tile_specs(TO // N_PROMPT_SEQ) + [
            _resident((D_MODEL, D_MODEL)), _resident((1, D_MODEL)),
            _resident((D_MODEL, ROUTE_LANES)), _resident((1, ROUTE_LANES))],
        out_specs=[tile(D_MODEL),
                   pl.BlockSpec((LROWS, D_MODEL), lambda i: (prev(i), 0)),
                   routed(ROUTE_LANES), routed(ROUTE_LANES),
                   pl.BlockSpec((1, 1, ROUTE_LANES), lambda i: (prev(i), 0, 0))],
        out_shape=(jax.ShapeDtypeStruct((N_TOK, D_MODEL), F32),
                   jax.ShapeDtypeStruct((N_OT * LROWS, D_MODEL), BF16),
                   jax.ShapeDtypeStruct((N_TOK, ROUTE_LANES), I32),
                   jax.ShapeDtypeStruct((N_TOK, ROUTE_LANES), F32),
                   jax.ShapeDtypeStruct((N_OT, 1, ROUTE_LANES), I32)),
        scratch_shapes=[pltpu.VMEM((MODEL_LB, TO, LANES), F32),
                        pltpu.VMEM((TO, ROUTE_LANES), F32), pltpu.VMEM((TO, ROUTE_LANES), F32),
                        pltpu.VMEM((TO, D_MODEL), BF16), pltpu.VMEM((TO, D_MODEL), BF16),
                        pltpu.VMEM((D_MODEL, D_MODEL), BF16)],
        compiler_params=_cparams(("arbitrary",), 56),
        name="out_proj_router",
    )(m, xp, xp, xp, xp, xs, w_out_bf, g, w_router, b_router)


def _expert_kernel(te_ref, nrows_ref, nused_ref, psrc_ref, pdst_ref, plen_ref, tot_ref, kfirst_ref, kend_ref,
                   first_ref, par_ref, nexte_ref,
                   xl_hbm, wg_hbm, wu_hbm, wd_hbm, ol_hbm,
                   xbuf, obuf, zbuf, wg_s, wu_s, wd_s, wg_st, wu_st, wd_st, gsem, wsem, zsem, esem):
    i = pl.program_id(0)
    nused = nused_ref[0]
    slot = i % 2
    rows8 = lambda v: pl.multiple_of(v, RUN)

    def for_pieces(tile, fn):
        def body(k, _):
            idx = tile * N_OT + k
            n = plen_ref[idx]

            @pl.when(n > 0)
            def _():
                fn(rows8(psrc_ref[idx]), rows8(pdst_ref[idx]), rows8(n))
            return 0
        lax.fori_loop(kfirst_ref[tile], kend_ref[tile], body, 0)

    def gather(tile, s):
        for_pieces(tile, lambda src, dst, n: pltpu.make_async_copy(
            xl_hbm.at[pl.ds(src, n)], xbuf.at[s, pl.ds(dst, n)], gsem.at[s]).start())

    def scatter(tile, s):
        for_pieces(tile, lambda src, dst, n: pltpu.make_async_copy(
            obuf.at[s, pl.ds(dst, n)], ol_hbm.at[pl.ds(src, n)], wsem.at[s]).start())

    def wait_gather(tile, s):
        n = rows8(nrows_ref[tile])
        pltpu.make_async_copy(xl_hbm.at[pl.ds(0, n)], xbuf.at[s, pl.ds(0, n)], gsem.at[s]).wait()

    def wait_scatter(tile, s):
        n = rows8(nrows_ref[tile])
        pltpu.make_async_copy(obuf.at[s, pl.ds(0, n)], ol_hbm.at[pl.ds(0, n)], wsem.at[s]).wait()

    def tail_copy(t):
        used = rows8(tot_ref[t])
        n = rows8(LROWS - tot_ref[t])
        dst = rows8(t * LROWS + used)
        return pltpu.make_async_copy(zbuf.at[pl.ds(0, n)], ol_hbm.at[pl.ds(dst, n)], zsem.at[0])

    def weight_copies(e, b):
        return [pltpu.make_async_copy(src.at[e], dst.at[b], esem.at[b, k])
                for k, (src, dst) in enumerate(((wg_hbm, wg_st), (wu_hbm, wu_st), (wd_hbm, wd_st)))]

    @pl.when(i == 0)
    def _():
        for c in weight_copies(te_ref[0], par_ref[0]):
            c.start()
        zbuf[...] = jnp.zeros_like(zbuf)
        xbuf[...] = jnp.zeros_like(xbuf)

        def start(t, _):
            tail_copy(t).start()
            return 0

        def wait(t, _):
            tail_copy(t).wait()
            return 0
        lax.fori_loop(0, N_OT, start, 0)
        lax.fori_loop(0, N_OT, wait, 0)
        gather(0, 0)

    @pl.when(i + 1 < nused)
    def _():
        gather(i + 1, 1 - slot)

    @pl.when(i < nused)
    def _():
        @pl.when(first_ref[i] == 1)
        def _():
            b = par_ref[i]
            for c in weight_copies(te_ref[i], b):
                c.wait()
            wg_s[...] = wg_st[b].astype(BF16)
            wu_s[...] = wu_st[b].astype(BF16)
            wd_s[...] = wd_st[b].astype(BF16)

            @pl.when(nexte_ref[i] >= 0)
            def _():
                for c in weight_copies(nexte_ref[i], 1 - b):
                    c.start()

        wait_gather(i, slot)

        @pl.when(i >= 2)
        def _():
            wait_scatter(i - 2, slot)

        def run_expert(rows):
            xb = xbuf[slot, 0:rows]
            hg = jnp.dot(xb, wg_s[...], preferred_element_type=F32)
            hu = jnp.dot(xb, wu_s[...], preferred_element_type=F32)
            hid = jax.nn.silu(hg) * hu
            obuf[slot, 0:rows] = jnp.dot(
                hid.astype(BF16), wd_s[...], preferred_element_type=F32).astype(BF16)

        @pl.when(nrows_ref[i] <= TME // 2)
        def _():
            run_expert(TME // 2)

        @pl.when(nrows_ref[i] > TME // 2)
        def _():
            run_expert(TME)

        scatter(i, slot)

        @pl.when(i == nused - 1)
        def _():
            @pl.when(i >= 1)
            def _():
                wait_scatter(i - 1, 1 - slot)
            wait_scatter(i, slot)


def _experts(plan, xl, wg, wu, wd):
    anyspec = pl.BlockSpec(memory_space=pl.ANY)
    grid_spec = pltpu.PrefetchScalarGridSpec(
        num_scalar_prefetch=len(plan),
        grid=(N_ETILES,),
        in_specs=[anyspec, anyspec, anyspec, anyspec],
        out_specs=anyspec,
        scratch_shapes=[
            pltpu.VMEM((2, TME, D_MODEL), BF16),
            pltpu.VMEM((2, TME, D_MODEL), BF16),
            pltpu.VMEM((LROWS - 2 * TO, D_MODEL), BF16),
            pltpu.VMEM((D_MODEL, D_EXPERT), BF16),
            pltpu.VMEM((D_MODEL, D_EXPERT), BF16),
            pltpu.VMEM((D_EXPERT, D_MODEL), BF16),
            pltpu.VMEM((2, D_MODEL, D_EXPERT), F32),
            pltpu.VMEM((2, D_MODEL, D_EXPERT), F32),
            pltpu.VMEM((2, D_EXPERT, D_MODEL), F32),
            pltpu.SemaphoreType.DMA((2,)),
            pltpu.SemaphoreType.DMA((2,)),
            pltpu.SemaphoreType.DMA((1,)),
            pltpu.SemaphoreType.DMA((2, 3)),
        ],
    )
    return pl.pallas_call(
        _expert_kernel,
        grid_spec=grid_spec,
        out_shape=jax.ShapeDtypeStruct((N_OT * LROWS, D_MODEL), BF16),
        compiler_params=_cparams(("arbitrary",), 40),
        name="routed_experts",
    )(*plan, xl, wg, wu, wd)


def _combine_kernel(h_ref, route_ref, gw_ref, ol_ref, g_ref, yp_ref, ys_ref, y_s):
    i = pl.program_id(0)
    steps_p = TO // N_PROMPT_SEQ
    steps_s = TO // N_SAMPLE_SEQ
    route = route_ref[...]
    gw = gw_ref[...]
    jcol = lax.broadcasted_iota(I32, (TO, LROWS), 1)
    mix = (jnp.where(jcol == route[:, 2:3], gw[:, 0:1], 0.0)
           + jnp.where(jcol == route[:, 3:4], gw[:, 1:2], 0.0))
    moe = jnp.dot(mix.astype(BF16), ol_ref[...], preferred_element_type=F32)
    y = _rmsnorm_rows(h_ref[...] + moe, g_ref[...])
    for c in range(MODEL_LB):
        y_s[c] = y[:, c * LANES:(c + 1) * LANES]

    @pl.when(i < N_OT_PROMPT)
    def _():
        for b in range(N_PROMPT_SEQ):
            yp_ref[b] = _seq_rows(y_s, b, N_PROMPT_SEQ, steps_p)

    @pl.when(i >= N_OT_PROMPT)
    def _():
        k0 = (i - N_OT_PROMPT) * steps_s
        for b in range(N_SAMPLE_SEQ):
            rows = pl.ds(pl.multiple_of(b * SAMPLE_LEN + k0, steps_s), steps_s)
            ys_ref[rows, :] = _seq_rows(y_s, b, N_SAMPLE_SEQ, steps_s)


def _combine(h, route, gw, ol, g):
    tile = lambda n: pl.BlockSpec((TO, n), lambda i: (i, 0))
    return pl.pallas_call(
        _combine_kernel,
        grid=(N_OT,),
        in_specs=[tile(D_MODEL), tile(ROUTE_LANES), tile(ROUTE_LANES),
                  pl.BlockSpec((LROWS, D_MODEL), lambda i: (i, 0)),
                  pl.BlockSpec((1, D_MODEL), lambda i: (0, 0))],
        out_specs=[
            pl.BlockSpec((N_PROMPT_SEQ, TO // N_PROMPT_SEQ, D_MODEL),
                         lambda i: (0, jnp.minimum(i, N_OT_PROMPT - 1), 0)),
            pl.BlockSpec((N_SAMPLE_TOK, D_MODEL), lambda i: (0, 0)),
        ],
        out_shape=(jax.ShapeDtypeStruct((N_PROMPT_SEQ, PROMPT_LEN, D_MODEL), F32),
                   jax.ShapeDtypeStruct((N_SAMPLE_TOK, D_MODEL), F32)),
        scratch_shapes=[pltpu.VMEM((MODEL_LB, TO, LANES), F32)],
        compiler_params=_cparams(("arbitrary",), 48),
        name="combine_norm",
    )(h, route, gw, ol, g)


def _dispatch_plan(cnt):
    experts = jnp.arange(N_EXPERTS, dtype=I32)
    cnt = cnt[:, 0, :N_EXPERTS]
    run = ((cnt + RUN - 1) // RUN) * RUN
    local_start = jnp.cumsum(run, axis=1) - run
    local_used = jnp.sum(run, axis=1)
    total = jnp.sum(run, axis=0)
    padded = ((total + TME - 1) // TME) * TME
    ends = jnp.cumsum(padded)
    starts = ends - padded
    run_start = starts[None, :] + jnp.cumsum(run, axis=0) - run
    n_used = (ends[-1] // TME).astype(I32)
    tile = jnp.arange(N_ETILES, dtype=I32)
    tile_start = tile * TME
    te = jnp.minimum(jnp.sum((ends[None, :] <= tile_start[:, None]).astype(I32), axis=1), N_EXPERTS - 1)
    mine = (te[:, None] == experts[None, :])[:, None, :]
    of_tile = lambda a: jnp.sum(jnp.where(mine, a[None], 0), axis=-1)
    s0 = of_tile(run_start)
    lo = jnp.maximum(s0, tile_start[:, None])
    hi = jnp.minimum(s0 + of_tile(run), tile_start[:, None] + TME)
    plen = jnp.where((tile < n_used)[:, None], jnp.maximum(hi - lo, 0), 0)
    live = plen > 0
    psrc = jnp.where(live, jnp.arange(N_OT, dtype=I32)[None, :] * LROWS + of_tile(local_start) + lo - s0, 0)
    pdst = jnp.where(live, lo - tile_start[:, None], 0)
    last = jnp.sum(jnp.where(tile == n_used - 1, te, 0))
    te = jnp.where(tile < n_used, te, last)
    flat = lambda a: a.reshape(-1).astype(I32)
    none_yet = lambda a: jnp.sum((jnp.cumsum(a.astype(I32), axis=1) == 0).astype(I32), axis=1)
    k_first = none_yet(live)
    k_end = N_OT - none_yet(live[:, ::-1])
    first = jnp.logical_and(tile < n_used, jnp.concatenate([jnp.array([True]), te[1:] != te[:-1]]))
    parity = (jnp.cumsum(first.astype(I32)) - 1) % 2
    next_first = lax.cummin(jnp.where(first, tile, N_ETILES), axis=0, reverse=True)
    next_first = jnp.concatenate([next_first[1:], jnp.array([N_ETILES], I32)])
    next_expert = jnp.where(next_first < N_ETILES, te[jnp.minimum(next_first, N_ETILES - 1)], -1)
    return (te.astype(I32), flat(jnp.sum(plen, axis=1)), n_used.reshape(1),
            flat(psrc), flat(pdst), flat(plen), local_used.astype(I32), flat(k_first), flat(k_end),
            flat(first), flat(parity), flat(next_expert))


def kernel(x_prompt, x_sample, state_ssm_re, state_ssm_im, cache_conv, norm_mix_g, w_in, lam_re, lam_im, log_dt, b_re, b_im, c_re, c_im, d_skip, w_ssm_glu, conv_w, conv_b, conv_ln_g, conv_ln_b, w_conv_out, w_out, norm_ffn_g, w_router_group, b_router_group, w_router_expert, b_router_expert, w_exp_gate, w_exp_up, w_exp_down, norm_final_g):
    assert w_in.shape[0] == 1, "single-layer trunk"
    xp = x_prompt.reshape(N_PROMPT_TOK, D_MODEL)
    xs = x_sample.reshape(N_SAMPLE_TOK, D_MODEL)
    w_in_bf = w_in[0].astype(BF16)
    row = lambda a: a.reshape(1, -1)

    ar_rep, ai_rep, bbr, bbi = _discretise(lam_re[0], lam_im[0], log_dt[0], b_re[0], b_im[0])
    pick = lambda a: a[:, :SSM_STATE].reshape(GROUP_BLOCKS, 1, GB_STATE)
    bm = jnp.concatenate([_block_diag_in(bbr), _block_diag_in(bbi)], axis=-1).astype(BF16)
    cm = jnp.concatenate([_block_diag_out(c_re[0]), -_block_diag_out(c_im[0])], axis=1).astype(BF16)
    dsk = d_skip[0].reshape(GROUP_BLOCKS, 1, GB_IN)
    ar, ai = pick(ar_rep), pick(ai_rep)
    s0r = state_ssm_re[0].reshape(N_SAMPLE_SEQ, -1)
    s0i = state_ssm_im[0].reshape(N_SAMPLE_SEQ, -1)

    u3, v3, xn3 = _inproj(xp, xs, row(norm_mix_g[0]), w_in_bf)
    actp, pfr, pfi = _scan_prompt(u3, bm, cm, ar, ai, dsk)
    acts, sfr, sfi = _scan_sample(u3, bm, cm, ar, ai, dsk, s0r, s0i)

    hist_s = cache_conv[0].transpose(1, 0, 2).reshape(HIST_S, D_CONV)
    cv, ncp, ncs = _conv(v3, hist_s, conv_w[0], row(conv_b[0]), conv_ln_g[0], conv_ln_b[0])

    m = _merge(xn3, actp, acts, cv, w_in_bf, w_ssm_glu[0].astype(BF16), w_conv_out[0].astype(BF16))

    pad_lanes = ROUTE_LANES - N_EXP_GROUPS - N_EXPERTS
    w_router = jnp.concatenate(
        [w_router_group[0], w_router_expert[0], jnp.zeros((D_MODEL, pad_lanes), F32)], axis=1)
    b_router = jnp.concatenate(
        [b_router_group[0], b_router_expert[0], jnp.zeros((pad_lanes,), F32)]).reshape(1, ROUTE_LANES)
    h, xl, route, gw, cnt = _outproj(m, xp, xs, w_out[0], row(norm_ffn_g[0]),
                                     w_router, b_router)

    ol = _experts(_dispatch_plan(cnt), xl, w_exp_gate[0], w_exp_up[0], w_exp_down[0])
    yp, ys = _combine(h, route, gw, ol, row(norm_final_g))

    st = lambda a, n: a.reshape(1, n, N_SSM_GROUPS, SSM_STATE)
    pst = lambda a: st(a.transpose(1, 0, 2), N_PROMPT_SEQ)
    ncp = ncp.reshape(CONV_BUF, N_PROMPT_SEQ, D_CONV).transpose(1, 0, 2)[None]
    ncs = ncs.reshape(CONV_BUF, N_SAMPLE_SEQ, D_CONV).transpose(1, 0, 2)[None]
    return (yp, ys.reshape(N_SAMPLE_SEQ, SAMPLE_LEN, D_MODEL), pst(pfr), pst(pfi), ncp,
            st(sfr, N_SAMPLE_SEQ), st(sfi, N_SAMPLE_SEQ), ncs)
```

```python
import functools

import jax
import jax.numpy as jnp
from jax import lax
from jax.experimental import pallas as pl
from jax.experimental.pallas import tpu as pltpu

F32 = jnp.float32
BF16 = jnp.bfloat16
I32 = jnp.int32

SUBLANES = 8
LANES = 128

D_MODEL = 2048
D_SSM = 1024
D_CONV = 1024
SSM_GROUP = 16
N_SSM_GROUPS = 64
SSM_STATE = 64
CONV_WIDTH = 31
CONV_BUF = CONV_WIDTH - 1
N_EXP_GROUPS = 4
EXP_PER_GROUP = 8
N_EXPERTS = 32
D_EXPERT = 256
EPS = 1e-6

N_PROMPT_SEQ = 4
PROMPT_LEN = 2048
N_SAMPLE_SEQ = 16
SAMPLE_LEN = 32
N_PROMPT_TOK = N_PROMPT_SEQ * PROMPT_LEN
N_SAMPLE_TOK = N_SAMPLE_SEQ * SAMPLE_LEN
N_TOK = N_PROMPT_TOK + N_SAMPLE_TOK

TM = 512
N_PROMPT_TILES = N_PROMPT_TOK // TM
N_TILES = N_TOK // TM
PROMPT_STEPS = TM // N_PROMPT_SEQ
GROUP_BLOCKS = 4
GROUPS_PER_BLOCK = N_SSM_GROUPS // GROUP_BLOCKS
GB_IN = GROUPS_PER_BLOCK * SSM_GROUP
GB_STATE = GROUPS_PER_BLOCK * SSM_STATE
GB_LB = GB_STATE // LANES
HIST_P = CONV_BUF * N_PROMPT_SEQ
HIST_S = CONV_BUF * N_SAMPLE_SEQ
ROUTE_LANES = 128
N_PAIRS = 2 * N_TOK
TO = 256
N_OT = N_TOK // TO
N_OT_PROMPT = N_PROMPT_TOK // TO
HALVES = TM // TO
RUN = 16
LROWS = 1024
TME = 512
N_ETILES = -(-(N_PAIRS + N_OT * N_EXPERTS * (RUN - 1) + N_EXPERTS * (TME - 1)) // TME)
assert 2 * TO + (RUN - 1) * N_EXPERTS <= LROWS
MODEL_LB = D_MODEL // LANES
SSM_LB = D_SSM // LANES
CONV_LB = D_CONV // LANES


def _cparams(sem, vmem_mb):
    return pltpu.CompilerParams(dimension_semantics=sem, vmem_limit_bytes=vmem_mb * 1024 * 1024)


def _resident(shape):
    return pl.BlockSpec(shape, lambda *_: (0,) * len(shape), pipeline_mode=pl.Buffered(1))


def _x_tile_specs(steps=PROMPT_STEPS):
    blocks_per_seq = PROMPT_LEN // steps
    n_prompt = N_PROMPT_SEQ * blocks_per_seq // N_PROMPT_SEQ

    def prompt(b):
        return pl.BlockSpec(
            (steps, D_MODEL),
            lambda i, *_: (b * blocks_per_seq + jnp.minimum(i, n_prompt - 1), 0))

    return [prompt(b) for b in range(N_PROMPT_SEQ)] + [_resident((N_SAMPLE_TOK, D_MODEL))]


def _to_mixer_order(ref3, val, n_seq, steps, at=0):
    for c in range(val.shape[1] // LANES):
        for b in range(n_seq):
            ref3[at + c, pl.ds(b, steps, stride=n_seq), :] = val[b * steps:(b + 1) * steps,
                                                                 c * LANES:(c + 1) * LANES]


def _seq_rows(ref3, b, n_seq, steps):
    return jnp.concatenate(
        [ref3[c, pl.ds(b, steps, stride=n_seq), :] for c in range(ref3.shape[0])], axis=1)


def _tile_to_mixer_order(i, ref3, xs, sample):
    @pl.when(i < N_PROMPT_TILES)
    def _():
        _to_mixer_order(ref3, xs(), N_PROMPT_SEQ, PROMPT_STEPS)

    @pl.when(i >= N_PROMPT_TILES)
    def _():
        _to_mixer_order(ref3, sample(), N_SAMPLE_SEQ, SAMPLE_LEN)


def _lane_blocks(ref3):
    return jnp.concatenate([ref3[c] for c in range(ref3.shape[0])], axis=1)


def _pair_specs(rows, cols, n_prompt_tiles):
    p = pl.BlockSpec((rows, cols), lambda i, *_: (jnp.minimum(i, n_prompt_tiles - 1), 0))
    s = pl.BlockSpec((rows, cols), lambda i, *_: (jnp.maximum(i - n_prompt_tiles, 0), 0))
    return p, s


def _disc_kernel(lr_ref, li_ref, ldt_ref, br_ref, bi_ref, ar_ref, ai_ref, bbr_ref, bbi_ref):
    lr = lr_ref[...]
    li = li_ref[...]
    dt = jnp.exp(ldt_ref[...])
    mag = jnp.exp(lr * dt)
    ar = mag * jnp.cos(li * dt)
    ai = mag * jnp.sin(li * dt)
    den = lr * lr + li * li
    nr = ar - 1.0
    cr = (nr * lr + ai * li) / den
    ci = (ai * lr - nr * li) / den
    br = br_ref[...]
    bi = bi_ref[...]
    ar_ref[...] = ar
    ai_ref[...] = ai
    bbr_ref[...] = cr * br - ci * bi
    bbi_ref[...] = cr * bi + ci * br


def _discretise(lam_re, lam_im, log_dt, b_re, b_im):
    shp = jax.ShapeDtypeStruct((N_SSM_GROUPS, SSM_STATE * SSM_GROUP), F32)
    rep = lambda a: jnp.tile(a, (1, SSM_GROUP))
    ldt = jnp.broadcast_to(log_dt[:, None], (N_SSM_GROUPS, SSM_STATE * SSM_GROUP))
    hp = lambda b: b.transpose(0, 2, 1).reshape(N_SSM_GROUPS, -1)
    return pl.pallas_call(_disc_kernel, out_shape=(shp, shp, shp, shp), name="s5_discretise")(
        rep(lam_re), rep(lam_im), ldt, hp(b_re), hp(b_im))


def _block_diag(blocks, rows_per_group, cols_per_group):
    wide = jnp.tile(blocks, (1, 1, GROUPS_PER_BLOCK))
    r = lax.broadcasted_iota(I32, wide.shape, 1) // rows_per_group
    c = lax.broadcasted_iota(I32, wide.shape, 2) // cols_per_group
    return jnp.where(r == c, wide, 0.0)


def _block_diag_in(bb):
    return _block_diag(bb.reshape(GROUP_BLOCKS, GB_IN, SSM_STATE), SSM_GROUP, SSM_STATE)


def _block_diag_out(c):
    return _block_diag(c.transpose(0, 2, 1).reshape(GROUP_BLOCKS, GB_STATE, SSM_GROUP),
                       SSM_STATE, SSM_GROUP)


def _rmsnorm_rows(x, g):
    r = lax.rsqrt(jnp.mean(x * x, axis=-1, keepdims=True) + EPS)
    return x * r * g


def _inproj_kernel(x0_ref, x1_ref, x2_ref, x3_ref, xs_ref, g_ref, wu_ref, wa_ref, wb_ref,
                   u_ref, v_ref, xn_ref):
    i = pl.program_id(0)
    dot = functools.partial(jnp.dot, preferred_element_type=F32)

    def tile(x, n_seq, steps):
        xn = _rmsnorm_rows(x, g_ref[...])
        xb = xn.astype(BF16)
        n_pieces = 4
        wide = D_SSM // n_pieces
        for k in range(n_pieces):
            cols = slice(k * wide, (k + 1) * wide)
            at = k * (wide // LANES)
            _to_mixer_order(u_ref, dot(xb, wu_ref[:, cols]), n_seq, steps, at)
            xcols = slice(k * D_MODEL // n_pieces, (k + 1) * D_MODEL // n_pieces)
            _to_mixer_order(xn_ref, xn[:, xcols], n_seq, steps, k * (MODEL_LB // n_pieces))
            v = dot(xb, wa_ref[:, cols]) * jax.nn.sigmoid(dot(xb, wb_ref[:, cols]))
            _to_mixer_order(v_ref, v, n_seq, steps, at)

    @pl.when(i < N_PROMPT_TILES)
    def _():
        x = jnp.concatenate([x0_ref[...], x1_ref[...], x2_ref[...], x3_ref[...]], axis=0)
        tile(x, N_PROMPT_SEQ, PROMPT_STEPS)

    @pl.when(i >= N_PROMPT_TILES)
    def _():
        tile(xs_ref[...], N_SAMPLE_SEQ, SAMPLE_LEN)


def _inproj(xp, xs, g, w_in_bf):
    blocked = lambda nb: pl.BlockSpec((nb, TM, LANES), lambda i: (0, i, 0))
    wcols = lambda j: pl.BlockSpec((D_MODEL, D_SSM), lambda i: (0, j), pipeline_mode=pl.Buffered(1))
    return pl.pallas_call(
        _inproj_kernel,
        grid=(N_TILES,),
        in_specs=_x_tile_specs() + [_resident((1, D_MODEL)), wcols(0), wcols(1), wcols(2)],
        out_specs=[blocked(SSM_LB), blocked(CONV_LB), blocked(MODEL_LB)],
        out_shape=(jax.ShapeDtypeStruct((SSM_LB, N_TOK, LANES), F32),
                   jax.ShapeDtypeStruct((CONV_LB, N_TOK, LANES), F32),
                   jax.ShapeDtypeStruct((MODEL_LB, N_TOK, LANES), F32)),
        compiler_params=_cparams(("arbitrary",), 56),
        name="in_proj",
    )(xp, xp, xp, xp, xs, g, w_in_bf, w_in_bf, w_in_bf)


N_SCAN_TILES = GROUP_BLOCKS * N_PROMPT_TILES
SCAN_STAGES = 3


def _scan_prompt_kernel(ua_ref, uc_ref, bm_ref, cm_ref, ar_ref, ai_ref, dsk_ref,
                        act_ref, pfr_ref, pfi_ref, x0, x1, x2, cst, car, fin):
    n = pl.program_id(0)
    q_s = jnp.clip(n - 1, 0, N_SCAN_TILES - 1)
    t_s = q_s % N_PROMPT_TILES
    valid_s = jnp.logical_and(n >= 1, n <= N_SCAN_TILES)

    @pl.when(n == 0)
    def _():
        for buf in (x0, x1, x2):
            buf[...] = jnp.zeros_like(buf)
        car[...] = jnp.zeros_like(car)
        fin[...] = jnp.zeros_like(fin)

    shape = (SUBLANES, GB_STATE)
    top = lax.broadcasted_iota(I32, shape, 0) < N_PROMPT_SEQ
    arb = jnp.broadcast_to(ar_ref[0], shape)
    aib = jnp.broadcast_to(ai_ref[0], shape)
    cst[0] = jnp.where(top, 0.0, arb)
    cst[1] = jnp.where(top, 0.0, aib)
    cst[2] = jnp.where(top, arb, arb * arb - aib * aib)
    cst[3] = jnp.where(top, aib, 2.0 * arb * aib)

    def project_in_pieces(xa, n_pieces=8):
        u = jnp.concatenate([ua_ref[0], ua_ref[1]], axis=1).astype(BF16)
        lb = 2 * GB_LB // n_pieces

        def piece(k):
            x = jnp.dot(u, bm_ref[0, :, k * lb * LANES:(k + 1) * lb * LANES],
                        preferred_element_type=F32)
            for c in range(lb):
                xa[k * lb + c] = x[:, c * LANES:(c + 1) * LANES]
        return [functools.partial(piece, k) for k in range(n_pieces)]

    def project_out_pieces(xc, n_pieces=8):
        lb = 2 * GB_LB // n_pieces
        acc = []

        def piece(k):
            h = jnp.concatenate([xc[k * lb + c] for c in range(lb)], axis=1).astype(BF16)
            part = jnp.dot(h, cm_ref[0, k * lb * LANES:(k + 1) * lb * LANES, :],
                           preferred_element_type=F32)
            acc[:] = [part if not acc else acc[0] + part]

        def final():
            u = jnp.concatenate([uc_ref[0], uc_ref[1]], axis=1)
            act_ref[...] = jax.nn.gelu(acc[0] + dsk_ref[0] * u).astype(BF16)
        return [functools.partial(piece, k) for k in range(n_pieces)], final

    def recur(xs, between=()):
        every = (TM // SUBLANES) // max(len(between), 1)
        top1 = lax.broadcasted_iota(I32, (SUBLANES, LANES), 0) < N_PROMPT_SEQ
        first = t_s == 0
        hr = [jnp.where(first, 0.0, car[:, c * LANES:(c + 1) * LANES]) for c in range(GB_LB)]
        hi = [jnp.where(first, 0.0, car[:, GB_STATE + c * LANES:GB_STATE + (c + 1) * LANES])
              for c in range(GB_LB)]
        for r in range(TM // SUBLANES):
            if between and r % every == 0:
                between[r // every]()
            rows = slice(r * SUBLANES, (r + 1) * SUBLANES)
            for c in range(GB_LB):
                cols = slice(c * LANES, (c + 1) * LANES)
                xr = xs[c, rows, :]
                xi = xs[GB_LB + c, rows, :]
                a1r, a1i, a2r, a2i = cst[0, :, cols], cst[1, :, cols], cst[2, :, cols], cst[3, :, cols]
                sxr = pltpu.roll(xr, N_PROMPT_SEQ, axis=0)
                sxi = pltpu.roll(xi, N_PROMPT_SEQ, axis=0)
                tr = xr + a1r * sxr - a1i * sxi
                ti = xi + a1r * sxi + a1i * sxr
                nr = tr + a2r * hr[c] - a2i * hi[c]
                ni = ti + a2r * hi[c] + a2i * hr[c]
                xs[c, rows, :] = nr
                xs[GB_LB + c, rows, :] = ni
                hr[c] = jnp.where(top1, pltpu.roll(nr, N_PROMPT_SEQ, axis=0), nr)
                hi[c] = jnp.where(top1, pltpu.roll(ni, N_PROMPT_SEQ, axis=0), ni)
        state = jnp.concatenate(hr + hi, axis=1)
        car[...] = state
        last = jnp.logical_and(valid_s, t_s == N_PROMPT_TILES - 1)
        fin[...] = jnp.where(last, state, fin[...])
        pfr_ref[0] = fin[N_PROMPT_SEQ:SUBLANES, 0:GB_STATE]
        pfi_ref[0] = fin[N_PROMPT_SEQ:SUBLANES, GB_STATE:2 * GB_STATE]

    bufs = (x0, x1, x2)
    for r in range(SCAN_STAGES):
        @pl.when(n % SCAN_STAGES == r)
        def _():
            ins = project_in_pieces(bufs[r])
            outs, finish_out = project_out_pieces(bufs[(r + 1) % SCAN_STAGES])
            recur(bufs[(r + 2) % SCAN_STAGES], [p for pair in zip(outs, ins) for p in pair])
            finish_out()


def _scan_prompt(u3, bm, cm, ar, ai, dsk):
    q_a = lambda n: jnp.minimum(n, N_SCAN_TILES - 1)
    q_s = lambda n: jnp.clip(n - 1, 0, N_SCAN_TILES - 1)
    q_c = lambda n: jnp.clip(n - 2, 0, N_SCAN_TILES - 1)
    gb = lambda q: q // N_PROMPT_TILES
    tt = lambda q: q % N_PROMPT_TILES
    u_spec = lambda q: pl.BlockSpec((GB_IN // LANES, TM, LANES), lambda n: (gb(q(n)), tt(q(n)), 0))
    gb3 = lambda last, q: pl.BlockSpec((1,) + last, lambda n: (gb(q(n)), 0, 0))
    state = jax.ShapeDtypeStruct((GROUP_BLOCKS, N_PROMPT_SEQ, GB_STATE), F32)
    xbuf = pltpu.VMEM((2 * GB_LB, TM, LANES), F32)
    return pl.pallas_call(
        _scan_prompt_kernel,
        grid=(N_SCAN_TILES + SCAN_STAGES - 1,),
        in_specs=[
            u_spec(q_a), u_spec(q_c),
            gb3((GB_IN, 2 * GB_STATE), q_a),
            gb3((2 * GB_STATE, GB_IN), q_c),
            gb3((1, GB_STATE), q_s),
            gb3((1, GB_STATE), q_s),
            gb3((1, GB_IN), q_c),
        ],
        out_specs=[
            pl.BlockSpec((TM, GB_IN), lambda n: (tt(q_c(n)), gb(q_c(n)))),
            gb3((N_PROMPT_SEQ, GB_STATE), q_s),
            gb3((N_PROMPT_SEQ, GB_STATE), q_s),
        ],
        out_shape=(jax.ShapeDtypeStruct((N_PROMPT_TOK, D_SSM), BF16), state, state),
        scratch_shapes=[xbuf, xbuf, xbuf,
                        pltpu.VMEM((4, SUBLANES, GB_STATE), F32),
                        pltpu.VMEM((SUBLANES, 2 * GB_STATE), F32),
                        pltpu.VMEM((SUBLANES, 2 * GB_STATE), F32)],
        compiler_params=_cparams(("arbitrary",), 48),
        name="s5_scan_prompt",
    )(u3, u3, bm, cm, ar, ai, dsk)


def _scan_sample_kernel(u_ref, bm_ref, cm_ref, ar_ref, ai_ref, dsk_ref, s0r_ref, s0i_ref,
                        act_ref, sfr_ref, sfi_ref, xs_ref):
    half_lb = GB_LB // 2
    half_cols = half_lb * LANES
    u = jnp.concatenate([u_ref[0], u_ref[1]], axis=1)
    x = jnp.dot(u.astype(BF16), bm_ref[0], preferred_element_type=F32)
    for c in range(2 * GB_LB):
        xs_ref[c] = x[:, c * LANES:(c + 1) * LANES]

    def load_half(rows, half, imag):
        c0 = imag * GB_LB + half * half_lb
        return jnp.concatenate([xs_ref[c0 + c, rows, :] for c in range(half_lb)], axis=1)

    def store_half(rows, half, imag, val):
        c0 = imag * GB_LB + half * half_lb
        for c in range(half_lb):
            xs_ref[c0 + c, rows, :] = val[:, c * LANES:(c + 1) * LANES]

    shape = (N_SAMPLE_SEQ, half_cols)
    for half in range(2):
        cols = slice(half * half_cols, (half + 1) * half_cols)
        arb = jnp.broadcast_to(ar_ref[0][:, cols], shape)
        aib = jnp.broadcast_to(ai_ref[0][:, cols], shape)

        def body(k, carry):
            hr, hi = carry
            rows = pl.ds(pl.multiple_of(k * N_SAMPLE_SEQ, N_SAMPLE_SEQ), N_SAMPLE_SEQ)
            xr = load_half(rows, half, 0)
            xi = load_half(rows, half, 1)
            nhr = xr + arb * hr - aib * hi
            nhi = xi + arb * hi + aib * hr
            store_half(rows, half, 0, nhr)
            store_half(rows, half, 1, nhi)
            return nhr, nhi

        hr, hi = lax.fori_loop(0, SAMPLE_LEN, body, (s0r_ref[:, cols], s0i_ref[:, cols]))
        sfr_ref[:, cols] = hr
        sfi_ref[:, cols] = hi

    h = _lane_blocks(xs_ref).astype(BF16)
    y = jnp.dot(h, cm_ref[0], preferred_element_type=F32) + dsk_ref[0] * u
    act_ref[...] = jax.nn.gelu(y).astype(BF16)


def _scan_sample(u3, bm, cm, ar, ai, dsk, s0r, s0i):
    gb3 = lambda last: pl.BlockSpec((1,) + last, lambda g: (g, 0, 0))
    state = lambda: pl.BlockSpec((N_SAMPLE_SEQ, GB_STATE), lambda g: (0, g))
    sst = jax.ShapeDtypeStruct((N_SAMPLE_SEQ, N_SSM_GROUPS * SSM_STATE), F32)
    return pl.pallas_call(
        _scan_sample_kernel,
        grid=(GROUP_BLOCKS,),
        in_specs=[
            pl.BlockSpec((GB_IN // LANES, TM, LANES), lambda g: (g, N_PROMPT_TILES, 0)),
            gb3((GB_IN, 2 * GB_STATE)), gb3((2 * GB_STATE, GB_IN)),
            gb3((1, GB_STATE)), gb3((1, GB_STATE)), gb3((1, GB_IN)),
            state(), state(),
        ],
        out_specs=[pl.BlockSpec((TM, GB_IN), lambda g: (0, g)), state(), state()],
        out_shape=(jax.ShapeDtypeStruct((N_SAMPLE_TOK, D_SSM), BF16), sst, sst),
        scratch_shapes=[pltpu.VMEM((2 * GB_LB, TM, LANES), F32)],
        compiler_params=_cparams(("arbitrary",), 32),
        name="s5_scan_sample",
    )(u3, bm, cm, ar, ai, dsk, s0r, s0i)


CONV_ROWS = 64
EXT_ROWS = HIST_S + TM


def _conv_kernel(v_ref, hist_ref, w_ref, b_ref, lg_ref, lb_ref,
                 cv_ref, ncp_ref, ncs_ref, ext_ref, sh_ref, acc_ref):
    t = pl.program_id(0)

    def taps(n_seq):
        half = SUBLANES // 2
        if n_seq % SUBLANES:
            lo = HIST_S - HIST_P - SUBLANES
            sh_ref[:, lo:EXT_ROWS - SUBLANES, :] = ext_ref[:, lo + half:EXT_ROWS - half, :]

        def body(rb, _):
            r0 = pl.multiple_of(rb * CONV_ROWS, CONV_ROWS)
            for cb in range(CONV_LB):
                cols = slice(cb * LANES, (cb + 1) * LANES)
                acc = None
                for j in range(CONV_WIDTH):
                    start = HIST_S - (CONV_BUF - j) * n_seq
                    src = ext_ref
                    if start % SUBLANES:
                        src, start = sh_ref, start - half
                    rows = pl.ds(pl.multiple_of(r0 + start, SUBLANES), CONV_ROWS)
                    term = src[cb, rows, :] * w_ref[j:j + 1, cols]
                    acc = term if acc is None else acc + term
                acc_ref[cb, pl.ds(r0, CONV_ROWS), :] = acc + b_ref[:, cols]
            return 0

        lax.fori_loop(0, TM // CONV_ROWS, body, 0)

    @pl.when(t == 0)
    def _():
        ext_ref[:, 0:HIST_S, :] = jnp.zeros((CONV_LB, HIST_S, LANES), F32)

    @pl.when(t == N_PROMPT_TILES)
    def _():
        for c in range(CONV_LB):
            ext_ref[c, 0:HIST_S, :] = hist_ref[:, c * LANES:(c + 1) * LANES]

    ext_ref[:, HIST_S:EXT_ROWS, :] = v_ref[...]

    @pl.when(t < N_PROMPT_TILES)
    def _():
        taps(N_PROMPT_SEQ)

    @pl.when(t == N_PROMPT_TILES)
    def _():
        taps(N_SAMPLE_SEQ)

    y = acc_ref[...]
    mu = jnp.sum(jnp.sum(y, axis=0), axis=-1, keepdims=True) / D_CONV
    yc = y - mu[None]
    var = jnp.sum(jnp.sum(yc * yc, axis=0), axis=-1, keepdims=True) / D_CONV
    z = jax.nn.silu(yc * lax.rsqrt(var + EPS)[None] * lg_ref[...] + lb_ref[...])
    for c in range(CONV_LB):
        cv_ref[:, c * LANES:(c + 1) * LANES] = z[c].astype(BF16)

    @pl.when(t == N_PROMPT_TILES - 1)
    def _():
        for c in range(CONV_LB):
            ncp_ref[:, c * LANES:(c + 1) * LANES] = ext_ref[c, EXT_ROWS - HIST_P:EXT_ROWS, :]

    @pl.when(t < N_PROMPT_TILES - 1)
    def _():
        ext_ref[:, HIST_S - HIST_P:HIST_S, :] = ext_ref[:, EXT_ROWS - HIST_P:EXT_ROWS, :]

    @pl.when(t == N_PROMPT_TILES)
    def _():
        for c in range(CONV_LB):
            ncs_ref[:, c * LANES:(c + 1) * LANES] = ext_ref[c, EXT_ROWS - HIST_S:EXT_ROWS, :]


def _conv(v3, hist_s, conv_w, conv_b, ln_g, ln_b):
    row = lambda n: pl.BlockSpec((n, D_CONV), lambda t: (0, 0))
    lane3 = pl.BlockSpec((CONV_LB, 1, LANES), lambda t: (0, 0, 0))
    ext = pltpu.VMEM((CONV_LB, EXT_ROWS, LANES), F32)
    return pl.pallas_call(
        _conv_kernel,
        grid=(N_TILES,),
        in_specs=[pl.BlockSpec((CONV_LB, TM, LANES), lambda t: (0, t, 0)),
                  row(HIST_S), row(CONV_WIDTH), row(1), lane3, lane3],
        out_specs=[pl.BlockSpec((TM, D_CONV), lambda t: (t, 0)), row(HIST_P), row(HIST_S)],
        out_shape=(jax.ShapeDtypeStruct((N_TOK, D_CONV), BF16),
                   jax.ShapeDtypeStruct((HIST_P, D_CONV), F32),
                   jax.ShapeDtypeStruct((HIST_S, D_CONV), F32)),
        scratch_shapes=[ext, ext, pltpu.VMEM((CONV_LB, TM, LANES), F32)],
        compiler_params=_cparams(("arbitrary",), 48),
        name="conv_branch",
    )(v3, hist_s, conv_w, conv_b,
      ln_g.reshape(CONV_LB, 1, LANES), ln_b.reshape(CONV_LB, 1, LANES))


MERGE_COLS = 512
N_MERGE = D_MODEL // MERGE_COLS


def _merge_kernel(xn_ref, actp_ref, acts_ref, cv_ref, wga0_ref, wga1_ref, wgb0_ref, wgb1_ref,
                  wglu_ref, wco_ref, m_ref):
    i = pl.program_id(0)
    dot = functools.partial(jnp.dot, preferred_element_type=F32)
    xn = _lane_blocks(xn_ref).astype(BF16)
    act = jnp.where(i < N_PROMPT_TILES, actp_ref[...], acts_ref[...])
    cv = cv_ref[...]
    per_block = D_SSM // MERGE_COLS
    for c in range(N_MERGE):
        cols = pl.ds(c * MERGE_COLS, MERGE_COLS)
        in_block = pl.ds((c % per_block) * MERGE_COLS, MERGE_COLS)
        wga = (wga0_ref, wga1_ref)[c // per_block]
        wgb = (wgb0_ref, wgb1_ref)[c // per_block]
        ya = dot(act, wglu_ref[:, cols]) * jax.nn.sigmoid(
            dot(act, wglu_ref[:, pl.ds(D_MODEL + c * MERGE_COLS, MERGE_COLS)]))
        yb = dot(cv, wco_ref[:, cols])
        m = (jax.nn.sigmoid(dot(xn, wga[:, in_block])) * ya
             + jax.nn.sigmoid(dot(xn, wgb[:, in_block])) * yb)
        m_ref[:, cols] = m.astype(BF16)


def _merge(xn3, actp, acts, cv, w_in_bf, w_glu_bf, w_co_bf):
    ga0 = (D_SSM + 2 * D_CONV) // D_SSM
    wcols = lambda j: pl.BlockSpec((D_MODEL, D_SSM), lambda i: (0, j), pipeline_mode=pl.Buffered(1))
    return pl.pallas_call(
        _merge_kernel,
        grid=(N_TILES,),
        in_specs=[pl.BlockSpec((MODEL_LB, TM, LANES), lambda i: (0, i, 0)),
                  pl.BlockSpec((TM, D_SSM), lambda i: (jnp.minimum(i, N_PROMPT_TILES - 1), 0)),
                  _resident((N_SAMPLE_TOK, D_SSM)),
                  pl.BlockSpec((TM, D_CONV), lambda i: (i, 0)),
                  wcols(ga0), wcols(ga0 + 1), wcols(ga0 + 2), wcols(ga0 + 3),
                  _resident((D_SSM, 2 * D_MODEL)), _resident((D_CONV, D_MODEL))],
        out_specs=pl.BlockSpec((TM, D_MODEL), lambda i: (i, 0)),
        out_shape=jax.ShapeDtypeStruct((N_TOK, D_MODEL), BF16),
        compiler_params=_cparams(("arbitrary",), 58),
        name="gated_merge",
    )(xn3, actp, acts, cv, w_in_bf, w_in_bf, w_in_bf, w_in_bf, w_glu_bf, w_co_bf)


def _outproj_kernel(m_ref, x0_ref, x1_ref, x2_ref, x3_ref, xs_ref, wo_ref, g_ref, wr_ref, br_ref,
                    h_ref, xl_ref, route_ref, gw_ref, cnt_ref, xr_s, lg0, lg1, xb0, xb1, wo_s):
    i = pl.program_id(0)
    tile = jnp.minimum(i, N_OT - 1)
    steps_p = TO // N_PROMPT_SEQ
    steps_s = TO // N_SAMPLE_SEQ

    @pl.when(i == 0)
    def _():
        lg1[...] = jnp.zeros_like(lg1)
        xb1[...] = jnp.zeros_like(xb1)
        quarter = D_MODEL // 4
        for k in range(4):
            wo_s[k * quarter:(k + 1) * quarter, :] = wo_ref[k * quarter:(k + 1) * quarter, :].astype(BF16)

    @pl.when(tile < N_OT_PROMPT)
    def _():
        x = jnp.concatenate([x0_ref[...], x1_ref[...], x2_ref[...], x3_ref[...]], axis=0)
        _to_mixer_order(xr_s, x, N_PROMPT_SEQ, steps_p)

    @pl.when(tile >= N_OT_PROMPT)
    def _():
        k0 = (tile - N_OT_PROMPT) * steps_s
        x = jnp.concatenate(
            [xs_ref[pl.ds(pl.multiple_of(b * SAMPLE_LEN + k0, steps_s), steps_s), :]
             for b in range(N_SAMPLE_SEQ)], axis=0)
        _to_mixer_order(xr_s, x, N_SAMPLE_SEQ, steps_s)

    sets = ((lg0, xb0), (lg1, xb1))
    for r in range(2):
        @pl.when(i % 2 == r)
        def _():
            lg_prev, xb_prev = sets[1 - r]
            _alternate(
                _project(m_ref, xr_s, wo_s, g_ref, wr_ref, br_ref, h_ref, *sets[r]),
                _route_and_sort(lg_prev[...], xb_prev[...], xl_ref, route_ref, gw_ref, cnt_ref))


def _alternate(*chains):
    chains = list(chains)
    while chains:
        for chain in list(chains):
            try:
                next(chain)
            except StopIteration:
                chains.remove(chain)


def _project(m_ref, xr_s, wo_ref, g_ref, wr_ref, br_ref, h_ref, lg_s, xb_s):
    dot = functools.partial(jnp.dot, preferred_element_type=F32)
    n_chunks = 4
    lb = MODEL_LB // n_chunks
    parts = []
    for c in range(n_chunks):
        cols = slice(c * lb * LANES, (c + 1) * lb * LANES)
        x = jnp.concatenate([xr_s[c * lb + k] for k in range(lb)], axis=1)
        part = x + dot(m_ref[...], wo_ref[:, cols])
        h_ref[:, cols] = part
        parts.append(part)
        yield
    h = jnp.concatenate(parts, axis=1)
    xn = _rmsnorm_rows(h, g_ref[...])
    x_hi = xn.astype(BF16)
    x_lo = (xn - x_hi.astype(F32)).astype(BF16)
    w = wr_ref[...]
    w_hi = w.astype(BF16)
    w_lo = (w - w_hi.astype(F32)).astype(BF16)
    xb_s[...] = x_hi
    yield
    both = dot(x_hi, jnp.concatenate([w_hi, w_lo], axis=1))
    yield
    lg_s[...] = (both[:, :ROUTE_LANES] + (dot(x_lo, w_hi) + both[:, ROUTE_LANES:])) + br_ref[...]


def _route_and_sort(logits, x_hi, xl_ref, route_ref, gw_ref, cnt_ref):
    dot = functools.partial(jnp.dot, preferred_element_type=F32)
    lane = lax.broadcasted_iota(I32, logits.shape, 1)
    neg = -jnp.inf
    first = lambda hit: jnp.min(jnp.where(hit, lane, ROUTE_LANES), axis=-1, keepdims=True)
    gmask = lane < N_EXP_GROUPS
    lg = jnp.where(gmask, logits, neg)
    gmax = jnp.max(lg, axis=-1, keepdims=True)
    gsel = first(lg == gmax)
    psum = jnp.sum(jnp.where(gmask, jnp.exp(logits - gmax), 0.0), axis=-1, keepdims=True)
    pg_sel = 1.0 / psum
    e_lane = lane - N_EXP_GROUPS
    emask = (e_lane >= 0) & (e_lane < N_EXPERTS) & ((e_lane // EXP_PER_GROUP) == gsel)
    le = jnp.where(emask, logits, neg)
    v1 = jnp.max(le, axis=-1, keepdims=True)
    i1 = first(le == v1)
    le2 = jnp.where(lane == i1, neg, le)
    v2 = jnp.max(le2, axis=-1, keepdims=True)
    i2 = first(le2 == v2)
    z = jnp.exp(v2 - v1)
    w1 = pg_sel / (1.0 + z)
    w2 = pg_sel * z / (1.0 + z)
    e1 = i1 - N_EXP_GROUPS
    e2 = i2 - N_EXP_GROUPS
    gw_ref[...] = jnp.where(lane == 0, w1, jnp.where(lane == 1, w2, 0.0))
    yield
    oh1 = lane == e1
    oh2 = lane == e2
    hits = jnp.where(oh1 | oh2, 1.0, 0.0)
    rr = lax.broadcasted_iota(I32, (TO, TO), 0)
    cc = lax.broadcasted_iota(I32, (TO, TO), 1)
    before = dot(jnp.where(cc < rr, 1.0, 0.0).astype(BF16), hits.astype(BF16))
    counts = jnp.sum(hits, axis=0, keepdims=True)
    groups = jnp.floor((counts + (RUN - 1.0)) * (1.0 / RUN))
    er = lax.broadcasted_iota(I32, (ROUTE_LANES, ROUTE_LANES), 0)
    ec = lax.broadcasted_iota(I32, (ROUTE_LANES, ROUTE_LANES), 1)
    groups_before = dot(jnp.broadcast_to(groups, (SUBLANES, ROUTE_LANES)).astype(BF16),
                        jnp.where(er < ec, 1.0, 0.0).astype(BF16))[0:1]
    local = groups_before * RUN + before
    lpos1 = jnp.sum(jnp.where(oh1, local, 0.0), axis=-1, keepdims=True)
    lpos2 = jnp.sum(jnp.where(oh2, local, 0.0), axis=-1, keepdims=True)
    route_ref[...] = jnp.where(lane == 0, e1, jnp.where(lane == 1, e2, jnp.where(
        lane == 2, lpos1.astype(I32), jnp.where(lane == 3, lpos2.astype(I32), 0))))
    cnt_ref[0] = counts.astype(I32)
    yield
    lpos_t = jnp.where(lane == 0, lpos1, jnp.where(lane == 1, lpos2, 0.0)).T.astype(I32)
    n_chunks = 4
    rows = LROWS // n_chunks
    for q in range(n_chunks):
        jrow = q * rows + lax.broadcasted_iota(I32, (rows, TO), 0)
        sel = (jrow == lpos_t[0:1, :]) | (jrow == lpos_t[1:2, :])
        xl_ref[q * rows:(q + 1) * rows, :] = dot(jnp.where(sel, 1.0, 0.0).astype(BF16), x_hi).astype(BF16)
        yield


def _outproj(m, xp, xs, w_out_bf, g, w_router, b_router):
    cur = lambda i: jnp.minimum(i, N_OT - 1)
    prev = lambda i: jnp.maximum(i - 1, 0)
    tile = lambda n: pl.BlockSpec((TO, n), lambda i: (cur(i), 0))
    routed = lambda n: pl.BlockSpec((TO, n), lambda i: (prev(i), 0))
    return pl.pallas_call(
        _outproj_kernel,
        grid=(N_OT + 1,),
        in_specs=[tile(D_MODEL)] + _x_tile_specs(TO // N_PROMPT_SEQ) + [
            _resident((D_MODEL, D_MODEL)), _resident((1, D_MODEL)),
            _resident((D_MODEL, ROUTE_LANES)), _resident((1, ROUTE_LANES))],
        out_specs=[tile(D_MODEL),
                   pl.BlockSpec((LROWS, D_MODEL), lambda i: (prev(i), 0)),
                   routed(ROUTE_LANES), routed(ROUTE_LANES),
                   pl.BlockSpec((1, 1, ROUTE_LANES), lambda i: (prev(i), 0, 0))],
        out_shape=(jax.ShapeDtypeStruct((N_TOK, D_MODEL), F32),
                   jax.ShapeDtypeStruct((N_OT * LROWS, D_MODEL), BF16),
                   jax.ShapeDtypeStruct((N_TOK, ROUTE_LANES), I32),
                   jax.ShapeDtypeStruct((N_TOK, ROUTE_LANES), F32),
                   jax.ShapeDtypeStruct((N_OT, 1, ROUTE_LANES), I32)),
        scratch_shapes=[pltpu.VMEM((MODEL_LB, TO, LANES), F32),
                        pltpu.VMEM((TO, ROUTE_LANES), F32), pltpu.VMEM((TO, ROUTE_LANES), F32),
                        pltpu.VMEM((TO, D_MODEL), BF16), pltpu.VMEM((TO, D_MODEL), BF16),
                        pltpu.VMEM((D_MODEL, D_MODEL), BF16)],
        compiler_params=_cparams(("arbitrary",), 56),
        name="out_proj_router",
    )(m, xp, xp, xp, xp, xs, w_out_bf, g, w_router, b_router)


def _expert_kernel(te_ref, nrows_ref, nused_ref, psrc_ref, pdst_ref, plen_ref, tot_ref, kfirst_ref, kend_ref,
                   first_ref, par_ref, nexte_ref,
                   xl_hbm, wg_hbm, wu_hbm, wd_hbm, ol_hbm,
                   xbuf, obuf, zbuf, wg_s, wu_s, wd_s, wg_st, wu_st, wd_st, gsem, wsem, zsem, esem):
    i = pl.program_id(0)
    nused = nused_ref[0]
    slot = i % 2
    rows8 = lambda v: pl.multiple_of(v, RUN)

    def for_pieces(tile, fn):
        def body(k, _):
            idx = tile * N_OT + k
            n = plen_ref[idx]

            @pl.when(n > 0)
            def _():
                fn(rows8(psrc_ref[idx]), rows8(pdst_ref[idx]), rows8(n))
            return 0
        lax.fori_loop(kfirst_ref[tile], kend_ref[tile], body, 0)

    def gather(tile, s):
        for_pieces(tile, lambda src, dst, n: pltpu.make_async_copy(
            xl_hbm.at[pl.ds(src, n)], xbuf.at[s, pl.ds(dst, n)], gsem.at[s]).start())

    def scatter(tile, s):
        for_pieces(tile, lambda src, dst, n: pltpu.make_async_copy(
            obuf.at[s, pl.ds(dst, n)], ol_hbm.at[pl.ds(src, n)], wsem.at[s]).start())

    def wait_gather(tile, s):
        n = rows8(nrows_ref[tile])
        pltpu.make_async_copy(xl_hbm.at[pl.ds(0, n)], xbuf.at[s, pl.ds(0, n)], gsem.at[s]).wait()

    def wait_scatter(tile, s):
        n = rows8(nrows_ref[tile])
        pltpu.make_async_copy(obuf.at[s, pl.ds(0, n)], ol_hbm.at[pl.ds(0, n)], wsem.at[s]).wait()

    def tail_copy(t):
        used = rows8(tot_ref[t])
        n = rows8(LROWS - tot_ref[t])
        dst = rows8(t * LROWS + used)
        return pltpu.make_async_copy(zbuf.at[pl.ds(0, n)], ol_hbm.at[pl.ds(dst, n)], zsem.at[0])

    def weight_copies(e, b):
        return [pltpu.make_async_copy(src.at[e], dst.at[b], esem.at[b, k])
                for k, (src, dst) in enumerate(((wg_hbm, wg_st), (wu_hbm, wu_st), (wd_hbm, wd_st)))]

    @pl.when(i == 0)
    def _():
        for c in weight_copies(te_ref[0], par_ref[0]):
            c.start()
        zbuf[...] = jnp.zeros_like(zbuf)
        xbuf[...] = jnp.zeros_like(xbuf)

        def start(t, _):
            tail_copy(t).start()
            return 0

        def wait(t, _):
            tail_copy(t).wait()
            return 0
        lax.fori_loop(0, N_OT, start, 0)
        lax.fori_loop(0, N_OT, wait, 0)
        gather(0, 0)

    @pl.when(i + 1 < nused)
    def _():
        gather(i + 1, 1 - slot)

    @pl.when(i < nused)
    def _():
        @pl.when(first_ref[i] == 1)
        def _():
            b = par_ref[i]
            for c in weight_copies(te_ref[i], b):
                c.wait()
            wg_s[...] = wg_st[b].astype(BF16)
            wu_s[...] = wu_st[b].astype(BF16)
            wd_s[...] = wd_st[b].astype(BF16)

            @pl.when(nexte_ref[i] >= 0)
            def _():
                for c in weight_copies(nexte_ref[i], 1 - b):
                    c.start()

        wait_gather(i, slot)

        @pl.when(i >= 2)
        def _():
            wait_scatter(i - 2, slot)

        def run_expert(rows):
            xb = xbuf[slot, 0:rows]
            hg = jnp.dot(xb, wg_s[...], preferred_element_type=F32)
            hu = jnp.dot(xb, wu_s[...], preferred_element_type=F32)
            hid = jax.nn.silu(hg) * hu
            obuf[slot, 0:rows] = jnp.dot(
                hid.astype(BF16), wd_s[...], preferred_element_type=F32).astype(BF16)

        quarter = TME // 4
        for k in range(1, 5):
            @pl.when(jnp.logical_and(nrows_ref[i] > (k - 1) * quarter, nrows_ref[i] <= k * quarter))
            def _():
                run_expert(k * quarter)

        scatter(i, slot)

        @pl.when(i == nused - 1)
        def _():
            @pl.when(i >= 1)
            def _():
                wait_scatter(i - 1, 1 - slot)
            wait_scatter(i, slot)


def _experts(plan, xl, wg, wu, wd):
    anyspec = pl.BlockSpec(memory_space=pl.ANY)
    grid_spec = pltpu.PrefetchScalarGridSpec(
        num_scalar_prefetch=len(plan),
        grid=(N_ETILES,),
        in_specs=[anyspec, anyspec, anyspec, anyspec],
        out_specs=anyspec,
        scratch_shapes=[
            pltpu.VMEM((2, TME, D_MODEL), BF16),
            pltpu.VMEM((2, TME, D_MODEL), BF16),
            pltpu.VMEM((LROWS - 2 * TO, D_MODEL), BF16),
            pltpu.VMEM((D_MODEL, D_EXPERT), BF16),
            pltpu.VMEM((D_MODEL, D_EXPERT), BF16),
            pltpu.VMEM((D_EXPERT, D_MODEL), BF16),
            pltpu.VMEM((2, D_MODEL, D_EXPERT), F32),
            pltpu.VMEM((2, D_MODEL, D_EXPERT), F32),
            pltpu.VMEM((2, D_EXPERT, D_MODEL), F32),
            pltpu.SemaphoreType.DMA((2,)),
            pltpu.SemaphoreType.DMA((2,)),
            pltpu.SemaphoreType.DMA((1,)),
            pltpu.SemaphoreType.DMA((2, 3)),
        ],
    )
    return pl.pallas_call(
        _expert_kernel,
        grid_spec=grid_spec,
        out_shape=jax.ShapeDtypeStruct((N_OT * LROWS, D_MODEL), BF16),
        compiler_params=_cparams(("arbitrary",), 40),
        name="routed_experts",
    )(*plan, xl, wg, wu, wd)


def _combine_kernel(h_ref, route_ref, gw_ref, ol_ref, g_ref, yp_ref, ys_ref, y_s):
    i = pl.program_id(0)
    steps_p = TO // N_PROMPT_SEQ
    steps_s = TO // N_SAMPLE_SEQ
    route = route_ref[...]
    gw = gw_ref[...]
    jcol = lax.broadcasted_iota(I32, (TO, LROWS), 1)
    mix = (jnp.where(jcol == route[:, 2:3], gw[:, 0:1], 0.0)
           + jnp.where(jcol == route[:, 3:4], gw[:, 1:2], 0.0))
    moe = jnp.dot(mix.astype(BF16), ol_ref[...], preferred_element_type=F32)
    y = _rmsnorm_rows(h_ref[...] + moe, g_ref[...])
    for c in range(MODEL_LB):
        y_s[c] = y[:, c * LANES:(c + 1) * LANES]

    @pl.when(i < N_OT_PROMPT)
    def _():
        for b in range(N_PROMPT_SEQ):
            yp_ref[b] = _seq_rows(y_s, b, N_PROMPT_SEQ, steps_p)

    @pl.when(i >= N_OT_PROMPT)
    def _():
        k0 = (i - N_OT_PROMPT) * steps_s
        for b in range(N_SAMPLE_SEQ):
            rows = pl.ds(pl.multiple_of(b * SAMPLE_LEN + k0, steps_s), steps_s)
            ys_ref[rows, :] = _seq_rows(y_s, b, N_SAMPLE_SEQ, steps_s)


def _combine(h, route, gw, ol, g):
    tile = lambda n: pl.BlockSpec((TO, n), lambda i: (i, 0))
    return pl.pallas_call(
        _combine_kernel,
        grid=(N_OT,),
        in_specs=[tile(D_MODEL), tile(ROUTE_LANES), tile(ROUTE_LANES),
                  pl.BlockSpec((LROWS, D_MODEL), lambda i: (i, 0)),
                  pl.BlockSpec((1, D_MODEL), lambda i: (0, 0))],
        out_specs=[
            pl.BlockSpec((N_PROMPT_SEQ, TO // N_PROMPT_SEQ, D_MODEL),
                         lambda i: (0, jnp.minimum(i, N_OT_PROMPT - 1), 0)),
            pl.BlockSpec((N_SAMPLE_TOK, D_MODEL), lambda i: (0, 0)),
        ],
        out_shape=(jax.ShapeDtypeStruct((N_PROMPT_SEQ, PROMPT_LEN, D_MODEL), F32),
                   jax.ShapeDtypeStruct((N_SAMPLE_TOK, D_MODEL), F32)),
        scratch_shapes=[pltpu.VMEM((MODEL_LB, TO, LANES), F32)],
        compiler_params=_cparams(("arbitrary",), 48),
        name="combine_norm",
    )(h, route, gw, ol, g)


def _dispatch_plan(cnt):
    experts = jnp.arange(N_EXPERTS, dtype=I32)
    cnt = cnt[:, 0, :N_EXPERTS]
    run = ((cnt + RUN - 1) // RUN) * RUN
    local_start = jnp.cumsum(run, axis=1) - run
    local_used = jnp.sum(run, axis=1)
    total = jnp.sum(run, axis=0)
    padded = ((total + TME - 1) // TME) * TME
    ends = jnp.cumsum(padded)
    starts = ends - padded
    run_start = starts[None, :] + jnp.cumsum(run, axis=0) - run
    n_used = (ends[-1] // TME).astype(I32)
    tile = jnp.arange(N_ETILES, dtype=I32)
    tile_start = tile * TME
    te = jnp.minimum(jnp.sum((ends[None, :] <= tile_start[:, None]).astype(I32), axis=1), N_EXPERTS - 1)
    mine = (te[:, None] == experts[None, :])[:, None, :]
    of_tile = lambda a: jnp.sum(jnp.where(mine, a[None], 0), axis=-1)
    s0 = of_tile(run_start)
    lo = jnp.maximum(s0, tile_start[:, None])
    hi = jnp.minimum(s0 + of_tile(run), tile_start[:, None] + TME)
    plen = jnp.where((tile < n_used)[:, None], jnp.maximum(hi - lo, 0), 0)
    live = plen > 0
    psrc = jnp.where(live, jnp.arange(N_OT, dtype=I32)[None, :] * LROWS + of_tile(local_start) + lo - s0, 0)
    pdst = jnp.where(live, lo - tile_start[:, None], 0)
    last = jnp.sum(jnp.where(tile == n_used - 1, te, 0))
    te = jnp.where(tile < n_used, te, last)
    flat = lambda a: a.reshape(-1).astype(I32)
    none_yet = lambda a: jnp.sum((jnp.cumsum(a.astype(I32), axis=1) == 0).astype(I32), axis=1)
    k_first = none_yet(live)
    k_end = N_OT - none_yet(live[:, ::-1])
    first = jnp.logical_and(tile < n_used, jnp.concatenate([jnp.array([True]), te[1:] != te[:-1]]))
    parity = (jnp.cumsum(first.astype(I32)) - 1) % 2
    next_first = lax.cummin(jnp.where(first, tile, N_ETILES), axis=0, reverse=True)
    next_first = jnp.concatenate([next_first[1:], jnp.array([N_ETILES], I32)])
    next_expert = jnp.where(next_first < N_ETILES, te[jnp.minimum(next_first, N_ETILES - 1)], -1)
    return (te.astype(I32), flat(jnp.sum(plen, axis=1)), n_used.reshape(1),
            flat(psrc), flat(pdst), flat(plen), local_used.astype(I32), flat(k_first), flat(k_end),
            flat(first), flat(parity), flat(next_expert))


def kernel(x_prompt, x_sample, state_ssm_re, state_ssm_im, cache_conv, norm_mix_g, w_in, lam_re, lam_im, log_dt, b_re, b_im, c_re, c_im, d_skip, w_ssm_glu, conv_w, conv_b, conv_ln_g, conv_ln_b, w_conv_out, w_out, norm_ffn_g, w_router_group, b_router_group, w_router_expert, b_router_expert, w_exp_gate, w_exp_up, w_exp_down, norm_final_g):
    assert w_in.shape[0] == 1, "single-layer trunk"
    xp = x_prompt.reshape(N_PROMPT_TOK, D_MODEL)
    xs = x_sample.reshape(N_SAMPLE_TOK, D_MODEL)
    w_in_bf = w_in[0].astype(BF16)
    row = lambda a: a.reshape(1, -1)

    ar_rep, ai_rep, bbr, bbi = _discretise(lam_re[0], lam_im[0], log_dt[0], b_re[0], b_im[0])
    pick = lambda a: a[:, :SSM_STATE].reshape(GROUP_BLOCKS, 1, GB_STATE)
    bm = jnp.concatenate([_block_diag_in(bbr), _block_diag_in(bbi)], axis=-1).astype(BF16)
    cm = jnp.concatenate([_block_diag_out(c_re[0]), -_block_diag_out(c_im[0])], axis=1).astype(BF16)
    dsk = d_skip[0].reshape(GROUP_BLOCKS, 1, GB_IN)
    ar, ai = pick(ar_rep), pick(ai_rep)
    s0r = state_ssm_re[0].reshape(N_SAMPLE_SEQ, -1)
    s0i = state_ssm_im[0].reshape(N_SAMPLE_SEQ, -1)

    u3, v3, xn3 = _inproj(xp, xs, row(norm_mix_g[0]), w_in_bf)
    actp, pfr, pfi = _scan_prompt(u3, bm, cm, ar, ai, dsk)
    acts, sfr, sfi = _scan_sample(u3, bm, cm, ar, ai, dsk, s0r, s0i)

    hist_s = cache_conv[0].transpose(1, 0, 2).reshape(HIST_S, D_CONV)
    cv, ncp, ncs = _conv(v3, hist_s, conv_w[0], row(conv_b[0]), conv_ln_g[0], conv_ln_b[0])

    m = _merge(xn3, actp, acts, cv, w_in_bf, w_ssm_glu[0].astype(BF16), w_conv_out[0].astype(BF16))

    pad_lanes = ROUTE_LANES - N_EXP_GROUPS - N_EXPERTS
    w_router = jnp.concatenate(
        [w_router_group[0], w_router_expert[0], jnp.zeros((D_MODEL, pad_lanes), F32)], axis=1)
    b_router = jnp.concatenate(
        [b_router_group[0], b_router_expert[0], jnp.zeros((pad_lanes,), F32)]).reshape(1, ROUTE_LANES)
    h, xl, route, gw, cnt = _outproj(m, xp, xs, w_out[0], row(norm_ffn_g[0]),
                                     w_router, b_router)

    ol = _experts(_dispatch_plan(cnt), xl, w_exp_gate[0], w_exp_up[0], w_exp_down[0])
    yp, ys = _combine(h, route, gw, ol, row(norm_final_g))

    st = lambda a, n: a.reshape(1, n, N_SSM_GROUPS, SSM_STATE)
    pst = lambda a: st(a.transpose(1, 0, 2), N_PROMPT_SEQ)
    ncp = ncp.reshape(CONV_BUF, N_PROMPT_SEQ, D_CONV).transpose(1, 0, 2)[None]
    ncs = ncs.reshape(CONV_BUF, N_SAMPLE_SEQ, D_CONV).transpose(1, 0, 2)[None]
    return (yp, ys.reshape(N_SAMPLE_SEQ, SAMPLE_LEN, D_MODEL), pst(pfr), pst(pfi), ncp,
            st(sfr, N_SAMPLE_SEQ), st(sfi, N_SAMPLE_SEQ), ncs)
```

```python
import functools

import jax
import jax.numpy as jnp
from jax import lax
from jax.experimental import pallas as pl
from jax.experimental.pallas import tpu as pltpu

F32 = jnp.float32
BF16 = jnp.bfloat16
I32 = jnp.int32

SUBLANES = 8
LANES = 128

D_MODEL = 2048
D_SSM = 1024
D_CONV = 1024
SSM_GROUP = 16
N_SSM_GROUPS = 64
SSM_STATE = 64
CONV_WIDTH = 31
CONV_BUF = CONV_WIDTH - 1
N_EXP_GROUPS = 4
EXP_PER_GROUP = 8
N_EXPERTS = 32
D_EXPERT = 256
EPS = 1e-6

N_PROMPT_SEQ = 4
PROMPT_LEN = 2048
N_SAMPLE_SEQ = 16
SAMPLE_LEN = 32
N_PROMPT_TOK = N_PROMPT_SEQ * PROMPT_LEN
N_SAMPLE_TOK = N_SAMPLE_SEQ * SAMPLE_LEN
N_TOK = N_PROMPT_TOK + N_SAMPLE_TOK

TM = 512
N_PROMPT_TILES = N_PROMPT_TOK // TM
N_TILES = N_TOK // TM
PROMPT_STEPS = TM // N_PROMPT_SEQ
GROUP_BLOCKS = 4
GROUPS_PER_BLOCK = N_SSM_GROUPS // GROUP_BLOCKS
GB_IN = GROUPS_PER_BLOCK * SSM_GROUP
GB_STATE = GROUPS_PER_BLOCK * SSM_STATE
GB_LB = GB_STATE // LANES
HIST_P = CONV_BUF * N_PROMPT_SEQ
HIST_S = CONV_BUF * N_SAMPLE_SEQ
ROUTE_LANES = 128
N_PAIRS = 2 * N_TOK
TO = 256
N_OT = N_TOK // TO
N_OT_PROMPT = N_PROMPT_TOK // TO
HALVES = TM // TO
RUN = 16
LROWS = 1024
TME = 512
N_ETILES = -(-(N_PAIRS + N_OT * N_EXPERTS * (RUN - 1) + N_EXPERTS * (TME - 1)) // TME)
assert 2 * TO + (RUN - 1) * N_EXPERTS <= LROWS
MODEL_LB = D_MODEL // LANES
SSM_LB = D_SSM // LANES
CONV_LB = D_CONV // LANES


def _cparams(sem, vmem_mb):
    return pltpu.CompilerParams(dimension_semantics=sem, vmem_limit_bytes=vmem_mb * 1024 * 1024)


def _resident(shape):
    return pl.BlockSpec(shape, lambda *_: (0,) * len(shape), pipeline_mode=pl.Buffered(1))


def _x_tile_specs(steps=PROMPT_STEPS):
    blocks_per_seq = PROMPT_LEN // steps
    n_prompt = N_PROMPT_SEQ * blocks_per_seq // N_PROMPT_SEQ

    def prompt(b):
        return pl.BlockSpec(
            (steps, D_MODEL),
            lambda i, *_: (b * blocks_per_seq + jnp.minimum(i, n_prompt - 1), 0))

    return [prompt(b) for b in range(N_PROMPT_SEQ)] + [_resident((N_SAMPLE_TOK, D_MODEL))]


def _to_mixer_order(ref3, val, n_seq, steps, at=0):
    for c in range(val.shape[1] // LANES):
        for b in range(n_seq):
            ref3[at + c, pl.ds(b, steps, stride=n_seq), :] = val[b * steps:(b + 1) * steps,
                                                                 c * LANES:(c + 1) * LANES]


def _seq_rows(ref3, b, n_seq, steps):
    return jnp.concatenate(
        [ref3[c, pl.ds(b, steps, stride=n_seq), :] for c in range(ref3.shape[0])], axis=1)


def _tile_to_mixer_order(i, ref3, xs, sample):
    @pl.when(i < N_PROMPT_TILES)
    def _():
        _to_mixer_order(ref3, xs(), N_PROMPT_SEQ, PROMPT_STEPS)

    @pl.when(i >= N_PROMPT_TILES)
    def _():
        _to_mixer_order(ref3, sample(), N_SAMPLE_SEQ, SAMPLE_LEN)


def _lane_blocks(ref3):
    return jnp.concatenate([ref3[c] for c in range(ref3.shape[0])], axis=1)


def _pair_specs(rows, cols, n_prompt_tiles):
    p = pl.BlockSpec((rows, cols), lambda i, *_: (jnp.minimum(i, n_prompt_tiles - 1), 0))
    s = pl.BlockSpec((rows, cols), lambda i, *_: (jnp.maximum(i - n_prompt_tiles, 0), 0))
    return p, s


def _disc_kernel(lr_ref, li_ref, ldt_ref, br_ref, bi_ref, ar_ref, ai_ref, bbr_ref, bbi_ref):
    lr = lr_ref[...]
    li = li_ref[...]
    dt = jnp.exp(ldt_ref[...])
    mag = jnp.exp(lr * dt)
    ar = mag * jnp.cos(li * dt)
    ai = mag * jnp.sin(li * dt)
    den = lr * lr + li * li
    nr = ar - 1.0
    cr = (nr * lr + ai * li) / den
    ci = (ai * lr - nr * li) / den
    br = br_ref[...]
    bi = bi_ref[...]
    ar_ref[...] = ar
    ai_ref[...] = ai
    bbr_ref[...] = cr * br - ci * bi
    bbi_ref[...] = cr * bi + ci * br


def _discretise(lam_re, lam_im, log_dt, b_re, b_im):
    shp = jax.ShapeDtypeStruct((N_SSM_GROUPS, SSM_STATE * SSM_GROUP), F32)
    rep = lambda a: jnp.tile(a, (1, SSM_GROUP))
    ldt = jnp.broadcast_to(log_dt[:, None], (N_SSM_GROUPS, SSM_STATE * SSM_GROUP))
    hp = lambda b: b.transpose(0, 2, 1).reshape(N_SSM_GROUPS, -1)
    return pl.pallas_call(_disc_kernel, out_shape=(shp, shp, shp, shp), name="s5_discretise")(
        rep(lam_re), rep(lam_im), ldt, hp(b_re), hp(b_im))


def _block_diag(blocks, rows_per_group, cols_per_group):
    wide = jnp.tile(blocks, (1, 1, GROUPS_PER_BLOCK))
    r = lax.broadcasted_iota(I32, wide.shape, 1) // rows_per_group
    c = lax.broadcasted_iota(I32, wide.shape, 2) // cols_per_group
    return jnp.where(r == c, wide, 0.0)


def _block_diag_in(bb):
    return _block_diag(bb.reshape(GROUP_BLOCKS, GB_IN, SSM_STATE), SSM_GROUP, SSM_STATE)


def _block_diag_out(c):
    return _block_diag(c.transpose(0, 2, 1).reshape(GROUP_BLOCKS, GB_STATE, SSM_GROUP),
                       SSM_STATE, SSM_GROUP)


def _rmsnorm_rows(x, g):
    r = lax.rsqrt(jnp.mean(x * x, axis=-1, keepdims=True) + EPS)
    return x * r * g


def _inproj_kernel(x0_ref, x1_ref, x2_ref, x3_ref, xs_ref, g_ref, wu_ref, wa_ref, wb_ref,
                   u_ref, v_ref, xn_ref):
    i = pl.program_id(0)
    dot = functools.partial(jnp.dot, preferred_element_type=F32)

    def tile(x, n_seq, steps):
        xn = _rmsnorm_rows(x, g_ref[...])
        xb = xn.astype(BF16)
        n_pieces = 4
        wide = D_SSM // n_pieces
        for k in range(n_pieces):
            cols = slice(k * wide, (k + 1) * wide)
            at = k * (wide // LANES)
            _to_mixer_order(u_ref, dot(xb, wu_ref[:, cols]), n_seq, steps, at)
            xcols = slice(k * D_MODEL // n_pieces, (k + 1) * D_MODEL // n_pieces)
            _to_mixer_order(xn_ref, xn[:, xcols], n_seq, steps, k * (MODEL_LB // n_pieces))
            v = dot(xb, wa_ref[:, cols]) * jax.nn.sigmoid(dot(xb, wb_ref[:, cols]))
            _to_mixer_order(v_ref, v, n_seq, steps, at)

    @pl.when(i < N_PROMPT_TILES)
    def _():
        x = jnp.concatenate([x0_ref[...], x1_ref[...], x2_ref[...], x3_ref[...]], axis=0)
        tile(x, N_PROMPT_SEQ, PROMPT_STEPS)

    @pl.when(i >= N_PROMPT_TILES)
    def _():
        tile(xs_ref[...], N_SAMPLE_SEQ, SAMPLE_LEN)


def _inproj(xp, xs, g, w_in_bf):
    blocked = lambda nb: pl.BlockSpec((nb, TM, LANES), lambda i: (0, i, 0))
    wcols = lambda j: pl.BlockSpec((D_MODEL, D_SSM), lambda i: (0, j), pipeline_mode=pl.Buffered(1))
    return pl.pallas_call(
        _inproj_kernel,
        grid=(N_TILES,),
        in_specs=_x_tile_specs() + [_resident((1, D_MODEL)), wcols(0), wcols(1), wcols(2)],
        out_specs=[blocked(SSM_LB), blocked(CONV_LB), blocked(MODEL_LB)],
        out_shape=(jax.ShapeDtypeStruct((SSM_LB, N_TOK, LANES), F32),
                   jax.ShapeDtypeStruct((CONV_LB, N_TOK, LANES), F32),
                   jax.ShapeDtypeStruct((MODEL_LB, N_TOK, LANES), F32)),
        compiler_params=_cparams(("arbitrary",), 56),
        name="in_proj",
    )(xp, xp, xp, xp, xs, g, w_in_bf, w_in_bf, w_in_bf)


N_SCAN_TILES = GROUP_BLOCKS * N_PROMPT_TILES
SCAN_STAGES = 3


def _scan_prompt_kernel(ua_ref, uc_ref, bm_ref, cm_ref, ar_ref, ai_ref, dsk_ref,
                        act_ref, pfr_ref, pfi_ref, x0, x1, x2, cst, car, fin):
    n = pl.program_id(0)
    q_s = jnp.clip(n - 1, 0, N_SCAN_TILES - 1)
    t_s = q_s % N_PROMPT_TILES
    valid_s = jnp.logical_and(n >= 1, n <= N_SCAN_TILES)

    @pl.when(n == 0)
    def _():
        for buf in (x0, x1, x2):
            buf[...] = jnp.zeros_like(buf)
        car[...] = jnp.zeros_like(car)
        fin[...] = jnp.zeros_like(fin)

    shape = (SUBLANES, GB_STATE)
    top = lax.broadcasted_iota(I32, shape, 0) < N_PROMPT_SEQ
    arb = jnp.broadcast_to(ar_ref[0], shape)
    aib = jnp.broadcast_to(ai_ref[0], shape)
    cst[0] = jnp.where(top, 0.0, arb)
    cst[1] = jnp.where(top, 0.0, aib)
    cst[2] = jnp.where(top, arb, arb * arb - aib * aib)
    cst[3] = jnp.where(top, aib, 2.0 * arb * aib)

    def project_in_pieces(xa, n_pieces=8):
        u = jnp.concatenate([ua_ref[0], ua_ref[1]], axis=1).astype(BF16)
        lb = 2 * GB_LB // n_pieces

        def piece(k):
            x = jnp.dot(u, bm_ref[0, :, k * lb * LANES:(k + 1) * lb * LANES],
                        preferred_element_type=F32)
            for c in range(lb):
                xa[k * lb + c] = x[:, c * LANES:(c + 1) * LANES]
        return [functools.partial(piece, k) for k in range(n_pieces)]

    def project_out_pieces(xc, n_pieces=8):
        lb = 2 * GB_LB // n_pieces
        acc = []

        def piece(k):
            h = jnp.concatenate([xc[k * lb + c] for c in range(lb)], axis=1).astype(BF16)
            part = jnp.dot(h, cm_ref[0, k * lb * LANES:(k + 1) * lb * LANES, :],
                           preferred_element_type=F32)
            acc[:] = [part if not acc else acc[0] + part]

        def final():
            u = jnp.concatenate([uc_ref[0], uc_ref[1]], axis=1)
            act_ref[...] = jax.nn.gelu(acc[0] + dsk_ref[0] * u).astype(BF16)
        return [functools.partial(piece, k) for k in range(n_pieces)], final

    def recur(xs, between=()):
        every = (TM // SUBLANES) // max(len(between), 1)
        top1 = lax.broadcasted_iota(I32, (SUBLANES, LANES), 0) < N_PROMPT_SEQ
        first = t_s == 0
        hr = [jnp.where(first, 0.0, car[:, c * LANES:(c + 1) * LANES]) for c in range(GB_LB)]
        hi = [jnp.where(first, 0.0, car[:, GB_STATE + c * LANES:GB_STATE + (c + 1) * LANES])
              for c in range(GB_LB)]
        for r in range(TM // SUBLANES):
            if between and r % every == 0:
                between[r // every]()
            rows = slice(r * SUBLANES, (r + 1) * SUBLANES)
            for c in range(GB_LB):
                cols = slice(c * LANES, (c + 1) * LANES)
                xr = xs[c, rows, :]
                xi = xs[GB_LB + c, rows, :]
                a1r, a1i, a2r, a2i = cst[0, :, cols], cst[1, :, cols], cst[2, :, cols], cst[3, :, cols]
                sxr = pltpu.roll(xr, N_PROMPT_SEQ, axis=0)
                sxi = pltpu.roll(xi, N_PROMPT_SEQ, axis=0)
                tr = xr + a1r * sxr - a1i * sxi
                ti = xi + a1r * sxi + a1i * sxr
                nr = tr + a2r * hr[c] - a2i * hi[c]
                ni = ti + a2r * hi[c] + a2i * hr[c]
                xs[c, rows, :] = nr
                xs[GB_LB + c, rows, :] = ni
                hr[c] = jnp.where(top1, pltpu.roll(nr, N_PROMPT_SEQ, axis=0), nr)
                hi[c] = jnp.where(top1, pltpu.roll(ni, N_PROMPT_SEQ, axis=0), ni)
        state = jnp.concatenate(hr + hi, axis=1)
        car[...] = state
        last = jnp.logical_and(valid_s, t_s == N_PROMPT_TILES - 1)
        fin[...] = jnp.where(last, state, fin[...])
        pfr_ref[0] = fin[N_PROMPT_SEQ:SUBLANES, 0:GB_STATE]
        pfi_ref[0] = fin[N_PROMPT_SEQ:SUBLANES, GB_STATE:2 * GB_STATE]

    bufs = (x0, x1, x2)
    for r in range(SCAN_STAGES):
        @pl.when(n % SCAN_STAGES == r)
        def _():
            ins = project_in_pieces(bufs[r])
            outs, finish_out = project_out_pieces(bufs[(r + 1) % SCAN_STAGES])
            recur(bufs[(r + 2) % SCAN_STAGES], [p for pair in zip(outs, ins) for p in pair])
            finish_out()


def _scan_prompt(u3, bm, cm, ar, ai, dsk):
    q_a = lambda n: jnp.minimum(n, N_SCAN_TILES - 1)
    q_s = lambda n: jnp.clip(n - 1, 0, N_SCAN_TILES - 1)
    q_c = lambda n: jnp.clip(n - 2, 0, N_SCAN_TILES - 1)
    gb = lambda q: q // N_PROMPT_TILES
    tt = lambda q: q % N_PROMPT_TILES
    u_spec = lambda q: pl.BlockSpec((GB_IN // LANES, TM, LANES), lambda n: (gb(q(n)), tt(q(n)), 0))
    gb3 = lambda last, q: pl.BlockSpec((1,) + last, lambda n: (gb(q(n)), 0, 0))
    state = jax.ShapeDtypeStruct((GROUP_BLOCKS, N_PROMPT_SEQ, GB_STATE), F32)
    xbuf = pltpu.VMEM((2 * GB_LB, TM, LANES), F32)
    return pl.pallas_call(
        _scan_prompt_kernel,
        grid=(N_SCAN_TILES + SCAN_STAGES - 1,),
        in_specs=[
            u_spec(q_a), u_spec(q_c),
            gb3((GB_IN, 2 * GB_STATE), q_a),
            gb3((2 * GB_STATE, GB_IN), q_c),
            gb3((1, GB_STATE), q_s),
            gb3((1, GB_STATE), q_s),
            gb3((1, GB_IN), q_c),
        ],
        out_specs=[
            pl.BlockSpec((TM, GB_IN), lambda n: (tt(q_c(n)), gb(q_c(n)))),
            gb3((N_PROMPT_SEQ, GB_STATE), q_s),
            gb3((N_PROMPT_SEQ, GB_STATE), q_s),
        ],
        out_shape=(jax.ShapeDtypeStruct((N_PROMPT_TOK, D_SSM), BF16), state, state),
        scratch_shapes=[xbuf, xbuf, xbuf,
                        pltpu.VMEM((4, SUBLANES, GB_STATE), F32),
                        pltpu.VMEM((SUBLANES, 2 * GB_STATE), F32),
                        pltpu.VMEM((SUBLANES, 2 * GB_STATE), F32)],
        compiler_params=_cparams(("arbitrary",), 48),
        name="s5_scan_prompt",
    )(u3, u3, bm, cm, ar, ai, dsk)


def _scan_sample_kernel(u_ref, bm_ref, cm_ref, ar_ref, ai_ref, dsk_ref, s0r_ref, s0i_ref,
                        act_ref, sfr_ref, sfi_ref, xs_ref):
    half_lb = GB_LB // 2
    half_cols = half_lb * LANES
    u = jnp.concatenate([u_ref[0], u_ref[1]], axis=1)
    x = jnp.dot(u.astype(BF16), bm_ref[0], preferred_element_type=F32)
    for c in range(2 * GB_LB):
        xs_ref[c] = x[:, c * LANES:(c + 1) * LANES]

    def load_half(rows, half, imag):
        c0 = imag * GB_LB + half * half_lb
        return jnp.concatenate([xs_ref[c0 + c, rows, :] for c in range(half_lb)], axis=1)

    def store_half(rows, half, imag, val):
        c0 = imag * GB_LB + half * half_lb
        for c in range(half_lb):
            xs_ref[c0 + c, rows, :] = val[:, c * LANES:(c + 1) * LANES]

    shape = (N_SAMPLE_SEQ, half_cols)
    for half in range(2):
        cols = slice(half * half_cols, (half + 1) * half_cols)
        arb = jnp.broadcast_to(ar_ref[0][:, cols], shape)
        aib = jnp.broadcast_to(ai_ref[0][:, cols], shape)

        def body(k, carry):
            hr, hi = carry
            rows = pl.ds(pl.multiple_of(k * N_SAMPLE_SEQ, N_SAMPLE_SEQ), N_SAMPLE_SEQ)
            xr = load_half(rows, half, 0)
            xi = load_half(rows, half, 1)
            nhr = xr + arb * hr - aib * hi
            nhi = xi + arb * hi + aib * hr
            store_half(rows, half, 0, nhr)
            store_half(rows, half, 1, nhi)
            return nhr, nhi

        hr, hi = lax.fori_loop(0, SAMPLE_LEN, body, (s0r_ref[:, cols], s0i_ref[:, cols]))
        sfr_ref[:, cols] = hr
        sfi_ref[:, cols] = hi

    h = _lane_blocks(xs_ref).astype(BF16)
    y = jnp.dot(h, cm_ref[0], preferred_element_type=F32) + dsk_ref[0] * u
    act_ref[...] = jax.nn.gelu(y).astype(BF16)


def _scan_sample(u3, bm, cm, ar, ai, dsk, s0r, s0i):
    gb3 = lambda last: pl.BlockSpec((1,) + last, lambda g: (g, 0, 0))
    state = lambda: pl.BlockSpec((N_SAMPLE_SEQ, GB_STATE), lambda g: (0, g))
    sst = jax.ShapeDtypeStruct((N_SAMPLE_SEQ, N_SSM_GROUPS * SSM_STATE), F32)
    return pl.pallas_call(
        _scan_sample_kernel,
        grid=(GROUP_BLOCKS,),
        in_specs=[
            pl.BlockSpec((GB_IN // LANES, TM, LANES), lambda g: (g, N_PROMPT_TILES, 0)),
            gb3((GB_IN, 2 * GB_STATE)), gb3((2 * GB_STATE, GB_IN)),
            gb3((1, GB_STATE)), gb3((1, GB_STATE)), gb3((1, GB_IN)),
            state(), state(),
        ],
        out_specs=[pl.BlockSpec((TM, GB_IN), lambda g: (0, g)), state(), state()],
        out_shape=(jax.ShapeDtypeStruct((N_SAMPLE_TOK, D_SSM), BF16), sst, sst),
        scratch_shapes=[pltpu.VMEM((2 * GB_LB, TM, LANES), F32)],
        compiler_params=_cparams(("arbitrary",), 32),
        name="s5_scan_sample",
    )(u3, bm, cm, ar, ai, dsk, s0r, s0i)


CONV_ROWS = 64
EXT_ROWS = HIST_S + TM


def _conv_kernel(v_ref, hist_ref, w_ref, b_ref, lg_ref, lb_ref,
                 cv_ref, ncp_ref, ncs_ref, ext_ref, sh_ref, acc_ref):
    t = pl.program_id(0)

    def taps(n_seq):
        half = SUBLANES // 2
        if n_seq % SUBLANES:
            lo = HIST_S - HIST_P - SUBLANES
            sh_ref[:, lo:EXT_ROWS - SUBLANES, :] = ext_ref[:, lo + half:EXT_ROWS - half, :]

        def body(rb, _):
            r0 = pl.multiple_of(rb * CONV_ROWS, CONV_ROWS)
            for cb in range(CONV_LB):
                cols = slice(cb * LANES, (cb + 1) * LANES)
                acc = None
                for j in range(CONV_WIDTH):
                    start = HIST_S - (CONV_BUF - j) * n_seq
                    src = ext_ref
                    if start % SUBLANES:
                        src, start = sh_ref, start - half
                    rows = pl.ds(pl.multiple_of(r0 + start, SUBLANES), CONV_ROWS)
                    term = src[cb, rows, :] * w_ref[j:j + 1, cols]
                    acc = term if acc is None else acc + term
                acc_ref[cb, pl.ds(r0, CONV_ROWS), :] = acc + b_ref[:, cols]
            return 0

        lax.fori_loop(0, TM // CONV_ROWS, body, 0)

    @pl.when(t == 0)
    def _():
        ext_ref[:, 0:HIST_S, :] = jnp.zeros((CONV_LB, HIST_S, LANES), F32)

    @pl.when(t == N_PROMPT_TILES)
    def _():
        for c in range(CONV_LB):
            ext_ref[c, 0:HIST_S, :] = hist_ref[:, c * LANES:(c + 1) * LANES]

    ext_ref[:, HIST_S:EXT_ROWS, :] = v_ref[...]

    @pl.when(t < N_PROMPT_TILES)
    def _():
        taps(N_PROMPT_SEQ)

    @pl.when(t == N_PROMPT_TILES)
    def _():
        taps(N_SAMPLE_SEQ)

    y = acc_ref[...]
    mu = jnp.sum(jnp.sum(y, axis=0), axis=-1, keepdims=True) / D_CONV
    yc = y - mu[None]
    var = jnp.sum(jnp.sum(yc * yc, axis=0), axis=-1, keepdims=True) / D_CONV
    z = jax.nn.silu(yc * lax.rsqrt(var + EPS)[None] * lg_ref[...] + lb_ref[...])
    for c in range(CONV_LB):
        cv_ref[:, c * LANES:(c + 1) * LANES] = z[c].astype(BF16)

    @pl.when(t == N_PROMPT_TILES - 1)
    def _():
        for c in range(CONV_LB):
            ncp_ref[:, c * LANES:(c + 1) * LANES] = ext_ref[c, EXT_ROWS - HIST_P:EXT_ROWS, :]

    @pl.when(t < N_PROMPT_TILES - 1)
    def _():
        ext_ref[:, HIST_S - HIST_P:HIST_S, :] = ext_ref[:, EXT_ROWS - HIST_P:EXT_ROWS, :]

    @pl.when(t == N_PROMPT_TILES)
    def _():
        for c in range(CONV_LB):
            ncs_ref[:, c * LANES:(c + 1) * LANES] = ext_ref[c, EXT_ROWS - HIST_S:EXT_ROWS, :]


def _conv(v3, hist_s, conv_w, conv_b, ln_g, ln_b):
    row = lambda n: pl.BlockSpec((n, D_CONV), lambda t: (0, 0))
    lane3 = pl.BlockSpec((CONV_LB, 1, LANES), lambda t: (0, 0, 0))
    ext = pltpu.VMEM((CONV_LB, EXT_ROWS, LANES), F32)
    return pl.pallas_call(
        _conv_kernel,
        grid=(N_TILES,),
        in_specs=[pl.BlockSpec((CONV_LB, TM, LANES), lambda t: (0, t, 0)),
                  row(HIST_S), row(CONV_WIDTH), row(1), lane3, lane3],
        out_specs=[pl.BlockSpec((TM, D_CONV), lambda t: (t, 0)), row(HIST_P), row(HIST_S)],
        out_shape=(jax.ShapeDtypeStruct((N_TOK, D_CONV), BF16),
                   jax.ShapeDtypeStruct((HIST_P, D_CONV), F32),
                   jax.ShapeDtypeStruct((HIST_S, D_CONV), F32)),
        scratch_shapes=[ext, ext, pltpu.VMEM((CONV_LB, TM, LANES), F32)],
        compiler_params=_cparams(("arbitrary",), 48),
        name="conv_branch",
    )(v3, hist_s, conv_w, conv_b,
      ln_g.reshape(CONV_LB, 1, LANES), ln_b.reshape(CONV_LB, 1, LANES))


MERGE_COLS = 256
N_MERGE = D_MODEL // MERGE_COLS


def _merge_kernel(xn_ref, actp_ref, acts_ref, cv_ref, wga0_ref, wga1_ref, wgb0_ref, wgb1_ref,
                  wglu_ref, wco_ref, m_ref):
    i = pl.program_id(0)
    dot = functools.partial(jnp.dot, preferred_element_type=F32)
    xn = _lane_blocks(xn_ref).astype(BF16)
    act = jnp.where(i < N_PROMPT_TILES, actp_ref[...], acts_ref[...])
    cv = cv_ref[...]
    per_block = D_SSM // MERGE_COLS
    for c in range(N_MERGE):
        cols = pl.ds(c * MERGE_COLS, MERGE_COLS)
        in_block = pl.ds((c % per_block) * MERGE_COLS, MERGE_COLS)
        wga = (wga0_ref, wga1_ref)[c // per_block]
        wgb = (wgb0_ref, wgb1_ref)[c // per_block]
        ga = jax.nn.sigmoid(dot(xn, wga[:, in_block]))
        ya = dot(act, wglu_ref[:, cols]) * jax.nn.sigmoid(
            dot(act, wglu_ref[:, pl.ds(D_MODEL + c * MERGE_COLS, MERGE_COLS)]))
        gb = jax.nn.sigmoid(dot(xn, wgb[:, in_block]))
        yb = dot(cv, wco_ref[:, cols])
        m_ref[:, cols] = (ga * ya + gb * yb).astype(BF16)


def _merge(xn3, actp, acts, cv, w_in_bf, w_glu_bf, w_co_bf):
    ga0 = (D_SSM + 2 * D_CONV) // D_SSM
    wcols = lambda j: pl.BlockSpec((D_MODEL, D_SSM), lambda i: (0, j), pipeline_mode=pl.Buffered(1))
    return pl.pallas_call(
        _merge_kernel,
        grid=(N_TILES,),
        in_specs=[pl.BlockSpec((MODEL_LB, TM, LANES), lambda i: (0, i, 0)),
                  pl.BlockSpec((TM, D_SSM), lambda i: (jnp.minimum(i, N_PROMPT_TILES - 1), 0)),
                  _resident((N_SAMPLE_TOK, D_SSM)),
                  pl.BlockSpec((TM, D_CONV), lambda i: (i, 0)),
                  wcols(ga0), wcols(ga0 + 1), wcols(ga0 + 2), wcols(ga0 + 3),
                  _resident((D_SSM, 2 * D_MODEL)), _resident((D_CONV, D_MODEL))],
        out_specs=pl.BlockSpec((TM, D_MODEL), lambda i: (i, 0)),
        out_shape=jax.ShapeDtypeStruct((N_TOK, D_MODEL), BF16),
        compiler_params=_cparams(("arbitrary",), 58),
        name="gated_merge",
    )(xn3, actp, acts, cv, w_in_bf, w_in_bf, w_in_bf, w_in_bf, w_glu_bf, w_co_bf)


def _outproj_kernel(m_ref, x0_ref, x1_ref, x2_ref, x3_ref, xs_ref, wo_ref, g_ref, wr_ref, br_ref,
                    h_ref, xl_ref, route_ref, gw_ref, cnt_ref, xr_s, lg0, lg1, xb0, xb1, wo_s):
    i = pl.program_id(0)
    tile = jnp.minimum(i, N_OT - 1)
    steps_p = TO // N_PROMPT_SEQ
    steps_s = TO // N_SAMPLE_SEQ

    @pl.when(i == 0)
    def _():
        lg1[...] = jnp.zeros_like(lg1)
        xb1[...] = jnp.zeros_like(xb1)
        quarter = D_MODEL // 4
        for k in range(4):
            wo_s[k * quarter:(k + 1) * quarter, :] = wo_ref[k * quarter:(k + 1) * quarter, :].astype(BF16)

    @pl.when(tile < N_OT_PROMPT)
    def _():
        x = jnp.concatenate([x0_ref[...], x1_ref[...], x2_ref[...], x3_ref[...]], axis=0)
        _to_mixer_order(xr_s, x, N_PROMPT_SEQ, steps_p)

    @pl.when(tile >= N_OT_PROMPT)
    def _():
        k0 = (tile - N_OT_PROMPT) * steps_s
        x = jnp.concatenate(
            [xs_ref[pl.ds(pl.multiple_of(b * SAMPLE_LEN + k0, steps_s), steps_s), :]
             for b in range(N_SAMPLE_SEQ)], axis=0)
        _to_mixer_order(xr_s, x, N_SAMPLE_SEQ, steps_s)

    sets = ((lg0, xb0), (lg1, xb1))
    for r in range(2):
        @pl.when(i % 2 == r)
        def _():
            lg_prev, xb_prev = sets[1 - r]
            _alternate(
                _project(m_ref, xr_s, wo_s, g_ref, wr_ref, br_ref, h_ref, *sets[r]),
                _route_and_sort(lg_prev[...], xb_prev[...], xl_ref, route_ref, gw_ref, cnt_ref))


def _alternate(*chains):
    chains = list(chains)
    while chains:
        for chain in list(chains):
            try:
                next(chain)
            except StopIteration:
                chains.remove(chain)


def _project(m_ref, xr_s, wo_ref, g_ref, wr_ref, br_ref, h_ref, lg_s, xb_s):
    dot = functools.partial(jnp.dot, preferred_element_type=F32)
    n_chunks = 4
    lb = MODEL_LB // n_chunks
    parts = []
    for c in range(n_chunks):
        cols = slice(c * lb * LANES, (c + 1) * lb * LANES)
        x = jnp.concatenate([xr_s[c * lb + k] for k in range(lb)], axis=1)
        part = x + dot(m_ref[...], wo_ref[:, cols])
        h_ref[:, cols] = part
        parts.append(part)
        yield
    h = jnp.concatenate(parts, axis=1)
    xn = _rmsnorm_rows(h, g_ref[...])
    x_hi = xn.astype(BF16)
    x_lo = (xn - x_hi.astype(F32)).astype(BF16)
    w = wr_ref[...]
    w_hi = w.astype(BF16)
    w_lo = (w - w_hi.astype(F32)).astype(BF16)
    xb_s[...] = x_hi
    yield
    both = dot(x_hi, jnp.concatenate([w_hi, w_lo], axis=1))
    yield
    lg_s[...] = (both[:, :ROUTE_LANES] + (dot(x_lo, w_hi) + both[:, ROUTE_LANES:])) + br_ref[...]


def _route_and_sort(logits, x_hi, xl_ref, route_ref, gw_ref, cnt_ref):
    dot = functools.partial(jnp.dot, preferred_element_type=F32)
    lane = lax.broadcasted_iota(I32, logits.shape, 1)
    neg = -jnp.inf
    first = lambda hit: jnp.min(jnp.where(hit, lane, ROUTE_LANES), axis=-1, keepdims=True)
    gmask = lane < N_EXP_GROUPS
    lg = jnp.where(gmask, logits, neg)
    gmax = jnp.max(lg, axis=-1, keepdims=True)
    gsel = first(lg == gmax)
    psum = jnp.sum(jnp.where(gmask, jnp.exp(logits - gmax), 0.0), axis=-1, keepdims=True)
    pg_sel = 1.0 / psum
    e_lane = lane - N_EXP_GROUPS
    emask = (e_lane >= 0) & (e_lane < N_EXPERTS) & ((e_lane // EXP_PER_GROUP) == gsel)
    le = jnp.where(emask, logits, neg)
    v1 = jnp.max(le, axis=-1, keepdims=True)
    i1 = first(le == v1)
    le2 = jnp.where(lane == i1, neg, le)
    v2 = jnp.max(le2, axis=-1, keepdims=True)
    i2 = first(le2 == v2)
    z = jnp.exp(v2 - v1)
    w1 = pg_sel / (1.0 + z)
    w2 = pg_sel * z / (1.0 + z)
    e1 = i1 - N_EXP_GROUPS
    e2 = i2 - N_EXP_GROUPS
    gw_ref[...] = jnp.where(lane == 0, w1, jnp.where(lane == 1, w2, 0.0))
    yield
    oh1 = lane == e1
    oh2 = lane == e2
    hits = jnp.where(oh1 | oh2, 1.0, 0.0)
    rr = lax.broadcasted_iota(I32, (TO, TO), 0)
    cc = lax.broadcasted_iota(I32, (TO, TO), 1)
    before = dot(jnp.where(cc < rr, 1.0, 0.0).astype(BF16), hits.astype(BF16))
    counts = jnp.sum(hits, axis=0, keepdims=True)
    groups = jnp.floor((counts + (RUN - 1.0)) * (1.0 / RUN))
    er = lax.broadcasted_iota(I32, (ROUTE_LANES, ROUTE_LANES), 0)
    ec = lax.broadcasted_iota(I32, (ROUTE_LANES, ROUTE_LANES), 1)
    groups_before = dot(jnp.broadcast_to(groups, (SUBLANES, ROUTE_LANES)).astype(BF16),
                        jnp.where(er < ec, 1.0, 0.0).astype(BF16))[0:1]
    local = groups_before * RUN + before
    lpos1 = jnp.sum(jnp.where(oh1, local, 0.0), axis=-1, keepdims=True)
    lpos2 = jnp.sum(jnp.where(oh2, local, 0.0), axis=-1, keepdims=True)
    route_ref[...] = jnp.where(lane == 0, e1, jnp.where(lane == 1, e2, jnp.where(
        lane == 2, lpos1.astype(I32), jnp.where(lane == 3, lpos2.astype(I32), 0))))
    cnt_ref[0] = counts.astype(I32)
    yield
    lpos_t = jnp.where(lane == 0, lpos1, jnp.where(lane == 1, lpos2, 0.0)).T.astype(I32)
    n_chunks = 4
    rows = LROWS // n_chunks
    for q in range(n_chunks):
        jrow = q * rows + lax.broadcasted_iota(I32, (rows, TO), 0)
        sel = (jrow == lpos_t[0:1, :]) | (jrow == lpos_t[1:2, :])
        xl_ref[q * rows:(q + 1) * rows, :] = dot(jnp.where(sel, 1.0, 0.0).astype(BF16), x_hi).astype(BF16)
        yield


def _outproj(m, xp, xs, w_out_bf, g, w_router, b_router):
    cur = lambda i: jnp.minimum(i, N_OT - 1)
    prev = lambda i: jnp.maximum(i - 1, 0)
    tile = lambda n: pl.BlockSpec((TO, n), lambda i: (cur(i), 0))
    routed = lambda n: pl.BlockSpec((TO, n), lambda i: (prev(i), 0))
    return pl.pallas_call(
        _outproj_kernel,
        grid=(N_OT + 1,),
        in_specs=[tile(D_MODEL)] + _x_tile_specs(TO // N_PROMPT_SEQ) + [
            _resident((D_MODEL, D_MODEL)), _resident((1, D_MODEL)),
            _resident((D_MODEL, ROUTE_LANES)), _resident((1, ROUTE_LANES))],
        out_specs=[tile(D_MODEL),
                   pl.BlockSpec((LROWS, D_MODEL), lambda i: (prev(i), 0)),
                   routed(ROUTE_LANES), routed(ROUTE_LANES),
                   pl.BlockSpec((1, 1, ROUTE_LANES), lambda i: (prev(i), 0, 0))],
        out_shape=(jax.ShapeDtypeStruct((N_TOK, D_MODEL), F32),
                   jax.ShapeDtypeStruct((N_OT * LROWS, D_MODEL), BF16),
                   jax.ShapeDtypeStruct((N_TOK, ROUTE_LANES), I32),
                   jax.ShapeDtypeStruct((N_TOK, ROUTE_LANES), F32),
                   jax.ShapeDtypeStruct((N_OT, 1, ROUTE_LANES), I32)),
        scratch_shapes=[pltpu.VMEM((MODEL_LB, TO, LANES), F32),
                        pltpu.VMEM((TO, ROUTE_LANES), F32), pltpu.VMEM((TO, ROUTE_LANES), F32),
                        pltpu.VMEM((TO, D_MODEL), BF16), pltpu.VMEM((TO, D_MODEL), BF16),
                        pltpu.VMEM((D_MODEL, D_MODEL), BF16)],
        compiler_params=_cparams(("arbitrary",), 56),
        name="out_proj_router",
    )(m, xp, xp, xp, xp, xs, w_out_bf, g, w_router, b_router)


def _expert_kernel(te_ref, nrows_ref, nused_ref, psrc_ref, pdst_ref, plen_ref, tot_ref, kfirst_ref, kend_ref,
                   first_ref, par_ref, nexte_ref,
                   xl_hbm, wg_hbm, wu_hbm, wd_hbm, ol_hbm,
                   xbuf, obuf, zbuf, wg_s, wu_s, wd_s, wg_st, wu_st, wd_st, gsem, wsem, zsem, esem):
    i = pl.program_id(0)
    nused = nused_ref[0]
    slot = i % 2
    rows8 = lambda v: pl.multiple_of(v, RUN)

    def for_pieces(tile, fn):
        def body(k, _):
            idx = tile * N_OT + k
            n = plen_ref[idx]

            @pl.when(n > 0)
            def _():
                fn(rows8(psrc_ref[idx]), rows8(pdst_ref[idx]), rows8(n))
            return 0
        lax.fori_loop(kfirst_ref[tile], kend_ref[tile], body, 0)

    def gather(tile, s):
        for_pieces(tile, lambda src, dst, n: pltpu.make_async_copy(
            xl_hbm.at[pl.ds(src, n)], xbuf.at[s, pl.ds(dst, n)], gsem.at[s]).start())

    def scatter(tile, s):
        for_pieces(tile, lambda src, dst, n: pltpu.make_async_copy(
            obuf.at[s, pl.ds(dst, n)], ol_hbm.at[pl.ds(src, n)], wsem.at[s]).start())

    def wait_gather(tile, s):
        n = rows8(nrows_ref[tile])
        pltpu.make_async_copy(xl_hbm.at[pl.ds(0, n)], xbuf.at[s, pl.ds(0, n)], gsem.at[s]).wait()

    def wait_scatter(tile, s):
        n = rows8(nrows_ref[tile])
        pltpu.make_async_copy(obuf.at[s, pl.ds(0, n)], ol_hbm.at[pl.ds(0, n)], wsem.at[s]).wait()

    def tail_copy(t):
        used = rows8(tot_ref[t])
        n = rows8(LROWS - tot_ref[t])
        dst = rows8(t * LROWS + used)
        return pltpu.make_async_copy(zbuf.at[pl.ds(0, n)], ol_hbm.at[pl.ds(dst, n)], zsem.at[0])

    def weight_copies(e, b):
        return [pltpu.make_async_copy(src.at[e], dst.at[b], esem.at[b, k])
                for k, (src, dst) in enumerate(((wg_hbm, wg_st), (wu_hbm, wu_st), (wd_hbm, wd_st)))]

    @pl.when(i == 0)
    def _():
        for c in weight_copies(te_ref[0], par_ref[0]):
            c.start()
        zbuf[...] = jnp.zeros_like(zbuf)
        xbuf[...] = jnp.zeros_like(xbuf)

        def start(t, _):
            tail_copy(t).start()
            return 0

        def wait(t, _):
            tail_copy(t).wait()
            return 0
        lax.fori_loop(0, N_OT, start, 0)
        lax.fori_loop(0, N_OT, wait, 0)
        gather(0, 0)

    @pl.when(i + 1 < nused)
    def _():
        gather(i + 1, 1 - slot)

    @pl.when(i < nused)
    def _():
        @pl.when(first_ref[i] == 1)
        def _():
            b = par_ref[i]
            for c in weight_copies(te_ref[i], b):
                c.wait()
            wg_s[...] = wg_st[b].astype(BF16)
            wu_s[...] = wu_st[b].astype(BF16)
            wd_s[...] = wd_st[b].astype(BF16)

            @pl.when(nexte_ref[i] >= 0)
            def _():
                for c in weight_copies(nexte_ref[i], 1 - b):
                    c.start()

        wait_gather(i, slot)

        @pl.when(i >= 2)
        def _():
            wait_scatter(i - 2, slot)

        def run_expert(rows):
            xb = xbuf[slot, 0:rows]
            hg = jnp.dot(xb, wg_s[...], preferred_element_type=F32)
            hu = jnp.dot(xb, wu_s[...], preferred_element_type=F32)
            hid = jax.nn.silu(hg) * hu
            obuf[slot, 0:rows] = jnp.dot(
                hid.astype(BF16), wd_s[...], preferred_element_type=F32).astype(BF16)

        quarter = TME // 4
        for k in range(1, 5):
            @pl.when(jnp.logical_and(nrows_ref[i] > (k - 1) * quarter, nrows_ref[i] <= k * quarter))
            def _():
                run_expert(k * quarter)

        scatter(i, slot)

        @pl.when(i == nused - 1)
        def _():
            @pl.when(i >= 1)
            def _():
                wait_scatter(i - 1, 1 - slot)
            wait_scatter(i, slot)


def _experts(plan, xl, wg, wu, wd):
    anyspec = pl.BlockSpec(memory_space=pl.ANY)
    grid_spec = pltpu.PrefetchScalarGridSpec(
        num_scalar_prefetch=len(plan),
        grid=(N_ETILES,),
        in_specs=[anyspec, anyspec, anyspec, anyspec],
        out_specs=anyspec,
        scratch_shapes=[
            pltpu.VMEM((2, TME, D_MODEL), BF16),
            pltpu.VMEM((2, TME, D_MODEL), BF16),
            pltpu.VMEM((LROWS - 2 * TO, D_MODEL), BF16),
            pltpu.VMEM((D_MODEL, D_EXPERT), BF16),
            pltpu.VMEM((D_MODEL, D_EXPERT), BF16),
            pltpu.VMEM((D_EXPERT, D_MODEL), BF16),
            pltpu.VMEM((2, D_MODEL, D_EXPERT), F32),
            pltpu.VMEM((2, D_MODEL, D_EXPERT), F32),
            pltpu.VMEM((2, D_EXPERT, D_MODEL), F32),
            pltpu.SemaphoreType.DMA((2,)),
            pltpu.SemaphoreType.DMA((2,)),
            pltpu.SemaphoreType.DMA((1,)),
            pltpu.SemaphoreType.DMA((2, 3)),
        ],
    )
    return pl.pallas_call(
        _expert_kernel,
        grid_spec=grid_spec,
        out_shape=jax.ShapeDtypeStruct((N_OT * LROWS, D_MODEL), BF16),
        compiler_params=_cparams(("arbitrary",), 40),
        name="routed_experts",
    )(*plan, xl, wg, wu, wd)


def _combine_kernel(h_ref, route_ref, gw_ref, ol_ref, g_ref, yp_ref, ys_ref, y_s):
    i = pl.program_id(0)
    steps_p = TO // N_PROMPT_SEQ
    steps_s = TO // N_SAMPLE_SEQ
    route = route_ref[...]
    gw = gw_ref[...]
    jcol = lax.broadcasted_iota(I32, (TO, LROWS), 1)
    mix = (jnp.where(jcol == route[:, 2:3], gw[:, 0:1], 0.0)
           + jnp.where(jcol == route[:, 3:4], gw[:, 1:2], 0.0))
    moe = jnp.dot(mix.astype(BF16), ol_ref[...], preferred_element_type=F32)
    y = _rmsnorm_rows(h_ref[...] + moe, g_ref[...])
    for c in range(MODEL_LB):
        y_s[c] = y[:, c * LANES:(c + 1) * LANES]

    @pl.when(i < N_OT_PROMPT)
    def _():
        for b in range(N_PROMPT_SEQ):
            yp_ref[b] = _seq_rows(y_s, b, N_PROMPT_SEQ, steps_p)

    @pl.when(i >= N_OT_PROMPT)
    def _():
        k0 = (i - N_OT_PROMPT) * steps_s
        for b in range(N_SAMPLE_SEQ):
            rows = pl.ds(pl.multiple_of(b * SAMPLE_LEN + k0, steps_s), steps_s)
            ys_ref[rows, :] = _seq_rows(y_s, b, N_SAMPLE_SEQ, steps_s)


def _combine(h, route, gw, ol, g):
    tile = lambda n: pl.BlockSpec((TO, n), lambda i: (i, 0))
    return pl.pallas_call(
        _combine_kernel,
        grid=(N_OT,),
        in_specs=[tile(D_MODEL), tile(ROUTE_LANES), tile(ROUTE_LANES),
                  pl.BlockSpec((LROWS, D_MODEL), lambda i: (i, 0)),
                  pl.BlockSpec((1, D_MODEL), lambda i: (0, 0))],
        out_specs=[
            pl.BlockSpec((N_PROMPT_SEQ, TO // N_PROMPT_SEQ, D_MODEL),
                         lambda i: (0, jnp.minimum(i, N_OT_PROMPT - 1), 0)),
            pl.BlockSpec((N_SAMPLE_TOK, D_MODEL), lambda i: (0, 0)),
        ],
        out_shape=(jax.ShapeDtypeStruct((N_PROMPT_SEQ, PROMPT_LEN, D_MODEL), F32),
                   jax.ShapeDtypeStruct((N_SAMPLE_TOK, D_MODEL), F32)),
        scratch_shapes=[pltpu.VMEM((MODEL_LB, TO, LANES), F32)],
        compiler_params=_cparams(("arbitrary",), 48),
        name="combine_norm",
    )(h, route, gw, ol, g)


def _dispatch_plan(cnt):
    experts = jnp.arange(N_EXPERTS, dtype=I32)
    cnt = cnt[:, 0, :N_EXPERTS]
    run = ((cnt + RUN - 1) // RUN) * RUN
    local_start = jnp.cumsum(run, axis=1) - run
    local_used = jnp.sum(run, axis=1)
    total = jnp.sum(run, axis=0)
    padded = ((total + TME - 1) // TME) * TME
    ends = jnp.cumsum(padded)
    starts = ends - padded
    run_start = starts[None, :] + jnp.cumsum(run, axis=0) - run
    n_used = (ends[-1] // TME).astype(I32)
    tile = jnp.arange(N_ETILES, dtype=I32)
    tile_start = tile * TME
    te = jnp.minimum(jnp.sum((ends[None, :] <= tile_start[:, None]).astype(I32), axis=1), N_EXPERTS - 1)
    mine = (te[:, None] == experts[None, :])[:, None, :]
    of_tile = lambda a: jnp.sum(jnp.where(mine, a[None], 0), axis=-1)
    s0 = of_tile(run_start)
    lo = jnp.maximum(s0, tile_start[:, None])
    hi = jnp.minimum(s0 + of_tile(run), tile_start[:, None] + TME)
    plen = jnp.where((tile < n_used)[:, None], jnp.maximum(hi - lo, 0), 0)
    live = plen > 0
    psrc = jnp.where(live, jnp.arange(N_OT, dtype=I32)[None, :] * LROWS + of_tile(local_start) + lo - s0, 0)
    pdst = jnp.where(live, lo - tile_start[:, None], 0)
    last = jnp.sum(jnp.where(tile == n_used - 1, te, 0))
    te = jnp.where(tile < n_used, te, last)
    flat = lambda a: a.reshape(-1).astype(I32)
    none_yet = lambda a: jnp.sum((jnp.cumsum(a.astype(I32), axis=1) == 0).astype(I32), axis=1)
    k_first = none_yet(live)
    k_end = N_OT - none_yet(live[:, ::-1])
    first = jnp.logical_and(tile < n_used, jnp.concatenate([jnp.array([True]), te[1:] != te[:-1]]))
    parity = (jnp.cumsum(first.astype(I32)) - 1) % 2
    next_first = lax.cummin(jnp.where(first, tile, N_ETILES), axis=0, reverse=True)
    next_first = jnp.concatenate([next_first[1:], jnp.array([N_ETILES], I32)])
    next_expert = jnp.where(next_first < N_ETILES, te[jnp.minimum(next_first, N_ETILES - 1)], -1)
    return (te.astype(I32), flat(jnp.sum(plen, axis=1)), n_used.reshape(1),
            flat(psrc), flat(pdst), flat(plen), local_used.astype(I32), flat(k_first), flat(k_end),
            flat(first), flat(parity), flat(next_expert))


def kernel(x_prompt, x_sample, state_ssm_re, state_ssm_im, cache_conv, norm_mix_g, w_in, lam_re, lam_im, log_dt, b_re, b_im, c_re, c_im, d_skip, w_ssm_glu, conv_w, conv_b, conv_ln_g, conv_ln_b, w_conv_out, w_out, norm_ffn_g, w_router_group, b_router_group, w_router_expert, b_router_expert, w_exp_gate, w_exp_up, w_exp_down, norm_final_g):
    assert w_in.shape[0] == 1, "single-layer trunk"
    xp = x_prompt.reshape(N_PROMPT_TOK, D_MODEL)
    xs = x_sample.reshape(N_SAMPLE_TOK, D_MODEL)
    w_in_bf = w_in[0].astype(BF16)
    row = lambda a: a.reshape(1, -1)

    ar_rep, ai_rep, bbr, bbi = _discretise(lam_re[0], lam_im[0], log_dt[0], b_re[0], b_im[0])
    pick = lambda a: a[:, :SSM_STATE].reshape(GROUP_BLOCKS, 1, GB_STATE)
    bm = jnp.concatenate([_block_diag_in(bbr), _block_diag_in(bbi)], axis=-1).astype(BF16)
    cm = jnp.concatenate([_block_diag_out(c_re[0]), -_block_diag_out(c_im[0])], axis=1).astype(BF16)
    dsk = d_skip[0].reshape(GROUP_BLOCKS, 1, GB_IN)
    ar, ai = pick(ar_rep), pick(ai_rep)
    s0r = state_ssm_re[0].reshape(N_SAMPLE_SEQ, -1)
    s0i = state_ssm_im[0].reshape(N_SAMPLE_SEQ, -1)

    u3, v3, xn3 = _inproj(xp, xs, row(norm_mix_g[0]), w_in_bf)
    actp, pfr, pfi = _scan_prompt(u3, bm, cm, ar, ai, dsk)
    acts, sfr, sfi = _scan_sample(u3, bm, cm, ar, ai, dsk, s0r, s0i)

    hist_s = cache_conv[0].transpose(1, 0, 2).reshape(HIST_S, D_CONV)
    cv, ncp, ncs = _conv(v3, hist_s, conv_w[0], row(conv_b[0]), conv_ln_g[0], conv_ln_b[0])

    m = _merge(xn3, actp, acts, cv, w_in_bf, w_ssm_glu[0].astype(BF16), w_conv_out[0].astype(BF16))

    pad_lanes = ROUTE_LANES - N_EXP_GROUPS - N_EXPERTS
    w_router = jnp.concatenate(
        [w_router_group[0], w_router_expert[0], jnp.zeros((D_MODEL, pad_lanes), F32)], axis=1)
    b_router = jnp.concatenate(
        [b_router_group[0], b_router_expert[0], jnp.zeros((pad_lanes,), F32)]).reshape(1, ROUTE_LANES)
    h, xl, route, gw, cnt = _outproj(m, xp, xs, w_out[0], row(norm_ffn_g[0]),
                                     w_router, b_router)

    ol = _experts(_dispatch_plan(cnt), xl, w_exp_gate[0], w_exp_up[0], w_exp_down[0])
    yp, ys = _combine(h, route, gw, ol, row(norm_final_g))

    st = lambda a, n: a.reshape(1, n, N_SSM_GROUPS, SSM_STATE)
    pst = lambda a: st(a.transpose(1, 0, 2), N_PROMPT_SEQ)
    ncp = ncp.reshape(CONV_BUF, N_PROMPT_SEQ, D_CONV).transpose(1, 0, 2)[None]
    ncs = ncs.reshape(CONV_BUF, N_SAMPLE_SEQ, D_CONV).transpose(1, 0, 2)[None]
    return (yp, ys.reshape(N_SAMPLE_SEQ, SAMPLE_LEN, D_MODEL), pst(pfr), pst(pfi), ncp,
            st(sfr, N_SAMPLE_SEQ), st(sfi, N_SAMPLE_SEQ), ncs)
```
